```python
import math
import jax, jax.numpy as jnp
from jax import lax
import numpy as np

D_MODEL = 1024
BATCH = 8
SEQ = 4096
DEPTH = 1

DA_HEADS = 4
DA_HEAD_DIM = 64
DA_V_DIM = 2 * DA_HEAD_DIM
DA_WIDTH = DA_HEADS * DA_V_DIM
DA_QK_COLS = DA_HEADS * 2 * DA_HEAD_DIM
RW_HEADS = 8
RW_HEAD_DIM = 64
RW_WIDTH = RW_HEADS * RW_HEAD_DIM
DECAY_LORA = 64
AAA_LORA = 64
GATE_LORA = 128
MIX_WIDTH = DA_WIDTH + RW_WIDTH
DA_SIZES = (DA_QK_COLS, DA_QK_COLS, DA_WIDTH)
RW_SIZES = (RW_WIDTH, RW_WIDTH, RW_WIDTH, DECAY_LORA, AAA_LORA, GATE_LORA)
DA_COLS = DA_QK_COLS * 2 + DA_WIDTH
RW_COLS = RW_WIDTH * 3 + DECAY_LORA + AAA_LORA + GATE_LORA
IN_COLS = DA_COLS + RW_COLS
ROPE_THETA = 500000.0
ROPE_DIM = DA_HEAD_DIM // 4
Q_BLOCK = 128
N_EXPERTS = 32
TOP_K = 4
D_EXPERT = D_MODEL
SWIGLU_ALPHA = 1.702
SWIGLU_LIMIT = 7.0
NORM_EPS = 1e-6
SUBLN_EPS = 1e-5
LN_X_EPS = 64e-5
N_MOD = 6

kernel_name = 'hybrid_diffattn_rwkv7_moe'


def rms_norm(x, w, eps=NORM_EPS):
    xf = x.astype(jnp.float32)
    y = xf * lax.rsqrt(jnp.mean(xf * xf, axis=-1, keepdims=True) + eps)
    return (y * w.astype(jnp.float32)).astype(x.dtype)


def lambda_init_fn(layer):
    return 0.8 - 0.6 * math.exp(-0.3 * layer)


def split_cols(a, sizes):
    idx, s = [], 0
    for n in sizes[:-1]:
        s += n
        idx.append(s)
    return jnp.split(a, idx, axis=-1)


def partial_rope(x, cos, sin):
    half = ROPE_DIM // 2
    x1 = x[..., :half]
    x2 = x[..., half:ROPE_DIM]
    out = jnp.concatenate([x1 * cos - x2 * sin, x2 * cos + x1 * sin, x[..., ROPE_DIM:]], axis=-1)
    return out.astype(x.dtype)


def token_shift(p, mu):
    prev = jnp.pad(p, ((0, 0), (1, 0), (0, 0)))[:, :-1]
    return p + (prev - p) * mu


def diff_attention(q, k, v, cos, sin, lam_q1, lam_k1, lam_q2, lam_k2, subln_w, lambda_init):
    b, t = q.shape[0], q.shape[1]
    f32 = jnp.float32
    q = partial_rope(q.reshape(b, t, 2 * DA_HEADS, DA_HEAD_DIM), cos, sin) * (DA_HEAD_DIM ** -0.5)
    k = partial_rope(k.reshape(b, t, 2 * DA_HEADS, DA_HEAD_DIM), cos, sin)
    q = q.transpose(0, 2, 1, 3)
    k = k.transpose(0, 2, 1, 3)
    v = v.reshape(b, t, DA_HEADS, DA_V_DIM).transpose(0, 2, 1, 3)
    lam = (jnp.exp(jnp.sum(lam_q1.astype(f32) * lam_k1.astype(f32)))
           - jnp.exp(jnp.sum(lam_q2.astype(f32) * lam_k2.astype(f32))) + lambda_init)
    kpos = jnp.arange(t)

    def one_block(i):
        q_blk = lax.dynamic_slice_in_dim(q, i * Q_BLOCK, Q_BLOCK, axis=2)
        s = jnp.einsum('bhqd,bhkd->bhqk', q_blk, k).astype(f32)
        qpos = i * Q_BLOCK + jnp.arange(Q_BLOCK)
        s = jnp.where(qpos[:, None] >= kpos[None, :], s, -jnp.inf)
        p = jax.nn.softmax(s, axis=-1).reshape(b, DA_HEADS, 2, Q_BLOCK, t)
        attn = p[:, :, 0] - lam * p[:, :, 1]
        return jnp.einsum('bhqk,bhkd->bhqd', attn.astype(v.dtype), v)

    o = lax.map(one_block, jnp.arange(t // Q_BLOCK))
    o = o.transpose(1, 0, 3, 2, 4).reshape(b, t, DA_HEADS, DA_V_DIM)
    o = rms_norm(o, subln_w, SUBLN_EPS) * (1.0 - lambda_init)
    return o.reshape(b, t, DA_WIDTH)


def rwkv7_time_mix(r, k, v, w_lora, a_lora, g_lora, w0, w2, a0, a2, g2, k_k, k_a, r_k, ln_w, ln_b):
    b, t = r.shape[0], r.shape[1]
    f32 = jnp.float32
    log_w = -jax.nn.softplus(-(w0 + jnp.tanh(w_lora) @ w2).astype(f32)) - 0.5
    decay = jnp.exp(-jnp.exp(log_w))
    a = jax.nn.sigmoid((a0 + a_lora @ a2).astype(f32))
    g = jax.nn.sigmoid(g_lora) @ g2

    def heads(z):
        return z.astype(f32).reshape(b, t, RW_HEADS, RW_HEAD_DIM)

    kk = heads(k * k_k)
    kk = kk / jnp.maximum(jnp.linalg.norm(kk, axis=-1, keepdims=True), 1e-12)
    k_mod = k.astype(f32) * (1.0 + (a - 1.0) * k_a.astype(f32))
    r_h, k_h, v_h, w_h, a_h = heads(r), heads(k_mod), heads(v), heads(decay), heads(a)
    delta_a = -kk
    delta_b = kk * a_h

    def step(state, inp):
        r_t, w_t, k_t, v_t, da_t, db_t = inp
        sa = jnp.einsum('bhij,bhj->bhi', state, da_t)
        state = (state * w_t[:, :, None, :] + sa[..., None] * db_t[:, :, None, :]
                 + v_t[..., None] * k_t[:, :, None, :])
        return state, jnp.einsum('bhij,bhj->bhi', state, r_t)

    def tm(z):
        return z.transpose(1, 0, 2, 3)

    state0 = jnp.zeros((b, RW_HEADS, RW_HEAD_DIM, RW_HEAD_DIM), f32)
    _, y = lax.scan(step, state0, (tm(r_h), tm(w_h), tm(k_h), tm(v_h), tm(delta_a), tm(delta_b)))
    y = tm(y)
    mean = jnp.mean(y, axis=-1, keepdims=True)
    var = jnp.mean(jnp.square(y - mean), axis=-1, keepdims=True)
    y = ((y - mean) * lax.rsqrt(var + LN_X_EPS) * ln_w.astype(f32).reshape(RW_HEADS, RW_HEAD_DIM)
         + ln_b.astype(f32).reshape(RW_HEADS, RW_HEAD_DIM))
    y = y + jnp.sum(r_h * k_h * r_k.astype(f32), axis=-1, keepdims=True) * v_h
    return (y.reshape(b, t, RW_WIDTH) * g.astype(f32)).astype(r.dtype)


def clamped_swiglu(hid):
    x_glu = jnp.minimum(hid[..., ::2], SWIGLU_LIMIT)
    x_lin = jnp.clip(hid[..., 1::2], -SWIGLU_LIMIT, SWIGLU_LIMIT)
    return x_glu * jax.nn.sigmoid(SWIGLU_ALPHA * x_glu) * (x_lin + 1.0)


def moe_ffn(h, router_w, router_b, w1, b1, w2, b2):
    b, t, d = h.shape
    tok = h.reshape(b * t, d)
    logits = (tok @ router_w + router_b).astype(jnp.float32)
    top_vals, top_idx = lax.top_k(logits, TOP_K)
    top_w = jax.nn.softmax(top_vals, axis=-1)
    gates = jnp.sum(jax.nn.one_hot(top_idx, N_EXPERTS, dtype=jnp.float32) * top_w[..., None], axis=1)

    def expert(acc, xs):
        w1_e, b1_e, w2_e, b2_e, g_e = xs
        out = clamped_swiglu(tok @ w1_e + b1_e) @ w2_e + b2_e
        return acc + g_e[:, None] * out, None

    acc0 = jnp.zeros((b * t, d), jnp.float32)
    acc, _ = lax.scan(expert, acc0, (w1, b1, w2, b2, gates.T))
    return acc.reshape(b, t, d).astype(h.dtype)


def hybrid_layer(x, c, cos, sin, lambda_init, ada_w, ada_b, pre_mix_norm, post_mix_norm,
                 pre_ffn_norm, post_ffn_norm, w_in, w_out, da_lambda_q1, da_lambda_k1,
                 da_lambda_q2, da_lambda_k2, da_subln, rw_mu, rw_w0, rw_w2, rw_a0, rw_a2,
                 rw_g2, rw_k_k, rw_k_a, rw_r_k, rw_ln_w, rw_ln_b, router_w, router_b,
                 moe_w1, moe_b1, moe_w2, moe_b2):
    mod = jax.nn.silu(c) @ ada_w + ada_b
    sh1, sc1, gt1, sh2, sc2, gt2 = [m[:, None, :] for m in jnp.split(mod, N_MOD, axis=-1)]

    h = rms_norm(x, pre_mix_norm) * (1.0 + sc1) + sh1
    proj = h @ w_in
    da_part, rw_part = proj[..., :DA_COLS], proj[..., DA_COLS:]
    da_q, da_k, da_v = split_cols(da_part, DA_SIZES)
    rw_part = token_shift(rw_part, rw_mu)
    rw_r, rw_k, rw_v, rw_wl, rw_al, rw_gl = split_cols(rw_part, RW_SIZES)
    y_da = diff_attention(da_q, da_k, da_v, cos, sin, da_lambda_q1, da_lambda_k1,
                          da_lambda_q2, da_lambda_k2, da_subln, lambda_init)
    y_rw = rwkv7_time_mix(rw_r, rw_k, rw_v, rw_wl, rw_al, rw_gl, rw_w0, rw_w2, rw_a0, rw_a2,
                          rw_g2, rw_k_k, rw_k_a, rw_r_k, rw_ln_w, rw_ln_b)
    y = jnp.concatenate([y_da, y_rw], axis=-1) @ w_out
    x = x + gt1 * rms_norm(y, post_mix_norm)

    h = rms_norm(x, pre_ffn_norm) * (1.0 + sc2) + sh2
    y = moe_ffn(h, router_w, router_b, moe_w1, moe_b1, moe_w2, moe_b2)
    return x + gt2 * rms_norm(y, post_ffn_norm)


def setup_inputs(seed: int = 0) -> dict:
    key = jax.random.key(seed)
    ks = iter(jax.random.split(key, 40))
    f32 = jnp.float32
    L, D, E, F = DEPTH, D_MODEL, N_EXPERTS, D_EXPERT

    def nrm(shape, scale):
        return jax.random.normal(next(ks), shape, f32) * scale

    def gain(shape):
        return 1.0 + nrm(shape, 0.02)

    x = nrm((BATCH, SEQ, D), 1.0)
    c = nrm((BATCH, D), 1.0)
    offset = jax.random.randint(next(ks), (BATCH, 1), 0, 2048, jnp.int32)
    positions = offset + jnp.arange(SEQ, dtype=jnp.int32)[None, :]
    return {
        'x': x,
        'c': c,
        'positions': positions,
        'ada_w': nrm((L, D, N_MOD * D), 0.5 * D ** -0.5),
        'ada_b': nrm((L, N_MOD * D), 0.02),
        'pre_mix_norm': gain((L, D)),
        'post_mix_norm': gain((L, D)),
        'pre_ffn_norm': gain((L, D)),
        'post_ffn_norm': gain((L, D)),
        'w_in': nrm((L, D, IN_COLS), D ** -0.5),
        'w_out': nrm((L, MIX_WIDTH, D), MIX_WIDTH ** -0.5),
        'da_lambda_q1': nrm((L, DA_HEAD_DIM), 0.1),
        'da_lambda_k1': nrm((L, DA_HEAD_DIM), 0.1),
        'da_lambda_q2': nrm((L, DA_HEAD_DIM), 0.1),
        'da_lambda_k2': nrm((L, DA_HEAD_DIM), 0.1),
        'da_subln': gain((L, DA_V_DIM)),
        'rw_mu': jax.random.uniform(next(ks), (L, RW_COLS), f32),
        'rw_w0': -4.0 + 4.0 * jax.random.uniform(next(ks), (L, RW_WIDTH), f32),
        'rw_w2': nrm((L, DECAY_LORA, RW_WIDTH), 0.1 * DECAY_LORA ** -0.5),
        'rw_a0': nrm((L, RW_WIDTH), 0.1),
        'rw_a2': nrm((L, AAA_LORA, RW_WIDTH), 0.1 * AAA_LORA ** -0.5),
        'rw_g2': nrm((L, GATE_LORA, RW_WIDTH), GATE_LORA ** -0.5),
        'rw_k_k': 0.85 + nrm((L, RW_WIDTH), 0.02),
        'rw_k_a': gain((L, RW_WIDTH)),
        'rw_r_k': nrm((L, RW_HEADS, RW_HEAD_DIM), 0.1),
        'rw_ln_w': gain((L, RW_WIDTH)),
        'rw_ln_b': nrm((L, RW_WIDTH), 0.02),
        'router_w': nrm((L, D, E), D ** -0.5),
        'router_b': nrm((L, E), 0.01),
        'moe_w1': nrm((L, E, D, 2 * F), D ** -0.5),
        'moe_b1': nrm((L, E, 2 * F), 0.01),
        'moe_w2': nrm((L, E, F, D), F ** -0.5),
        'moe_b2': nrm((L, E, D), 0.01),
    }


def reference(x, c, positions, ada_w, ada_b, pre_mix_norm, post_mix_norm, pre_ffn_norm,
              post_ffn_norm, w_in, w_out, da_lambda_q1, da_lambda_k1, da_lambda_q2,
              da_lambda_k2, da_subln, rw_mu, rw_w0, rw_w2, rw_a0, rw_a2, rw_g2, rw_k_k,
              rw_k_a, rw_r_k, rw_ln_w, rw_ln_b, router_w, router_b, moe_w1, moe_b1,
              moe_w2, moe_b2):
    inv_freq = ROPE_THETA ** (-jnp.arange(0, ROPE_DIM, 2, dtype=jnp.float32) / ROPE_DIM)
    ang = positions.astype(jnp.float32)[..., None] * inv_freq
    cos = jnp.cos(ang)[:, :, None, :]
    sin = jnp.sin(ang)[:, :, None, :]
    for l in range(DEPTH):
        x = hybrid_layer(x, c, cos, sin, lambda_init_fn(l), ada_w[l], ada_b[l],
                         pre_mix_norm[l], post_mix_norm[l], pre_ffn_norm[l], post_ffn_norm[l],
                         w_in[l], w_out[l], da_lambda_q1[l], da_lambda_k1[l], da_lambda_q2[l],
                         da_lambda_k2[l], da_subln[l], rw_mu[l], rw_w0[l], rw_w2[l], rw_a0[l],
                         rw_a2[l], rw_g2[l], rw_k_k[l], rw_k_a[l], rw_r_k[l], rw_ln_w[l],
                         rw_ln_b[l], router_w[l], router_b[l], moe_w1[l], moe_b1[l],
                         moe_w2[l], moe_b2[l])
    return x
```

```python
import functools
import math

import jax
import jax.numpy as jnp
from jax import lax
from jax.experimental import pallas as pl
from jax.experimental.pallas import tpu as pltpu

F32 = jnp.float32
BF16 = jnp.bfloat16
HIGHEST = lax.Precision.HIGHEST

DA_HEADS = 4
DA_HEAD_DIM = 64
DA_V_DIM = 128
DA_WIDTH = 512
RW_HEADS = 8
RW_HEAD_DIM = 64
RW_WIDTH = 512
DECAY_LORA = 64
AAA_LORA = 64
GATE_LORA = 128
DA_COLS = 1536
RW_COLS = 1792
ROPE_THETA = 500000.0
ROPE_DIM = 16
N_EXPERTS = 32
TOP_K = 4
SWIGLU_ALPHA = 1.702
SWIGLU_LIMIT = 7.0
NORM_EPS = 1e-6
SUBLN_EPS = 1e-5
LN_X_EPS = 64e-5
N_MOD = 6

LANES = 128
VMEM_LIMIT_BYTES = 56 * 1024 * 1024

PROJ_ROWS = 512
ATTN_BLOCK = 256
RW_CHUNK = 64
RW_BLOCK = 256
OUT_ROWS = 512
EXPERT_ROWS = 512
DISPATCH_TOKENS = 512
COMBINE_TOKENS = 256


def _params(*sem):
    return pltpu.CompilerParams(dimension_semantics=sem, vmem_limit_bytes=VMEM_LIMIT_BYTES)


def _bdot(a, b):
    return jnp.dot(a.astype(BF16), b.astype(BF16), preferred_element_type=F32)


def _fdot(a, b):
    return jnp.dot(a, b, preferred_element_type=F32, precision=HIGHEST)


def _fdot_nt(a, b):
    return lax.dot_general(a, b, (((1,), (1,)), ((), ())), preferred_element_type=F32,
                           precision=HIGHEST)


def _fdot_tn(a, b):
    return lax.dot_general(a, b, (((0,), (0,)), ((), ())), preferred_element_type=F32,
                           precision=HIGHEST)


def _rms(x, w, eps):
    return x * lax.rsqrt(jnp.mean(x * x, axis=-1, keepdims=True) + eps) * w


def _mod_kernel(c_ref, w_ref, b_ref, o_ref):
    c = c_ref[...]
    s = c * jax.nn.sigmoid(c)
    o_ref[...] = _bdot(s, w_ref[...]) + b_ref[...]


def _mod(c, ada_w, ada_b):
    b, d = c.shape
    n = ada_w.shape[1]
    return pl.pallas_call(
        _mod_kernel,
        grid=(n // d,),
        in_specs=[pl.BlockSpec((b, d), lambda j: (0, 0)),
                  pl.BlockSpec((d, d), lambda j: (0, j)),
                  pl.BlockSpec((1, d), lambda j: (0, j))],
        out_specs=pl.BlockSpec((b, d), lambda j: (0, j)),
        out_shape=jax.ShapeDtypeStruct((b, n), F32),
        compiler_params=_params("parallel"),
        name="mod",
    )(c, ada_w, ada_b.reshape(1, n))


def _proj_kernel(x_ref, pos_ref, mod_ref, nw_ref, invf_ref, w_ref, mu_ref,
                 q_ref, k_ref, v_ref, rw_ref, carry_ref):
    ti = pl.program_id(1)

    @pl.when(ti == 0)
    def _():
        carry_ref[...] = jnp.zeros_like(carry_ref)

    x = x_ref[...]
    h = _rms(x, nw_ref[...], NORM_EPS) * (1.0 + mod_ref[1:2, :]) + mod_ref[0:1, :]
    hb = h.astype(BF16)

    ang = pos_ref[...].astype(F32) * invf_ref[...]
    cos, sin = jnp.cos(ang), jnp.sin(ang)
    l64 = lax.broadcasted_iota(jnp.int32, ang.shape, 1) % DA_HEAD_DIM
    half = ROPE_DIM // 2
    c_tab = jnp.where(l64 < ROPE_DIM, cos, 1.0)
    s_lo = jnp.where(l64 < half, -sin, 0.0)
    s_hi = jnp.where((l64 >= half) & (l64 < ROPE_DIM), sin, 0.0)

    def rope(z):
        up = pltpu.roll(z, LANES - half, axis=1)
        dn = pltpu.roll(z, half, axis=1)
        return z * c_tab + up * s_lo + dn * s_hi

    for g in range(DA_WIDTH // LANES):
        sl = slice(g * LANES, (g + 1) * LANES)
        qg = jnp.dot(hb, w_ref[:, sl], preferred_element_type=F32)
        q_ref[:, sl] = (rope(qg) * (DA_HEAD_DIM ** -0.5)).astype(q_ref.dtype)
        kg = jnp.dot(hb, w_ref[:, DA_WIDTH + g * LANES:DA_WIDTH + (g + 1) * LANES],
                     preferred_element_type=F32)
        k_ref[:, sl] = rope(kg).astype(k_ref.dtype)
    v_ref[...] = jnp.dot(hb, w_ref[:, 2 * DA_WIDTH:DA_COLS],
                         preferred_element_type=F32).astype(v_ref.dtype)

    p = jnp.dot(hb, w_ref[:, DA_COLS:], preferred_element_type=F32)
    rows = p.shape[0]
    prev = pltpu.roll(p, 1, axis=0)
    first = lax.broadcasted_iota(jnp.int32, p.shape, 0) == 0
    prev = jnp.where(first, carry_ref[0:1, :], prev)
    rw_ref[...] = p + (prev - p) * mu_ref[...]
    carry_ref[0:1, :] = p[rows - 1:rows, :]


def _proj(x, pos3, mod3, norm_w, invf, w_in_b, mu):
    b, t, d = x.shape
    tm = min(PROJ_ROWS, t)
    n_in = w_in_b.shape[1]
    blk = lambda w: pl.BlockSpec((None, tm, w), lambda bi, ti: (bi, ti, 0))
    full = lambda r, c: pl.BlockSpec((r, c), lambda bi, ti: (0, 0))
    return pl.pallas_call(
        _proj_kernel,
        grid=(b, t // tm),
        in_specs=[blk(d), blk(1),
                  pl.BlockSpec((None, N_MOD, d), lambda bi, ti: (bi, 0, 0)),
                  full(1, d), full(1, LANES), full(d, n_in), full(1, RW_COLS)],
        out_specs=[blk(DA_WIDTH), blk(DA_WIDTH), blk(DA_WIDTH), blk(RW_COLS)],
        out_shape=[jax.ShapeDtypeStruct((b, t, DA_WIDTH), BF16)] * 3
        + [jax.ShapeDtypeStruct((b, t, RW_COLS), F32)],
        scratch_shapes=[pltpu.VMEM((8, RW_COLS), F32)],
        compiler_params=_params("parallel", "arbitrary"),
        name="proj",
    )(x, pos3, mod3, norm_w, invf, w_in_b, mu)


def _attn_kernel(q_ref, k_ref, v_ref, lam_ref, subln_ref, o_ref, m_ref, l_ref, acc_ref,
                 *, lambda_init):
    qi = pl.program_id(2)
    tq = q_ref.shape[0]
    q = q_ref[...]
    lane = lax.broadcasted_iota(jnp.int32, q.shape, 1)
    zero = jnp.zeros_like(q)
    qq = jnp.concatenate([jnp.where(lane < DA_HEAD_DIM, q, zero),
                          jnp.where(lane >= DA_HEAD_DIM, q, zero)], axis=0)

    m_ref[...] = jnp.full(m_ref.shape, -jnp.inf, F32)
    l_ref[...] = jnp.zeros(l_ref.shape, F32)
    acc_ref[...] = jnp.zeros(acc_ref.shape, F32)

    def step(j, masked):
        kb = k_ref[pl.ds(pl.multiple_of(j * tq, tq), tq), :]
        vb = v_ref[pl.ds(pl.multiple_of(j * tq, tq), tq), :]
        s = lax.dot_general(qq, kb, (((1,), (1,)), ((), ())), preferred_element_type=F32)
        if masked:
            r = lax.broadcasted_iota(jnp.int32, s.shape, 0) % tq
            c = lax.broadcasted_iota(jnp.int32, s.shape, 1)
            s = jnp.where(r >= c, s, -jnp.inf)
        m_old = m_ref[...]
        m_new = jnp.maximum(m_old, jnp.max(s, axis=-1, keepdims=True))
        alpha = jnp.exp(m_old - m_new)
        p = jnp.exp(s - m_new)
        l_ref[...] = alpha * l_ref[...] + jnp.sum(p, axis=-1, keepdims=True)
        acc_ref[...] = alpha * acc_ref[...] + jnp.dot(p.astype(vb.dtype), vb,
                                                      preferred_element_type=F32)
        m_ref[...] = m_new

    def body(j, carry):
        step(j, False)
        return carry

    lax.fori_loop(0, qi, body, 0)
    step(qi, True)

    o = acc_ref[...] / l_ref[...]
    lam = (jnp.exp(jnp.sum(lam_ref[0:1, :] * lam_ref[1:2, :], axis=-1, keepdims=True))
           - jnp.exp(jnp.sum(lam_ref[2:3, :] * lam_ref[3:4, :], axis=-1, keepdims=True))
           + lambda_init)
    d = o[:tq, :] - lam * o[tq:, :]
    o_ref[...] = (_rms(d, subln_ref[...], SUBLN_EPS) * (1.0 - lambda_init)).astype(o_ref.dtype)


def _attn(q, k, v, lam4, subln, lambda_init):
    b, t, _ = q.shape
    tq = min(ATTN_BLOCK, t)
    return pl.pallas_call(
        functools.partial(_attn_kernel, lambda_init=lambda_init),
        grid=(b, DA_HEADS, t // tq),
        in_specs=[pl.BlockSpec((None, tq, DA_V_DIM), lambda bi, h, qi: (bi, qi, h)),
                  pl.BlockSpec((None, t, DA_V_DIM), lambda bi, h, qi: (bi, 0, h)),
                  pl.BlockSpec((None, t, DA_V_DIM), lambda bi, h, qi: (bi, 0, h)),
                  pl.BlockSpec((4, DA_HEAD_DIM), lambda bi, h, qi: (0, 0)),
                  pl.BlockSpec((1, DA_V_DIM), lambda bi, h, qi: (0, 0))],
        out_specs=pl.BlockSpec((None, tq, DA_V_DIM), lambda bi, h, qi: (bi, qi, h)),
        out_shape=jax.ShapeDtypeStruct((b, t, DA_WIDTH), BF16),
        scratch_shapes=[pltpu.VMEM((2 * tq, 1), F32), pltpu.VMEM((2 * tq, 1), F32),
                        pltpu.VMEM((2 * tq, DA_V_DIM), F32)],
        compiler_params=_params("parallel", "parallel", "arbitrary"),
        name="attn",
    )(q, k, v, lam4, subln)


def _rwkv_kernel(rw_ref, w0_ref, w2_ref, a0_ref, a2_ref, g2_ref, kk_ref, ka_ref, rk_ref,
                 lnw_ref, lnb_ref, o_ref, state_ref, r_s, k_s, v_s, lw_s, kk_s, a_s, g_s):
    ti = pl.program_id(1)

    @pl.when(ti == 0)
    def _():
        state_ref[...] = jnp.zeros_like(state_ref)

    w = RW_WIDTH
    rw = rw_ref[...]
    k = rw[:, w:2 * w]
    wl = rw[:, 3 * w:3 * w + DECAY_LORA]
    al = rw[:, 3 * w + DECAY_LORA:3 * w + DECAY_LORA + AAA_LORA]
    gl = rw[:, 3 * w + DECAY_LORA + AAA_LORA:]
    z = -(w0_ref[...] + _bdot(jnp.tanh(wl), w2_ref[...]))
    softplus = jnp.maximum(z, 0.0) + jnp.log(1.0 + jnp.exp(-jnp.abs(z)))
    a = jax.nn.sigmoid(a0_ref[...] + _bdot(al, a2_ref[...]))
    r_s[...] = rw[:, 0:w]
    v_s[...] = rw[:, 2 * w:3 * w]
    lw_s[...] = -jnp.exp(-softplus - 0.5)
    a_s[...] = a
    g_s[...] = _bdot(jax.nn.sigmoid(gl), g2_ref[...])
    kk_s[...] = k * kk_ref[...]
    k_s[...] = k * (1.0 + (a - 1.0) * ka_ref[...])

    c_len = RW_CHUNK
    n = RW_HEAD_DIM
    row = lax.broadcasted_iota(jnp.int32, (c_len, c_len), 0)
    col = lax.broadcasted_iota(jnp.int32, (c_len, c_len), 1)
    incl = row >= col
    strict = row > col
    tri = jnp.where(incl, 1.0, 0.0).astype(F32)
    eye = jnp.where(row == col, 1.0, 0.0).astype(F32)

    def chunk(ci, carry):
        rows = pl.ds(pl.multiple_of(ci * c_len, c_len), c_len)
        for h in range(RW_HEADS):
            cols = slice(h * n, (h + 1) * n)
            r = r_s[rows, cols]
            kh = k_s[rows, cols]
            v = v_s[rows, cols]
            lw = lw_s[rows, cols]
            kk = kk_s[rows, cols]
            kk = kk / jnp.maximum(jnp.sqrt(jnp.sum(kk * kk, axis=-1, keepdims=True)), 1e-12)
            kka = kk * a_s[rows, cols]
            cum = _fdot(tri, lw)
            end = cum[c_len - 1:c_len, :]
            e_neg = jnp.exp(-cum)
            e_end = jnp.exp(end - cum)
            rt = r * jnp.exp(cum)
            at = -kk * jnp.exp(cum - lw)
            bt = kka * e_neg
            kt = kh * e_neg
            a_ab = jnp.where(strict, _fdot_nt(at, bt), 0.0)
            a_ak = jnp.where(strict, _fdot_nt(at, kt), 0.0)
            a_rb = jnp.where(incl, _fdot_nt(rt, bt), 0.0)
            a_rk = jnp.where(incl, _fdot_nt(rt, kt), 0.0)
            inv = eye + a_ab
            pw = a_ab
            for _ in range(5):
                pw = _fdot(pw, pw)
                inv = inv + _fdot(inv, pw)
            s0 = state_ref[h]
            u = _fdot(inv, _fdot_nt(at, s0) + _fdot(a_ak, v))
            y = _fdot_nt(rt, s0) + _fdot(a_rb, u) + _fdot(a_rk, v)
            state_ref[h] = (s0 * jnp.exp(end) + _fdot_tn(u, kka * e_end) + _fdot_tn(v, kh * e_end))

            mean = jnp.mean(y, axis=-1, keepdims=True)
            yc = y - mean
            var = jnp.mean(yc * yc, axis=-1, keepdims=True)
            yn = yc * lax.rsqrt(var + LN_X_EPS) * lnw_ref[:, cols] + lnb_ref[:, cols]
            bonus = jnp.sum(r * kh * rk_ref[:, cols], axis=-1, keepdims=True) * v
            o_ref[rows, cols] = ((yn + bonus) * g_s[rows, cols]).astype(o_ref.dtype)
        return carry

    lax.fori_loop(0, rw_ref.shape[0] // c_len, chunk, 0)


def _rwkv(rw, w0, w2, a0, a2, g2, k_k, k_a, r_k, ln_w, ln_b):
    b, t, _ = rw.shape
    tb = min(RW_BLOCK, t)
    w = RW_WIDTH
    vec = pl.BlockSpec((1, w), lambda bi, ti: (0, 0))
    mat = lambda r: pl.BlockSpec((r, w), lambda bi, ti: (0, 0))
    return pl.pallas_call(
        _rwkv_kernel,
        grid=(b, t // tb),
        in_specs=[pl.BlockSpec((None, tb, RW_COLS), lambda bi, ti: (bi, ti, 0)),
                  vec, mat(DECAY_LORA), vec, mat(AAA_LORA), mat(GATE_LORA), vec, vec, vec, vec, vec],
        out_specs=pl.BlockSpec((None, tb, w), lambda bi, ti: (bi, ti, 0)),
        out_shape=jax.ShapeDtypeStruct((b, t, w), BF16),
        scratch_shapes=[pltpu.VMEM((RW_HEADS, RW_HEAD_DIM, RW_HEAD_DIM), F32)]
        + [pltpu.VMEM((tb, w), F32)] * 7,
        compiler_params=_params("parallel", "arbitrary"),
        name="rwkv",
    )(rw, w0, w2, a0, a2, g2, k_k, k_a, r_k.reshape(1, w), ln_w, ln_b)


def _pack_rows(x):
    half = x.shape[1] // 2
    hi = pltpu.bitcast(x[:, :half].astype(BF16).astype(F32), jnp.uint32)
    lo = pltpu.bitcast(x[:, half:].astype(BF16).astype(F32), jnp.uint32)
    return hi | (lo >> 16)


def _unpack_rows(u):
    hi = pltpu.bitcast(u & jnp.uint32(0xFFFF0000), F32)
    lo = pltpu.bitcast(u << 16, F32)
    return jnp.concatenate([hi, lo], axis=1)


def _out_kernel(yda_ref, yrw_ref, x_ref, mod_ref, wo_ref, pmn_ref, pfn_ref, rw_ref, rb_ref,
                x1_ref, h2_ref, idx_ref, wgt_ref):
    y = (jnp.dot(yda_ref[...], wo_ref[0:DA_WIDTH, :], preferred_element_type=F32)
         + jnp.dot(yrw_ref[...], wo_ref[DA_WIDTH:, :], preferred_element_type=F32))
    x1 = x_ref[...] + mod_ref[2:3, :] * _rms(y, pmn_ref[...], NORM_EPS)
    x1_ref[...] = x1
    h2 = _rms(x1, pfn_ref[...], NORM_EPS) * (1.0 + mod_ref[4:5, :]) + mod_ref[3:4, :]
    h2_ref[...] = _pack_rows(h2)

    h_hi = h2.astype(BF16)
    h_lo = (h2 - h_hi.astype(F32)).astype(BF16)
    rw = rw_ref[...]
    w_hi = rw.astype(BF16)
    w_lo = (rw - w_hi.astype(F32)).astype(BF16)
    logits = (jnp.dot(h_hi, w_hi, preferred_element_type=F32)
              + jnp.dot(h_hi, w_lo, preferred_element_type=F32)
              + jnp.dot(h_lo, w_hi, preferred_element_type=F32)) + rb_ref[...]

    lane = lax.broadcasted_iota(jnp.int32, logits.shape, 1)
    slot = lax.broadcasted_iota(jnp.int32, idx_ref.shape, 1)
    idx = jnp.zeros(idx_ref.shape, jnp.int32)
    val = jnp.zeros(idx_ref.shape, F32)
    top = None
    for j in range(TOP_K):
        m = jnp.max(logits, axis=-1, keepdims=True)
        i = jnp.min(jnp.where(logits == m, lane, N_EXPERTS), axis=-1, keepdims=True)
        top = m if top is None else top
        idx = jnp.where(slot == j, i, idx)
        val = jnp.where(slot == j, jnp.exp(m - top), val)
        logits = jnp.where(lane == i, -jnp.inf, logits)
    idx_ref[...] = idx
    wgt_ref[...] = val / jnp.sum(val, axis=-1, keepdims=True)


def _out(y_da, y_rw, x, mod3, w_out_b, post_mix_norm, pre_ffn_norm, router_w, router_b):
    b, t, d = x.shape
    tm = min(OUT_ROWS, t)
    e = router_w.shape[1]
    blk = lambda w: pl.BlockSpec((None, tm, w), lambda bi, ti: (bi, ti, 0))
    full = lambda r, c: pl.BlockSpec((r, c), lambda bi, ti: (0, 0))
    return pl.pallas_call(
        _out_kernel,
        grid=(b, t // tm),
        in_specs=[blk(DA_WIDTH), blk(RW_WIDTH), blk(d),
                  pl.BlockSpec((None, N_MOD, d), lambda bi, ti: (bi, 0, 0)),
                  full(d, d), full(1, d), full(1, d), full(d, e), full(1, e)],
        out_specs=[blk(d), blk(d // 2), blk(TOP_K), blk(TOP_K)],
        out_shape=[jax.ShapeDtypeStruct((b, t, d), F32),
                   jax.ShapeDtypeStruct((b, t, d // 2), jnp.uint32),
                   jax.ShapeDtypeStruct((b, t, TOP_K), jnp.int32),
                   jax.ShapeDtypeStruct((b, t, TOP_K), F32)],
        compiler_params=_params("parallel", "parallel"),
        name="out",
    )(y_da, y_rw, x, mod3, w_out_b, post_mix_norm, pre_ffn_norm, router_w,
      router_b.reshape(1, e))


def _route(top_idx, rows_per_tile, n_tiles):
    e_flat = top_idx.reshape(-1)
    onehot = (e_flat[:, None] == jnp.arange(N_EXPERTS, dtype=jnp.int32)[None, :]).astype(jnp.int32)
    csum = jnp.cumsum(onehot, axis=0)
    counts = csum[-1]
    padded = (counts + rows_per_tile - 1) // rows_per_tile * rows_per_tile
    ends = jnp.cumsum(padded)
    starts = ends - padded
    pos = jnp.sum((csum - onehot + starts[None, :]) * onehot, axis=1)
    n_active = ends[-1] // rows_per_tile
    tile_start = jnp.arange(n_tiles, dtype=jnp.int32) * rows_per_tile
    tile = jnp.minimum(tile_start, ends[-1] - 1)
    tile_expert = jnp.sum((tile[:, None] >= ends[None, :]).astype(jnp.int32), axis=1)
    return pos.astype(jnp.int32), tile_expert.astype(jnp.int32), n_active.reshape(1).astype(jnp.int32)


def _dispatch_kernel(pos_ref, h_ref, xs_in_ref, xs_ref, sem):
    del xs_in_ref
    n = pos_ref.shape[1]
    base = pl.program_id(0) * (n // TOP_K)

    def row_copy(j):
        return pltpu.make_async_copy(h_ref.at[pl.ds(base + j // TOP_K, 1)],
                                     xs_ref.at[pl.ds(pos_ref[0, j], 1)], sem)

    def issue(j, carry):
        row_copy(j).start()
        return carry

    def drain(j, carry):
        row_copy(j).wait()
        return carry

    lax.fori_loop(0, n, issue, 0, unroll=8)
    lax.fori_loop(0, n, drain, 0, unroll=8)


def _dispatch(pos, h2p, n_rows):
    n, w = h2p.shape
    tb = min(DISPATCH_TOKENS, n)
    pos3 = pos.reshape(n // tb, 1, tb * TOP_K)
    xs0 = jnp.zeros((n_rows, w), h2p.dtype)
    return pl.pallas_call(
        _dispatch_kernel,
        grid=(n // tb,),
        in_specs=[pl.BlockSpec((None, 1, tb * TOP_K), lambda i: (i, 0, 0),
                               memory_space=pltpu.SMEM),
                  pl.BlockSpec(memory_space=pl.ANY),
                  pl.BlockSpec(memory_space=pl.ANY)],
        out_specs=pl.BlockSpec(memory_space=pl.ANY),
        out_shape=jax.ShapeDtypeStruct((n_rows, w), h2p.dtype),
        scratch_shapes=[pltpu.SemaphoreType.DMA(())],
        input_output_aliases={2: 0},
        compiler_params=_params("arbitrary"),
        name="dispatch",
    )(pos3, h2p, xs0)


def _expert_kernel(te_ref, na_ref, xs_ref, w1g_ref, w1l_ref, b1g_ref, b1l_ref, w2_ref, b2_ref,
                   ys_ref):
    del te_ref
    i = pl.program_id(0)

    @pl.when(i < na_ref[0])
    def _():
        x = _unpack_rows(xs_ref[...]).astype(BF16)
        glu = jnp.dot(x, w1g_ref[...], preferred_element_type=F32) + b1g_ref[...]
        lin = jnp.dot(x, w1l_ref[...], preferred_element_type=F32) + b1l_ref[...]
        glu = jnp.minimum(glu, SWIGLU_LIMIT)
        lin = jnp.clip(lin, -SWIGLU_LIMIT, SWIGLU_LIMIT)
        act = glu * jax.nn.sigmoid(SWIGLU_ALPHA * glu) * (lin + 1.0)
        y = jnp.dot(act.astype(BF16), w2_ref[...], preferred_element_type=F32) + b2_ref[...]
        ys_ref[...] = _pack_rows(y)

    @pl.when(i >= na_ref[0])
    def _():
        ys_ref[...] = jnp.zeros_like(ys_ref)


def _experts(tile_expert, n_active, xs, w1g, w1l, b1g, b1l, w2, b2):
    n_rows, w = xs.shape
    tm = EXPERT_ROWS
    d, f = w1g.shape[1], w1g.shape[2]
    wspec = lambda r, c: pl.BlockSpec((None, r, c), lambda i, te, na: (te[i], 0, 0))
    return pl.pallas_call(
        _expert_kernel,
        grid_spec=pltpu.PrefetchScalarGridSpec(
            num_scalar_prefetch=2,
            grid=(n_rows // tm,),
            in_specs=[pl.BlockSpec((tm, w), lambda i, te, na: (i, 0)),
                      wspec(d, f), wspec(d, f), wspec(1, f), wspec(1, f), wspec(f, d),
                      wspec(1, d)],
            out_specs=pl.BlockSpec((tm, w), lambda i, te, na: (i, 0))),
        out_shape=jax.ShapeDtypeStruct((n_rows, w), jnp.uint32),
        compiler_params=_params("arbitrary"),
        name="expert",
    )(tile_expert, n_active, xs, w1g, w1l, b1g, b1l, w2, b2)


def _combine_kernel(pos_ref, ys_ref, wgt_ref, x1_ref, mod_ref, nw_ref, o_ref, buf_ref, sem):
    tc = x1_ref.shape[0]
    n = pos_ref.shape[1]

    def row_copy(j):
        dst = (j % TOP_K) * tc + j // TOP_K
        return pltpu.make_async_copy(ys_ref.at[pl.ds(pos_ref[0, j], 1)],
                                     buf_ref.at[pl.ds(dst, 1)], sem)

    def issue(j, carry):
        row_copy(j).start()
        return carry

    def drain(j, carry):
        row_copy(j).wait()
        return carry

    lax.fori_loop(0, n, issue, 0, unroll=8)
    lax.fori_loop(0, n, drain, 0, unroll=8)

    wgt = wgt_ref[...]
    acc = jnp.zeros(x1_ref.shape, F32)
    for j in range(TOP_K):
        rows = _unpack_rows(buf_ref[j * tc:(j + 1) * tc, :])
        acc = acc + wgt[:, j:j + 1] * rows
    o_ref[...] = x1_ref[...] + mod_ref[5:6, :] * _rms(acc, nw_ref[...], NORM_EPS)


def _combine(pos, ys, wgt, x1, mod3, post_ffn_norm):
    b, t, d = x1.shape
    tc = min(COMBINE_TOKENS, t)
    nt = t // tc
    pos3 = pos.reshape(b * nt, 1, tc * TOP_K)
    blk = lambda w: pl.BlockSpec((None, tc, w), lambda bi, ti: (bi, ti, 0))
    return pl.pallas_call(
        _combine_kernel,
        grid=(b, nt),
        in_specs=[pl.BlockSpec((None, 1, tc * TOP_K), lambda bi, ti: (bi * nt + ti, 0, 0),
                               memory_space=pltpu.SMEM),
                  pl.BlockSpec(memory_space=pl.ANY),
                  blk(TOP_K), blk(d),
                  pl.BlockSpec((None, N_MOD, d), lambda bi, ti: (bi, 0, 0)),
                  pl.BlockSpec((1, d), lambda bi, ti: (0, 0))],
        out_specs=blk(d),
        out_shape=jax.ShapeDtypeStruct((b, t, d), F32),
        scratch_shapes=[pltpu.VMEM((tc * TOP_K, d // 2), jnp.uint32),
                        pltpu.SemaphoreType.DMA(())],
        compiler_params=_params("arbitrary", "arbitrary"),
        name="combine",
    )(pos3, ys, wgt, x1, mod3, post_ffn_norm)


def _stages(x, c, positions, ada_w, ada_b, pre_mix_norm, post_mix_norm, pre_ffn_norm,
            post_ffn_norm, w_in, w_out, da_lambda_q1, da_lambda_k1, da_lambda_q2, da_lambda_k2,
            da_subln, rw_mu, rw_w0, rw_w2, rw_a0, rw_a2, rw_g2, rw_k_k, rw_k_a, rw_r_k, rw_ln_w,
            rw_ln_b, router_w, router_b, moe_w1, moe_b1, moe_w2, moe_b2):
    b, t, d = x.shape
    res = {}
    lambda_init = 0.8 - 0.6 * math.exp(-0.3 * 0)
    mod = _mod(c, ada_w[0], ada_b[0])
    res["mod"] = mod
    mod3 = mod.reshape(b, N_MOD, d)
    inv_freq = ROPE_THETA ** (-jnp.arange(0, ROPE_DIM, 2, dtype=F32) / ROPE_DIM)
    invf = jnp.tile(inv_freq, LANES // (ROPE_DIM // 2)).reshape(1, LANES)
    q, k, v, rw = _proj(x, positions.reshape(b, t, 1), mod3, pre_mix_norm, invf,
                        w_in[0].astype(BF16), rw_mu)
    res.update(q=q, k=k, v=v, rw=rw)
    lam4 = jnp.concatenate([da_lambda_q1, da_lambda_k1, da_lambda_q2, da_lambda_k2], axis=0)
    y_da = _attn(q, k, v, lam4, da_subln, lambda_init)
    res["y_da"] = y_da
    y_rw = _rwkv(rw, rw_w0, rw_w2[0], rw_a0, rw_a2[0], rw_g2[0], rw_k_k, rw_k_a, rw_r_k[0],
                 rw_ln_w, rw_ln_b)
    res["y_rw"] = y_rw
    x1, h2p, top_idx, top_w = _out(y_da, y_rw, x, mod3, w_out[0].astype(BF16), post_mix_norm,
                                   pre_ffn_norm, router_w[0], router_b[0])
    res.update(x1=x1, top_idx=top_idx, top_w=top_w)
    n = b * t
    n_tiles = n * TOP_K // EXPERT_ROWS + N_EXPERTS
    pos, tile_expert, n_active = _route(top_idx.reshape(n, TOP_K), EXPERT_ROWS, n_tiles)
    xs = _dispatch(pos, h2p.reshape(n, d // 2), n_tiles * EXPERT_ROWS)
    w1 = moe_w1[0]
    b1 = moe_b1[0]
    ys = _experts(tile_expert, n_active, xs,
                  w1[:, :, 0::2].astype(BF16), w1[:, :, 1::2].astype(BF16),
                  b1[:, None, 0::2], b1[:, None, 1::2],
                  moe_w2[0].astype(BF16), moe_b2[0][:, None, :])
    res["final"] = _combine(pos, ys, top_w, x1, mod3, post_ffn_norm)
    return res


stages = _stages


def kernel(x, c, positions, ada_w, ada_b, pre_mix_norm, post_mix_norm, pre_ffn_norm, post_ffn_norm, w_in, w_out, da_lambda_q1, da_lambda_k1, da_lambda_q2, da_lambda_k2, da_subln, rw_mu, rw_w0, rw_w2, rw_a0, rw_a2, rw_g2, rw_k_k, rw_k_a, rw_r_k, rw_ln_w, rw_ln_b, router_w, router_b, moe_w1, moe_b1, moe_w2, moe_b2):
    res = _stages(x, c, positions, ada_w, ada_b, pre_mix_norm, post_mix_norm, pre_ffn_norm,
                  post_ffn_norm, w_in, w_out, da_lambda_q1, da_lambda_k1, da_lambda_q2,
                  da_lambda_k2, da_subln, rw_mu, rw_w0, rw_w2, rw_a0, rw_a2, rw_g2, rw_k_k,
                  rw_k_a, rw_r_k, rw_ln_w, rw_ln_b, router_w, router_b, moe_w1, moe_b1,
                  moe_w2, moe_b2)
    return res["final"]
```

```python
import functools
import math

import jax
import jax.numpy as jnp
from jax import lax
from jax.experimental import pallas as pl
from jax.experimental.pallas import tpu as pltpu

F32 = jnp.float32
BF16 = jnp.bfloat16
HIGHEST = lax.Precision.HIGHEST

DA_HEADS = 4
DA_HEAD_DIM = 64
DA_V_DIM = 128
DA_WIDTH = 512
RW_HEADS = 8
RW_HEAD_DIM = 64
RW_WIDTH = 512
DECAY_LORA = 64
AAA_LORA = 64
GATE_LORA = 128
DA_COLS = 1536
RW_COLS = 1792
ROPE_THETA = 500000.0
ROPE_DIM = 16
N_EXPERTS = 32
TOP_K = 4
SWIGLU_ALPHA = 1.702
SWIGLU_LIMIT = 7.0
NORM_EPS = 1e-6
SUBLN_EPS = 1e-5
LN_X_EPS = 64e-5
N_MOD = 6

LANES = 128
VMEM_LIMIT_BYTES = 56 * 1024 * 1024

PROJ_ROWS = 512
ATTN_BLOCK = 256
ATTN_KV_BLOCK = 512
RW_CHUNK = 64
RW_BLOCK = 256
OUT_ROWS = 512
EXPERT_ROWS = 512
DISPATCH_TOKENS = 512
COMBINE_TOKENS = 256


def _params(*sem):
    return pltpu.CompilerParams(dimension_semantics=sem, vmem_limit_bytes=VMEM_LIMIT_BYTES)


def _bdot(a, b):
    return jnp.dot(a.astype(BF16), b.astype(BF16), preferred_element_type=F32)


def _bdot_nt(a, b):
    return lax.dot_general(a.astype(BF16), b.astype(BF16), (((1,), (1,)), ((), ())),
                           preferred_element_type=F32)


def _bdot_tn(a, b):
    return lax.dot_general(a.astype(BF16), b.astype(BF16), (((0,), (0,)), ((), ())),
                           preferred_element_type=F32)


def _rms(x, w, eps):
    return x * lax.rsqrt(jnp.mean(x * x, axis=-1, keepdims=True) + eps) * w


def _mod_kernel(c_ref, w_ref, b_ref, o_ref):
    c = c_ref[...]
    s = c * jax.nn.sigmoid(c)
    o_ref[...] = _bdot(s, w_ref[...]) + b_ref[...]


def _mod(c, ada_w, ada_b):
    b, d = c.shape
    n = ada_w.shape[1]
    return pl.pallas_call(
        _mod_kernel,
        grid=(n // d,),
        in_specs=[pl.BlockSpec((b, d), lambda j: (0, 0)),
                  pl.BlockSpec((d, d), lambda j: (0, j)),
                  pl.BlockSpec((1, d), lambda j: (0, j))],
        out_specs=pl.BlockSpec((b, d), lambda j: (0, j)),
        out_shape=jax.ShapeDtypeStruct((b, n), F32),
        compiler_params=_params("parallel"),
        name="mod",
    )(c, ada_w, ada_b.reshape(1, n))


def _proj_kernel(x_ref, pos_ref, mod_ref, nw_ref, invf_ref, w_ref, mu_ref,
                 q_ref, k_ref, v_ref, rw_ref, carry_ref):
    ti = pl.program_id(1)

    @pl.when(ti == 0)
    def _():
        carry_ref[...] = jnp.zeros_like(carry_ref)

    x = x_ref[...]
    h = _rms(x, nw_ref[...], NORM_EPS) * (1.0 + mod_ref[1:2, :]) + mod_ref[0:1, :]
    hb = h.astype(BF16)

    ang = pos_ref[...].astype(F32) * invf_ref[...]
    cos, sin = jnp.cos(ang), jnp.sin(ang)
    l64 = lax.broadcasted_iota(jnp.int32, ang.shape, 1) % DA_HEAD_DIM
    half = ROPE_DIM // 2
    c_tab = jnp.where(l64 < ROPE_DIM, cos, 1.0)
    s_lo = jnp.where(l64 < half, -sin, 0.0)
    s_hi = jnp.where((l64 >= half) & (l64 < ROPE_DIM), sin, 0.0)

    def rope(z):
        up = pltpu.roll(z, LANES - half, axis=1)
        dn = pltpu.roll(z, half, axis=1)
        return z * c_tab + up * s_lo + dn * s_hi

    for g in range(DA_WIDTH // LANES):
        sl = slice(g * LANES, (g + 1) * LANES)
        qg = jnp.dot(hb, w_ref[:, sl], preferred_element_type=F32)
        q_ref[:, sl] = (rope(qg) * (DA_HEAD_DIM ** -0.5)).astype(q_ref.dtype)
        kg = jnp.dot(hb, w_ref[:, DA_WIDTH + g * LANES:DA_WIDTH + (g + 1) * LANES],
                     preferred_element_type=F32)
        k_ref[:, sl] = rope(kg).astype(k_ref.dtype)
    v_ref[...] = jnp.dot(hb, w_ref[:, 2 * DA_WIDTH:DA_COLS],
                         preferred_element_type=F32).astype(v_ref.dtype)

    p = jnp.dot(hb, w_ref[:, DA_COLS:], preferred_element_type=F32)
    rows = p.shape[0]
    prev = pltpu.roll(p, 1, axis=0)
    first = lax.broadcasted_iota(jnp.int32, p.shape, 0) == 0
    prev = jnp.where(first, carry_ref[0:1, :], prev)
    rw_ref[...] = p + (prev - p) * mu_ref[...]
    carry_ref[0:1, :] = p[rows - 1:rows, :]


def _proj(x, pos3, mod3, norm_w, invf, w_in_b, mu):
    b, t, d = x.shape
    tm = min(PROJ_ROWS, t)
    n_in = w_in_b.shape[1]
    blk = lambda w: pl.BlockSpec((None, tm, w), lambda bi, ti: (bi, ti, 0))
    full = lambda r, c: pl.BlockSpec((r, c), lambda bi, ti: (0, 0))
    return pl.pallas_call(
        _proj_kernel,
        grid=(b, t // tm),
        in_specs=[blk(d), blk(1),
                  pl.BlockSpec((None, N_MOD, d), lambda bi, ti: (bi, 0, 0)),
                  full(1, d), full(1, LANES), full(d, n_in), full(1, RW_COLS)],
        out_specs=[blk(DA_WIDTH), blk(DA_WIDTH), blk(DA_WIDTH), blk(RW_COLS)],
        out_shape=[jax.ShapeDtypeStruct((b, t, DA_WIDTH), BF16)] * 3
        + [jax.ShapeDtypeStruct((b, t, RW_COLS), F32)],
        scratch_shapes=[pltpu.VMEM((8, RW_COLS), F32)],
        compiler_params=_params("parallel", "arbitrary"),
        name="proj",
    )(x, pos3, mod3, norm_w, invf, w_in_b, mu)


def _attn_kernel(q_ref, k_ref, v_ref, lam_ref, subln_ref, o_ref, m_ref, l_ref, acc_ref,
                 *, lambda_init):
    qi = pl.program_id(2)
    tq = q_ref.shape[0]
    q = q_ref[...]
    lane = lax.broadcasted_iota(jnp.int32, q.shape, 1)
    zero = jnp.zeros_like(q)
    qq = jnp.concatenate([jnp.where(lane < DA_HEAD_DIM, q, zero),
                          jnp.where(lane >= DA_HEAD_DIM, q, zero)], axis=0)

    m_ref[...] = jnp.full(m_ref.shape, -jnp.inf, F32)
    l_ref[...] = jnp.zeros(l_ref.shape, F32)
    acc_ref[...] = jnp.zeros(acc_ref.shape, F32)
    tk = ATTN_KV_BLOCK if k_ref.shape[0] % ATTN_KV_BLOCK == 0 else tq
    rep = tk // LANES

    def step(j, masked):
        kb = k_ref[pl.ds(pl.multiple_of(j * tk, tk), tk), :]
        vb = v_ref[pl.ds(pl.multiple_of(j * tk, tk), tk), :]
        s = lax.dot_general(qq, kb, (((1,), (1,)), ((), ())), preferred_element_type=F32)
        if masked:
            qpos = qi * tq + lax.broadcasted_iota(jnp.int32, s.shape, 0) % tq
            kpos = j * tk + lax.broadcasted_iota(jnp.int32, s.shape, 1)
            s = jnp.where(qpos >= kpos, s, -jnp.inf)
        m_old = m_ref[...]
        m_new = jnp.maximum(m_old, jnp.max(s, axis=-1, keepdims=True))
        alpha = jnp.exp(m_old - m_new)
        p = jnp.exp(s - pltpu.repeat(m_new, rep, axis=1))
        l_ref[...] = alpha * l_ref[...] + jnp.sum(p, axis=-1, keepdims=True)
        acc_ref[...] = alpha * acc_ref[...] + jnp.dot(p.astype(vb.dtype), vb,
                                                      preferred_element_type=F32)
        m_ref[...] = m_new

    def body(j, carry):
        step(j, False)
        return carry

    n_full = (qi * tq) // tk
    lax.fori_loop(0, n_full, body, 0)
    step(n_full, True)

    o = acc_ref[...] / l_ref[...]
    lam = (jnp.exp(jnp.sum(lam_ref[0:1, :] * lam_ref[1:2, :], axis=-1, keepdims=True))
           - jnp.exp(jnp.sum(lam_ref[2:3, :] * lam_ref[3:4, :], axis=-1, keepdims=True))
           + lambda_init)
    d = o[:tq, :] - lam * o[tq:, :]
    o_ref[...] = (_rms(d, subln_ref[...], SUBLN_EPS) * (1.0 - lambda_init)).astype(o_ref.dtype)


def _attn(q, k, v, lam4, subln, lambda_init):
    b, t, _ = q.shape
    tq = min(ATTN_BLOCK, t)
    return pl.pallas_call(
        functools.partial(_attn_kernel, lambda_init=lambda_init),
        grid=(b, DA_HEADS, t // tq),
        in_specs=[pl.BlockSpec((None, tq, DA_V_DIM), lambda bi, h, qi: (bi, qi, h)),
                  pl.BlockSpec((None, t, DA_V_DIM), lambda bi, h, qi: (bi, 0, h)),
                  pl.BlockSpec((None, t, DA_V_DIM), lambda bi, h, qi: (bi, 0, h)),
                  pl.BlockSpec((4, DA_HEAD_DIM), lambda bi, h, qi: (0, 0)),
                  pl.BlockSpec((1, DA_V_DIM), lambda bi, h, qi: (0, 0))],
        out_specs=pl.BlockSpec((None, tq, DA_V_DIM), lambda bi, h, qi: (bi, qi, h)),
        out_shape=jax.ShapeDtypeStruct((b, t, DA_WIDTH), BF16),
        scratch_shapes=[pltpu.VMEM((2 * tq, LANES), F32), pltpu.VMEM((2 * tq, LANES), F32),
                        pltpu.VMEM((2 * tq, DA_V_DIM), F32)],
        compiler_params=_params("parallel", "parallel", "arbitrary"),
        name="attn",
    )(q, k, v, lam4, subln)


def _rwkv_kernel(rw_ref, w0_ref, w2_ref, a0_ref, a2_ref, g2_ref, kk_ref, ka_ref, rk_ref,
                 lnw_ref, lnb_ref, o_ref, state_ref, r_s, k_s, v_s, lw_s, kk_s, a_s, g_s, cum_s):
    ti = pl.program_id(1)

    @pl.when(ti == 0)
    def _():
        state_ref[...] = jnp.zeros_like(state_ref)

    w = RW_WIDTH
    rw = rw_ref[...]
    k = rw[:, w:2 * w]
    wl = rw[:, 3 * w:3 * w + DECAY_LORA]
    al = rw[:, 3 * w + DECAY_LORA:3 * w + DECAY_LORA + AAA_LORA]
    gl = rw[:, 3 * w + DECAY_LORA + AAA_LORA:]
    z = -(w0_ref[...] + _bdot(jnp.tanh(wl), w2_ref[...]))
    softplus = jnp.maximum(z, 0.0) + jnp.log(1.0 + jnp.exp(-jnp.abs(z)))
    a = jax.nn.sigmoid(a0_ref[...] + _bdot(al, a2_ref[...]))
    r_s[...] = rw[:, 0:w]
    v_s[...] = rw[:, 2 * w:3 * w]
    lw_s[...] = -jnp.exp(-softplus - 0.5)
    a_s[...] = a
    g_s[...] = _bdot(jax.nn.sigmoid(gl), g2_ref[...])
    kk_s[...] = k * kk_ref[...]
    k_s[...] = k * (1.0 + (a - 1.0) * ka_ref[...])

    c_len = RW_CHUNK
    n = RW_HEAD_DIM
    tb = rw_ref.shape[0]

    br = lax.broadcasted_iota(jnp.int32, (tb, tb), 0)
    bc = lax.broadcasted_iota(jnp.int32, (tb, tb), 1)
    tri = jnp.where((br >= bc) & (br // c_len == bc // c_len), 1.0, 0.0).astype(BF16)
    lw_all = lw_s[...]
    lw_hi = lw_all.astype(BF16)
    rem = lw_all - lw_hi.astype(F32)
    lw_mid = rem.astype(BF16)
    lw_lo = (rem - lw_mid.astype(F32)).astype(BF16)
    cum_s[...] = (jnp.dot(tri, lw_hi, preferred_element_type=F32)
                  + jnp.dot(tri, lw_mid, preferred_element_type=F32)
                  + jnp.dot(tri, lw_lo, preferred_element_type=F32))

    row = lax.broadcasted_iota(jnp.int32, (c_len, 2 * c_len), 0)
    col = lax.broadcasted_iota(jnp.int32, (c_len, 2 * c_len), 1)
    incl2 = row >= col % c_len
    strict2 = row > col % c_len
    eye = jnp.where(lax.broadcasted_iota(jnp.int32, (c_len, c_len), 0)
                    == lax.broadcasted_iota(jnp.int32, (c_len, c_len), 1), 1.0, 0.0).astype(F32)

    def chunk(ci, carry):
        rows = pl.ds(pl.multiple_of(ci * c_len, c_len), c_len)
        for h in range(RW_HEADS):
            cols = slice(h * n, (h + 1) * n)
            r = r_s[rows, cols]
            kh = k_s[rows, cols]
            v = v_s[rows, cols]
            lw = lw_s[rows, cols]
            cum = cum_s[rows, cols]
            kk = kk_s[rows, cols]
            kk = kk / jnp.maximum(jnp.sqrt(jnp.sum(kk * kk, axis=-1, keepdims=True)), 1e-12)
            kka = kk * a_s[rows, cols]
            end = jnp.sum(lw, axis=0, keepdims=True)
            e_neg = jnp.exp(-cum)
            e_end = jnp.exp(end - cum)
            rt = r * jnp.exp(cum)
            at = -kk * jnp.exp(cum - lw)
            left =jnp.concatenate([at, rt], axis=0)
            g = _bdot_nt(left, jnp.concatenate([kka * e_neg, kh * e_neg], axis=0))
            a_a = jnp.where(strict2, g[:c_len, :], 0.0)
            a_r = jnp.where(incl2, g[c_len:, :], 0.0)
            a_ab = a_a[:, :c_len]
            inv = eye + a_ab
            pw = a_ab
            for _ in range(5):
                pw = _bdot(pw, pw)
                inv = inv + _bdot(inv, pw)
            s0 = state_ref[h]
            ls = _bdot_nt(left, s0)
            u = _bdot(inv, ls[:c_len, :]
                      + _bdot(a_a[:, c_len:], v))
            uv = jnp.concatenate([u, v], axis=0)
            y = ls[c_len:, :] + _bdot(a_r, uv)
            state_ref[h] = s0 * jnp.exp(end) + _bdot_tn(
                uv, jnp.concatenate([kka * e_end, kh * e_end], axis=0))

            mean = jnp.mean(y, axis=-1, keepdims=True)
            yc = y - mean
            var = jnp.mean(yc * yc, axis=-1, keepdims=True)
            yn = yc * lax.rsqrt(var + LN_X_EPS) * lnw_ref[:, cols] + lnb_ref[:, cols]
            bonus = jnp.sum(r * kh * rk_ref[:, cols], axis=-1, keepdims=True) * v
            o_ref[rows, cols] = ((yn + bonus) * g_s[rows, cols]).astype(o_ref.dtype)
        return carry

    lax.fori_loop(0, rw_ref.shape[0] // c_len, chunk, 0)


def _rwkv(rw, w0, w2, a0, a2, g2, k_k, k_a, r_k, ln_w, ln_b):
    b, t, _ = rw.shape
    tb = min(RW_BLOCK, t)
    w = RW_WIDTH
    vec = pl.BlockSpec((1, w), lambda bi, ti: (0, 0))
    mat = lambda r: pl.BlockSpec((r, w), lambda bi, ti: (0, 0))
    return pl.pallas_call(
        _rwkv_kernel,
        grid=(b, t // tb),
        in_specs=[pl.BlockSpec((None, tb, RW_COLS), lambda bi, ti: (bi, ti, 0)),
                  vec, mat(DECAY_LORA), vec, mat(AAA_LORA), mat(GATE_LORA), vec, vec, vec, vec, vec],
        out_specs=pl.BlockSpec((None, tb, w), lambda bi, ti: (bi, ti, 0)),
        out_shape=jax.ShapeDtypeStruct((b, t, w), BF16),
        scratch_shapes=[pltpu.VMEM((RW_HEADS, RW_HEAD_DIM, RW_HEAD_DIM), F32)]
        + [pltpu.VMEM((tb, w), F32)] * 8,
        compiler_params=_params("parallel", "arbitrary"),
        name="rwkv",
    )(rw, w0, w2, a0, a2, g2, k_k, k_a, r_k.reshape(1, w), ln_w, ln_b)


def _pack_rows(x):
    half = x.shape[1] // 2
    hi = pltpu.bitcast(x[:, :half].astype(BF16).astype(F32), jnp.uint32)
    lo = pltpu.bitcast(x[:, half:].astype(BF16).astype(F32), jnp.uint32)
    return hi | (lo >> 16)


def _unpack_rows(u):
    hi = pltpu.bitcast(u & jnp.uint32(0xFFFF0000), F32)
    lo = pltpu.bitcast(u << 16, F32)
    return jnp.concatenate([hi, lo], axis=1)


def _out_kernel(yda_ref, yrw_ref, x_ref, mod_ref, wo_ref, pmn_ref, pfn_ref, rw_ref, rb_ref,
                x1_ref, h2_ref, idx_ref, wgt_ref):
    y = (jnp.dot(yda_ref[...], wo_ref[0:DA_WIDTH, :], preferred_element_type=F32)
         + jnp.dot(yrw_ref[...], wo_ref[DA_WIDTH:, :], preferred_element_type=F32))
    x1 = x_ref[...] + mod_ref[2:3, :] * _rms(y, pmn_ref[...], NORM_EPS)
    x1_ref[...] = x1
    h2 = _rms(x1, pfn_ref[...], NORM_EPS) * (1.0 + mod_ref[4:5, :]) + mod_ref[3:4, :]
    h2_ref[...] = _pack_rows(h2)

    h_hi = h2.astype(BF16)
    h_lo = (h2 - h_hi.astype(F32)).astype(BF16)
    rw = rw_ref[...]
    w_hi = rw.astype(BF16)
    w_lo = (rw - w_hi.astype(F32)).astype(BF16)
    logits = (jnp.dot(h_hi, w_hi, preferred_element_type=F32)
              + jnp.dot(h_hi, w_lo, preferred_element_type=F32)
              + jnp.dot(h_lo, w_hi, preferred_element_type=F32)) + rb_ref[...]

    lane = lax.broadcasted_iota(jnp.int32, logits.shape, 1)
    slot = lax.broadcasted_iota(jnp.int32, idx_ref.shape, 1)
    idx = jnp.zeros(idx_ref.shape, jnp.int32)
    val = jnp.zeros(idx_ref.shape, F32)
    top = None
    for j in range(TOP_K):
        m = jnp.max(logits, axis=-1, keepdims=True)
        i = jnp.min(jnp.where(logits == m, lane, N_EXPERTS), axis=-1, keepdims=True)
        top = m if top is None else top
        idx = jnp.where(slot == j, i, idx)
        val = jnp.where(slot == j, jnp.exp(m - top), val)
        logits = jnp.where(lane == i, -jnp.inf, logits)
    idx_ref[...] = idx
    wgt_ref[...] = val / jnp.sum(val, axis=-1, keepdims=True)


def _out(y_da, y_rw, x, mod3, w_out_b, post_mix_norm, pre_ffn_norm, router_w, router_b):
    b, t, d = x.shape
    tm = min(OUT_ROWS, t)
    e = router_w.shape[1]
    blk = lambda w: pl.BlockSpec((None, tm, w), lambda bi, ti: (bi, ti, 0))
    full = lambda r, c: pl.BlockSpec((r, c), lambda bi, ti: (0, 0))
    return pl.pallas_call(
        _out_kernel,
        grid=(b, t // tm),
        in_specs=[blk(DA_WIDTH), blk(RW_WIDTH), blk(d),
                  pl.BlockSpec((None, N_MOD, d), lambda bi, ti: (bi, 0, 0)),
                  full(d, d), full(1, d), full(1, d), full(d, e), full(1, e)],
        out_specs=[blk(d), blk(d // 2), blk(TOP_K), blk(TOP_K)],
        out_shape=[jax.ShapeDtypeStruct((b, t, d), F32),
                   jax.ShapeDtypeStruct((b, t, d // 2), jnp.uint32),
                   jax.ShapeDtypeStruct((b, t, TOP_K), jnp.int32),
                   jax.ShapeDtypeStruct((b, t, TOP_K), F32)],
        compiler_params=_params("parallel", "parallel"),
        name="out",
    )(y_da, y_rw, x, mod3, w_out_b, post_mix_norm, pre_ffn_norm, router_w,
      router_b.reshape(1, e))


def _route(top_idx, rows_per_tile, n_tiles):
    e_flat = top_idx.reshape(-1)
    onehot = (e_flat[:, None] == jnp.arange(N_EXPERTS, dtype=jnp.int32)[None, :]).astype(jnp.int32)
    csum = jnp.cumsum(onehot, axis=0)
    counts = csum[-1]
    padded = (counts + rows_per_tile - 1) // rows_per_tile * rows_per_tile
    ends = jnp.cumsum(padded)
    starts = ends - padded
    pos = jnp.sum((csum - onehot + starts[None, :]) * onehot, axis=1)
    n_active = ends[-1] // rows_per_tile
    tile_start = jnp.arange(n_tiles, dtype=jnp.int32) * rows_per_tile
    tile = jnp.minimum(tile_start, ends[-1] - 1)
    tile_expert = jnp.sum((tile[:, None] >= ends[None, :]).astype(jnp.int32), axis=1)
    return pos.astype(jnp.int32), tile_expert.astype(jnp.int32), n_active.reshape(1).astype(jnp.int32)


def _dispatch_kernel(pos_ref, h_ref, xs_in_ref, xs_ref, sem):
    del xs_in_ref
    n = pos_ref.shape[1]

    def row_copy(j):
        return pltpu.make_async_copy(h_ref.at[pl.ds(j // TOP_K, 1)],
                                     xs_ref.at[pl.ds(pos_ref[0, j], 1)], sem)

    def issue(j, carry):
        row_copy(j).start()
        return carry

    def drain(j, carry):
        row_copy(j).wait()
        return carry

    lax.fori_loop(0, n, issue, 0, unroll=8)
    lax.fori_loop(0, n, drain, 0, unroll=8)


def _dispatch(pos, h2p, n_rows):
    n, w = h2p.shape
    tb = min(DISPATCH_TOKENS, n)
    pos3 = pos.reshape(n // tb, 1, tb * TOP_K)
    xs0 = jnp.zeros((n_rows, w), h2p.dtype)
    return pl.pallas_call(
        _dispatch_kernel,
        grid=(n // tb,),
        in_specs=[pl.BlockSpec((None, 1, tb * TOP_K), lambda i: (i, 0, 0),
                               memory_space=pltpu.SMEM),
                  pl.BlockSpec((tb, w), lambda i: (i, 0)),
                  pl.BlockSpec(memory_space=pl.ANY)],
        out_specs=pl.BlockSpec(memory_space=pl.ANY),
        out_shape=jax.ShapeDtypeStruct((n_rows, w), h2p.dtype),
        scratch_shapes=[pltpu.SemaphoreType.DMA(())],
        input_output_aliases={2: 0},
        compiler_params=_params("arbitrary"),
        name="dispatch",
    )(pos3, h2p, xs0)


def _expert_kernel(te_ref, na_ref, xs_ref, w1_ref, b1_ref, w2_ref, b2_ref, ys_ref,
                   w1p_s, b1p_s, w2b_s, act_s):
    i = pl.program_id(0)
    active = i < na_ref[0]
    fresh = jnp.logical_or(i == 0, te_ref[i] != te_ref[jnp.maximum(i - 1, 0)])
    grp = 2 * LANES
    n_grp = w1_ref.shape[1] // grp

    @pl.when(jnp.logical_and(active, fresh))
    def _():
        src = lax.broadcasted_iota(jnp.int32, (grp, grp), 0)
        dst = lax.broadcasted_iota(jnp.int32, (grp, grp), 1)
        perm = jnp.where(src == jnp.where(dst < LANES, 2 * dst, 2 * (dst - LANES) + 1),
                         1.0, 0.0).astype(BF16)
        for g in range(n_grp):
            sl = slice(g * grp, (g + 1) * grp)
            w1p_s[:, sl] = jnp.dot(w1_ref[:, sl].astype(BF16), perm,
                                   preferred_element_type=F32).astype(BF16)
            b = b1_ref[:, sl]
            b_hi = b.astype(BF16)
            b_lo = (b - b_hi.astype(F32)).astype(BF16)
            b1p_s[:, sl] = (jnp.dot(b_hi, perm, preferred_element_type=F32)
                            + jnp.dot(b_lo, perm, preferred_element_type=F32))
        w2b_s[...] = w2_ref[...].astype(BF16)

    @pl.when(active)
    def _():
        x = _unpack_rows(xs_ref[...]).astype(BF16)
        hid = jnp.dot(x, w1p_s[...], preferred_element_type=F32) + b1p_s[0:1, :]
        for g in range(n_grp):
            glu = jnp.minimum(hid[:, g * grp:g * grp + LANES], SWIGLU_LIMIT)
            lin = jnp.clip(hid[:, g * grp + LANES:(g + 1) * grp], -SWIGLU_LIMIT, SWIGLU_LIMIT)
            act_s[:, g * LANES:(g + 1) * LANES] = (
                glu * jax.nn.sigmoid(SWIGLU_ALPHA * glu) * (lin + 1.0)).astype(BF16)
        y = jnp.dot(act_s[...], w2b_s[...], preferred_element_type=F32) + b2_ref[...]
        ys_ref[...] = _pack_rows(y)

    @pl.when(jnp.logical_not(active))
    def _():
        ys_ref[...] = jnp.zeros_like(ys_ref)


def _experts(tile_expert, n_active, xs, w1, b1, w2, b2):
    n_rows, w = xs.shape
    tm = EXPERT_ROWS
    d, f2 = w1.shape[1], w1.shape[2]
    f = f2 // 2
    wspec = lambda r, c: pl.BlockSpec((None, r, c), lambda i, te, na: (te[i], 0, 0))
    return pl.pallas_call(
        _expert_kernel,
        grid_spec=pltpu.PrefetchScalarGridSpec(
            num_scalar_prefetch=2,
            grid=(n_rows // tm,),
            in_specs=[pl.BlockSpec((tm, w), lambda i, te, na: (i, 0)),
                      wspec(d, f2), wspec(8, f2), wspec(f, d), wspec(1, d)],
            out_specs=pl.BlockSpec((tm, w), lambda i, te, na: (i, 0)),
            scratch_shapes=[pltpu.VMEM((d, f2), BF16), pltpu.VMEM((8, f2), F32),
                            pltpu.VMEM((f, d), BF16), pltpu.VMEM((tm, f), BF16)]),
        out_shape=jax.ShapeDtypeStruct((n_rows, w), jnp.uint32),
        compiler_params=_params("arbitrary"),
        name="expert",
    )(tile_expert, n_active, xs, w1, b1, w2, b2)


def _combine_kernel(pos_ref, ys_ref, wgt_ref, x1_ref, mod_ref, nw_ref, o_ref, buf_ref, sem):
    tc = x1_ref.shape[0]
    n = pos_ref.shape[1]

    def row_copy(j):
        dst = (j % TOP_K) * tc + j // TOP_K
        return pltpu.make_async_copy(ys_ref.at[pl.ds(pos_ref[0, j], 1)],
                                     buf_ref.at[pl.ds(dst, 1)], sem)

    def issue(j, carry):
        row_copy(j).start()
        return carry

    def drain(j, carry):
        row_copy(j).wait()
        return carry

    lax.fori_loop(0, n, issue, 0, unroll=8)
    lax.fori_loop(0, n, drain, 0, unroll=8)

    wgt = wgt_ref[...]
    acc = jnp.zeros(x1_ref.shape, F32)
    for j in range(TOP_K):
        rows = _unpack_rows(buf_ref[j * tc:(j + 1) * tc, :])
        acc = acc + wgt[:, j:j + 1] * rows
    o_ref[...] = x1_ref[...] + mod_ref[5:6, :] * _rms(acc, nw_ref[...], NORM_EPS)


def _combine(pos, ys, wgt, x1, mod3, post_ffn_norm):
    b, t, d = x1.shape
    tc = min(COMBINE_TOKENS, t)
    nt = t // tc
    pos3 = pos.reshape(b * nt, 1, tc * TOP_K)
    blk = lambda w: pl.BlockSpec((None, tc, w), lambda bi, ti: (bi, ti, 0))
    return pl.pallas_call(
        _combine_kernel,
        grid=(b, nt),
        in_specs=[pl.BlockSpec((None, 1, tc * TOP_K), lambda bi, ti: (bi * nt + ti, 0, 0),
                               memory_space=pltpu.SMEM),
                  pl.BlockSpec(memory_space=pl.ANY),
                  blk(TOP_K), blk(d),
                  pl.BlockSpec((None, N_MOD, d), lambda bi, ti: (bi, 0, 0)),
                  pl.BlockSpec((1, d), lambda bi, ti: (0, 0))],
        out_specs=blk(d),
        out_shape=jax.ShapeDtypeStruct((b, t, d), F32),
        scratch_shapes=[pltpu.VMEM((tc * TOP_K, d // 2), jnp.uint32),
                        pltpu.SemaphoreType.DMA(())],
        compiler_params=_params("arbitrary", "arbitrary"),
        name="combine",
    )(pos3, ys, wgt, x1, mod3, post_ffn_norm)


def _stages(x, c, positions, ada_w, ada_b, pre_mix_norm, post_mix_norm, pre_ffn_norm,
            post_ffn_norm, w_in, w_out, da_lambda_q1, da_lambda_k1, da_lambda_q2, da_lambda_k2,
            da_subln, rw_mu, rw_w0, rw_w2, rw_a0, rw_a2, rw_g2, rw_k_k, rw_k_a, rw_r_k, rw_ln_w,
            rw_ln_b, router_w, router_b, moe_w1, moe_b1, moe_w2, moe_b2):
    b, t, d = x.shape
    res = {}
    lambda_init = 0.8 - 0.6 * math.exp(-0.3 * 0)
    mod = _mod(c, ada_w[0], ada_b[0])
    res["mod"] = mod
    mod3 = mod.reshape(b, N_MOD, d)
    inv_freq = ROPE_THETA ** (-jnp.arange(0, ROPE_DIM, 2, dtype=F32) / ROPE_DIM)
    invf = jnp.tile(inv_freq, LANES // (ROPE_DIM // 2)).reshape(1, LANES)
    q, k, v, rw = _proj(x, positions.reshape(b, t, 1), mod3, pre_mix_norm, invf,
                        w_in[0].astype(BF16), rw_mu)
    res.update(q=q, k=k, v=v, rw=rw)
    lam4 = jnp.concatenate([da_lambda_q1, da_lambda_k1, da_lambda_q2, da_lambda_k2], axis=0)
    y_da = _attn(q, k, v, lam4, da_subln, lambda_init)
    res["y_da"] = y_da
    y_rw = _rwkv(rw, rw_w0, rw_w2[0], rw_a0, rw_a2[0], rw_g2[0], rw_k_k, rw_k_a, rw_r_k[0],
                 rw_ln_w, rw_ln_b)
    res["y_rw"] = y_rw
    x1, h2p, top_idx, top_w = _out(y_da, y_rw, x, mod3, w_out[0].astype(BF16), post_mix_norm,
                                   pre_ffn_norm, router_w[0], router_b[0])
    res.update(x1=x1, top_idx=top_idx, top_w=top_w)
    n = b * t
    n_tiles = n * TOP_K // EXPERT_ROWS + N_EXPERTS
    pos, tile_expert, n_active = _route(top_idx.reshape(n, TOP_K), EXPERT_ROWS, n_tiles)
    xs = _dispatch(pos, h2p.reshape(n, d // 2), n_tiles * EXPERT_ROWS)
    b1 = jnp.broadcast_to(moe_b1[0][:, None, :], (N_EXPERTS, 8, moe_b1.shape[-1]))
    ys = _experts(tile_expert, n_active, xs, moe_w1[0], b1, moe_w2[0], moe_b2[0][:, None, :])
    res["final"] = _combine(pos, ys, top_w, x1, mod3, post_ffn_norm)
    return res


stages = _stages


def kernel(x, c, positions, ada_w, ada_b, pre_mix_norm, post_mix_norm, pre_ffn_norm, post_ffn_norm, w_in, w_out, da_lambda_q1, da_lambda_k1, da_lambda_q2, da_lambda_k2, da_subln, rw_mu, rw_w0, rw_w2, rw_a0, rw_a2, rw_g2, rw_k_k, rw_k_a, rw_r_k, rw_ln_w, rw_ln_b, router_w, router_b, moe_w1, moe_b1, moe_w2, moe_b2):
    res = _stages(x, c, positions, ada_w, ada_b, pre_mix_norm, post_mix_norm, pre_ffn_norm,
                  post_ffn_norm, w_in, w_out, da_lambda_q1, da_lambda_k1, da_lambda_q2,
                  da_lambda_k2, da_subln, rw_mu, rw_w0, rw_w2, rw_a0, rw_a2, rw_g2, rw_k_k,
                  rw_k_a, rw_r_k, rw_ln_w, rw_ln_b, router_w, router_b, moe_w1, moe_b1,
                  moe_w2, moe_b2)
    return res["final"]
```

```python
import functools
import math

import jax
import jax.numpy as jnp
from jax import lax
from jax.experimental import pallas as pl
from jax.experimental.pallas import tpu as pltpu

F32 = jnp.float32
BF16 = jnp.bfloat16
HIGHEST = lax.Precision.HIGHEST

DA_HEADS = 4
DA_HEAD_DIM = 64
DA_V_DIM = 128
DA_WIDTH = 512
RW_HEADS = 8
RW_HEAD_DIM = 64
RW_WIDTH = 512
DECAY_LORA = 64
AAA_LORA = 64
GATE_LORA = 128
DA_COLS = 1536
RW_COLS = 1792
ROPE_THETA = 500000.0
ROPE_DIM = 16
N_EXPERTS = 32
TOP_K = 4
SWIGLU_ALPHA = 1.702
SWIGLU_LIMIT = 7.0
NORM_EPS = 1e-6
SUBLN_EPS = 1e-5
LN_X_EPS = 64e-5
N_MOD = 6

LANES = 128
VMEM_LIMIT_BYTES = 56 * 1024 * 1024

PROJ_ROWS = 512
ATTN_BLOCK = 256
ATTN_KV_BLOCK = 512
RW_CHUNK = 64
RW_BLOCK = 256
OUT_ROWS = 512
EXPERT_ROWS = 512
DISPATCH_TOKENS = 512
COMBINE_TOKENS = 256


def _params(*sem):
    return pltpu.CompilerParams(dimension_semantics=sem, vmem_limit_bytes=VMEM_LIMIT_BYTES)


def _bdot(a, b):
    return jnp.dot(a.astype(BF16), b.astype(BF16), preferred_element_type=F32)


def _bdot_nt(a, b):
    return lax.dot_general(a.astype(BF16), b.astype(BF16), (((1,), (1,)), ((), ())),
                           preferred_element_type=F32)


def _bdot_tn(a, b):
    return lax.dot_general(a.astype(BF16), b.astype(BF16), (((0,), (0,)), ((), ())),
                           preferred_element_type=F32)


def _rms(x, w, eps):
    return x * lax.rsqrt(jnp.mean(x * x, axis=-1, keepdims=True) + eps) * w


def _mod_kernel(c_ref, w_ref, b_ref, o_ref):
    c = c_ref[...]
    s = c * jax.nn.sigmoid(c)
    o_ref[...] = _bdot(s, w_ref[...]) + b_ref[...]


def _mod(c, ada_w, ada_b):
    b, d = c.shape
    n = ada_w.shape[1]
    return pl.pallas_call(
        _mod_kernel,
        grid=(n // d,),
        in_specs=[pl.BlockSpec((b, d), lambda j: (0, 0)),
                  pl.BlockSpec((d, d), lambda j: (0, j)),
                  pl.BlockSpec((1, d), lambda j: (0, j))],
        out_specs=pl.BlockSpec((b, d), lambda j: (0, j)),
        out_shape=jax.ShapeDtypeStruct((b, n), F32),
        compiler_params=_params("parallel"),
        name="mod",
    )(c, ada_w, ada_b.reshape(1, n))


def _proj_kernel(x_ref, pos_ref, mod_ref, nw_ref, invf_ref, w_ref, mu_ref,
                 q_ref, k_ref, v_ref, rw_ref, carry_ref):
    ti = pl.program_id(1)

    @pl.when(ti == 0)
    def _():
        carry_ref[...] = jnp.zeros_like(carry_ref)

    x = x_ref[...]
    h = _rms(x, nw_ref[...], NORM_EPS) * (1.0 + mod_ref[1:2, :]) + mod_ref[0:1, :]
    hb = h.astype(BF16)

    ang = pos_ref[...].astype(F32) * invf_ref[...]
    cos, sin = jnp.cos(ang), jnp.sin(ang)
    l64 = lax.broadcasted_iota(jnp.int32, ang.shape, 1) % DA_HEAD_DIM
    half = ROPE_DIM // 2
    c_tab = jnp.where(l64 < ROPE_DIM, cos, 1.0)
    s_lo = jnp.where(l64 < half, -sin, 0.0)
    s_hi = jnp.where((l64 >= half) & (l64 < ROPE_DIM), sin, 0.0)

    def rope(z):
        up = pltpu.roll(z, LANES - half, axis=1)
        dn = pltpu.roll(z, half, axis=1)
        return z * c_tab + up * s_lo + dn * s_hi

    for g in range(DA_WIDTH // LANES):
        sl = slice(g * LANES, (g + 1) * LANES)
        qg = jnp.dot(hb, w_ref[:, sl], preferred_element_type=F32)
        q_ref[:, sl] = (rope(qg) * (DA_HEAD_DIM ** -0.5)).astype(q_ref.dtype)
        kg = jnp.dot(hb, w_ref[:, DA_WIDTH + g * LANES:DA_WIDTH + (g + 1) * LANES],
                     preferred_element_type=F32)
        k_ref[:, sl] = rope(kg).astype(k_ref.dtype)
    v_ref[...] = jnp.dot(hb, w_ref[:, 2 * DA_WIDTH:DA_COLS],
                         preferred_element_type=F32).astype(v_ref.dtype)

    p = jnp.dot(hb, w_ref[:, DA_COLS:], preferred_element_type=F32)
    rows = p.shape[0]
    prev = pltpu.roll(p, 1, axis=0)
    first = lax.broadcasted_iota(jnp.int32, p.shape, 0) == 0
    prev = jnp.where(first, carry_ref[0:1, :], prev)
    rw_ref[...] = p + (prev - p) * mu_ref[...]
    carry_ref[0:1, :] = p[rows - 1:rows, :]


def _proj(x, pos3, mod3, norm_w, invf, w_in_b, mu):
    b, t, d = x.shape
    tm = min(PROJ_ROWS, t)
    n_in = w_in_b.shape[1]
    blk = lambda w: pl.BlockSpec((None, tm, w), lambda bi, ti: (bi, ti, 0))
    full = lambda r, c: pl.BlockSpec((r, c), lambda bi, ti: (0, 0))
    return pl.pallas_call(
        _proj_kernel,
        grid=(b, t // tm),
        in_specs=[blk(d), blk(1),
                  pl.BlockSpec((None, N_MOD, d), lambda bi, ti: (bi, 0, 0)),
                  full(1, d), full(1, LANES), full(d, n_in), full(1, RW_COLS)],
        out_specs=[blk(DA_WIDTH), blk(DA_WIDTH), blk(DA_WIDTH), blk(RW_COLS)],
        out_shape=[jax.ShapeDtypeStruct((b, t, DA_WIDTH), BF16)] * 3
        + [jax.ShapeDtypeStruct((b, t, RW_COLS), F32)],
        scratch_shapes=[pltpu.VMEM((8, RW_COLS), F32)],
        compiler_params=_params("parallel", "arbitrary"),
        name="proj",
    )(x, pos3, mod3, norm_w, invf, w_in_b, mu)


def _attn_kernel(q_ref, k_ref, v_ref, lam_ref, subln_ref, o_ref, m_ref, l_ref, acc_ref,
                 *, lambda_init):
    qi = pl.program_id(2)
    tq = q_ref.shape[0]
    q = q_ref[...]
    lane = lax.broadcasted_iota(jnp.int32, q.shape, 1)
    zero = jnp.zeros_like(q)
    qq = jnp.concatenate([jnp.where(lane < DA_HEAD_DIM, q, zero),
                          jnp.where(lane >= DA_HEAD_DIM, q, zero)], axis=0)

    m_ref[...] = jnp.full(m_ref.shape, -jnp.inf, F32)
    l_ref[...] = jnp.zeros(l_ref.shape, F32)
    acc_ref[...] = jnp.zeros(acc_ref.shape, F32)
    tk = ATTN_KV_BLOCK if k_ref.shape[0] % ATTN_KV_BLOCK == 0 else tq
    rep = tk // LANES

    def step(j, masked):
        kb = k_ref[pl.ds(pl.multiple_of(j * tk, tk), tk), :]
        vb = v_ref[pl.ds(pl.multiple_of(j * tk, tk), tk), :]
        s = lax.dot_general(qq, kb, (((1,), (1,)), ((), ())), preferred_element_type=F32)
        if masked:
            qpos = qi * tq + lax.broadcasted_iota(jnp.int32, s.shape, 0) % tq
            kpos = j * tk + lax.broadcasted_iota(jnp.int32, s.shape, 1)
            s = jnp.where(qpos >= kpos, s, -jnp.inf)
        m_old = m_ref[...]
        m_new = jnp.maximum(m_old, jnp.max(s, axis=-1, keepdims=True))
        alpha = jnp.exp(m_old - m_new)
        p = jnp.exp(s - jnp.concatenate([m_new] * rep, axis=1))
        l_ref[...] = alpha * l_ref[...] + jnp.sum(p, axis=-1, keepdims=True)
        acc_ref[...] = alpha * acc_ref[...] + jnp.dot(p.astype(vb.dtype), vb,
                                                      preferred_element_type=F32)
        m_ref[...] = m_new

    def body(j, carry):
        step(j, False)
        return carry

    n_full = (qi * tq) // tk
    lax.fori_loop(0, n_full, body, 0)
    step(n_full, True)

    o = acc_ref[...] / l_ref[...]
    lam = (jnp.exp(jnp.sum(lam_ref[0:1, :] * lam_ref[1:2, :], axis=-1, keepdims=True))
           - jnp.exp(jnp.sum(lam_ref[2:3, :] * lam_ref[3:4, :], axis=-1, keepdims=True))
           + lambda_init)
    d = o[:tq, :] - lam * o[tq:, :]
    o_ref[...] = (_rms(d, subln_ref[...], SUBLN_EPS) * (1.0 - lambda_init)).astype(o_ref.dtype)


def _attn(q, k, v, lam4, subln, lambda_init):
    b, t, _ = q.shape
    tq = min(ATTN_BLOCK, t)
    return pl.pallas_call(
        functools.partial(_attn_kernel, lambda_init=lambda_init),
        grid=(b, DA_HEADS, t // tq),
        in_specs=[pl.BlockSpec((None, tq, DA_V_DIM), lambda bi, h, qi: (bi, qi, h)),
                  pl.BlockSpec((None, t, DA_V_DIM), lambda bi, h, qi: (bi, 0, h)),
                  pl.BlockSpec((None, t, DA_V_DIM), lambda bi, h, qi: (bi, 0, h)),
                  pl.BlockSpec((4, DA_HEAD_DIM), lambda bi, h, qi: (0, 0)),
                  pl.BlockSpec((1, DA_V_DIM), lambda bi, h, qi: (0, 0))],
        out_specs=pl.BlockSpec((None, tq, DA_V_DIM), lambda bi, h, qi: (bi, qi, h)),
        out_shape=jax.ShapeDtypeStruct((b, t, DA_WIDTH), BF16),
        scratch_shapes=[pltpu.VMEM((2 * tq, LANES), F32), pltpu.VMEM((2 * tq, LANES), F32),
                        pltpu.VMEM((2 * tq, DA_V_DIM), F32)],
        compiler_params=_params("parallel", "parallel", "arbitrary"),
        name="attn",
    )(q, k, v, lam4, subln)


def _rwkv_kernel(rw_ref, w0_ref, w2_ref, a0_ref, a2_ref, g2_ref, kk_ref, ka_ref, rk_ref,
                 lnw_ref, lnb_ref, o_ref, state_ref, r_s, k_s, v_s, lw_s, kk_s, a_s, g_s, cum_s):
    ti = pl.program_id(1)

    @pl.when(ti == 0)
    def _():
        state_ref[...] = jnp.zeros_like(state_ref)

    w = RW_WIDTH
    rw = rw_ref[...]
    k = rw[:, w:2 * w]
    wl = rw[:, 3 * w:3 * w + DECAY_LORA]
    al = rw[:, 3 * w + DECAY_LORA:3 * w + DECAY_LORA + AAA_LORA]
    gl = rw[:, 3 * w + DECAY_LORA + AAA_LORA:]
    z = -(w0_ref[...] + _bdot(jnp.tanh(wl), w2_ref[...]))
    softplus = jnp.maximum(z, 0.0) + jnp.log(1.0 + jnp.exp(-jnp.abs(z)))
    a = jax.nn.sigmoid(a0_ref[...] + _bdot(al, a2_ref[...]))
    r_s[...] = rw[:, 0:w]
    v_s[...] = rw[:, 2 * w:3 * w]
    lw_s[...] = -jnp.exp(-softplus - 0.5)
    a_s[...] = a
    g_s[...] = _bdot(jax.nn.sigmoid(gl), g2_ref[...])
    kk_s[...] = k * kk_ref[...]
    k_s[...] = k * (1.0 + (a - 1.0) * ka_ref[...])

    c_len = RW_CHUNK
    n = RW_HEAD_DIM
    tb = rw_ref.shape[0]

    br = lax.broadcasted_iota(jnp.int32, (tb, tb), 0)
    bc = lax.broadcasted_iota(jnp.int32, (tb, tb), 1)
    tri = jnp.where((br >= bc) & (br // c_len == bc // c_len), 1.0, 0.0).astype(BF16)
    lw_all = lw_s[...]
    lw_hi = lw_all.astype(BF16)
    rem = lw_all - lw_hi.astype(F32)
    lw_mid = rem.astype(BF16)
    lw_lo = (rem - lw_mid.astype(F32)).astype(BF16)
    cum_s[...] = (jnp.dot(tri, lw_hi, preferred_element_type=F32)
                  + jnp.dot(tri, lw_mid, preferred_element_type=F32)
                  + jnp.dot(tri, lw_lo, preferred_element_type=F32))

    row = lax.broadcasted_iota(jnp.int32, (c_len, 2 * c_len), 0)
    col = lax.broadcasted_iota(jnp.int32, (c_len, 2 * c_len), 1)
    incl2 = row >= col % c_len
    strict2 = row > col % c_len
    eye = jnp.where(lax.broadcasted_iota(jnp.int32, (c_len, c_len), 0)
                    == lax.broadcasted_iota(jnp.int32, (c_len, c_len), 1), 1.0, 0.0).astype(F32)

    def chunk(ci, carry):
        rows = pl.ds(pl.multiple_of(ci * c_len, c_len), c_len)
        heads = range(RW_HEADS)
        sl = [slice(h * n, (h + 1) * n) for h in heads]
        r = [r_s[rows, c] for c in sl]
        kh = [k_s[rows, c] for c in sl]
        v = [v_s[rows, c] for c in sl]
        lw = [lw_s[rows, c] for c in sl]
        cum = [cum_s[rows, c] for c in sl]
        kk = [kk_s[rows, c] for c in sl]
        kk = [x / jnp.maximum(jnp.sqrt(jnp.sum(x * x, axis=-1, keepdims=True)), 1e-12) for x in kk]
        kka = [kk[h] * a_s[rows, sl[h]] for h in heads]
        end = [jnp.sum(x, axis=0, keepdims=True) for x in lw]
        e_neg = [jnp.exp(-x) for x in cum]
        e_end = [jnp.exp(end[h] - cum[h]) for h in heads]
        left = [jnp.concatenate([-kk[h] * jnp.exp(cum[h] - lw[h]), r[h] * jnp.exp(cum[h])], axis=0)
                for h in heads]
        g = [_bdot_nt(left[h], jnp.concatenate([kka[h] * e_neg[h], kh[h] * e_neg[h]], axis=0))
             for h in heads]
        a_a = [jnp.where(strict2, x[:c_len, :], 0.0) for x in g]
        a_r = [jnp.where(incl2, x[c_len:, :], 0.0) for x in g]
        pw = [x[:, :c_len] for x in a_a]
        inv = [eye + x for x in pw]
        for _ in range(5):
            pw = [_bdot(x, x) for x in pw]
            inv = [inv[h] + _bdot(inv[h], pw[h]) for h in heads]
        akv = [_bdot(a_a[h][:, c_len:], v[h]) for h in heads]
        s0 = [state_ref[h] for h in heads]
        ls = [_bdot_nt(left[h], s0[h]) for h in heads]
        u = [_bdot(inv[h], ls[h][:c_len, :] + akv[h]) for h in heads]
        uv = [jnp.concatenate([u[h], v[h]], axis=0) for h in heads]
        y = [ls[h][c_len:, :] + _bdot(a_r[h], uv[h]) for h in heads]
        for h in heads:
            state_ref[h] = s0[h] * jnp.exp(end[h]) + _bdot_tn(
                uv[h], jnp.concatenate([kka[h] * e_end[h], kh[h] * e_end[h]], axis=0))
        for h in heads:
            mean = jnp.mean(y[h], axis=-1, keepdims=True)
            yc = y[h] - mean
            var = jnp.mean(yc * yc, axis=-1, keepdims=True)
            yn = yc * lax.rsqrt(var + LN_X_EPS) * lnw_ref[:, sl[h]] + lnb_ref[:, sl[h]]
            bonus = jnp.sum(r[h] * kh[h] * rk_ref[:, sl[h]], axis=-1, keepdims=True) * v[h]
            o_ref[rows, sl[h]] = ((yn + bonus) * g_s[rows, sl[h]]).astype(o_ref.dtype)
        return carry

    lax.fori_loop(0, rw_ref.shape[0] // c_len, chunk, 0)


def _rwkv(rw, w0, w2, a0, a2, g2, k_k, k_a, r_k, ln_w, ln_b):
    b, t, _ = rw.shape
    tb = min(RW_BLOCK, t)
    w = RW_WIDTH
    vec = pl.BlockSpec((1, w), lambda bi, ti: (0, 0))
    mat = lambda r: pl.BlockSpec((r, w), lambda bi, ti: (0, 0))
    return pl.pallas_call(
        _rwkv_kernel,
        grid=(b, t // tb),
        in_specs=[pl.BlockSpec((None, tb, RW_COLS), lambda bi, ti: (bi, ti, 0)),
                  vec, mat(DECAY_LORA), vec, mat(AAA_LORA), mat(GATE_LORA), vec, vec, vec, vec, vec],
        out_specs=pl.BlockSpec((None, tb, w), lambda bi, ti: (bi, ti, 0)),
        out_shape=jax.ShapeDtypeStruct((b, t, w), BF16),
        scratch_shapes=[pltpu.VMEM((RW_HEADS, RW_HEAD_DIM, RW_HEAD_DIM), F32)]
        + [pltpu.VMEM((tb, w), F32)] * 8,
        compiler_params=_params("parallel", "arbitrary"),
        name="rwkv",
    )(rw, w0, w2, a0, a2, g2, k_k, k_a, r_k.reshape(1, w), ln_w, ln_b)


def _pack_rows(x):
    half = x.shape[1] // 2
    hi = pltpu.bitcast(x[:, :half].astype(BF16).astype(F32), jnp.uint32)
    lo = pltpu.bitcast(x[:, half:].astype(BF16).astype(F32), jnp.uint32)
    return hi | (lo >> 16)


def _unpack_rows(u):
    hi = pltpu.bitcast(u & jnp.uint32(0xFFFF0000), F32)
    lo = pltpu.bitcast(u << 16, F32)
    return jnp.concatenate([hi, lo], axis=1)


def _out_kernel(yda_ref, yrw_ref, x_ref, mod_ref, wo_ref, pmn_ref, pfn_ref, rw_ref, rb_ref,
                x1_ref, h2_ref, idx_ref, wgt_ref):
    y = (jnp.dot(yda_ref[...], wo_ref[0:DA_WIDTH, :], preferred_element_type=F32)
         + jnp.dot(yrw_ref[...], wo_ref[DA_WIDTH:, :], preferred_element_type=F32))
    x1 = x_ref[...] + mod_ref[2:3, :] * _rms(y, pmn_ref[...], NORM_EPS)
    x1_ref[...] = x1
    h2 = _rms(x1, pfn_ref[...], NORM_EPS) * (1.0 + mod_ref[4:5, :]) + mod_ref[3:4, :]
    h2_ref[...] = _pack_rows(h2)

    h_hi = h2.astype(BF16)
    h_lo = (h2 - h_hi.astype(F32)).astype(BF16)
    rw = rw_ref[...]
    w_hi = rw.astype(BF16)
    w_lo = (rw - w_hi.astype(F32)).astype(BF16)
    logits = (jnp.dot(h_hi, w_hi, preferred_element_type=F32)
              + jnp.dot(h_hi, w_lo, preferred_element_type=F32)
              + jnp.dot(h_lo, w_hi, preferred_element_type=F32)) + rb_ref[...]

    lane = lax.broadcasted_iota(jnp.int32, logits.shape, 1)
    slot = lax.broadcasted_iota(jnp.int32, idx_ref.shape, 1)
    idx = jnp.zeros(idx_ref.shape, jnp.int32)
    val = jnp.zeros(idx_ref.shape, F32)
    top = None
    for j in range(TOP_K):
        m = jnp.max(logits, axis=-1, keepdims=True)
        i = jnp.min(jnp.where(logits == m, lane, N_EXPERTS), axis=-1, keepdims=True)
        top = m if top is None else top
        idx = jnp.where(slot == j, i, idx)
        val = jnp.where(slot == j, jnp.exp(m - top), val)
        logits = jnp.where(lane == i, -jnp.inf, logits)
    idx_ref[...] = idx
    wgt_ref[...] = val / jnp.sum(val, axis=-1, keepdims=True)


def _out(y_da, y_rw, x, mod3, w_out_b, post_mix_norm, pre_ffn_norm, router_w, router_b):
    b, t, d = x.shape
    tm = min(OUT_ROWS, t)
    e = router_w.shape[1]
    blk = lambda w: pl.BlockSpec((None, tm, w), lambda bi, ti: (bi, ti, 0))
    full = lambda r, c: pl.BlockSpec((r, c), lambda bi, ti: (0, 0))
    return pl.pallas_call(
        _out_kernel,
        grid=(b, t // tm),
        in_specs=[blk(DA_WIDTH), blk(RW_WIDTH), blk(d),
                  pl.BlockSpec((None, N_MOD, d), lambda bi, ti: (bi, 0, 0)),
                  full(d, d), full(1, d), full(1, d), full(d, e), full(1, e)],
        out_specs=[blk(d), blk(d // 2), blk(TOP_K), blk(TOP_K)],
        out_shape=[jax.ShapeDtypeStruct((b, t, d), F32),
                   jax.ShapeDtypeStruct((b, t, d // 2), jnp.uint32),
                   jax.ShapeDtypeStruct((b, t, TOP_K), jnp.int32),
                   jax.ShapeDtypeStruct((b, t, TOP_K), F32)],
        compiler_params=_params("parallel", "parallel"),
        name="out",
    )(y_da, y_rw, x, mod3, w_out_b, post_mix_norm, pre_ffn_norm, router_w,
      router_b.reshape(1, e))


def _route(top_idx, rows_per_tile, n_tiles):
    e_flat = top_idx.reshape(-1)
    onehot = (e_flat[:, None] == jnp.arange(N_EXPERTS, dtype=jnp.int32)[None, :]).astype(jnp.int32)
    csum = jnp.cumsum(onehot, axis=0)
    counts = csum[-1]
    padded = (counts + rows_per_tile - 1) // rows_per_tile * rows_per_tile
    ends = jnp.cumsum(padded)
    starts = ends - padded
    pos = jnp.sum((csum - onehot + starts[None, :]) * onehot, axis=1)
    n_active = ends[-1] // rows_per_tile
    tile_start = jnp.arange(n_tiles, dtype=jnp.int32) * rows_per_tile
    tile = jnp.minimum(tile_start, ends[-1] - 1)
    tile_expert = jnp.sum((tile[:, None] >= ends[None, :]).astype(jnp.int32), axis=1)
    return pos.astype(jnp.int32), tile_expert.astype(jnp.int32), n_active.reshape(1).astype(jnp.int32)


def _dispatch_kernel(pos_ref, h_ref, xs_in_ref, xs_ref, sem):
    del xs_in_ref
    n = pos_ref.shape[1]

    def row_copy(j):
        return pltpu.make_async_copy(h_ref.at[pl.ds(j // TOP_K, 1)],
                                     xs_ref.at[pl.ds(pos_ref[0, j], 1)], sem)

    def issue(j, carry):
        row_copy(j).start()
        return carry

    def drain(j, carry):
        row_copy(j).wait()
        return carry

    lax.fori_loop(0, n, issue, 0, unroll=8)
    lax.fori_loop(0, n, drain, 0, unroll=8)


def _dispatch(pos, h2p, n_rows):
    n, w = h2p.shape
    tb = min(DISPATCH_TOKENS, n)
    pos3 = pos.reshape(n // tb, 1, tb * TOP_K)
    xs0 = jnp.zeros((n_rows, w), h2p.dtype)
    return pl.pallas_call(
        _dispatch_kernel,
        grid=(n // tb,),
        in_specs=[pl.BlockSpec((None, 1, tb * TOP_K), lambda i: (i, 0, 0),
                               memory_space=pltpu.SMEM),
                  pl.BlockSpec((tb, w), lambda i: (i, 0)),
                  pl.BlockSpec(memory_space=pl.ANY)],
        out_specs=pl.BlockSpec(memory_space=pl.ANY),
        out_shape=jax.ShapeDtypeStruct((n_rows, w), h2p.dtype),
        scratch_shapes=[pltpu.SemaphoreType.DMA(())],
        input_output_aliases={2: 0},
        compiler_params=_params("arbitrary"),
        name="dispatch",
    )(pos3, h2p, xs0)


def _expert_kernel(te_ref, na_ref, xs_ref, w1_ref, b1_ref, w2_ref, b2_ref, ys_ref,
                   w1p_s, b1p_s, w2b_s, act_s):
    i = pl.program_id(0)
    active = i < na_ref[0]
    fresh = jnp.logical_or(i == 0, te_ref[i] != te_ref[jnp.maximum(i - 1, 0)])
    grp = 2 * LANES
    n_grp = w1_ref.shape[1] // grp

    @pl.when(jnp.logical_and(active, fresh))
    def _():
        src = lax.broadcasted_iota(jnp.int32, (grp, grp), 0)
        dst = lax.broadcasted_iota(jnp.int32, (grp, grp), 1)
        perm = jnp.where(src == jnp.where(dst < LANES, 2 * dst, 2 * (dst - LANES) + 1),
                         1.0, 0.0).astype(BF16)
        for g in range(n_grp):
            sl = slice(g * grp, (g + 1) * grp)
            w1p_s[:, sl] = jnp.dot(w1_ref[:, sl].astype(BF16), perm,
                                   preferred_element_type=F32).astype(BF16)
            b = b1_ref[:, sl]
            b_hi = b.astype(BF16)
            b_lo = (b - b_hi.astype(F32)).astype(BF16)
            b1p_s[:, sl] = (jnp.dot(b_hi, perm, preferred_element_type=F32)
                            + jnp.dot(b_lo, perm, preferred_element_type=F32))
        w2b_s[...] = w2_ref[...].astype(BF16)

    @pl.when(active)
    def _():
        x = _unpack_rows(xs_ref[...]).astype(BF16)
        hid = jnp.dot(x, w1p_s[...], preferred_element_type=F32) + b1p_s[0:1, :]
        for g in range(n_grp):
            glu = jnp.minimum(hid[:, g * grp:g * grp + LANES], SWIGLU_LIMIT)
            lin = jnp.clip(hid[:, g * grp + LANES:(g + 1) * grp], -SWIGLU_LIMIT, SWIGLU_LIMIT)
            act_s[:, g * LANES:(g + 1) * LANES] = (
                glu * jax.nn.sigmoid(SWIGLU_ALPHA * glu) * (lin + 1.0)).astype(BF16)
        y = jnp.dot(act_s[...], w2b_s[...], preferred_element_type=F32) + b2_ref[...]
        ys_ref[...] = _pack_rows(y)

    @pl.when(jnp.logical_not(active))
    def _():
        ys_ref[...] = jnp.zeros_like(ys_ref)


def _experts(tile_expert, n_active, xs, w1, b1, w2, b2):
    n_rows, w = xs.shape
    tm = EXPERT_ROWS
    d, f2 = w1.shape[1], w1.shape[2]
    f = f2 // 2
    wspec = lambda r, c: pl.BlockSpec((None, r, c), lambda i, te, na: (te[i], 0, 0))
    return pl.pallas_call(
        _expert_kernel,
        grid_spec=pltpu.PrefetchScalarGridSpec(
            num_scalar_prefetch=2,
            grid=(n_rows // tm,),
            in_specs=[pl.BlockSpec((tm, w), lambda i, te, na: (i, 0)),
                      wspec(d, f2), wspec(8, f2), wspec(f, d), wspec(1, d)],
            out_specs=pl.BlockSpec((tm, w), lambda i, te, na: (i, 0)),
            scratch_shapes=[pltpu.VMEM((d, f2), BF16), pltpu.VMEM((8, f2), F32),
                            pltpu.VMEM((f, d), BF16), pltpu.VMEM((tm, f), BF16)]),
        out_shape=jax.ShapeDtypeStruct((n_rows, w), jnp.uint32),
        compiler_params=_params("arbitrary"),
        name="expert",
    )(tile_expert, n_active, xs, w1, b1, w2, b2)


def _combine_kernel(pos_ref, ys_ref, wgt_ref, x1_ref, mod_ref, nw_ref, o_ref, buf_ref, sem):
    tc = x1_ref.shape[0]
    n = pos_ref.shape[1]

    def row_copy(j):
        dst = (j % TOP_K) * tc + j // TOP_K
        return pltpu.make_async_copy(ys_ref.at[pl.ds(pos_ref[0, j], 1)],
                                     buf_ref.at[pl.ds(dst, 1)], sem)

    def issue(j, carry):
        row_copy(j).start()
        return carry

    def drain(j, carry):
        row_copy(j).wait()
        return carry

    lax.fori_loop(0, n, issue, 0, unroll=8)
    lax.fori_loop(0, n, drain, 0, unroll=8)

    wgt = wgt_ref[...]
    acc = jnp.zeros(x1_ref.shape, F32)
    for j in range(TOP_K):
        rows = _unpack_rows(buf_ref[j * tc:(j + 1) * tc, :])
        acc = acc + wgt[:, j:j + 1] * rows
    o_ref[...] = x1_ref[...] + mod_ref[5:6, :] * _rms(acc, nw_ref[...], NORM_EPS)


def _combine(pos, ys, wgt, x1, mod3, post_ffn_norm):
    b, t, d = x1.shape
    tc = min(COMBINE_TOKENS, t)
    nt = t // tc
    pos3 = pos.reshape(b * nt, 1, tc * TOP_K)
    blk = lambda w: pl.BlockSpec((None, tc, w), lambda bi, ti: (bi, ti, 0))
    return pl.pallas_call(
        _combine_kernel,
        grid=(b, nt),
        in_specs=[pl.BlockSpec((None, 1, tc * TOP_K), lambda bi, ti: (bi * nt + ti, 0, 0),
                               memory_space=pltpu.SMEM),
                  pl.BlockSpec(memory_space=pl.ANY),
                  blk(TOP_K), blk(d),
                  pl.BlockSpec((None, N_MOD, d), lambda bi, ti: (bi, 0, 0)),
                  pl.BlockSpec((1, d), lambda bi, ti: (0, 0))],
        out_specs=blk(d),
        out_shape=jax.ShapeDtypeStruct((b, t, d), F32),
        scratch_shapes=[pltpu.VMEM((tc * TOP_K, d // 2), jnp.uint32),
                        pltpu.SemaphoreType.DMA(())],
        compiler_params=_params("arbitrary", "arbitrary"),
        name="combine",
    )(pos3, ys, wgt, x1, mod3, post_ffn_norm)


def _stages(x, c, positions, ada_w, ada_b, pre_mix_norm, post_mix_norm, pre_ffn_norm,
            post_ffn_norm, w_in, w_out, da_lambda_q1, da_lambda_k1, da_lambda_q2, da_lambda_k2,
            da_subln, rw_mu, rw_w0, rw_w2, rw_a0, rw_a2, rw_g2, rw_k_k, rw_k_a, rw_r_k, rw_ln_w,
            rw_ln_b, router_w, router_b, moe_w1, moe_b1, moe_w2, moe_b2):
    b, t, d = x.shape
    res = {}
    lambda_init = 0.8 - 0.6 * math.exp(-0.3 * 0)
    mod = _mod(c, ada_w[0], ada_b[0])
    res["mod"] = mod
    mod3 = mod.reshape(b, N_MOD, d)
    inv_freq = ROPE_THETA ** (-jnp.arange(0, ROPE_DIM, 2, dtype=F32) / ROPE_DIM)
    invf = jnp.tile(inv_freq, LANES // (ROPE_DIM // 2)).reshape(1, LANES)
    q, k, v, rw = _proj(x, positions.reshape(b, t, 1), mod3, pre_mix_norm, invf,
                        w_in[0].astype(BF16), rw_mu)
    res.update(q=q, k=k, v=v, rw=rw)
    lam4 = jnp.concatenate([da_lambda_q1, da_lambda_k1, da_lambda_q2, da_lambda_k2], axis=0)
    y_da = _attn(q, k, v, lam4, da_subln, lambda_init)
    res["y_da"] = y_da
    y_rw = _rwkv(rw, rw_w0, rw_w2[0], rw_a0, rw_a2[0], rw_g2[0], rw_k_k, rw_k_a, rw_r_k[0],
                 rw_ln_w, rw_ln_b)
    res["y_rw"] = y_rw
    x1, h2p, top_idx, top_w = _out(y_da, y_rw, x, mod3, w_out[0].astype(BF16), post_mix_norm,
                                   pre_ffn_norm, router_w[0], router_b[0])
    res.update(x1=x1, top_idx=top_idx, top_w=top_w)
    n = b * t
    n_tiles = n * TOP_K // EXPERT_ROWS + N_EXPERTS
    pos, tile_expert, n_active = _route(top_idx.reshape(n, TOP_K), EXPERT_ROWS, n_tiles)
    xs = _dispatch(pos, h2p.reshape(n, d // 2), n_tiles * EXPERT_ROWS)
    b1 = jnp.broadcast_to(moe_b1[0][:, None, :], (N_EXPERTS, 8, moe_b1.shape[-1]))
    ys = _experts(tile_expert, n_active, xs, moe_w1[0], b1, moe_w2[0], moe_b2[0][:, None, :])
    res["final"] = _combine(pos, ys, top_w, x1, mod3, post_ffn_norm)
    return res


stages = _stages


def kernel(x, c, positions, ada_w, ada_b, pre_mix_norm, post_mix_norm, pre_ffn_norm, post_ffn_norm, w_in, w_out, da_lambda_q1, da_lambda_k1, da_lambda_q2, da_lambda_k2, da_subln, rw_mu, rw_w0, rw_w2, rw_a0, rw_a2, rw_g2, rw_k_k, rw_k_a, rw_r_k, rw_ln_w, rw_ln_b, router_w, router_b, moe_w1, moe_b1, moe_w2, moe_b2):
    res = _stages(x, c, positions, ada_w, ada_b, pre_mix_norm, post_mix_norm, pre_ffn_norm,
                  post_ffn_norm, w_in, w_out, da_lambda_q1, da_lambda_k1, da_lambda_q2,
                  da_lambda_k2, da_subln, rw_mu, rw_w0, rw_w2, rw_a0, rw_a2, rw_g2, rw_k_k,
                  rw_k_a, rw_r_k, rw_ln_w, rw_ln_b, router_w, router_b, moe_w1, moe_b1,
                  moe_w2, moe_b2)
    return res["final"]
```

```python
import functools
import math

import jax
import jax.numpy as jnp
from jax import lax
from jax.experimental import pallas as pl
from jax.experimental.pallas import tpu as pltpu

F32 = jnp.float32
BF16 = jnp.bfloat16
HIGHEST = lax.Precision.HIGHEST

DA_HEADS = 4
DA_HEAD_DIM = 64
DA_V_DIM = 128
DA_WIDTH = 512
RW_HEADS = 8
RW_HEAD_DIM = 64
RW_WIDTH = 512
DECAY_LORA = 64
AAA_LORA = 64
GATE_LORA = 128
DA_COLS = 1536
RW_COLS = 1792
ROPE_THETA = 500000.0
ROPE_DIM = 16
N_EXPERTS = 32
TOP_K = 4
SWIGLU_ALPHA = 1.702
SWIGLU_LIMIT = 7.0
NORM_EPS = 1e-6
SUBLN_EPS = 1e-5
LN_X_EPS = 64e-5
N_MOD = 6

LANES = 128
VMEM_LIMIT_BYTES = 56 * 1024 * 1024

PROJ_ROWS = 512
ATTN_BLOCK = 256
ATTN_KV_BLOCK = 512
RW_CHUNK = 64
RW_BLOCK = 256
OUT_ROWS = 512
EXPERT_ROWS = 512
DISPATCH_TOKENS = 512
COMBINE_TOKENS = 256


def _params(*sem):
    return pltpu.CompilerParams(dimension_semantics=sem, vmem_limit_bytes=VMEM_LIMIT_BYTES)


def _bdot(a, b):
    return jnp.dot(a.astype(BF16), b.astype(BF16), preferred_element_type=F32)


def _bdot_nt(a, b):
    return lax.dot_general(a.astype(BF16), b.astype(BF16), (((1,), (1,)), ((), ())),
                           preferred_element_type=F32)


def _bdot_tn(a, b):
    return lax.dot_general(a.astype(BF16), b.astype(BF16), (((0,), (0,)), ((), ())),
                           preferred_element_type=F32)


def _rms(x, w, eps):
    return x * lax.rsqrt(jnp.mean(x * x, axis=-1, keepdims=True) + eps) * w


def _mod_kernel(c_ref, w_ref, b_ref, o_ref):
    c = c_ref[...]
    s = c * jax.nn.sigmoid(c)
    o_ref[...] = _bdot(s, w_ref[...]) + b_ref[...]


def _mod(c, ada_w, ada_b):
    b, d = c.shape
    n = ada_w.shape[1]
    return pl.pallas_call(
        _mod_kernel,
        grid=(n // d,),
        in_specs=[pl.BlockSpec((b, d), lambda j: (0, 0)),
                  pl.BlockSpec((d, d), lambda j: (0, j)),
                  pl.BlockSpec((1, d), lambda j: (0, j))],
        out_specs=pl.BlockSpec((b, d), lambda j: (0, j)),
        out_shape=jax.ShapeDtypeStruct((b, n), F32),
        compiler_params=_params("parallel"),
        name="mod",
    )(c, ada_w, ada_b.reshape(1, n))


def _proj_kernel(x_ref, pos_ref, mod_ref, nw_ref, invf_ref, w_ref, mu_ref,
                 q_ref, k_ref, v_ref, rw_ref, carry_ref):
    ti = pl.program_id(1)

    @pl.when(ti == 0)
    def _():
        carry_ref[...] = jnp.zeros_like(carry_ref)

    x = x_ref[...]
    h = _rms(x, nw_ref[...], NORM_EPS) * (1.0 + mod_ref[1:2, :]) + mod_ref[0:1, :]
    hb = h.astype(BF16)

    ang = pos_ref[...].astype(F32) * invf_ref[...]
    cos, sin = jnp.cos(ang), jnp.sin(ang)
    l64 = lax.broadcasted_iota(jnp.int32, ang.shape, 1) % DA_HEAD_DIM
    half = ROPE_DIM // 2
    c_tab = jnp.where(l64 < ROPE_DIM, cos, 1.0)
    s_lo = jnp.where(l64 < half, -sin, 0.0)
    s_hi = jnp.where((l64 >= half) & (l64 < ROPE_DIM), sin, 0.0)

    def rope(z):
        up = pltpu.roll(z, LANES - half, axis=1)
        dn = pltpu.roll(z, half, axis=1)
        return z * c_tab + up * s_lo + dn * s_hi

    for g in range(DA_WIDTH // LANES):
        sl = slice(g * LANES, (g + 1) * LANES)
        qg = jnp.dot(hb, w_ref[:, sl], preferred_element_type=F32)
        q_ref[:, sl] = (rope(qg) * (DA_HEAD_DIM ** -0.5)).astype(q_ref.dtype)
        kg = jnp.dot(hb, w_ref[:, DA_WIDTH + g * LANES:DA_WIDTH + (g + 1) * LANES],
                     preferred_element_type=F32)
        k_ref[:, sl] = rope(kg).astype(k_ref.dtype)
    v_ref[...] = jnp.dot(hb, w_ref[:, 2 * DA_WIDTH:DA_COLS],
                         preferred_element_type=F32).astype(v_ref.dtype)

    p = jnp.dot(hb, w_ref[:, DA_COLS:], preferred_element_type=F32)
    rows = p.shape[0]
    prev = pltpu.roll(p, 1, axis=0)
    first = lax.broadcasted_iota(jnp.int32, p.shape, 0) == 0
    prev = jnp.where(first, carry_ref[0:1, :], prev)
    rw_ref[...] = p + (prev - p) * mu_ref[...]
    carry_ref[0:1, :] = p[rows - 1:rows, :]


def _proj(x, pos3, mod3, norm_w, invf, w_in_b, mu):
    b, t, d = x.shape
    tm = min(PROJ_ROWS, t)
    n_in = w_in_b.shape[1]
    blk = lambda w: pl.BlockSpec((None, tm, w), lambda bi, ti: (bi, ti, 0))
    full = lambda r, c: pl.BlockSpec((r, c), lambda bi, ti: (0, 0))
    return pl.pallas_call(
        _proj_kernel,
        grid=(b, t // tm),
        in_specs=[blk(d), blk(1),
                  pl.BlockSpec((None, N_MOD, d), lambda bi, ti: (bi, 0, 0)),
                  full(1, d), full(1, LANES), full(d, n_in), full(1, RW_COLS)],
        out_specs=[blk(DA_WIDTH), blk(DA_WIDTH), blk(DA_WIDTH), blk(RW_COLS)],
        out_shape=[jax.ShapeDtypeStruct((b, t, DA_WIDTH), BF16)] * 3
        + [jax.ShapeDtypeStruct((b, t, RW_COLS), F32)],
        scratch_shapes=[pltpu.VMEM((8, RW_COLS), F32)],
        compiler_params=_params("parallel", "arbitrary"),
        name="proj",
    )(x, pos3, mod3, norm_w, invf, w_in_b, mu)


def _attn_kernel(q_ref, k_ref, v_ref, lam_ref, subln_ref, o_ref, m_ref, l_ref, acc_ref,
                 *, lambda_init):
    qi = pl.program_id(2)
    tq = q_ref.shape[0]
    q = q_ref[...]
    lane = lax.broadcasted_iota(jnp.int32, q.shape, 1)
    zero = jnp.zeros_like(q)
    qq = jnp.concatenate([jnp.where(lane < DA_HEAD_DIM, q, zero),
                          jnp.where(lane >= DA_HEAD_DIM, q, zero)], axis=0)

    m_ref[...] = jnp.full(m_ref.shape, -jnp.inf, F32)
    l_ref[...] = jnp.zeros(l_ref.shape, F32)
    acc_ref[...] = jnp.zeros(acc_ref.shape, F32)
    tk = ATTN_KV_BLOCK if k_ref.shape[0] % ATTN_KV_BLOCK == 0 else tq
    rep = tk // LANES

    def step(j, masked):
        kb = k_ref[pl.ds(pl.multiple_of(j * tk, tk), tk), :]
        vb = v_ref[pl.ds(pl.multiple_of(j * tk, tk), tk), :]
        s = lax.dot_general(qq, kb, (((1,), (1,)), ((), ())), preferred_element_type=F32)
        if masked:
            qpos = qi * tq + lax.broadcasted_iota(jnp.int32, s.shape, 0) % tq
            kpos = j * tk + lax.broadcasted_iota(jnp.int32, s.shape, 1)
            s = jnp.where(qpos >= kpos, s, -jnp.inf)
        m_old = m_ref[...]
        m_new = jnp.maximum(m_old, jnp.max(s, axis=-1, keepdims=True))
        alpha = jnp.exp(m_old - m_new)
        p = jnp.exp(s - jnp.concatenate([m_new] * rep, axis=1))
        l_ref[...] = alpha * l_ref[...] + jnp.sum(p, axis=-1, keepdims=True)
        acc_ref[...] = alpha * acc_ref[...] + jnp.dot(p.astype(vb.dtype), vb,
                                                      preferred_element_type=F32)
        m_ref[...] = m_new

    def body(j, carry):
        step(j, False)
        return carry

    n_full = (qi * tq) // tk
    lax.fori_loop(0, n_full, body, 0)
    step(n_full, True)

    o = acc_ref[...] / l_ref[...]
    lam = (jnp.exp(jnp.sum(lam_ref[0:1, :] * lam_ref[1:2, :], axis=-1, keepdims=True))
           - jnp.exp(jnp.sum(lam_ref[2:3, :] * lam_ref[3:4, :], axis=-1, keepdims=True))
           + lambda_init)
    d = o[:tq, :] - lam * o[tq:, :]
    o_ref[...] = (_rms(d, subln_ref[...], SUBLN_EPS) * (1.0 - lambda_init)).astype(o_ref.dtype)


def _attn(q, k, v, lam4, subln, lambda_init):
    b, t, _ = q.shape
    tq = min(ATTN_BLOCK, t)
    return pl.pallas_call(
        functools.partial(_attn_kernel, lambda_init=lambda_init),
        grid=(b, DA_HEADS, t // tq),
        in_specs=[pl.BlockSpec((None, tq, DA_V_DIM), lambda bi, h, qi: (bi, qi, h)),
                  pl.BlockSpec((None, t, DA_V_DIM), lambda bi, h, qi: (bi, 0, h)),
                  pl.BlockSpec((None, t, DA_V_DIM), lambda bi, h, qi: (bi, 0, h)),
                  pl.BlockSpec((4, DA_HEAD_DIM), lambda bi, h, qi: (0, 0)),
                  pl.BlockSpec((1, DA_V_DIM), lambda bi, h, qi: (0, 0))],
        out_specs=pl.BlockSpec((None, tq, DA_V_DIM), lambda bi, h, qi: (bi, qi, h)),
        out_shape=jax.ShapeDtypeStruct((b, t, DA_WIDTH), BF16),
        scratch_shapes=[pltpu.VMEM((2 * tq, LANES), F32), pltpu.VMEM((2 * tq, LANES), F32),
                        pltpu.VMEM((2 * tq, DA_V_DIM), F32)],
        compiler_params=_params("parallel", "parallel", "arbitrary"),
        name="attn",
    )(q, k, v, lam4, subln)


def _rwkv_kernel(rw_ref, w0_ref, w2_ref, a0_ref, a2_ref, g2_ref, kk_ref, ka_ref, rk_ref,
                 lnw_ref, lnb_ref, o_ref, state_ref, r_s, k_s, v_s, lw_s, kk_s, a_s, g_s, cum_s):
    ti = pl.program_id(1)

    @pl.when(ti == 0)
    def _():
        state_ref[...] = jnp.zeros_like(state_ref)

    w = RW_WIDTH
    rw = rw_ref[...]
    k = rw[:, w:2 * w]
    wl = rw[:, 3 * w:3 * w + DECAY_LORA]
    al = rw[:, 3 * w + DECAY_LORA:3 * w + DECAY_LORA + AAA_LORA]
    gl = rw[:, 3 * w + DECAY_LORA + AAA_LORA:]
    z = -(w0_ref[...] + _bdot(jnp.tanh(wl), w2_ref[...]))
    softplus = jnp.maximum(z, 0.0) + jnp.log(1.0 + jnp.exp(-jnp.abs(z)))
    a = jax.nn.sigmoid(a0_ref[...] + _bdot(al, a2_ref[...]))
    r_s[...] = rw[:, 0:w]
    v_s[...] = rw[:, 2 * w:3 * w]
    lw_s[...] = -jnp.exp(-softplus - 0.5)
    a_s[...] = a
    g_s[...] = _bdot(jax.nn.sigmoid(gl), g2_ref[...])
    kk_s[...] = k * kk_ref[...]
    k_s[...] = k * (1.0 + (a - 1.0) * ka_ref[...])

    c_len = RW_CHUNK
    n = RW_HEAD_DIM
    tb = rw_ref.shape[0]

    br = lax.broadcasted_iota(jnp.int32, (tb, tb), 0)
    bc = lax.broadcasted_iota(jnp.int32, (tb, tb), 1)
    tri = jnp.where((br >= bc) & (br // c_len == bc // c_len), 1.0, 0.0).astype(BF16)
    lw_all = lw_s[...]
    lw_hi = lw_all.astype(BF16)
    rem = lw_all - lw_hi.astype(F32)
    lw_mid = rem.astype(BF16)
    lw_lo = (rem - lw_mid.astype(F32)).astype(BF16)
    cum_s[...] = (jnp.dot(tri, lw_hi, preferred_element_type=F32)
                  + jnp.dot(tri, lw_mid, preferred_element_type=F32)
                  + jnp.dot(tri, lw_lo, preferred_element_type=F32))

    row = lax.broadcasted_iota(jnp.int32, (c_len, 2 * c_len), 0)
    col = lax.broadcasted_iota(jnp.int32, (c_len, 2 * c_len), 1)
    incl2 = row >= col % c_len
    strict2 = row > col % c_len
    eye = jnp.where(lax.broadcasted_iota(jnp.int32, (c_len, c_len), 0)
                    == lax.broadcasted_iota(jnp.int32, (c_len, c_len), 1), 1.0, 0.0).astype(F32)

    def chunk(ci, carry):
        rows = pl.ds(pl.multiple_of(ci * c_len, c_len), c_len)
        heads = range(RW_HEADS)
        sl = [slice(h * n, (h + 1) * n) for h in heads]
        r = [r_s[rows, c] for c in sl]
        kh = [k_s[rows, c] for c in sl]
        v = [v_s[rows, c] for c in sl]
        lw = [lw_s[rows, c] for c in sl]
        cum = [cum_s[rows, c] for c in sl]
        kk = [kk_s[rows, c] for c in sl]
        kk = [x / jnp.maximum(jnp.sqrt(jnp.sum(x * x, axis=-1, keepdims=True)), 1e-12) for x in kk]
        kka = [kk[h] * a_s[rows, sl[h]] for h in heads]
        end = [jnp.sum(x, axis=0, keepdims=True) for x in lw]
        e_neg = [jnp.exp(-x) for x in cum]
        e_end = [jnp.exp(end[h] - cum[h]) for h in heads]
        left = [jnp.concatenate([-kk[h] * jnp.exp(cum[h] - lw[h]), r[h] * jnp.exp(cum[h])], axis=0)
                for h in heads]
        g = [_bdot_nt(left[h], jnp.concatenate([kka[h] * e_neg[h], kh[h] * e_neg[h]], axis=0))
             for h in heads]
        a_a = [jnp.where(strict2, x[:c_len, :], 0.0) for x in g]
        a_r = [jnp.where(incl2, x[c_len:, :], 0.0) for x in g]
        pw = [x[:, :c_len] for x in a_a]
        inv = [eye + x for x in pw]
        for _ in range(5):
            pw = [_bdot(x, x) for x in pw]
            inv = [inv[h] + _bdot(inv[h], pw[h]) for h in heads]
        akv = [_bdot(a_a[h][:, c_len:], v[h]) for h in heads]
        s0 = [state_ref[h] for h in heads]
        ls = [_bdot_nt(left[h], s0[h]) for h in heads]
        u = [_bdot(inv[h], ls[h][:c_len, :] + akv[h]) for h in heads]
        uv = [jnp.concatenate([u[h], v[h]], axis=0) for h in heads]
        y = [ls[h][c_len:, :] + _bdot(a_r[h], uv[h]) for h in heads]
        for h in heads:
            state_ref[h] = s0[h] * jnp.exp(end[h]) + _bdot_tn(
                uv[h], jnp.concatenate([kka[h] * e_end[h], kh[h] * e_end[h]], axis=0))
        for h in heads:
            mean = jnp.mean(y[h], axis=-1, keepdims=True)
            yc = y[h] - mean
            var = jnp.mean(yc * yc, axis=-1, keepdims=True)
            yn = yc * lax.rsqrt(var + LN_X_EPS) * lnw_ref[:, sl[h]] + lnb_ref[:, sl[h]]
            bonus = jnp.sum(r[h] * kh[h] * rk_ref[:, sl[h]], axis=-1, keepdims=True) * v[h]
            o_ref[rows, sl[h]] = ((yn + bonus) * g_s[rows, sl[h]]).astype(o_ref.dtype)
        return carry

    lax.fori_loop(0, rw_ref.shape[0] // c_len, chunk, 0, unroll=2)


def _rwkv(rw, w0, w2, a0, a2, g2, k_k, k_a, r_k, ln_w, ln_b):
    b, t, _ = rw.shape
    tb = min(RW_BLOCK, t)
    w = RW_WIDTH
    vec = pl.BlockSpec((1, w), lambda bi, ti: (0, 0))
    mat = lambda r: pl.BlockSpec((r, w), lambda bi, ti: (0, 0))
    return pl.pallas_call(
        _rwkv_kernel,
        grid=(b, t // tb),
        in_specs=[pl.BlockSpec((None, tb, RW_COLS), lambda bi, ti: (bi, ti, 0)),
                  vec, mat(DECAY_LORA), vec, mat(AAA_LORA), mat(GATE_LORA), vec, vec, vec, vec, vec],
        out_specs=pl.BlockSpec((None, tb, w), lambda bi, ti: (bi, ti, 0)),
        out_shape=jax.ShapeDtypeStruct((b, t, w), BF16),
        scratch_shapes=[pltpu.VMEM((RW_HEADS, RW_HEAD_DIM, RW_HEAD_DIM), F32)]
        + [pltpu.VMEM((tb, w), F32)] * 8,
        compiler_params=_params("parallel", "arbitrary"),
        name="rwkv",
    )(rw, w0, w2, a0, a2, g2, k_k, k_a, r_k.reshape(1, w), ln_w, ln_b)


def _pack_rows(x):
    half = x.shape[1] // 2
    hi = pltpu.bitcast(x[:, :half].astype(BF16).astype(F32), jnp.uint32)
    lo = pltpu.bitcast(x[:, half:].astype(BF16).astype(F32), jnp.uint32)
    return hi | (lo >> 16)


def _unpack_rows(u):
    hi = pltpu.bitcast(u & jnp.uint32(0xFFFF0000), F32)
    lo = pltpu.bitcast(u << 16, F32)
    return jnp.concatenate([hi, lo], axis=1)


def _out_kernel(yda_ref, yrw_ref, x_ref, mod_ref, wo_ref, pmn_ref, pfn_ref, rw_ref, rb_ref,
                x1_ref, h2_ref, idx_ref, wgt_ref):
    y = (jnp.dot(yda_ref[...], wo_ref[0:DA_WIDTH, :], preferred_element_type=F32)
         + jnp.dot(yrw_ref[...], wo_ref[DA_WIDTH:, :], preferred_element_type=F32))
    x1 = x_ref[...] + mod_ref[2:3, :] * _rms(y, pmn_ref[...], NORM_EPS)
    x1_ref[...] = x1
    h2 = _rms(x1, pfn_ref[...], NORM_EPS) * (1.0 + mod_ref[4:5, :]) + mod_ref[3:4, :]
    h2_ref[...] = _pack_rows(h2)

    h_hi = h2.astype(BF16)
    h_lo = (h2 - h_hi.astype(F32)).astype(BF16)
    rw = rw_ref[...]
    w_hi = rw.astype(BF16)
    w_lo = (rw - w_hi.astype(F32)).astype(BF16)
    logits = (jnp.dot(h_hi, w_hi, preferred_element_type=F32)
              + jnp.dot(h_hi, w_lo, preferred_element_type=F32)
              + jnp.dot(h_lo, w_hi, preferred_element_type=F32)) + rb_ref[...]

    lane = lax.broadcasted_iota(jnp.int32, logits.shape, 1)
    slot = lax.broadcasted_iota(jnp.int32, idx_ref.shape, 1)
    idx = jnp.zeros(idx_ref.shape, jnp.int32)
    val = jnp.zeros(idx_ref.shape, F32)
    top = None
    for j in range(TOP_K):
        m = jnp.max(logits, axis=-1, keepdims=True)
        i = jnp.min(jnp.where(logits == m, lane, N_EXPERTS), axis=-1, keepdims=True)
        top = m if top is None else top
        idx = jnp.where(slot == j, i, idx)
        val = jnp.where(slot == j, jnp.exp(m - top), val)
        logits = jnp.where(lane == i, -jnp.inf, logits)
    idx_ref[...] = idx
    wgt_ref[...] = val / jnp.sum(val, axis=-1, keepdims=True)


def _out(y_da, y_rw, x, mod3, w_out_b, post_mix_norm, pre_ffn_norm, router_w, router_b):
    b, t, d = x.shape
    tm = min(OUT_ROWS, t)
    e = router_w.shape[1]
    blk = lambda w: pl.BlockSpec((None, tm, w), lambda bi, ti: (bi, ti, 0))
    full = lambda r, c: pl.BlockSpec((r, c), lambda bi, ti: (0, 0))
    return pl.pallas_call(
        _out_kernel,
        grid=(b, t // tm),
        in_specs=[blk(DA_WIDTH), blk(RW_WIDTH), blk(d),
                  pl.BlockSpec((None, N_MOD, d), lambda bi, ti: (bi, 0, 0)),
                  full(d, d), full(1, d), full(1, d), full(d, e), full(1, e)],
        out_specs=[blk(d), blk(d // 2), blk(TOP_K), blk(TOP_K)],
        out_shape=[jax.ShapeDtypeStruct((b, t, d), F32),
                   jax.ShapeDtypeStruct((b, t, d // 2), jnp.uint32),
                   jax.ShapeDtypeStruct((b, t, TOP_K), jnp.int32),
                   jax.ShapeDtypeStruct((b, t, TOP_K), F32)],
        compiler_params=_params("parallel", "parallel"),
        name="out",
    )(y_da, y_rw, x, mod3, w_out_b, post_mix_norm, pre_ffn_norm, router_w,
      router_b.reshape(1, e))


def _route(top_idx, rows_per_tile, n_tiles):
    e_flat = top_idx.reshape(-1)
    onehot = (e_flat[:, None] == jnp.arange(N_EXPERTS, dtype=jnp.int32)[None, :]).astype(jnp.int32)
    csum = jnp.cumsum(onehot, axis=0)
    counts = csum[-1]
    padded = (counts + rows_per_tile - 1) // rows_per_tile * rows_per_tile
    ends = jnp.cumsum(padded)
    starts = ends - padded
    pos = jnp.sum((csum - onehot + starts[None, :]) * onehot, axis=1)
    n_active = ends[-1] // rows_per_tile
    tile_start = jnp.arange(n_tiles, dtype=jnp.int32) * rows_per_tile
    tile = jnp.minimum(tile_start, ends[-1] - 1)
    tile_expert = jnp.sum((tile[:, None] >= ends[None, :]).astype(jnp.int32), axis=1)
    return pos.astype(jnp.int32), tile_expert.astype(jnp.int32), n_active.reshape(1).astype(jnp.int32)


def _dispatch_kernel(pos_ref, h_ref, xs_in_ref, xs_ref, sem):
    del xs_in_ref
    tb = h_ref.shape[0]

    def issue(t, carry):
        for j in range(TOP_K):
            pltpu.make_async_copy(h_ref.at[pl.ds(t, 1)],
                                  xs_ref.at[pl.ds(pos_ref[0, t * TOP_K + j], 1)], sem).start()
        return carry

    lax.fori_loop(0, tb, issue, 0, unroll=4)
    for j in range(TOP_K):
        pltpu.make_async_copy(h_ref, xs_ref.at[pl.ds(0, tb)], sem).wait()


def _dispatch(pos, h2p, n_rows):
    n, w = h2p.shape
    tb = min(DISPATCH_TOKENS, n)
    pos3 = pos.reshape(n // tb, 1, tb * TOP_K)
    xs0 = jnp.zeros((n_rows, w), h2p.dtype)
    return pl.pallas_call(
        _dispatch_kernel,
        grid=(n // tb,),
        in_specs=[pl.BlockSpec((None, 1, tb * TOP_K), lambda i: (i, 0, 0),
                               memory_space=pltpu.SMEM),
                  pl.BlockSpec((tb, w), lambda i: (i, 0)),
                  pl.BlockSpec(memory_space=pl.ANY)],
        out_specs=pl.BlockSpec(memory_space=pl.ANY),
        out_shape=jax.ShapeDtypeStruct((n_rows, w), h2p.dtype),
        scratch_shapes=[pltpu.SemaphoreType.DMA(())],
        input_output_aliases={2: 0},
        compiler_params=_params("arbitrary"),
        name="dispatch",
    )(pos3, h2p, xs0)


def _expert_kernel(te_ref, na_ref, xs_ref, w1_ref, b1_ref, w2_ref, b2_ref, ys_ref,
                   w1p_s, b1p_s, w2b_s, act_s):
    i = pl.program_id(0)
    active = i < na_ref[0]
    fresh = jnp.logical_or(i == 0, te_ref[i] != te_ref[jnp.maximum(i - 1, 0)])
    grp = 2 * LANES
    n_grp = w1_ref.shape[1] // grp

    @pl.when(jnp.logical_and(active, fresh))
    def _():
        src = lax.broadcasted_iota(jnp.int32, (grp, grp), 0)
        dst = lax.broadcasted_iota(jnp.int32, (grp, grp), 1)
        perm = jnp.where(src == jnp.where(dst < LANES, 2 * dst, 2 * (dst - LANES) + 1),
                         1.0, 0.0).astype(BF16)
        for g in range(n_grp):
            sl = slice(g * grp, (g + 1) * grp)
            w1p_s[:, sl] = jnp.dot(w1_ref[:, sl].astype(BF16), perm,
                                   preferred_element_type=F32).astype(BF16)
            b = b1_ref[:, sl]
            b_hi = b.astype(BF16)
            b_lo = (b - b_hi.astype(F32)).astype(BF16)
            b1p_s[:, sl] = (jnp.dot(b_hi, perm, preferred_element_type=F32)
                            + jnp.dot(b_lo, perm, preferred_element_type=F32))
        w2b_s[...] = w2_ref[...].astype(BF16)

    @pl.when(active)
    def _():
        x = _unpack_rows(xs_ref[...]).astype(BF16)
        hid = jnp.dot(x, w1p_s[...], preferred_element_type=F32) + b1p_s[0:1, :]
        for g in range(n_grp):
            glu = jnp.minimum(hid[:, g * grp:g * grp + LANES], SWIGLU_LIMIT)
            lin = jnp.clip(hid[:, g * grp + LANES:(g + 1) * grp], -SWIGLU_LIMIT, SWIGLU_LIMIT)
            act_s[:, g * LANES:(g + 1) * LANES] = (
                glu * jax.nn.sigmoid(SWIGLU_ALPHA * glu) * (lin + 1.0)).astype(BF16)
        y = jnp.dot(act_s[...], w2b_s[...], preferred_element_type=F32) + b2_ref[...]
        ys_ref[...] = _pack_rows(y)

    @pl.when(jnp.logical_not(active))
    def _():
        ys_ref[...] = jnp.zeros_like(ys_ref)


def _experts(tile_expert, n_active, xs, w1, b1, w2, b2):
    n_rows, w = xs.shape
    tm = EXPERT_ROWS
    d, f2 = w1.shape[1], w1.shape[2]
    f = f2 // 2
    wspec = lambda r, c: pl.BlockSpec((None, r, c), lambda i, te, na: (te[i], 0, 0))
    return pl.pallas_call(
        _expert_kernel,
        grid_spec=pltpu.PrefetchScalarGridSpec(
            num_scalar_prefetch=2,
            grid=(n_rows // tm,),
            in_specs=[pl.BlockSpec((tm, w), lambda i, te, na: (i, 0)),
                      wspec(d, f2), wspec(8, f2), wspec(f, d), wspec(1, d)],
            out_specs=pl.BlockSpec((tm, w), lambda i, te, na: (i, 0)),
            scratch_shapes=[pltpu.VMEM((d, f2), BF16), pltpu.VMEM((8, f2), F32),
                            pltpu.VMEM((f, d), BF16), pltpu.VMEM((tm, f), BF16)]),
        out_shape=jax.ShapeDtypeStruct((n_rows, w), jnp.uint32),
        compiler_params=_params("arbitrary"),
        name="expert",
    )(tile_expert, n_active, xs, w1, b1, w2, b2)


def _combine_kernel(pos_ref, ys_ref, wgt_ref, x1_ref, mod_ref, nw_ref, o_ref, buf_ref, sem):
    tc = x1_ref.shape[0]

    def issue(t, carry):
        for j in range(TOP_K):
            pltpu.make_async_copy(ys_ref.at[pl.ds(pos_ref[0, t * TOP_K + j], 1)],
                                  buf_ref.at[pl.ds(j * tc + t, 1)], sem).start()
        return carry

    lax.fori_loop(0, tc, issue, 0, unroll=4)
    for j in range(TOP_K):
        pltpu.make_async_copy(ys_ref.at[pl.ds(0, tc)], buf_ref.at[pl.ds(j * tc, tc)], sem).wait()

    wgt = wgt_ref[...]
    acc = jnp.zeros(x1_ref.shape, F32)
    for j in range(TOP_K):
        rows = _unpack_rows(buf_ref[j * tc:(j + 1) * tc, :])
        acc = acc + wgt[:, j:j + 1] * rows
    o_ref[...] = x1_ref[...] + mod_ref[5:6, :] * _rms(acc, nw_ref[...], NORM_EPS)


def _combine(pos, ys, wgt, x1, mod3, post_ffn_norm):
    b, t, d = x1.shape
    tc = min(COMBINE_TOKENS, t)
    nt = t // tc
    pos3 = pos.reshape(b * nt, 1, tc * TOP_K)
    blk = lambda w: pl.BlockSpec((None, tc, w), lambda bi, ti: (bi, ti, 0))
    return pl.pallas_call(
        _combine_kernel,
        grid=(b, nt),
        in_specs=[pl.BlockSpec((None, 1, tc * TOP_K), lambda bi, ti: (bi * nt + ti, 0, 0),
                               memory_space=pltpu.SMEM),
                  pl.BlockSpec(memory_space=pl.ANY),
                  blk(TOP_K), blk(d),
                  pl.BlockSpec((None, N_MOD, d), lambda bi, ti: (bi, 0, 0)),
                  pl.BlockSpec((1, d), lambda bi, ti: (0, 0))],
        out_specs=blk(d),
        out_shape=jax.ShapeDtypeStruct((b, t, d), F32),
        scratch_shapes=[pltpu.VMEM((tc * TOP_K, d // 2), jnp.uint32),
                        pltpu.SemaphoreType.DMA(())],
        compiler_params=_params("arbitrary", "arbitrary"),
        name="combine",
    )(pos3, ys, wgt, x1, mod3, post_ffn_norm)


def _stages(x, c, positions, ada_w, ada_b, pre_mix_norm, post_mix_norm, pre_ffn_norm,
            post_ffn_norm, w_in, w_out, da_lambda_q1, da_lambda_k1, da_lambda_q2, da_lambda_k2,
            da_subln, rw_mu, rw_w0, rw_w2, rw_a0, rw_a2, rw_g2, rw_k_k, rw_k_a, rw_r_k, rw_ln_w,
            rw_ln_b, router_w, router_b, moe_w1, moe_b1, moe_w2, moe_b2):
    b, t, d = x.shape
    res = {}
    lambda_init = 0.8 - 0.6 * math.exp(-0.3 * 0)
    mod = _mod(c, ada_w[0], ada_b[0])
    res["mod"] = mod
    mod3 = mod.reshape(b, N_MOD, d)
    inv_freq = ROPE_THETA ** (-jnp.arange(0, ROPE_DIM, 2, dtype=F32) / ROPE_DIM)
    invf = jnp.tile(inv_freq, LANES // (ROPE_DIM // 2)).reshape(1, LANES)
    q, k, v, rw = _proj(x, positions.reshape(b, t, 1), mod3, pre_mix_norm, invf,
                        w_in[0].astype(BF16), rw_mu)
    res.update(q=q, k=k, v=v, rw=rw)
    lam4 = jnp.concatenate([da_lambda_q1, da_lambda_k1, da_lambda_q2, da_lambda_k2], axis=0)
    y_da = _attn(q, k, v, lam4, da_subln, lambda_init)
    res["y_da"] = y_da
    y_rw = _rwkv(rw, rw_w0, rw_w2[0], rw_a0, rw_a2[0], rw_g2[0], rw_k_k, rw_k_a, rw_r_k[0],
                 rw_ln_w, rw_ln_b)
    res["y_rw"] = y_rw
    x1, h2p, top_idx, top_w = _out(y_da, y_rw, x, mod3, w_out[0].astype(BF16), post_mix_norm,
                                   pre_ffn_norm, router_w[0], router_b[0])
    res.update(x1=x1, top_idx=top_idx, top_w=top_w)
    n = b * t
    n_tiles = n * TOP_K // EXPERT_ROWS + N_EXPERTS
    pos, tile_expert, n_active = _route(top_idx.reshape(n, TOP_K), EXPERT_ROWS, n_tiles)
    xs = _dispatch(pos, h2p.reshape(n, d // 2), n_tiles * EXPERT_ROWS)
    b1 = jnp.broadcast_to(moe_b1[0][:, None, :], (N_EXPERTS, 8, moe_b1.shape[-1]))
    ys = _experts(tile_expert, n_active, xs, moe_w1[0], b1, moe_w2[0], moe_b2[0][:, None, :])
    res["final"] = _combine(pos, ys, top_w, x1, mod3, post_ffn_norm)
    return res


stages = _stages


def kernel(x, c, positions, ada_w, ada_b, pre_mix_norm, post_mix_norm, pre_ffn_norm, post_ffn_norm, w_in, w_out, da_lambda_q1, da_lambda_k1, da_lambda_q2, da_lambda_k2, da_subln, rw_mu, rw_w0, rw_w2, rw_a0, rw_a2, rw_g2, rw_k_k, rw_k_a, rw_r_k, rw_ln_w, rw_ln_b, router_w, router_b, moe_w1, moe_b1, moe_w2, moe_b2):
    res = _stages(x, c, positions, ada_w, ada_b, pre_mix_norm, post_mix_norm, pre_ffn_norm,
                  post_ffn_norm, w_in, w_out, da_lambda_q1, da_lambda_k1, da_lambda_q2,
                  da_lambda_k2, da_subln, rw_mu, rw_w0, rw_w2, rw_a0, rw_a2, rw_g2, rw_k_k,
                  rw_k_a, rw_r_k, rw_ln_w, rw_ln_b, router_w, router_b, moe_w1, moe_b1,
                  moe_w2, moe_b2)
    return res["final"]
```

```python
import functools
import math

import jax
import jax.numpy as jnp
from jax import lax
from jax.experimental import pallas as pl
from jax.experimental.pallas import tpu as pltpu

F32 = jnp.float32
BF16 = jnp.bfloat16

DA_HEADS = 4
DA_HEAD_DIM = 64
DA_V_DIM = 128
DA_WIDTH = 512
RW_HEADS = 8
RW_HEAD_DIM = 64
RW_WIDTH = 512
DECAY_LORA = 64
AAA_LORA = 64
GATE_LORA = 128
DA_COLS = 1536
RW_COLS = 1792
ROPE_THETA = 500000.0
ROPE_DIM = 16
N_EXPERTS = 32
TOP_K = 4
SWIGLU_ALPHA = 1.702
SWIGLU_LIMIT = 7.0
NORM_EPS = 1e-6
SUBLN_EPS = 1e-5
LN_X_EPS = 64e-5
N_MOD = 6

LANES = 128
VMEM_LIMIT_BYTES = 56 * 1024 * 1024

PROJ_ROWS = 512
ATTN_BLOCK = 256
ATTN_KV_BLOCK = 512
ATTN_HEAD_GROUP = 2
RW_CHUNK = 128
RW_BLOCK = 256
OUT_ROWS = 512
EXPERT_ROWS = 512
DISPATCH_TOKENS = 512
COMBINE_TOKENS = 256


def _params(*sem):
    return pltpu.CompilerParams(dimension_semantics=sem, vmem_limit_bytes=VMEM_LIMIT_BYTES)


def _bdot(a, b):
    return jnp.dot(a.astype(BF16), b.astype(BF16), preferred_element_type=F32)


def _bdot_nt(a, b):
    return lax.dot_general(a.astype(BF16), b.astype(BF16), (((1,), (1,)), ((), ())),
                           preferred_element_type=F32)


def _bdot_tn(a, b):
    return lax.dot_general(a.astype(BF16), b.astype(BF16), (((0,), (0,)), ((), ())),
                           preferred_element_type=F32)


def _rms(x, w, eps):
    return x * lax.rsqrt(jnp.mean(x * x, axis=-1, keepdims=True) + eps) * w


def _mod_kernel(c_ref, w_ref, b_ref, o_ref):
    c = c_ref[...]
    s = c * jax.nn.sigmoid(c)
    o_ref[...] = _bdot(s, w_ref[...]) + b_ref[...]


def _mod(c, ada_w, ada_b):
    b, d = c.shape
    n = ada_w.shape[1]
    return pl.pallas_call(
        _mod_kernel,
        grid=(n // d,),
        in_specs=[pl.BlockSpec((b, d), lambda j: (0, 0)),
                  pl.BlockSpec((d, d), lambda j: (0, j)),
                  pl.BlockSpec((1, d), lambda j: (0, j))],
        out_specs=pl.BlockSpec((b, d), lambda j: (0, j)),
        out_shape=jax.ShapeDtypeStruct((b, n), F32),
        compiler_params=_params("parallel"),
        name="mod",
    )(c, ada_w, ada_b.reshape(1, n))


def _proj_kernel(x_ref, pos_ref, mod_ref, nw_ref, invf_ref, w_ref, mu_ref,
                 q_ref, k_ref, v_ref, rw_ref, carry_ref):
    ti = pl.program_id(1)

    @pl.when(ti == 0)
    def _():
        carry_ref[...] = jnp.zeros_like(carry_ref)

    x = x_ref[...]
    h = _rms(x, nw_ref[...], NORM_EPS) * (1.0 + mod_ref[1:2, :]) + mod_ref[0:1, :]
    hb = h.astype(BF16)

    ang = pos_ref[...].astype(F32) * invf_ref[...]
    cos, sin = jnp.cos(ang), jnp.sin(ang)
    l64 = lax.broadcasted_iota(jnp.int32, ang.shape, 1) % DA_HEAD_DIM
    half = ROPE_DIM // 2
    c_tab = jnp.where(l64 < ROPE_DIM, cos, 1.0)
    s_lo = jnp.where(l64 < half, -sin, 0.0)
    s_hi = jnp.where((l64 >= half) & (l64 < ROPE_DIM), sin, 0.0)

    def rope(z):
        up = pltpu.roll(z, LANES - half, axis=1)
        dn = pltpu.roll(z, half, axis=1)
        return z * c_tab + up * s_lo + dn * s_hi

    for g in range(DA_WIDTH // LANES):
        sl = slice(g * LANES, (g + 1) * LANES)
        qg = jnp.dot(hb, w_ref[:, sl], preferred_element_type=F32)
        q_ref[:, sl] = (rope(qg) * (DA_HEAD_DIM ** -0.5)).astype(q_ref.dtype)
        kg = jnp.dot(hb, w_ref[:, DA_WIDTH + g * LANES:DA_WIDTH + (g + 1) * LANES],
                     preferred_element_type=F32)
        k_ref[:, sl] = rope(kg).astype(k_ref.dtype)
    v_ref[...] = jnp.dot(hb, w_ref[:, 2 * DA_WIDTH:DA_COLS],
                         preferred_element_type=F32).astype(v_ref.dtype)

    p = jnp.dot(hb, w_ref[:, DA_COLS:], preferred_element_type=F32)
    rows = p.shape[0]
    prev = pltpu.roll(p, 1, axis=0)
    first = lax.broadcasted_iota(jnp.int32, p.shape, 0) == 0
    prev = jnp.where(first, carry_ref[0:1, :], prev)
    rw_ref[...] = p + (prev - p) * mu_ref[...]
    carry_ref[0:1, :] = p[rows - 1:rows, :]


def _proj(x, pos3, mod3, norm_w, invf, w_in_b, mu):
    b, t, d = x.shape
    tm = min(PROJ_ROWS, t)
    n_in = w_in_b.shape[1]
    blk = lambda w: pl.BlockSpec((None, tm, w), lambda bi, ti: (bi, ti, 0))
    full = lambda r, c: pl.BlockSpec((r, c), lambda bi, ti: (0, 0))
    return pl.pallas_call(
        _proj_kernel,
        grid=(b, t // tm),
        in_specs=[blk(d), blk(1),
                  pl.BlockSpec((None, N_MOD, d), lambda bi, ti: (bi, 0, 0)),
                  full(1, d), full(1, LANES), full(d, n_in), full(1, RW_COLS)],
        out_specs=[blk(DA_WIDTH), blk(DA_WIDTH), blk(DA_WIDTH), blk(RW_COLS)],
        out_shape=[jax.ShapeDtypeStruct((b, t, DA_WIDTH), BF16)] * 3
        + [jax.ShapeDtypeStruct((b, t, RW_COLS), F32)],
        scratch_shapes=[pltpu.VMEM((8, RW_COLS), F32)],
        compiler_params=_params("parallel", "arbitrary"),
        name="proj",
    )(x, pos3, mod3, norm_w, invf, w_in_b, mu)


def _attn_kernel(q_ref, k_ref, v_ref, lam_ref, subln_ref, o_ref, m_ref, l_ref, acc_ref,
                 *, lambda_init):
    qi = pl.program_id(2)
    tq = q_ref.shape[0]
    heads = range(ATTN_HEAD_GROUP)
    hs = [slice(h * DA_V_DIM, (h + 1) * DA_V_DIM) for h in heads]
    lane = lax.broadcasted_iota(jnp.int32, (tq, DA_V_DIM), 1)
    qq = []
    for c in hs:
        q = q_ref[:, c]
        zero = jnp.zeros_like(q)
        qq.append(jnp.concatenate([jnp.where(lane < DA_HEAD_DIM, q, zero),
                                   jnp.where(lane >= DA_HEAD_DIM, q, zero)], axis=0))

    m_ref[...] = jnp.full(m_ref.shape, -jnp.inf, F32)
    l_ref[...] = jnp.zeros(l_ref.shape, F32)
    acc_ref[...] = jnp.zeros(acc_ref.shape, F32)
    tk = ATTN_KV_BLOCK if k_ref.shape[0] % ATTN_KV_BLOCK == 0 else tq
    rep = tk // LANES

    def step(j, masked):
        rows = pl.ds(pl.multiple_of(j * tk, tk), tk)
        s = [lax.dot_general(qq[h], k_ref[rows, hs[h]], (((1,), (1,)), ((), ())),
                             preferred_element_type=F32) for h in heads]
        if masked:
            qpos = qi * tq + lax.broadcasted_iota(jnp.int32, s[0].shape, 0) % tq
            kpos = j * tk + lax.broadcasted_iota(jnp.int32, s[0].shape, 1)
            s = [jnp.where(qpos >= kpos, x, -jnp.inf) for x in s]
        for h in heads:
            m_old = m_ref[h]
            m_new = jnp.maximum(m_old, jnp.max(s[h], axis=-1, keepdims=True))
            alpha = jnp.exp(m_old - m_new)
            p = jnp.exp(s[h] - jnp.concatenate([m_new] * rep, axis=1))
            l_ref[h] = alpha * l_ref[h] + jnp.sum(p, axis=-1, keepdims=True)
            acc_ref[h] = alpha * acc_ref[h] + jnp.dot(p.astype(v_ref.dtype), v_ref[rows, hs[h]],
                                                      preferred_element_type=F32)
            m_ref[h] = m_new

    def body(j, carry):
        step(j, False)
        return carry

    n_full = (qi * tq) // tk
    lax.fori_loop(0, n_full, body, 0)
    step(n_full, True)

    lam = (jnp.exp(jnp.sum(lam_ref[0:1, :] * lam_ref[1:2, :], axis=-1, keepdims=True))
           - jnp.exp(jnp.sum(lam_ref[2:3, :] * lam_ref[3:4, :], axis=-1, keepdims=True))
           + lambda_init)
    for h in heads:
        o = acc_ref[h] / l_ref[h]
        d = o[:tq, :] - lam * o[tq:, :]
        o_ref[:, hs[h]] = (_rms(d, subln_ref[...], SUBLN_EPS)
                           * (1.0 - lambda_init)).astype(o_ref.dtype)


def _attn(q, k, v, lam4, subln, lambda_init):
    b, t, _ = q.shape
    tq = min(ATTN_BLOCK, t)
    hg = ATTN_HEAD_GROUP
    gw = hg * DA_V_DIM
    return pl.pallas_call(
        functools.partial(_attn_kernel, lambda_init=lambda_init),
        grid=(b, DA_HEADS // hg, t // tq),
        in_specs=[pl.BlockSpec((None, tq, gw), lambda bi, h, qi: (bi, qi, h)),
                  pl.BlockSpec((None, t, gw), lambda bi, h, qi: (bi, 0, h)),
                  pl.BlockSpec((None, t, gw), lambda bi, h, qi: (bi, 0, h)),
                  pl.BlockSpec((4, DA_HEAD_DIM), lambda bi, h, qi: (0, 0)),
                  pl.BlockSpec((1, DA_V_DIM), lambda bi, h, qi: (0, 0))],
        out_specs=pl.BlockSpec((None, tq, gw), lambda bi, h, qi: (bi, qi, h)),
        out_shape=jax.ShapeDtypeStruct((b, t, DA_WIDTH), BF16),
        scratch_shapes=[pltpu.VMEM((hg, 2 * tq, LANES), F32), pltpu.VMEM((hg, 2 * tq, LANES), F32),
                        pltpu.VMEM((hg, 2 * tq, DA_V_DIM), F32)],
        compiler_params=_params("parallel", "parallel", "arbitrary"),
        name="attn",
    )(q, k, v, lam4, subln)


def _rwkv_kernel(rw_ref, w0_ref, w2_ref, a0_ref, a2_ref, g2_ref, kk_ref, ka_ref, rk_ref,
                 lnw_ref, lnb_ref, o_ref, state_ref, r_s, k_s, v_s, lw_s, kk_s, a_s, g_s, cum_s):
    ti = pl.program_id(1)

    @pl.when(ti == 0)
    def _():
        state_ref[...] = jnp.zeros_like(state_ref)

    w = RW_WIDTH
    rw = rw_ref[...]
    k = rw[:, w:2 * w]
    wl = rw[:, 3 * w:3 * w + DECAY_LORA]
    al = rw[:, 3 * w + DECAY_LORA:3 * w + DECAY_LORA + AAA_LORA]
    gl = rw[:, 3 * w + DECAY_LORA + AAA_LORA:]
    z = -(w0_ref[...] + _bdot(jnp.tanh(wl), w2_ref[...]))
    softplus = jnp.maximum(z, 0.0) + jnp.log(1.0 + jnp.exp(-jnp.abs(z)))
    a = jax.nn.sigmoid(a0_ref[...] + _bdot(al, a2_ref[...]))
    r_s[...] = rw[:, 0:w]
    v_s[...] = rw[:, 2 * w:3 * w]
    lw_s[...] = -jnp.exp(-softplus - 0.5)
    a_s[...] = a
    g_s[...] = _bdot(jax.nn.sigmoid(gl), g2_ref[...])
    kk_s[...] = k * kk_ref[...]
    k_s[...] = k * (1.0 + (a - 1.0) * ka_ref[...])

    c_len = RW_CHUNK
    n = RW_HEAD_DIM
    tb = rw_ref.shape[0]

    br = lax.broadcasted_iota(jnp.int32, (tb, tb), 0)
    bc = lax.broadcasted_iota(jnp.int32, (tb, tb), 1)
    tri = jnp.where((br >= bc) & (br // c_len == bc // c_len), 1.0, 0.0).astype(BF16)
    lw_all = lw_s[...]
    lw_hi = lw_all.astype(BF16)
    rem = lw_all - lw_hi.astype(F32)
    lw_mid = rem.astype(BF16)
    lw_lo = (rem - lw_mid.astype(F32)).astype(BF16)
    cum_s[...] = (jnp.dot(tri, lw_hi, preferred_element_type=F32)
                  + jnp.dot(tri, lw_mid, preferred_element_type=F32)
                  + jnp.dot(tri, lw_lo, preferred_element_type=F32))

    row = lax.broadcasted_iota(jnp.int32, (c_len, 2 * c_len), 0)
    col = lax.broadcasted_iota(jnp.int32, (c_len, 2 * c_len), 1)
    incl2 = row >= col % c_len
    strict2 = row > col % c_len
    eye = jnp.where(lax.broadcasted_iota(jnp.int32, (c_len, c_len), 0)
                    == lax.broadcasted_iota(jnp.int32, (c_len, c_len), 1), 1.0, 0.0).astype(F32)

    def chunk(ci, carry):
        rows = pl.ds(pl.multiple_of(ci * c_len, c_len), c_len)
        heads = range(RW_HEADS)
        sl = [slice(h * n, (h + 1) * n) for h in heads]
        r = [r_s[rows, c] for c in sl]
        kh = [k_s[rows, c] for c in sl]
        v = [v_s[rows, c] for c in sl]
        lw = [lw_s[rows, c] for c in sl]
        cum = [cum_s[rows, c] for c in sl]
        kk = [kk_s[rows, c] for c in sl]
        kk = [x / jnp.maximum(jnp.sqrt(jnp.sum(x * x, axis=-1, keepdims=True)), 1e-12) for x in kk]
        kka = [kk[h] * a_s[rows, sl[h]] for h in heads]
        end = [jnp.sum(x, axis=0, keepdims=True) for x in lw]
        e_neg = [jnp.exp(-x) for x in cum]
        e_end = [jnp.exp(end[h] - cum[h]) for h in heads]
        left = [jnp.concatenate([-kk[h] * jnp.exp(cum[h] - lw[h]), r[h] * jnp.exp(cum[h])], axis=0)
                for h in heads]
        g = [_bdot_nt(left[h], jnp.concatenate([kka[h] * e_neg[h], kh[h] * e_neg[h]], axis=0))
             for h in heads]
        a_a = [jnp.where(strict2, x[:c_len, :], 0.0) for x in g]
        a_r = [jnp.where(incl2, x[c_len:, :], 0.0) for x in g]
        pw = [x[:, :c_len] for x in a_a]
        inv = [eye + x for x in pw]
        for _ in range(c_len.bit_length() - 2):
            pw = [_bdot(x, x) for x in pw]
            inv = [inv[h] + _bdot(inv[h], pw[h]) for h in heads]
        akv = [_bdot(a_a[h][:, c_len:], v[h]) for h in heads]
        s0 = [state_ref[h] for h in heads]
        ls = [_bdot_nt(left[h], s0[h]) for h in heads]
        u = [_bdot(inv[h], ls[h][:c_len, :] + akv[h]) for h in heads]
        uv = [jnp.concatenate([u[h], v[h]], axis=0) for h in heads]
        y = [ls[h][c_len:, :] + _bdot(a_r[h], uv[h]) for h in heads]
        for h in heads:
            state_ref[h] = s0[h] * jnp.exp(end[h]) + _bdot_tn(
                uv[h], jnp.concatenate([kka[h] * e_end[h], kh[h] * e_end[h]], axis=0))
        for h in heads:
            mean = jnp.mean(y[h], axis=-1, keepdims=True)
            yc = y[h] - mean
            var = jnp.mean(yc * yc, axis=-1, keepdims=True)
            yn = yc * lax.rsqrt(var + LN_X_EPS) * lnw_ref[:, sl[h]] + lnb_ref[:, sl[h]]
            bonus = jnp.sum(r[h] * kh[h] * rk_ref[:, sl[h]], axis=-1, keepdims=True) * v[h]
            o_ref[rows, sl[h]] = ((yn + bonus) * g_s[rows, sl[h]]).astype(o_ref.dtype)
        return carry

    lax.fori_loop(0, rw_ref.shape[0] // c_len, chunk, 0, unroll=2)


def _rwkv(rw, w0, w2, a0, a2, g2, k_k, k_a, r_k, ln_w, ln_b):
    b, t, _ = rw.shape
    tb = min(RW_BLOCK, t)
    w = RW_WIDTH
    vec = pl.BlockSpec((1, w), lambda bi, ti: (0, 0))
    mat = lambda r: pl.BlockSpec((r, w), lambda bi, ti: (0, 0))
    return pl.pallas_call(
        _rwkv_kernel,
        grid=(b, t // tb),
        in_specs=[pl.BlockSpec((None, tb, RW_COLS), lambda bi, ti: (bi, ti, 0)),
                  vec, mat(DECAY_LORA), vec, mat(AAA_LORA), mat(GATE_LORA), vec, vec, vec, vec, vec],
        out_specs=pl.BlockSpec((None, tb, w), lambda bi, ti: (bi, ti, 0)),
        out_shape=jax.ShapeDtypeStruct((b, t, w), BF16),
        scratch_shapes=[pltpu.VMEM((RW_HEADS, RW_HEAD_DIM, RW_HEAD_DIM), F32)]
        + [pltpu.VMEM((tb, w), F32)] * 8,
        compiler_params=_params("parallel", "arbitrary"),
        name="rwkv",
    )(rw, w0, w2, a0, a2, g2, k_k, k_a, r_k.reshape(1, w), ln_w, ln_b)


def _pack_rows(x):
    half = x.shape[1] // 2
    hi = pltpu.bitcast(x[:, :half].astype(BF16).astype(F32), jnp.uint32)
    lo = pltpu.bitcast(x[:, half:].astype(BF16).astype(F32), jnp.uint32)
    return hi | (lo >> 16)


def _unpack_rows(u):
    hi = pltpu.bitcast(u & jnp.uint32(0xFFFF0000), F32)
    lo = pltpu.bitcast(u << 16, F32)
    return jnp.concatenate([hi, lo], axis=1)


def _out_kernel(yda_ref, yrw_ref, x_ref, mod_ref, wo_ref, pmn_ref, pfn_ref, rw_ref, rb_ref,
                x1_ref, h2_ref, idx_ref, wgt_ref):
    y = (jnp.dot(yda_ref[...], wo_ref[0:DA_WIDTH, :], preferred_element_type=F32)
         + jnp.dot(yrw_ref[...], wo_ref[DA_WIDTH:, :], preferred_element_type=F32))
    x1 = x_ref[...] + mod_ref[2:3, :] * _rms(y, pmn_ref[...], NORM_EPS)
    x1_ref[...] = x1
    h2 = _rms(x1, pfn_ref[...], NORM_EPS) * (1.0 + mod_ref[4:5, :]) + mod_ref[3:4, :]
    h2_ref[...] = _pack_rows(h2)

    h_hi = h2.astype(BF16)
    h_lo = (h2 - h_hi.astype(F32)).astype(BF16)
    rw = rw_ref[...]
    w_hi = rw.astype(BF16)
    w_lo = (rw - w_hi.astype(F32)).astype(BF16)
    logits = (jnp.dot(h_hi, w_hi, preferred_element_type=F32)
              + jnp.dot(h_hi, w_lo, preferred_element_type=F32)
              + jnp.dot(h_lo, w_hi, preferred_element_type=F32)) + rb_ref[...]

    lane = lax.broadcasted_iota(jnp.int32, logits.shape, 1)
    slot = lax.broadcasted_iota(jnp.int32, idx_ref.shape, 1)
    idx = jnp.zeros(idx_ref.shape, jnp.int32)
    val = jnp.zeros(idx_ref.shape, F32)
    top = None
    for j in range(TOP_K):
        m = jnp.max(logits, axis=-1, keepdims=True)
        i = jnp.min(jnp.where(logits == m, lane, N_EXPERTS), axis=-1, keepdims=True)
        top = m if top is None else top
        idx = jnp.where(slot == j, i, idx)
        val = jnp.where(slot == j, jnp.exp(m - top), val)
        logits = jnp.where(lane == i, -jnp.inf, logits)
    idx_ref[...] = idx
    wgt_ref[...] = val / jnp.sum(val, axis=-1, keepdims=True)


def _out(y_da, y_rw, x, mod3, w_out_b, post_mix_norm, pre_ffn_norm, router_w, router_b):
    b, t, d = x.shape
    tm = min(OUT_ROWS, t)
    e = router_w.shape[1]
    blk = lambda w: pl.BlockSpec((None, tm, w), lambda bi, ti: (bi, ti, 0))
    full = lambda r, c: pl.BlockSpec((r, c), lambda bi, ti: (0, 0))
    return pl.pallas_call(
        _out_kernel,
        grid=(b, t // tm),
        in_specs=[blk(DA_WIDTH), blk(RW_WIDTH), blk(d),
                  pl.BlockSpec((None, N_MOD, d), lambda bi, ti: (bi, 0, 0)),
                  full(d, d), full(1, d), full(1, d), full(d, e), full(1, e)],
        out_specs=[blk(d), blk(d // 2), blk(TOP_K), blk(TOP_K)],
        out_shape=[jax.ShapeDtypeStruct((b, t, d), F32),
                   jax.ShapeDtypeStruct((b, t, d // 2), jnp.uint32),
                   jax.ShapeDtypeStruct((b, t, TOP_K), jnp.int32),
                   jax.ShapeDtypeStruct((b, t, TOP_K), F32)],
        compiler_params=_params("parallel", "parallel"),
        name="out",
    )(y_da, y_rw, x, mod3, w_out_b, post_mix_norm, pre_ffn_norm, router_w,
      router_b.reshape(1, e))


def _route(top_idx, rows_per_tile, n_tiles):
    e_flat = top_idx.reshape(-1)
    onehot = (e_flat[:, None] == jnp.arange(N_EXPERTS, dtype=jnp.int32)[None, :]).astype(jnp.int32)
    csum = jnp.cumsum(onehot, axis=0)
    counts = csum[-1]
    padded = (counts + rows_per_tile - 1) // rows_per_tile * rows_per_tile
    ends = jnp.cumsum(padded)
    starts = ends - padded
    pos = jnp.sum((csum - onehot + starts[None, :]) * onehot, axis=1)
    n_active = ends[-1] // rows_per_tile
    tile_start = jnp.arange(n_tiles, dtype=jnp.int32) * rows_per_tile
    tile = jnp.minimum(tile_start, ends[-1] - 1)
    tile_expert = jnp.sum((tile[:, None] >= ends[None, :]).astype(jnp.int32), axis=1)
    return pos.astype(jnp.int32), tile_expert.astype(jnp.int32), n_active.reshape(1).astype(jnp.int32)


def _dispatch_kernel(pos_ref, h_ref, xs_in_ref, xs_ref, sem):
    del xs_in_ref
    tb = h_ref.shape[0]

    def issue(t, carry):
        for j in range(TOP_K):
            pltpu.make_async_copy(h_ref.at[pl.ds(t, 1)],
                                  xs_ref.at[pl.ds(pos_ref[0, t * TOP_K + j], 1)], sem).start()
        return carry

    lax.fori_loop(0, tb, issue, 0, unroll=4)
    for j in range(TOP_K):
        pltpu.make_async_copy(h_ref, xs_ref.at[pl.ds(0, tb)], sem).wait()


def _dispatch(pos, h2p, n_rows):
    n, w = h2p.shape
    tb = min(DISPATCH_TOKENS, n)
    pos3 = pos.reshape(n // tb, 1, tb * TOP_K)
    xs0 = jnp.zeros((n_rows, w), h2p.dtype)
    return pl.pallas_call(
        _dispatch_kernel,
        grid=(n // tb,),
        in_specs=[pl.BlockSpec((None, 1, tb * TOP_K), lambda i: (i, 0, 0),
                               memory_space=pltpu.SMEM),
                  pl.BlockSpec((tb, w), lambda i: (i, 0)),
                  pl.BlockSpec(memory_space=pl.ANY)],
        out_specs=pl.BlockSpec(memory_space=pl.ANY),
        out_shape=jax.ShapeDtypeStruct((n_rows, w), h2p.dtype),
        scratch_shapes=[pltpu.SemaphoreType.DMA(())],
        input_output_aliases={2: 0},
        compiler_params=_params("arbitrary"),
        name="dispatch",
    )(pos3, h2p, xs0)


def _expert_kernel(te_ref, na_ref, xs_ref, w1_ref, b1_ref, w2_ref, b2_ref, ys_ref,
                   w1p_s, b1p_s, w2b_s, act_s):
    i = pl.program_id(0)
    active = i < na_ref[0]
    fresh = jnp.logical_or(i == 0, te_ref[i] != te_ref[jnp.maximum(i - 1, 0)])
    grp = 2 * LANES
    n_grp = w1_ref.shape[1] // grp

    @pl.when(jnp.logical_and(active, fresh))
    def _():
        src = lax.broadcasted_iota(jnp.int32, (grp, grp), 0)
        dst = lax.broadcasted_iota(jnp.int32, (grp, grp), 1)
        perm = jnp.where(src == jnp.where(dst < LANES, 2 * dst, 2 * (dst - LANES) + 1),
                         1.0, 0.0).astype(BF16)
        for g in range(n_grp):
            sl = slice(g * grp, (g + 1) * grp)
            w1p_s[:, sl] = jnp.dot(w1_ref[:, sl].astype(BF16), perm,
                                   preferred_element_type=F32).astype(BF16)
            b = b1_ref[:, sl]
            b_hi = b.astype(BF16)
            b_lo = (b - b_hi.astype(F32)).astype(BF16)
            b1p_s[:, sl] = (jnp.dot(b_hi, perm, preferred_element_type=F32)
                            + jnp.dot(b_lo, perm, preferred_element_type=F32))
        w2b_s[...] = w2_ref[...].astype(BF16)

    @pl.when(active)
    def _():
        x = _unpack_rows(xs_ref[...]).astype(BF16)
        hid = jnp.dot(x, w1p_s[...], preferred_element_type=F32) + b1p_s[0:1, :]
        for g in range(n_grp):
            glu = jnp.minimum(hid[:, g * grp:g * grp + LANES], SWIGLU_LIMIT)
            lin = jnp.clip(hid[:, g * grp + LANES:(g + 1) * grp], -SWIGLU_LIMIT, SWIGLU_LIMIT)
            act_s[:, g * LANES:(g + 1) * LANES] = (
                glu * jax.nn.sigmoid(SWIGLU_ALPHA * glu) * (lin + 1.0)).astype(BF16)
        y = jnp.dot(act_s[...], w2b_s[...], preferred_element_type=F32) + b2_ref[...]
        ys_ref[...] = _pack_rows(y)

    @pl.when(jnp.logical_not(active))
    def _():
        ys_ref[...] = jnp.zeros_like(ys_ref)


def _experts(tile_expert, n_active, xs, w1, b1, w2, b2):
    n_rows, w = xs.shape
    tm = EXPERT_ROWS
    d, f2 = w1.shape[1], w1.shape[2]
    f = f2 // 2
    wspec = lambda r, c: pl.BlockSpec((None, r, c), lambda i, te, na: (te[i], 0, 0))
    return pl.pallas_call(
        _expert_kernel,
        grid_spec=pltpu.PrefetchScalarGridSpec(
            num_scalar_prefetch=2,
            grid=(n_rows // tm,),
            in_specs=[pl.BlockSpec((tm, w), lambda i, te, na: (i, 0)),
                      wspec(d, f2), wspec(8, f2), wspec(f, d), wspec(1, d)],
            out_specs=pl.BlockSpec((tm, w), lambda i, te, na: (i, 0)),
            scratch_shapes=[pltpu.VMEM((d, f2), BF16), pltpu.VMEM((8, f2), F32),
                            pltpu.VMEM((f, d), BF16), pltpu.VMEM((tm, f), BF16)]),
        out_shape=jax.ShapeDtypeStruct((n_rows, w), jnp.uint32),
        compiler_params=_params("arbitrary"),
        name="expert",
    )(tile_expert, n_active, xs, w1, b1, w2, b2)


def _combine_kernel(pos_ref, ys_ref, wgt_ref, x1_ref, mod_ref, nw_ref, o_ref, buf_ref, sem):
    tc = x1_ref.shape[0]

    def issue(t, carry):
        for j in range(TOP_K):
            pltpu.make_async_copy(ys_ref.at[pl.ds(pos_ref[0, t * TOP_K + j], 1)],
                                  buf_ref.at[pl.ds(j * tc + t, 1)], sem).start()
        return carry

    lax.fori_loop(0, tc, issue, 0, unroll=4)
    for j in range(TOP_K):
        pltpu.make_async_copy(ys_ref.at[pl.ds(0, tc)], buf_ref.at[pl.ds(j * tc, tc)], sem).wait()

    wgt = wgt_ref[...]
    acc = jnp.zeros(x1_ref.shape, F32)
    for j in range(TOP_K):
        rows = _unpack_rows(buf_ref[j * tc:(j + 1) * tc, :])
        acc = acc + wgt[:, j:j + 1] * rows
    o_ref[...] = x1_ref[...] + mod_ref[5:6, :] * _rms(acc, nw_ref[...], NORM_EPS)


def _combine(pos, ys, wgt, x1, mod3, post_ffn_norm):
    b, t, d = x1.shape
    tc = min(COMBINE_TOKENS, t)
    nt = t // tc
    pos3 = pos.reshape(b * nt, 1, tc * TOP_K)
    blk = lambda w: pl.BlockSpec((None, tc, w), lambda bi, ti: (bi, ti, 0))
    return pl.pallas_call(
        _combine_kernel,
        grid=(b, nt),
        in_specs=[pl.BlockSpec((None, 1, tc * TOP_K), lambda bi, ti: (bi * nt + ti, 0, 0),
                               memory_space=pltpu.SMEM),
                  pl.BlockSpec(memory_space=pl.ANY),
                  blk(TOP_K), blk(d),
                  pl.BlockSpec((None, N_MOD, d), lambda bi, ti: (bi, 0, 0)),
                  pl.BlockSpec((1, d), lambda bi, ti: (0, 0))],
        out_specs=blk(d),
        out_shape=jax.ShapeDtypeStruct((b, t, d), F32),
        scratch_shapes=[pltpu.VMEM((tc * TOP_K, d // 2), jnp.uint32),
                        pltpu.SemaphoreType.DMA(())],
        compiler_params=_params("arbitrary", "arbitrary"),
        name="combine",
    )(pos3, ys, wgt, x1, mod3, post_ffn_norm)


def _stages(x, c, positions, ada_w, ada_b, pre_mix_norm, post_mix_norm, pre_ffn_norm,
            post_ffn_norm, w_in, w_out, da_lambda_q1, da_lambda_k1, da_lambda_q2, da_lambda_k2,
            da_subln, rw_mu, rw_w0, rw_w2, rw_a0, rw_a2, rw_g2, rw_k_k, rw_k_a, rw_r_k, rw_ln_w,
            rw_ln_b, router_w, router_b, moe_w1, moe_b1, moe_w2, moe_b2):
    b, t, d = x.shape
    res = {}
    lambda_init = 0.8 - 0.6 * math.exp(-0.3 * 0)
    mod = _mod(c, ada_w[0], ada_b[0])
    res["mod"] = mod
    mod3 = mod.reshape(b, N_MOD, d)
    inv_freq = ROPE_THETA ** (-jnp.arange(0, ROPE_DIM, 2, dtype=F32) / ROPE_DIM)
    invf = jnp.tile(inv_freq, LANES // (ROPE_DIM // 2)).reshape(1, LANES)
    q, k, v, rw = _proj(x, positions.reshape(b, t, 1), mod3, pre_mix_norm, invf,
                        w_in[0].astype(BF16), rw_mu)
    res.update(q=q, k=k, v=v, rw=rw)
    lam4 = jnp.concatenate([da_lambda_q1, da_lambda_k1, da_lambda_q2, da_lambda_k2], axis=0)
    y_da = _attn(q, k, v, lam4, da_subln, lambda_init)
    res["y_da"] = y_da
    y_rw = _rwkv(rw, rw_w0, rw_w2[0], rw_a0, rw_a2[0], rw_g2[0], rw_k_k, rw_k_a, rw_r_k[0],
                 rw_ln_w, rw_ln_b)
    res["y_rw"] = y_rw
    x1, h2p, top_idx, top_w = _out(y_da, y_rw, x, mod3, w_out[0].astype(BF16), post_mix_norm,
                                   pre_ffn_norm, router_w[0], router_b[0])
    res.update(x1=x1, top_idx=top_idx, top_w=top_w)
    n = b * t
    n_tiles = n * TOP_K // EXPERT_ROWS + N_EXPERTS
    pos, tile_expert, n_active = _route(top_idx.reshape(n, TOP_K), EXPERT_ROWS, n_tiles)
    xs = _dispatch(pos, h2p.reshape(n, d // 2), n_tiles * EXPERT_ROWS)
    b1 = jnp.broadcast_to(moe_b1[0][:, None, :], (N_EXPERTS, 8, moe_b1.shape[-1]))
    ys = _experts(tile_expert, n_active, xs, moe_w1[0], b1, moe_w2[0], moe_b2[0][:, None, :])
    res["final"] = _combine(pos, ys, top_w, x1, mod3, post_ffn_norm)
    return res


stages = _stages


def kernel(x, c, positions, ada_w, ada_b, pre_mix_norm, post_mix_norm, pre_ffn_norm, post_ffn_norm, w_in, w_out, da_lambda_q1, da_lambda_k1, da_lambda_q2, da_lambda_k2, da_subln, rw_mu, rw_w0, rw_w2, rw_a0, rw_a2, rw_g2, rw_k_k, rw_k_a, rw_r_k, rw_ln_w, rw_ln_b, router_w, router_b, moe_w1, moe_b1, moe_w2, moe_b2):
    res = _stages(x, c, positions, ada_w, ada_b, pre_mix_norm, post_mix_norm, pre_ffn_norm,
                  post_ffn_norm, w_in, w_out, da_lambda_q1, da_lambda_k1, da_lambda_q2,
                  da_lambda_k2, da_subln, rw_mu, rw_w0, rw_w2, rw_a0, rw_a2, rw_g2, rw_k_k,
                  rw_k_a, rw_r_k, rw_ln_w, rw_ln_b, router_w, router_b, moe_w1, moe_b1,
                  moe_w2, moe_b2)
    return res["final"]
```

```python
import functools
import math

import jax
import jax.numpy as jnp
from jax import lax
from jax.experimental import pallas as pl
from jax.experimental.pallas import tpu as pltpu

F32 = jnp.float32
BF16 = jnp.bfloat16

DA_HEADS = 4
DA_HEAD_DIM = 64
DA_V_DIM = 128
DA_WIDTH = 512
RW_HEADS = 8
RW_HEAD_DIM = 64
RW_WIDTH = 512
DECAY_LORA = 64
AAA_LORA = 64
GATE_LORA = 128
DA_COLS = 1536
RW_COLS = 1792
ROPE_THETA = 500000.0
ROPE_DIM = 16
N_EXPERTS = 32
TOP_K = 4
SWIGLU_ALPHA = 1.702
SWIGLU_LIMIT = 7.0
NORM_EPS = 1e-6
SUBLN_EPS = 1e-5
LN_X_EPS = 64e-5
N_MOD = 6

LANES = 128
VMEM_LIMIT_BYTES = 56 * 1024 * 1024

PROJ_ROWS = 512
ATTN_BLOCK = 256
ATTN_KV_BLOCK = 512
ATTN_HEAD_GROUP = 2
RW_CHUNK = 128
RW_BLOCK = 256
OUT_ROWS = 512
EXPERT_ROWS = 512
DISPATCH_TOKENS = 2048
COMBINE_TOKENS = 1024
COMBINE_SUB_ROWS = 256


def _params(*sem):
    return pltpu.CompilerParams(dimension_semantics=sem, vmem_limit_bytes=VMEM_LIMIT_BYTES)


def _bdot(a, b):
    return jnp.dot(a.astype(BF16), b.astype(BF16), preferred_element_type=F32)


def _bdot_nt(a, b):
    return lax.dot_general(a.astype(BF16), b.astype(BF16), (((1,), (1,)), ((), ())),
                           preferred_element_type=F32)


def _bdot_tn(a, b):
    return lax.dot_general(a.astype(BF16), b.astype(BF16), (((0,), (0,)), ((), ())),
                           preferred_element_type=F32)


def _rms(x, w, eps):
    return x * lax.rsqrt(jnp.mean(x * x, axis=-1, keepdims=True) + eps) * w


def _mod_kernel(c_ref, w_ref, b_ref, o_ref):
    c = c_ref[...]
    s = c * jax.nn.sigmoid(c)
    o_ref[...] = _bdot(s, w_ref[...]) + b_ref[...]


def _mod(c, ada_w, ada_b):
    b, d = c.shape
    n = ada_w.shape[1]
    return pl.pallas_call(
        _mod_kernel,
        grid=(n // d,),
        in_specs=[pl.BlockSpec((b, d), lambda j: (0, 0)),
                  pl.BlockSpec((d, d), lambda j: (0, j)),
                  pl.BlockSpec((1, d), lambda j: (0, j))],
        out_specs=pl.BlockSpec((b, d), lambda j: (0, j)),
        out_shape=jax.ShapeDtypeStruct((b, n), F32),
        compiler_params=_params("parallel"),
        name="mod",
    )(c, ada_w, ada_b.reshape(1, n))


def _proj_kernel(x_ref, pos_ref, mod_ref, nw_ref, invf_ref, w_ref, mu_ref,
                 q_ref, k_ref, v_ref, rw_ref, carry_ref):
    ti = pl.program_id(1)

    @pl.when(ti == 0)
    def _():
        carry_ref[...] = jnp.zeros_like(carry_ref)

    x = x_ref[...]
    h = _rms(x, nw_ref[...], NORM_EPS) * (1.0 + mod_ref[1:2, :]) + mod_ref[0:1, :]
    hb = h.astype(BF16)

    ang = pos_ref[...].astype(F32) * invf_ref[...]
    cos, sin = jnp.cos(ang), jnp.sin(ang)
    l64 = lax.broadcasted_iota(jnp.int32, ang.shape, 1) % DA_HEAD_DIM
    half = ROPE_DIM // 2
    c_tab = jnp.where(l64 < ROPE_DIM, cos, 1.0)
    s_lo = jnp.where(l64 < half, -sin, 0.0)
    s_hi = jnp.where((l64 >= half) & (l64 < ROPE_DIM), sin, 0.0)

    def rope(z):
        up = pltpu.roll(z, LANES - half, axis=1)
        dn = pltpu.roll(z, half, axis=1)
        return z * c_tab + up * s_lo + dn * s_hi

    for g in range(DA_WIDTH // LANES):
        sl = slice(g * LANES, (g + 1) * LANES)
        qg = jnp.dot(hb, w_ref[:, sl], preferred_element_type=F32)
        q_ref[:, sl] = (rope(qg) * (DA_HEAD_DIM ** -0.5)).astype(q_ref.dtype)
        kg = jnp.dot(hb, w_ref[:, DA_WIDTH + g * LANES:DA_WIDTH + (g + 1) * LANES],
                     preferred_element_type=F32)
        k_ref[:, sl] = rope(kg).astype(k_ref.dtype)
    v_ref[...] = jnp.dot(hb, w_ref[:, 2 * DA_WIDTH:DA_COLS],
                         preferred_element_type=F32).astype(v_ref.dtype)

    p = jnp.dot(hb, w_ref[:, DA_COLS:], preferred_element_type=F32)
    rows = p.shape[0]
    prev = pltpu.roll(p, 1, axis=0)
    first = lax.broadcasted_iota(jnp.int32, p.shape, 0) == 0
    prev = jnp.where(first, carry_ref[0:1, :], prev)
    rw_ref[...] = p + (prev - p) * mu_ref[...]
    carry_ref[0:1, :] = p[rows - 1:rows, :]


def _proj(x, pos3, mod3, norm_w, invf, w_in_b, mu):
    b, t, d = x.shape
    tm = min(PROJ_ROWS, t)
    n_in = w_in_b.shape[1]
    blk = lambda w: pl.BlockSpec((None, tm, w), lambda bi, ti: (bi, ti, 0))
    full = lambda r, c: pl.BlockSpec((r, c), lambda bi, ti: (0, 0))
    return pl.pallas_call(
        _proj_kernel,
        grid=(b, t // tm),
        in_specs=[blk(d), blk(1),
                  pl.BlockSpec((None, N_MOD, d), lambda bi, ti: (bi, 0, 0)),
                  full(1, d), full(1, LANES), full(d, n_in), full(1, RW_COLS)],
        out_specs=[blk(DA_WIDTH), blk(DA_WIDTH), blk(DA_WIDTH), blk(RW_COLS)],
        out_shape=[jax.ShapeDtypeStruct((b, t, DA_WIDTH), BF16)] * 3
        + [jax.ShapeDtypeStruct((b, t, RW_COLS), F32)],
        scratch_shapes=[pltpu.VMEM((8, RW_COLS), F32)],
        compiler_params=_params("parallel", "arbitrary"),
        name="proj",
    )(x, pos3, mod3, norm_w, invf, w_in_b, mu)


def _attn_kernel(q_ref, k_ref, v_ref, lam_ref, subln_ref, o_ref, m_ref, l_ref, acc_ref,
                 *, lambda_init):
    qi = pl.program_id(2)
    tq = q_ref.shape[0]
    heads = range(ATTN_HEAD_GROUP)
    hs = [slice(h * DA_V_DIM, (h + 1) * DA_V_DIM) for h in heads]
    lane = lax.broadcasted_iota(jnp.int32, (tq, DA_V_DIM), 1)
    qq = []
    for c in hs:
        q = q_ref[:, c]
        zero = jnp.zeros_like(q)
        qq.append(jnp.concatenate([jnp.where(lane < DA_HEAD_DIM, q, zero),
                                   jnp.where(lane >= DA_HEAD_DIM, q, zero)], axis=0))

    m_ref[...] = jnp.full(m_ref.shape, -jnp.inf, F32)
    l_ref[...] = jnp.zeros(l_ref.shape, F32)
    acc_ref[...] = jnp.zeros(acc_ref.shape, F32)
    tk = ATTN_KV_BLOCK if k_ref.shape[0] % ATTN_KV_BLOCK == 0 else tq
    rep = tk // LANES

    def step(j, masked):
        rows = pl.ds(pl.multiple_of(j * tk, tk), tk)
        s = [lax.dot_general(qq[h], k_ref[rows, hs[h]], (((1,), (1,)), ((), ())),
                             preferred_element_type=F32) for h in heads]
        if masked:
            qpos = qi * tq + lax.broadcasted_iota(jnp.int32, s[0].shape, 0) % tq
            kpos = j * tk + lax.broadcasted_iota(jnp.int32, s[0].shape, 1)
            s = [jnp.where(qpos >= kpos, x, -jnp.inf) for x in s]
        for h in heads:
            m_old = m_ref[h]
            m_new = jnp.maximum(m_old, jnp.max(s[h], axis=-1, keepdims=True))
            alpha = jnp.exp(m_old - m_new)
            p = jnp.exp(s[h] - jnp.concatenate([m_new] * rep, axis=1))
            l_ref[h] = alpha * l_ref[h] + jnp.sum(p, axis=-1, keepdims=True)
            acc_ref[h] = alpha * acc_ref[h] + jnp.dot(p.astype(v_ref.dtype), v_ref[rows, hs[h]],
                                                      preferred_element_type=F32)
            m_ref[h] = m_new

    def body(j, carry):
        step(j, False)
        return carry

    n_full = (qi * tq) // tk
    lax.fori_loop(0, n_full, body, 0)
    step(n_full, True)

    lam = (jnp.exp(jnp.sum(lam_ref[0:1, :] * lam_ref[1:2, :], axis=-1, keepdims=True))
           - jnp.exp(jnp.sum(lam_ref[2:3, :] * lam_ref[3:4, :], axis=-1, keepdims=True))
           + lambda_init)
    for h in heads:
        o = acc_ref[h] / l_ref[h]
        d = o[:tq, :] - lam * o[tq:, :]
        o_ref[:, hs[h]] = (_rms(d, subln_ref[...], SUBLN_EPS)
                           * (1.0 - lambda_init)).astype(o_ref.dtype)


def _attn(q, k, v, lam4, subln, lambda_init):
    b, t, _ = q.shape
    tq = min(ATTN_BLOCK, t)
    hg = ATTN_HEAD_GROUP
    gw = hg * DA_V_DIM
    return pl.pallas_call(
        functools.partial(_attn_kernel, lambda_init=lambda_init),
        grid=(b, DA_HEADS // hg, t // tq),
        in_specs=[pl.BlockSpec((None, tq, gw), lambda bi, h, qi: (bi, qi, h)),
                  pl.BlockSpec((None, t, gw), lambda bi, h, qi: (bi, 0, h)),
                  pl.BlockSpec((None, t, gw), lambda bi, h, qi: (bi, 0, h)),
                  pl.BlockSpec((4, DA_HEAD_DIM), lambda bi, h, qi: (0, 0)),
                  pl.BlockSpec((1, DA_V_DIM), lambda bi, h, qi: (0, 0))],
        out_specs=pl.BlockSpec((None, tq, gw), lambda bi, h, qi: (bi, qi, h)),
        out_shape=jax.ShapeDtypeStruct((b, t, DA_WIDTH), BF16),
        scratch_shapes=[pltpu.VMEM((hg, 2 * tq, LANES), F32), pltpu.VMEM((hg, 2 * tq, LANES), F32),
                        pltpu.VMEM((hg, 2 * tq, DA_V_DIM), F32)],
        compiler_params=_params("parallel", "parallel", "arbitrary"),
        name="attn",
    )(q, k, v, lam4, subln)


def _rwkv_kernel(rw_ref, w0_ref, w2_ref, a0_ref, a2_ref, g2_ref, kk_ref, ka_ref, rk_ref,
                 lnw_ref, lnb_ref, o_ref, state_ref, r_s, k_s, v_s, lw_s, kk_s, a_s, g_s, cum_s):
    ti = pl.program_id(1)

    @pl.when(ti == 0)
    def _():
        state_ref[...] = jnp.zeros_like(state_ref)

    w = RW_WIDTH
    rw = rw_ref[...]
    k = rw[:, w:2 * w]
    wl = rw[:, 3 * w:3 * w + DECAY_LORA]
    al = rw[:, 3 * w + DECAY_LORA:3 * w + DECAY_LORA + AAA_LORA]
    gl = rw[:, 3 * w + DECAY_LORA + AAA_LORA:]
    z = -(w0_ref[...] + _bdot(jnp.tanh(wl), w2_ref[...]))
    softplus = jnp.maximum(z, 0.0) + jnp.log(1.0 + jnp.exp(-jnp.abs(z)))
    a = jax.nn.sigmoid(a0_ref[...] + _bdot(al, a2_ref[...]))
    r_s[...] = rw[:, 0:w]
    v_s[...] = rw[:, 2 * w:3 * w]
    lw_s[...] = -jnp.exp(-softplus - 0.5)
    a_s[...] = a
    g_s[...] = _bdot(jax.nn.sigmoid(gl), g2_ref[...])
    kk_s[...] = k * kk_ref[...]
    k_s[...] = k * (1.0 + (a - 1.0) * ka_ref[...])

    c_len = RW_CHUNK
    n = RW_HEAD_DIM
    tb = rw_ref.shape[0]

    br = lax.broadcasted_iota(jnp.int32, (tb, tb), 0)
    bc = lax.broadcasted_iota(jnp.int32, (tb, tb), 1)
    tri = jnp.where((br >= bc) & (br // c_len == bc // c_len), 1.0, 0.0).astype(BF16)
    lw_all = lw_s[...]
    lw_hi = lw_all.astype(BF16)
    rem = lw_all - lw_hi.astype(F32)
    lw_mid = rem.astype(BF16)
    lw_lo = (rem - lw_mid.astype(F32)).astype(BF16)
    cum_s[...] = (jnp.dot(tri, lw_hi, preferred_element_type=F32)
                  + jnp.dot(tri, lw_mid, preferred_element_type=F32)
                  + jnp.dot(tri, lw_lo, preferred_element_type=F32))

    row = lax.broadcasted_iota(jnp.int32, (c_len, 2 * c_len), 0)
    col = lax.broadcasted_iota(jnp.int32, (c_len, 2 * c_len), 1)
    incl2 = row >= col % c_len
    strict2 = row > col % c_len
    eye = jnp.where(lax.broadcasted_iota(jnp.int32, (c_len, c_len), 0)
                    == lax.broadcasted_iota(jnp.int32, (c_len, c_len), 1), 1.0, 0.0).astype(F32)

    def chunk(ci, carry):
        rows = pl.ds(pl.multiple_of(ci * c_len, c_len), c_len)
        heads = range(RW_HEADS)
        sl = [slice(h * n, (h + 1) * n) for h in heads]
        r = [r_s[rows, c] for c in sl]
        kh = [k_s[rows, c] for c in sl]
        v = [v_s[rows, c] for c in sl]
        lw = [lw_s[rows, c] for c in sl]
        cum = [cum_s[rows, c] for c in sl]
        kk = [kk_s[rows, c] for c in sl]
        kk = [x / jnp.maximum(jnp.sqrt(jnp.sum(x * x, axis=-1, keepdims=True)), 1e-12) for x in kk]
        kka = [kk[h] * a_s[rows, sl[h]] for h in heads]
        end = [jnp.sum(x, axis=0, keepdims=True) for x in lw]
        e_neg = [jnp.exp(-x) for x in cum]
        e_end = [jnp.exp(end[h] - cum[h]) for h in heads]
        left = [jnp.concatenate([-kk[h] * jnp.exp(cum[h] - lw[h]), r[h] * jnp.exp(cum[h])], axis=0)
                for h in heads]
        g = [_bdot_nt(left[h], jnp.concatenate([kka[h] * e_neg[h], kh[h] * e_neg[h]], axis=0))
             for h in heads]
        a_a = [jnp.where(strict2, x[:c_len, :], 0.0) for x in g]
        a_r = [jnp.where(incl2, x[c_len:, :], 0.0) for x in g]
        pw = [x[:, :c_len] for x in a_a]
        inv = [eye + x for x in pw]
        for _ in range(c_len.bit_length() - 2):
            pw = [_bdot(x, x) for x in pw]
            inv = [inv[h] + _bdot(inv[h], pw[h]) for h in heads]
        akv = [_bdot(a_a[h][:, c_len:], v[h]) for h in heads]
        s0 = [state_ref[h] for h in heads]
        ls = [_bdot_nt(left[h], s0[h]) for h in heads]
        u = [_bdot(inv[h], ls[h][:c_len, :] + akv[h]) for h in heads]
        uv = [jnp.concatenate([u[h], v[h]], axis=0) for h in heads]
        y = [ls[h][c_len:, :] + _bdot(a_r[h], uv[h]) for h in heads]
        for h in heads:
            state_ref[h] = s0[h] * jnp.exp(end[h]) + _bdot_tn(
                uv[h], jnp.concatenate([kka[h] * e_end[h], kh[h] * e_end[h]], axis=0))
        for h in heads:
            mean = jnp.mean(y[h], axis=-1, keepdims=True)
            yc = y[h] - mean
            var = jnp.mean(yc * yc, axis=-1, keepdims=True)
            yn = yc * lax.rsqrt(var + LN_X_EPS) * lnw_ref[:, sl[h]] + lnb_ref[:, sl[h]]
            bonus = jnp.sum(r[h] * kh[h] * rk_ref[:, sl[h]], axis=-1, keepdims=True) * v[h]
            o_ref[rows, sl[h]] = ((yn + bonus) * g_s[rows, sl[h]]).astype(o_ref.dtype)
        return carry

    lax.fori_loop(0, rw_ref.shape[0] // c_len, chunk, 0, unroll=2)


def _rwkv(rw, w0, w2, a0, a2, g2, k_k, k_a, r_k, ln_w, ln_b):
    b, t, _ = rw.shape
    tb = min(RW_BLOCK, t)
    w = RW_WIDTH
    vec = pl.BlockSpec((1, w), lambda bi, ti: (0, 0))
    mat = lambda r: pl.BlockSpec((r, w), lambda bi, ti: (0, 0))
    return pl.pallas_call(
        _rwkv_kernel,
        grid=(b, t // tb),
        in_specs=[pl.BlockSpec((None, tb, RW_COLS), lambda bi, ti: (bi, ti, 0)),
                  vec, mat(DECAY_LORA), vec, mat(AAA_LORA), mat(GATE_LORA), vec, vec, vec, vec, vec],
        out_specs=pl.BlockSpec((None, tb, w), lambda bi, ti: (bi, ti, 0)),
        out_shape=jax.ShapeDtypeStruct((b, t, w), BF16),
        scratch_shapes=[pltpu.VMEM((RW_HEADS, RW_HEAD_DIM, RW_HEAD_DIM), F32)]
        + [pltpu.VMEM((tb, w), F32)] * 8,
        compiler_params=_params("parallel", "arbitrary"),
        name="rwkv",
    )(rw, w0, w2, a0, a2, g2, k_k, k_a, r_k.reshape(1, w), ln_w, ln_b)


def _pack_rows(x):
    half = x.shape[1] // 2
    hi = pltpu.bitcast(x[:, :half].astype(BF16).astype(F32), jnp.uint32)
    lo = pltpu.bitcast(x[:, half:].astype(BF16).astype(F32), jnp.uint32)
    return hi | (lo >> 16)


def _unpack_rows(u):
    hi = pltpu.bitcast(u & jnp.uint32(0xFFFF0000), F32)
    lo = pltpu.bitcast(u << 16, F32)
    return jnp.concatenate([hi, lo], axis=1)


def _out_kernel(yda_ref, yrw_ref, x_ref, mod_ref, wo_ref, pmn_ref, pfn_ref, rw_ref, rb_ref,
                x1_ref, h2_ref, idx_ref, wgt_ref):
    y = (jnp.dot(yda_ref[...], wo_ref[0:DA_WIDTH, :], preferred_element_type=F32)
         + jnp.dot(yrw_ref[...], wo_ref[DA_WIDTH:, :], preferred_element_type=F32))
    x1 = x_ref[...] + mod_ref[2:3, :] * _rms(y, pmn_ref[...], NORM_EPS)
    x1_ref[...] = x1
    h2 = _rms(x1, pfn_ref[...], NORM_EPS) * (1.0 + mod_ref[4:5, :]) + mod_ref[3:4, :]
    h2_ref[...] = _pack_rows(h2)

    h_hi = h2.astype(BF16)
    h_lo = (h2 - h_hi.astype(F32)).astype(BF16)
    rw = rw_ref[...]
    w_hi = rw.astype(BF16)
    w_lo = (rw - w_hi.astype(F32)).astype(BF16)
    logits = (jnp.dot(h_hi, w_hi, preferred_element_type=F32)
              + jnp.dot(h_hi, w_lo, preferred_element_type=F32)
              + jnp.dot(h_lo, w_hi, preferred_element_type=F32)) + rb_ref[...]

    lane = lax.broadcasted_iota(jnp.int32, logits.shape, 1)
    slot = lax.broadcasted_iota(jnp.int32, idx_ref.shape, 1)
    idx = jnp.zeros(idx_ref.shape, jnp.int32)
    val = jnp.zeros(idx_ref.shape, F32)
    top = None
    for j in range(TOP_K):
        m = jnp.max(logits, axis=-1, keepdims=True)
        i = jnp.min(jnp.where(logits == m, lane, N_EXPERTS), axis=-1, keepdims=True)
        top = m if top is None else top
        idx = jnp.where(slot == j, i, idx)
        val = jnp.where(slot == j, jnp.exp(m - top), val)
        logits = jnp.where(lane == i, -jnp.inf, logits)
    idx_ref[...] = idx
    wgt_ref[...] = val / jnp.sum(val, axis=-1, keepdims=True)


def _out(y_da, y_rw, x, mod3, w_out_b, post_mix_norm, pre_ffn_norm, router_w, router_b):
    b, t, d = x.shape
    tm = min(OUT_ROWS, t)
    e = router_w.shape[1]
    blk = lambda w: pl.BlockSpec((None, tm, w), lambda bi, ti: (bi, ti, 0))
    full = lambda r, c: pl.BlockSpec((r, c), lambda bi, ti: (0, 0))
    return pl.pallas_call(
        _out_kernel,
        grid=(b, t // tm),
        in_specs=[blk(DA_WIDTH), blk(RW_WIDTH), blk(d),
                  pl.BlockSpec((None, N_MOD, d), lambda bi, ti: (bi, 0, 0)),
                  full(d, d), full(1, d), full(1, d), full(d, e), full(1, e)],
        out_specs=[blk(d), blk(d // 2), blk(TOP_K), blk(TOP_K)],
        out_shape=[jax.ShapeDtypeStruct((b, t, d), F32),
                   jax.ShapeDtypeStruct((b, t, d // 2), jnp.uint32),
                   jax.ShapeDtypeStruct((b, t, TOP_K), jnp.int32),
                   jax.ShapeDtypeStruct((b, t, TOP_K), F32)],
        compiler_params=_params("parallel", "parallel"),
        name="out",
    )(y_da, y_rw, x, mod3, w_out_b, post_mix_norm, pre_ffn_norm, router_w,
      router_b.reshape(1, e))


def _route(top_idx, rows_per_tile, n_tiles):
    e_flat = top_idx.reshape(-1)
    onehot = (e_flat[:, None] == jnp.arange(N_EXPERTS, dtype=jnp.int32)[None, :]).astype(jnp.int32)
    csum = jnp.cumsum(onehot, axis=0)
    counts = csum[-1]
    padded = (counts + rows_per_tile - 1) // rows_per_tile * rows_per_tile
    ends = jnp.cumsum(padded)
    starts = ends - padded
    pos = jnp.sum((csum - onehot + starts[None, :]) * onehot, axis=1)
    n_active = ends[-1] // rows_per_tile
    tile_start = jnp.arange(n_tiles, dtype=jnp.int32) * rows_per_tile
    tile = jnp.minimum(tile_start, ends[-1] - 1)
    tile_expert = jnp.sum((tile[:, None] >= ends[None, :]).astype(jnp.int32), axis=1)
    return pos.astype(jnp.int32), tile_expert.astype(jnp.int32), n_active.reshape(1).astype(jnp.int32)


def _dispatch_kernel(pos_ref, h_ref, xs_in_ref, xs_ref, sem):
    del xs_in_ref
    tb = h_ref.shape[0]

    def issue(t, carry):
        for j in range(TOP_K):
            pltpu.make_async_copy(h_ref.at[pl.ds(t, 1)],
                                  xs_ref.at[pl.ds(pos_ref[0, t * TOP_K + j], 1)], sem).start()
        return carry

    lax.fori_loop(0, tb, issue, 0, unroll=4)
    for j in range(TOP_K):
        pltpu.make_async_copy(h_ref, xs_ref.at[pl.ds(0, tb)], sem).wait()


def _dispatch(pos, h2p, n_rows):
    n, w = h2p.shape
    tb = min(DISPATCH_TOKENS, n)
    pos3 = pos.reshape(n // tb, 1, tb * TOP_K)
    xs0 = jnp.zeros((n_rows, w), h2p.dtype)
    return pl.pallas_call(
        _dispatch_kernel,
        grid=(n // tb,),
        in_specs=[pl.BlockSpec((None, 1, tb * TOP_K), lambda i: (i, 0, 0),
                               memory_space=pltpu.SMEM),
                  pl.BlockSpec((tb, w), lambda i: (i, 0)),
                  pl.BlockSpec(memory_space=pl.ANY)],
        out_specs=pl.BlockSpec(memory_space=pl.ANY),
        out_shape=jax.ShapeDtypeStruct((n_rows, w), h2p.dtype),
        scratch_shapes=[pltpu.SemaphoreType.DMA(())],
        input_output_aliases={2: 0},
        compiler_params=_params("arbitrary"),
        name="dispatch",
    )(pos3, h2p, xs0)


def _expert_kernel(te_ref, na_ref, xs_ref, w1_ref, b1_ref, w2_ref, b2_ref, ys_ref,
                   w1p_s, b1p_s, w2b_s, act_s):
    i = pl.program_id(0)
    active = i < na_ref[0]
    fresh = jnp.logical_or(i == 0, te_ref[i] != te_ref[jnp.maximum(i - 1, 0)])
    grp = 2 * LANES
    n_grp = w1_ref.shape[1] // grp

    @pl.when(jnp.logical_and(active, fresh))
    def _():
        src = lax.broadcasted_iota(jnp.int32, (grp, grp), 0)
        dst = lax.broadcasted_iota(jnp.int32, (grp, grp), 1)
        perm = jnp.where(src == jnp.where(dst < LANES, 2 * dst, 2 * (dst - LANES) + 1),
                         1.0, 0.0).astype(BF16)
        for g in range(n_grp):
            sl = slice(g * grp, (g + 1) * grp)
            w1p_s[:, sl] = jnp.dot(w1_ref[:, sl].astype(BF16), perm,
                                   preferred_element_type=F32).astype(BF16)
            b = b1_ref[:, sl]
            b_hi = b.astype(BF16)
            b_lo = (b - b_hi.astype(F32)).astype(BF16)
            b1p_s[:, sl] = (jnp.dot(b_hi, perm, preferred_element_type=F32)
                            + jnp.dot(b_lo, perm, preferred_element_type=F32))
        w2b_s[...] = w2_ref[...].astype(BF16)

    @pl.when(active)
    def _():
        x = _unpack_rows(xs_ref[...]).astype(BF16)
        hid = jnp.dot(x, w1p_s[...], preferred_element_type=F32) + b1p_s[0:1, :]
        for g in range(n_grp):
            glu = jnp.minimum(hid[:, g * grp:g * grp + LANES], SWIGLU_LIMIT)
            lin = jnp.clip(hid[:, g * grp + LANES:(g + 1) * grp], -SWIGLU_LIMIT, SWIGLU_LIMIT)
            act_s[:, g * LANES:(g + 1) * LANES] = (
                glu * jax.nn.sigmoid(SWIGLU_ALPHA * glu) * (lin + 1.0)).astype(BF16)
        y = jnp.dot(act_s[...], w2b_s[...], preferred_element_type=F32) + b2_ref[...]
        ys_ref[...] = _pack_rows(y)

    @pl.when(jnp.logical_not(active))
    def _():
        ys_ref[...] = jnp.zeros_like(ys_ref)


def _experts(tile_expert, n_active, xs, w1, b1, w2, b2):
    n_rows, w = xs.shape
    tm = EXPERT_ROWS
    d, f2 = w1.shape[1], w1.shape[2]
    f = f2 // 2
    wspec = lambda r, c: pl.BlockSpec((None, r, c), lambda i, te, na: (te[i], 0, 0))
    return pl.pallas_call(
        _expert_kernel,
        grid_spec=pltpu.PrefetchScalarGridSpec(
            num_scalar_prefetch=2,
            grid=(n_rows // tm,),
            in_specs=[pl.BlockSpec((tm, w), lambda i, te, na: (i, 0)),
                      wspec(d, f2), wspec(8, f2), wspec(f, d), wspec(1, d)],
            out_specs=pl.BlockSpec((tm, w), lambda i, te, na: (i, 0)),
            scratch_shapes=[pltpu.VMEM((d, f2), BF16), pltpu.VMEM((8, f2), F32),
                            pltpu.VMEM((f, d), BF16), pltpu.VMEM((tm, f), BF16)]),
        out_shape=jax.ShapeDtypeStruct((n_rows, w), jnp.uint32),
        compiler_params=_params("arbitrary"),
        name="expert",
    )(tile_expert, n_active, xs, w1, b1, w2, b2)


def _combine_kernel(pos_ref, ys_ref, wgt_ref, x1_ref, mod_ref, nw_ref, o_ref, buf_ref, sem):
    tc = x1_ref.shape[0]

    def issue(t, carry):
        for j in range(TOP_K):
            pltpu.make_async_copy(ys_ref.at[pl.ds(pos_ref[0, t * TOP_K + j], 1)],
                                  buf_ref.at[pl.ds(j * tc + t, 1)], sem).start()
        return carry

    lax.fori_loop(0, tc, issue, 0, unroll=4)
    for j in range(TOP_K):
        pltpu.make_async_copy(ys_ref.at[pl.ds(0, tc)], buf_ref.at[pl.ds(j * tc, tc)], sem).wait()

    sub = min(COMBINE_SUB_ROWS, tc)

    def mix(i, carry):
        r0 = pl.multiple_of(i * sub, sub)
        wgt = wgt_ref[pl.ds(r0, sub), :]
        acc = jnp.zeros((sub, x1_ref.shape[1]), F32)
        for j in range(TOP_K):
            acc = acc + wgt[:, j:j + 1] * _unpack_rows(buf_ref[pl.ds(j * tc + r0, sub), :])
        o_ref[pl.ds(r0, sub), :] = (x1_ref[pl.ds(r0, sub), :]
                                    + mod_ref[5:6, :] * _rms(acc, nw_ref[...], NORM_EPS))
        return carry

    lax.fori_loop(0, tc // sub, mix, 0)


def _combine(pos, ys, wgt, x1, mod3, post_ffn_norm):
    b, t, d = x1.shape
    tc = min(COMBINE_TOKENS, t)
    nt = t // tc
    pos3 = pos.reshape(b * nt, 1, tc * TOP_K)
    blk = lambda w: pl.BlockSpec((None, tc, w), lambda bi, ti: (bi, ti, 0))
    return pl.pallas_call(
        _combine_kernel,
        grid=(b, nt),
        in_specs=[pl.BlockSpec((None, 1, tc * TOP_K), lambda bi, ti: (bi * nt + ti, 0, 0),
                               memory_space=pltpu.SMEM),
                  pl.BlockSpec(memory_space=pl.ANY),
                  blk(TOP_K), blk(d),
                  pl.BlockSpec((None, N_MOD, d), lambda bi, ti: (bi, 0, 0)),
                  pl.BlockSpec((1, d), lambda bi, ti: (0, 0))],
        out_specs=blk(d),
        out_shape=jax.ShapeDtypeStruct((b, t, d), F32),
        scratch_shapes=[pltpu.VMEM((tc * TOP_K, d // 2), jnp.uint32),
                        pltpu.SemaphoreType.DMA(())],
        compiler_params=_params("arbitrary", "arbitrary"),
        name="combine",
    )(pos3, ys, wgt, x1, mod3, post_ffn_norm)


def _stages(x, c, positions, ada_w, ada_b, pre_mix_norm, post_mix_norm, pre_ffn_norm,
            post_ffn_norm, w_in, w_out, da_lambda_q1, da_lambda_k1, da_lambda_q2, da_lambda_k2,
            da_subln, rw_mu, rw_w0, rw_w2, rw_a0, rw_a2, rw_g2, rw_k_k, rw_k_a, rw_r_k, rw_ln_w,
            rw_ln_b, router_w, router_b, moe_w1, moe_b1, moe_w2, moe_b2):
    b, t, d = x.shape
    res = {}
    lambda_init = 0.8 - 0.6 * math.exp(-0.3 * 0)
    mod = _mod(c, ada_w[0], ada_b[0])
    res["mod"] = mod
    mod3 = mod.reshape(b, N_MOD, d)
    inv_freq = ROPE_THETA ** (-jnp.arange(0, ROPE_DIM, 2, dtype=F32) / ROPE_DIM)
    invf = jnp.tile(inv_freq, LANES // (ROPE_DIM // 2)).reshape(1, LANES)
    q, k, v, rw = _proj(x, positions.reshape(b, t, 1), mod3, pre_mix_norm, invf,
                        w_in[0].astype(BF16), rw_mu)
    res.update(q=q, k=k, v=v, rw=rw)
    lam4 = jnp.concatenate([da_lambda_q1, da_lambda_k1, da_lambda_q2, da_lambda_k2], axis=0)
    y_da = _attn(q, k, v, lam4, da_subln, lambda_init)
    res["y_da"] = y_da
    y_rw = _rwkv(rw, rw_w0, rw_w2[0], rw_a0, rw_a2[0], rw_g2[0], rw_k_k, rw_k_a, rw_r_k[0],
                 rw_ln_w, rw_ln_b)
    res["y_rw"] = y_rw
    x1, h2p, top_idx, top_w = _out(y_da, y_rw, x, mod3, w_out[0].astype(BF16), post_mix_norm,
                                   pre_ffn_norm, router_w[0], router_b[0])
    res.update(x1=x1, top_idx=top_idx, top_w=top_w)
    n = b * t
    n_tiles = n * TOP_K // EXPERT_ROWS + N_EXPERTS
    pos, tile_expert, n_active = _route(top_idx.reshape(n, TOP_K), EXPERT_ROWS, n_tiles)
    xs = _dispatch(pos, h2p.reshape(n, d // 2), n_tiles * EXPERT_ROWS)
    b1 = jnp.broadcast_to(moe_b1[0][:, None, :], (N_EXPERTS, 8, moe_b1.shape[-1]))
    ys = _experts(tile_expert, n_active, xs, moe_w1[0], b1, moe_w2[0], moe_b2[0][:, None, :])
    res["final"] = _combine(pos, ys, top_w, x1, mod3, post_ffn_norm)
    return res


stages = _stages


def kernel(x, c, positions, ada_w, ada_b, pre_mix_norm, post_mix_norm, pre_ffn_norm, post_ffn_norm, w_in, w_out, da_lambda_q1, da_lambda_k1, da_lambda_q2, da_lambda_k2, da_subln, rw_mu, rw_w0, rw_w2, rw_a0, rw_a2, rw_g2, rw_k_k, rw_k_a, rw_r_k, rw_ln_w, rw_ln_b, router_w, router_b, moe_w1, moe_b1, moe_w2, moe_b2):
    res = _stages(x, c, positions, ada_w, ada_b, pre_mix_norm, post_mix_norm, pre_ffn_norm,
                  post_ffn_norm, w_in, w_out, da_lambda_q1, da_lambda_k1, da_lambda_q2,
                  da_lambda_k2, da_subln, rw_mu, rw_w0, rw_w2, rw_a0, rw_a2, rw_g2, rw_k_k,
                  rw_k_a, rw_r_k, rw_ln_w, rw_ln_b, router_w, router_b, moe_w1, moe_b1,
                  moe_w2, moe_b2)
    return res["final"]
```

```python
import functools
import math

import jax
import jax.numpy as jnp
from jax import lax
from jax.experimental import pallas as pl
from jax.experimental.pallas import tpu as pltpu

F32 = jnp.float32
BF16 = jnp.bfloat16

DA_HEADS = 4
DA_HEAD_DIM = 64
DA_V_DIM = 128
DA_WIDTH = 512
RW_HEADS = 8
RW_HEAD_DIM = 64
RW_WIDTH = 512
DECAY_LORA = 64
AAA_LORA = 64
GATE_LORA = 128
DA_COLS = 1536
RW_COLS = 1792
ROPE_THETA = 500000.0
ROPE_DIM = 16
N_EXPERTS = 32
TOP_K = 4
SWIGLU_ALPHA = 1.702
SWIGLU_LIMIT = 7.0
NORM_EPS = 1e-6
SUBLN_EPS = 1e-5
LN_X_EPS = 64e-5
N_MOD = 6

LANES = 128
SUBLANES = 8
VMEM_LIMIT_BYTES = 56 * 1024 * 1024

PROJ_ROWS = 512
ATTN_BLOCK = 256
ATTN_KV_BLOCK = 512
ATTN_HEAD_GROUP = 2
RW_CHUNK = 128
RW_BLOCK = 256
OUT_ROWS = 512
EXPERT_ROWS = 512
DISPATCH_TOKENS = 2048
COMBINE_TOKENS = 1024
COMBINE_SUB_ROWS = 256


def _params(*sem):
    return pltpu.CompilerParams(dimension_semantics=sem, vmem_limit_bytes=VMEM_LIMIT_BYTES)


def _bdot(a, b):
    return jnp.dot(a.astype(BF16), b.astype(BF16), preferred_element_type=F32)


def _bdot_nt(a, b):
    return lax.dot_general(a.astype(BF16), b.astype(BF16), (((1,), (1,)), ((), ())),
                           preferred_element_type=F32)


def _bdot_tn(a, b):
    return lax.dot_general(a.astype(BF16), b.astype(BF16), (((0,), (0,)), ((), ())),
                           preferred_element_type=F32)


def _rms(x, w, eps):
    return x * lax.rsqrt(jnp.mean(x * x, axis=-1, keepdims=True) + eps) * w


def _mod_kernel(c_ref, w_ref, b_ref, o_ref):
    c = c_ref[...]
    s = c * jax.nn.sigmoid(c)
    o_ref[...] = _bdot(s, w_ref[...]) + b_ref[...]


def _mod(c, ada_w, ada_b):
    b, d = c.shape
    n = ada_w.shape[1]
    return pl.pallas_call(
        _mod_kernel,
        grid=(n // d,),
        in_specs=[pl.BlockSpec((b, d), lambda j: (0, 0)),
                  pl.BlockSpec((d, d), lambda j: (0, j)),
                  pl.BlockSpec((1, d), lambda j: (0, j))],
        out_specs=pl.BlockSpec((b, d), lambda j: (0, j)),
        out_shape=jax.ShapeDtypeStruct((b, n), F32),
        compiler_params=_params("parallel"),
        name="mod",
    )(c, ada_w, ada_b.reshape(1, n))


def _proj_kernel(x_ref, pos_ref, mod_ref, nw_ref, invf_ref, w_ref, mu_ref,
                 q_ref, k_ref, v_ref, rw_ref, carry_ref):
    ti = pl.program_id(1)

    @pl.when(ti == 0)
    def _():
        carry_ref[...] = jnp.zeros_like(carry_ref)

    x = x_ref[...]
    h = _rms(x, nw_ref[...], NORM_EPS) * (1.0 + mod_ref[1:2, :]) + mod_ref[0:1, :]
    hb = h.astype(BF16)

    ang = pos_ref[...].astype(F32) * invf_ref[...]
    cos, sin = jnp.cos(ang), jnp.sin(ang)
    l64 = lax.broadcasted_iota(jnp.int32, ang.shape, 1) % DA_HEAD_DIM
    half = ROPE_DIM // 2
    c_tab = jnp.where(l64 < ROPE_DIM, cos, 1.0)
    s_lo = jnp.where(l64 < half, -sin, 0.0)
    s_hi = jnp.where((l64 >= half) & (l64 < ROPE_DIM), sin, 0.0)

    def rope(z):
        up = pltpu.roll(z, LANES - half, axis=1)
        dn = pltpu.roll(z, half, axis=1)
        return z * c_tab + up * s_lo + dn * s_hi

    for g in range(DA_WIDTH // LANES):
        sl = slice(g * LANES, (g + 1) * LANES)
        qg = jnp.dot(hb, w_ref[:, sl], preferred_element_type=F32)
        q_ref[:, sl] = (rope(qg) * (DA_HEAD_DIM ** -0.5)).astype(q_ref.dtype)
        kg = jnp.dot(hb, w_ref[:, DA_WIDTH + g * LANES:DA_WIDTH + (g + 1) * LANES],
                     preferred_element_type=F32)
        k_ref[:, sl] = rope(kg).astype(k_ref.dtype)
    v_ref[...] = jnp.dot(hb, w_ref[:, 2 * DA_WIDTH:DA_COLS],
                         preferred_element_type=F32).astype(v_ref.dtype)

    p = jnp.dot(hb, w_ref[:, DA_COLS:], preferred_element_type=F32)
    rows = p.shape[0]
    prev = pltpu.roll(p, 1, axis=0)
    first = lax.broadcasted_iota(jnp.int32, p.shape, 0) == 0
    prev = jnp.where(first, carry_ref[0:1, :], prev)
    rw_ref[...] = p + (prev - p) * mu_ref[...]
    carry_ref[0:1, :] = p[rows - 1:rows, :]


def _proj(x, pos3, mod3, norm_w, invf, w_in_b, mu):
    b, t, d = x.shape
    tm = min(PROJ_ROWS, t)
    n_in = w_in_b.shape[1]
    blk = lambda w: pl.BlockSpec((None, tm, w), lambda bi, ti: (bi, ti, 0))
    full = lambda r, c: pl.BlockSpec((r, c), lambda bi, ti: (0, 0))
    return pl.pallas_call(
        _proj_kernel,
        grid=(b, t // tm),
        in_specs=[blk(d), blk(1),
                  pl.BlockSpec((None, N_MOD, d), lambda bi, ti: (bi, 0, 0)),
                  full(1, d), full(1, LANES), full(d, n_in), full(1, RW_COLS)],
        out_specs=[blk(DA_WIDTH), blk(DA_WIDTH), blk(DA_WIDTH), blk(RW_COLS)],
        out_shape=[jax.ShapeDtypeStruct((b, t, DA_WIDTH), BF16)] * 3
        + [jax.ShapeDtypeStruct((b, t, RW_COLS), F32)],
        scratch_shapes=[pltpu.VMEM((8, RW_COLS), F32)],
        compiler_params=_params("parallel", "arbitrary"),
        name="proj",
    )(x, pos3, mod3, norm_w, invf, w_in_b, mu)


def _attn_kernel(q_ref, k_ref, v_ref, lam_ref, subln_ref, o_ref, m_ref, l_ref, acc_ref,
                 *, lambda_init):
    qi = pl.program_id(2)
    tq = q_ref.shape[0]
    heads = range(ATTN_HEAD_GROUP)
    hs = [slice(h * DA_V_DIM, (h + 1) * DA_V_DIM) for h in heads]
    lane = lax.broadcasted_iota(jnp.int32, (tq, DA_V_DIM), 1)
    qq = []
    for c in hs:
        q = q_ref[:, c]
        zero = jnp.zeros_like(q)
        qq.append(jnp.concatenate([jnp.where(lane < DA_HEAD_DIM, q, zero),
                                   jnp.where(lane >= DA_HEAD_DIM, q, zero)], axis=0))

    m_ref[...] = jnp.full(m_ref.shape, -jnp.inf, F32)
    l_ref[...] = jnp.zeros(l_ref.shape, F32)
    acc_ref[...] = jnp.zeros(acc_ref.shape, F32)
    tk = ATTN_KV_BLOCK if k_ref.shape[0] % ATTN_KV_BLOCK == 0 else tq
    rep = tk // LANES

    def step(j, masked):
        rows = pl.ds(pl.multiple_of(j * tk, tk), tk)
        s = [lax.dot_general(qq[h], k_ref[rows, hs[h]], (((1,), (1,)), ((), ())),
                             preferred_element_type=F32) for h in heads]
        if masked:
            qpos = qi * tq + lax.broadcasted_iota(jnp.int32, s[0].shape, 0) % tq
            kpos = j * tk + lax.broadcasted_iota(jnp.int32, s[0].shape, 1)
            s = [jnp.where(qpos >= kpos, x, -jnp.inf) for x in s]
        for h in heads:
            m_old = m_ref[h]
            m_new = jnp.maximum(m_old, jnp.max(s[h], axis=-1, keepdims=True))
            alpha = jnp.exp(m_old - m_new)
            p = jnp.exp(s[h] - jnp.concatenate([m_new] * rep, axis=1))
            l_ref[h] = alpha * l_ref[h] + jnp.sum(p, axis=-1, keepdims=True)
            acc_ref[h] = alpha * acc_ref[h] + jnp.dot(p.astype(v_ref.dtype), v_ref[rows, hs[h]],
                                                      preferred_element_type=F32)
            m_ref[h] = m_new

    def body(j, carry):
        step(j, False)
        return carry

    n_full = (qi * tq) // tk
    lax.fori_loop(0, n_full, body, 0)
    step(n_full, True)

    lam = (jnp.exp(jnp.sum(lam_ref[0:1, :] * lam_ref[1:2, :], axis=-1, keepdims=True))
           - jnp.exp(jnp.sum(lam_ref[2:3, :] * lam_ref[3:4, :], axis=-1, keepdims=True))
           + lambda_init)
    for h in heads:
        o = acc_ref[h] / l_ref[h]
        d = o[:tq, :] - lam * o[tq:, :]
        o_ref[:, hs[h]] = (_rms(d, subln_ref[...], SUBLN_EPS)
                           * (1.0 - lambda_init)).astype(o_ref.dtype)


def _attn(q, k, v, lam4, subln, lambda_init):
    b, t, _ = q.shape
    tq = min(ATTN_BLOCK, t)
    hg = ATTN_HEAD_GROUP
    gw = hg * DA_V_DIM
    return pl.pallas_call(
        functools.partial(_attn_kernel, lambda_init=lambda_init),
        grid=(b, DA_HEADS // hg, t // tq),
        in_specs=[pl.BlockSpec((None, tq, gw), lambda bi, h, qi: (bi, qi, h)),
                  pl.BlockSpec((None, t, gw), lambda bi, h, qi: (bi, 0, h)),
                  pl.BlockSpec((None, t, gw), lambda bi, h, qi: (bi, 0, h)),
                  pl.BlockSpec((4, DA_HEAD_DIM), lambda bi, h, qi: (0, 0)),
                  pl.BlockSpec((1, DA_V_DIM), lambda bi, h, qi: (0, 0))],
        out_specs=pl.BlockSpec((None, tq, gw), lambda bi, h, qi: (bi, qi, h)),
        out_shape=jax.ShapeDtypeStruct((b, t, DA_WIDTH), BF16),
        scratch_shapes=[pltpu.VMEM((hg, 2 * tq, LANES), F32), pltpu.VMEM((hg, 2 * tq, LANES), F32),
                        pltpu.VMEM((hg, 2 * tq, DA_V_DIM), F32)],
        compiler_params=_params("parallel", "parallel", "arbitrary"),
        name="attn",
    )(q, k, v, lam4, subln)


def _rwkv_kernel(rw_ref, w0_ref, w2_ref, a0_ref, a2_ref, g2_ref, kk_ref, ka_ref, rk_ref,
                 lnw_ref, lnb_ref, o_ref, state_ref, r_s, k_s, v_s, lw_s, kk_s, a_s, g_s, cum_s):
    ti = pl.program_id(1)

    @pl.when(ti == 0)
    def _():
        state_ref[...] = jnp.zeros_like(state_ref)

    w = RW_WIDTH
    rw = rw_ref[...]
    k = rw[:, w:2 * w]
    wl = rw[:, 3 * w:3 * w + DECAY_LORA]
    al = rw[:, 3 * w + DECAY_LORA:3 * w + DECAY_LORA + AAA_LORA]
    gl = rw[:, 3 * w + DECAY_LORA + AAA_LORA:]
    z = -(w0_ref[...] + _bdot(jnp.tanh(wl), w2_ref[...]))
    softplus = jnp.maximum(z, 0.0) + jnp.log(1.0 + jnp.exp(-jnp.abs(z)))
    a = jax.nn.sigmoid(a0_ref[...] + _bdot(al, a2_ref[...]))
    r_s[...] = rw[:, 0:w]
    v_s[...] = rw[:, 2 * w:3 * w]
    lw_s[...] = -jnp.exp(-softplus - 0.5)
    a_s[...] = a
    g_s[...] = _bdot(jax.nn.sigmoid(gl), g2_ref[...])
    kk_s[...] = k * kk_ref[...]
    k_s[...] = k * (1.0 + (a - 1.0) * ka_ref[...])

    c_len = RW_CHUNK
    n = RW_HEAD_DIM
    tb = rw_ref.shape[0]

    br = lax.broadcasted_iota(jnp.int32, (tb, tb), 0)
    bc = lax.broadcasted_iota(jnp.int32, (tb, tb), 1)
    tri = jnp.where((br >= bc) & (br // c_len == bc // c_len), 1.0, 0.0).astype(BF16)
    lw_all = lw_s[...]
    lw_hi = lw_all.astype(BF16)
    rem = lw_all - lw_hi.astype(F32)
    lw_mid = rem.astype(BF16)
    lw_lo = (rem - lw_mid.astype(F32)).astype(BF16)
    cum_s[...] = (jnp.dot(tri, lw_hi, preferred_element_type=F32)
                  + jnp.dot(tri, lw_mid, preferred_element_type=F32)
                  + jnp.dot(tri, lw_lo, preferred_element_type=F32))

    row = lax.broadcasted_iota(jnp.int32, (c_len, 2 * c_len), 0)
    col = lax.broadcasted_iota(jnp.int32, (c_len, 2 * c_len), 1)
    incl2 = row >= col % c_len
    strict2 = row > col % c_len
    eye = jnp.where(lax.broadcasted_iota(jnp.int32, (c_len, c_len), 0)
                    == lax.broadcasted_iota(jnp.int32, (c_len, c_len), 1), 1.0, 0.0).astype(F32)

    def chunk(ci, carry):
        rows = pl.ds(pl.multiple_of(ci * c_len, c_len), c_len)
        heads = range(RW_HEADS)
        sl = [slice(h * n, (h + 1) * n) for h in heads]
        r = [r_s[rows, c] for c in sl]
        kh = [k_s[rows, c] for c in sl]
        v = [v_s[rows, c] for c in sl]
        lw = [lw_s[rows, c] for c in sl]
        cum = [cum_s[rows, c] for c in sl]
        kk = [kk_s[rows, c] for c in sl]
        kk = [x / jnp.maximum(jnp.sqrt(jnp.sum(x * x, axis=-1, keepdims=True)), 1e-12) for x in kk]
        kka = [kk[h] * a_s[rows, sl[h]] for h in heads]
        end = [jnp.sum(x, axis=0, keepdims=True) for x in lw]
        e_neg = [jnp.exp(-x) for x in cum]
        e_end = [jnp.exp(end[h] - cum[h]) for h in heads]
        left = [jnp.concatenate([-kk[h] * jnp.exp(cum[h] - lw[h]), r[h] * jnp.exp(cum[h])], axis=0)
                for h in heads]
        g = [_bdot_nt(left[h], jnp.concatenate([kka[h] * e_neg[h], kh[h] * e_neg[h]], axis=0))
             for h in heads]
        a_a = [jnp.where(strict2, x[:c_len, :], 0.0) for x in g]
        a_r = [jnp.where(incl2, x[c_len:, :], 0.0) for x in g]
        pw = [x[:, :c_len] for x in a_a]
        inv = [eye + x for x in pw]
        for _ in range(c_len.bit_length() - 2):
            pw = [_bdot(x, x) for x in pw]
            inv = [inv[h] + _bdot(inv[h], pw[h]) for h in heads]
        akv = [_bdot(a_a[h][:, c_len:], v[h]) for h in heads]
        s0 = [state_ref[h] for h in heads]
        ls = [_bdot_nt(left[h], s0[h]) for h in heads]
        u = [_bdot(inv[h], ls[h][:c_len, :] + akv[h]) for h in heads]
        uv = [jnp.concatenate([u[h], v[h]], axis=0) for h in heads]
        y = [ls[h][c_len:, :] + _bdot(a_r[h], uv[h]) for h in heads]
        for h in heads:
            state_ref[h] = s0[h] * jnp.exp(end[h]) + _bdot_tn(
                uv[h], jnp.concatenate([kka[h] * e_end[h], kh[h] * e_end[h]], axis=0))
        for h in heads:
            mean = jnp.mean(y[h], axis=-1, keepdims=True)
            yc = y[h] - mean
            var = jnp.mean(yc * yc, axis=-1, keepdims=True)
            yn = yc * lax.rsqrt(var + LN_X_EPS) * lnw_ref[:, sl[h]] + lnb_ref[:, sl[h]]
            bonus = jnp.sum(r[h] * kh[h] * rk_ref[:, sl[h]], axis=-1, keepdims=True) * v[h]
            o_ref[rows, sl[h]] = ((yn + bonus) * g_s[rows, sl[h]]).astype(o_ref.dtype)
        return carry

    lax.fori_loop(0, rw_ref.shape[0] // c_len, chunk, 0, unroll=2)


def _rwkv(rw, w0, w2, a0, a2, g2, k_k, k_a, r_k, ln_w, ln_b):
    b, t, _ = rw.shape
    tb = min(RW_BLOCK, t)
    w = RW_WIDTH
    vec = pl.BlockSpec((1, w), lambda bi, ti: (0, 0))
    mat = lambda r: pl.BlockSpec((r, w), lambda bi, ti: (0, 0))
    return pl.pallas_call(
        _rwkv_kernel,
        grid=(b, t // tb),
        in_specs=[pl.BlockSpec((None, tb, RW_COLS), lambda bi, ti: (bi, ti, 0)),
                  vec, mat(DECAY_LORA), vec, mat(AAA_LORA), mat(GATE_LORA), vec, vec, vec, vec, vec],
        out_specs=pl.BlockSpec((None, tb, w), lambda bi, ti: (bi, ti, 0)),
        out_shape=jax.ShapeDtypeStruct((b, t, w), BF16),
        scratch_shapes=[pltpu.VMEM((RW_HEADS, RW_HEAD_DIM, RW_HEAD_DIM), F32)]
        + [pltpu.VMEM((tb, w), F32)] * 8,
        compiler_params=_params("parallel", "arbitrary"),
        name="rwkv",
    )(rw, w0, w2, a0, a2, g2, k_k, k_a, r_k.reshape(1, w), ln_w, ln_b)


def _pack_rows(x):
    half = x.shape[1] // 2
    hi = pltpu.bitcast(x[:, :half].astype(BF16).astype(F32), jnp.uint32)
    lo = pltpu.bitcast(x[:, half:].astype(BF16).astype(F32), jnp.uint32)
    return hi | (lo >> 16)


def _unpack_rows(u):
    hi = pltpu.bitcast(u & jnp.uint32(0xFFFF0000), F32)
    lo = pltpu.bitcast(u << 16, F32)
    return jnp.concatenate([hi, lo], axis=1)


def _out_kernel(yda_ref, yrw_ref, x_ref, mod_ref, wo_ref, pmn_ref, pfn_ref, rw_ref, rb_ref,
                x1_ref, h2_ref, idx_ref, wgt_ref):
    y = (jnp.dot(yda_ref[...], wo_ref[0:DA_WIDTH, :], preferred_element_type=F32)
         + jnp.dot(yrw_ref[...], wo_ref[DA_WIDTH:, :], preferred_element_type=F32))
    x1 = x_ref[...] + mod_ref[2:3, :] * _rms(y, pmn_ref[...], NORM_EPS)
    x1_ref[...] = x1
    h2 = _rms(x1, pfn_ref[...], NORM_EPS) * (1.0 + mod_ref[4:5, :]) + mod_ref[3:4, :]
    h2_ref[...] = _pack_rows(h2)

    h_hi = h2.astype(BF16)
    h_lo = (h2 - h_hi.astype(F32)).astype(BF16)
    rw = rw_ref[...]
    w_hi = rw.astype(BF16)
    w_lo = (rw - w_hi.astype(F32)).astype(BF16)
    logits = (jnp.dot(h_hi, w_hi, preferred_element_type=F32)
              + jnp.dot(h_hi, w_lo, preferred_element_type=F32)
              + jnp.dot(h_lo, w_hi, preferred_element_type=F32)) + rb_ref[...]

    lane = lax.broadcasted_iota(jnp.int32, logits.shape, 1)
    slot = lax.broadcasted_iota(jnp.int32, idx_ref.shape, 1)
    idx = jnp.zeros(idx_ref.shape, jnp.int32)
    val = jnp.zeros(idx_ref.shape, F32)
    top = None
    for j in range(TOP_K):
        m = jnp.max(logits, axis=-1, keepdims=True)
        i = jnp.min(jnp.where(logits == m, lane, N_EXPERTS), axis=-1, keepdims=True)
        top = m if top is None else top
        idx = jnp.where(slot == j, i, idx)
        val = jnp.where(slot == j, jnp.exp(m - top), val)
        logits = jnp.where(lane == i, -jnp.inf, logits)
    idx_ref[...] = idx
    wgt_ref[...] = val / jnp.sum(val, axis=-1, keepdims=True)


def _out(y_da, y_rw, x, mod3, w_out_b, post_mix_norm, pre_ffn_norm, router_w, router_b):
    b, t, d = x.shape
    tm = min(OUT_ROWS, t)
    e = router_w.shape[1]
    blk = lambda w: pl.BlockSpec((None, tm, w), lambda bi, ti: (bi, ti, 0))
    full = lambda r, c: pl.BlockSpec((r, c), lambda bi, ti: (0, 0))
    return pl.pallas_call(
        _out_kernel,
        grid=(b, t // tm),
        in_specs=[blk(DA_WIDTH), blk(RW_WIDTH), blk(d),
                  pl.BlockSpec((None, N_MOD, d), lambda bi, ti: (bi, 0, 0)),
                  full(d, d), full(1, d), full(1, d), full(d, e), full(1, e)],
        out_specs=[blk(d), blk(d // 2), blk(TOP_K), blk(TOP_K)],
        out_shape=[jax.ShapeDtypeStruct((b, t, d), F32),
                   jax.ShapeDtypeStruct((b, t, d // 2), jnp.uint32),
                   jax.ShapeDtypeStruct((b, t, TOP_K), jnp.int32),
                   jax.ShapeDtypeStruct((b, t, TOP_K), F32)],
        compiler_params=_params("parallel", "parallel"),
        name="out",
    )(y_da, y_rw, x, mod3, w_out_b, post_mix_norm, pre_ffn_norm, router_w,
      router_b.reshape(1, e))


def _route(top_idx, rows_per_tile, n_tiles):
    e_flat = top_idx.reshape(-1)
    onehot = (e_flat[:, None] == jnp.arange(N_EXPERTS, dtype=jnp.int32)[None, :]).astype(jnp.int32)
    csum = jnp.cumsum(onehot, axis=0)
    counts = csum[-1]
    padded = (counts + rows_per_tile - 1) // rows_per_tile * rows_per_tile
    ends = jnp.cumsum(padded)
    starts = ends - padded
    pos = jnp.sum((csum - onehot + starts[None, :]) * onehot, axis=1)
    n_active = ends[-1] // rows_per_tile
    tile_start = jnp.arange(n_tiles, dtype=jnp.int32) * rows_per_tile
    tile = jnp.minimum(tile_start, ends[-1] - 1)
    tile_expert = jnp.sum((tile[:, None] >= ends[None, :]).astype(jnp.int32), axis=1)
    return pos.astype(jnp.int32), tile_expert.astype(jnp.int32), n_active.reshape(1).astype(jnp.int32)


def _row(ref, r):
    return ref.at[lax.shift_right_logical(r, 3), pl.ds(r & (SUBLANES - 1), 1)]


def _dispatch_kernel(pos_ref, h_ref, xs_in_ref, xs_ref, sem):
    del xs_in_ref
    n_tiles = h_ref.shape[0]

    def issue(tt, carry):
        for s in range(SUBLANES):
            for j in range(TOP_K):
                slot = pos_ref[0, tt * (SUBLANES * TOP_K) + s * TOP_K + j]
                pltpu.make_async_copy(h_ref.at[tt, pl.ds(s, 1)], _row(xs_ref, slot),
                                      sem).start(priority=j % 2)
        return carry

    lax.fori_loop(0, n_tiles, issue, 0)
    for j in range(TOP_K):
        pltpu.make_async_copy(h_ref, xs_ref.at[pl.ds(0, n_tiles)], sem).wait()


def _dispatch(pos, h2p, n_rows):
    n, w = h2p.shape
    tb = min(DISPATCH_TOKENS, n)
    pos3 = pos.reshape(n // tb, 1, tb * TOP_K)
    xs0 = jnp.zeros((n_rows // SUBLANES, SUBLANES, w), h2p.dtype)
    xs = pl.pallas_call(
        _dispatch_kernel,
        grid=(n // tb,),
        in_specs=[pl.BlockSpec((None, 1, tb * TOP_K), lambda i: (i, 0, 0),
                               memory_space=pltpu.SMEM),
                  pl.BlockSpec((tb // SUBLANES, SUBLANES, w), lambda i: (i, 0, 0)),
                  pl.BlockSpec(memory_space=pl.ANY)],
        out_specs=pl.BlockSpec(memory_space=pl.ANY),
        out_shape=jax.ShapeDtypeStruct(xs0.shape, h2p.dtype),
        scratch_shapes=[pltpu.SemaphoreType.DMA(())],
        input_output_aliases={2: 0},
        compiler_params=_params("arbitrary"),
        name="dispatch",
    )(pos3, h2p.reshape(n // SUBLANES, SUBLANES, w), xs0)
    return xs.reshape(n_rows, w)


def _expert_kernel(te_ref, na_ref, xs_ref, w1_ref, b1_ref, w2_ref, b2_ref, ys_ref,
                   w1p_s, b1p_s, w2b_s, act_s):
    i = pl.program_id(0)
    active = i < na_ref[0]
    fresh = jnp.logical_or(i == 0, te_ref[i] != te_ref[jnp.maximum(i - 1, 0)])
    grp = 2 * LANES
    n_grp = w1_ref.shape[1] // grp

    @pl.when(jnp.logical_and(active, fresh))
    def _():
        src = lax.broadcasted_iota(jnp.int32, (grp, grp), 0)
        dst = lax.broadcasted_iota(jnp.int32, (grp, grp), 1)
        perm = jnp.where(src == jnp.where(dst < LANES, 2 * dst, 2 * (dst - LANES) + 1),
                         1.0, 0.0).astype(BF16)
        for g in range(n_grp):
            sl = slice(g * grp, (g + 1) * grp)
            w1p_s[:, sl] = jnp.dot(w1_ref[:, sl].astype(BF16), perm,
                                   preferred_element_type=F32).astype(BF16)
            b = b1_ref[:, sl]
            b_hi = b.astype(BF16)
            b_lo = (b - b_hi.astype(F32)).astype(BF16)
            b1p_s[:, sl] = (jnp.dot(b_hi, perm, preferred_element_type=F32)
                            + jnp.dot(b_lo, perm, preferred_element_type=F32))
        w2b_s[...] = w2_ref[...].astype(BF16)

    @pl.when(active)
    def _():
        x = _unpack_rows(xs_ref[...]).astype(BF16)
        hid = jnp.dot(x, w1p_s[...], preferred_element_type=F32) + b1p_s[0:1, :]
        for g in range(n_grp):
            glu = jnp.minimum(hid[:, g * grp:g * grp + LANES], SWIGLU_LIMIT)
            lin = jnp.clip(hid[:, g * grp + LANES:(g + 1) * grp], -SWIGLU_LIMIT, SWIGLU_LIMIT)
            act_s[:, g * LANES:(g + 1) * LANES] = (
                glu * jax.nn.sigmoid(SWIGLU_ALPHA * glu) * (lin + 1.0)).astype(BF16)
        y = jnp.dot(act_s[...], w2b_s[...], preferred_element_type=F32) + b2_ref[...]
        ys_ref[...] = _pack_rows(y)

    @pl.when(jnp.logical_not(active))
    def _():
        ys_ref[...] = jnp.zeros_like(ys_ref)


def _experts(tile_expert, n_active, xs, w1, b1, w2, b2):
    n_rows, w = xs.shape
    tm = EXPERT_ROWS
    d, f2 = w1.shape[1], w1.shape[2]
    f = f2 // 2
    wspec = lambda r, c: pl.BlockSpec((None, r, c), lambda i, te, na: (te[i], 0, 0))
    return pl.pallas_call(
        _expert_kernel,
        grid_spec=pltpu.PrefetchScalarGridSpec(
            num_scalar_prefetch=2,
            grid=(n_rows // tm,),
            in_specs=[pl.BlockSpec((tm, w), lambda i, te, na: (i, 0)),
                      wspec(d, f2), wspec(8, f2), wspec(f, d), wspec(1, d)],
            out_specs=pl.BlockSpec((tm, w), lambda i, te, na: (i, 0)),
            scratch_shapes=[pltpu.VMEM((d, f2), BF16), pltpu.VMEM((8, f2), F32),
                            pltpu.VMEM((f, d), BF16), pltpu.VMEM((tm, f), BF16)]),
        out_shape=jax.ShapeDtypeStruct((n_rows, w), jnp.uint32),
        compiler_params=_params("arbitrary"),
        name="expert",
    )(tile_expert, n_active, xs, w1, b1, w2, b2)


def _combine_kernel(pos_ref, ys_ref, wgt_ref, x1_ref, mod_ref, nw_ref, o_ref, buf_ref, sem):
    tc = x1_ref.shape[0]

    n_tiles = tc // SUBLANES

    def issue(tt, carry):
        for s in range(SUBLANES):
            for j in range(TOP_K):
                slot = pos_ref[0, tt * (SUBLANES * TOP_K) + s * TOP_K + j]
                pltpu.make_async_copy(_row(ys_ref, slot), buf_ref.at[j, tt, pl.ds(s, 1)],
                                      sem).start(priority=j % 2)
        return carry

    lax.fori_loop(0, n_tiles, issue, 0)
    for j in range(TOP_K):
        pltpu.make_async_copy(ys_ref.at[pl.ds(0, n_tiles)], buf_ref.at[j], sem).wait()

    sub = min(COMBINE_SUB_ROWS, tc)

    def mix(i, carry):
        r0 = pl.multiple_of(i * sub, sub)
        t0 = pl.multiple_of(i * (sub // SUBLANES), sub // SUBLANES)
        wgt = wgt_ref[pl.ds(r0, sub), :]
        acc = jnp.zeros((sub, x1_ref.shape[1]), F32)
        for j in range(TOP_K):
            rows = buf_ref[j, pl.ds(t0, sub // SUBLANES)].reshape(sub, buf_ref.shape[-1])
            acc = acc + wgt[:, j:j + 1] * _unpack_rows(rows)
        o_ref[pl.ds(r0, sub), :] = (x1_ref[pl.ds(r0, sub), :]
                                    + mod_ref[5:6, :] * _rms(acc, nw_ref[...], NORM_EPS))
        return carry

    lax.fori_loop(0, tc // sub, mix, 0)


def _combine(pos, ys, wgt, x1, mod3, post_ffn_norm):
    b, t, d = x1.shape
    tc = min(COMBINE_TOKENS, t)
    nt = t // tc
    pos3 = pos.reshape(b * nt, 1, tc * TOP_K)
    blk = lambda w: pl.BlockSpec((None, tc, w), lambda bi, ti: (bi, ti, 0))
    return pl.pallas_call(
        _combine_kernel,
        grid=(b, nt),
        in_specs=[pl.BlockSpec((None, 1, tc * TOP_K), lambda bi, ti: (bi * nt + ti, 0, 0),
                               memory_space=pltpu.SMEM),
                  pl.BlockSpec(memory_space=pl.ANY),
                  blk(TOP_K), blk(d),
                  pl.BlockSpec((None, N_MOD, d), lambda bi, ti: (bi, 0, 0)),
                  pl.BlockSpec((1, d), lambda bi, ti: (0, 0))],
        out_specs=blk(d),
        out_shape=jax.ShapeDtypeStruct((b, t, d), F32),
        scratch_shapes=[pltpu.VMEM((TOP_K, tc // SUBLANES, SUBLANES, d // 2), jnp.uint32),
                        pltpu.SemaphoreType.DMA(())],
        compiler_params=_params("arbitrary", "arbitrary"),
        name="combine",
    )(pos3, ys.reshape(ys.shape[0] // SUBLANES, SUBLANES, ys.shape[1]), wgt, x1, mod3,
      post_ffn_norm)


def _stages(x, c, positions, ada_w, ada_b, pre_mix_norm, post_mix_norm, pre_ffn_norm,
            post_ffn_norm, w_in, w_out, da_lambda_q1, da_lambda_k1, da_lambda_q2, da_lambda_k2,
            da_subln, rw_mu, rw_w0, rw_w2, rw_a0, rw_a2, rw_g2, rw_k_k, rw_k_a, rw_r_k, rw_ln_w,
            rw_ln_b, router_w, router_b, moe_w1, moe_b1, moe_w2, moe_b2):
    b, t, d = x.shape
    res = {}
    lambda_init = 0.8 - 0.6 * math.exp(-0.3 * 0)
    mod = _mod(c, ada_w[0], ada_b[0])
    res["mod"] = mod
    mod3 = mod.reshape(b, N_MOD, d)
    inv_freq = ROPE_THETA ** (-jnp.arange(0, ROPE_DIM, 2, dtype=F32) / ROPE_DIM)
    invf = jnp.tile(inv_freq, LANES // (ROPE_DIM // 2)).reshape(1, LANES)
    q, k, v, rw = _proj(x, positions.reshape(b, t, 1), mod3, pre_mix_norm, invf,
                        w_in[0].astype(BF16), rw_mu)
    res.update(q=q, k=k, v=v, rw=rw)
    lam4 = jnp.concatenate([da_lambda_q1, da_lambda_k1, da_lambda_q2, da_lambda_k2], axis=0)
    y_da = _attn(q, k, v, lam4, da_subln, lambda_init)
    res["y_da"] = y_da
    y_rw = _rwkv(rw, rw_w0, rw_w2[0], rw_a0, rw_a2[0], rw_g2[0], rw_k_k, rw_k_a, rw_r_k[0],
                 rw_ln_w, rw_ln_b)
    res["y_rw"] = y_rw
    x1, h2p, top_idx, top_w = _out(y_da, y_rw, x, mod3, w_out[0].astype(BF16), post_mix_norm,
                                   pre_ffn_norm, router_w[0], router_b[0])
    res.update(x1=x1, top_idx=top_idx, top_w=top_w)
    n = b * t
    n_tiles = n * TOP_K // EXPERT_ROWS + N_EXPERTS
    pos, tile_expert, n_active = _route(top_idx.reshape(n, TOP_K), EXPERT_ROWS, n_tiles)
    xs = _dispatch(pos, h2p.reshape(n, d // 2), n_tiles * EXPERT_ROWS)
    b1 = jnp.broadcast_to(moe_b1[0][:, None, :], (N_EXPERTS, 8, moe_b1.shape[-1]))
    ys = _experts(tile_expert, n_active, xs, moe_w1[0], b1, moe_w2[0], moe_b2[0][:, None, :])
    res["final"] = _combine(pos, ys, top_w, x1, mod3, post_ffn_norm)
    return res


stages = _stages


def kernel(x, c, positions, ada_w, ada_b, pre_mix_norm, post_mix_norm, pre_ffn_norm, post_ffn_norm, w_in, w_out, da_lambda_q1, da_lambda_k1, da_lambda_q2, da_lambda_k2, da_subln, rw_mu, rw_w0, rw_w2, rw_a0, rw_a2, rw_g2, rw_k_k, rw_k_a, rw_r_k, rw_ln_w, rw_ln_b, router_w, router_b, moe_w1, moe_b1, moe_w2, moe_b2):
    res = _stages(x, c, positions, ada_w, ada_b, pre_mix_norm, post_mix_norm, pre_ffn_norm,
                  post_ffn_norm, w_in, w_out, da_lambda_q1, da_lambda_k1, da_lambda_q2,
                  da_lambda_k2, da_subln, rw_mu, rw_w0, rw_w2, rw_a0, rw_a2, rw_g2, rw_k_k,
                  rw_k_a, rw_r_k, rw_ln_w, rw_ln_b, router_w, router_b, moe_w1, moe_b1,
                  moe_w2, moe_b2)
    return res["final"]
```

```python
import functools
import math

import jax
import jax.numpy as jnp
from jax import lax
from jax.experimental import pallas as pl
from jax.experimental.pallas import tpu as pltpu

F32 = jnp.float32
BF16 = jnp.bfloat16

DA_HEADS = 4
DA_HEAD_DIM = 64
DA_V_DIM = 128
DA_WIDTH = 512
RW_HEADS = 8
RW_HEAD_DIM = 64
RW_WIDTH = 512
DECAY_LORA = 64
AAA_LORA = 64
GATE_LORA = 128
DA_COLS = 1536
RW_COLS = 1792
ROPE_THETA = 500000.0
ROPE_DIM = 16
N_EXPERTS = 32
TOP_K = 4
SWIGLU_ALPHA = 1.702
SWIGLU_LIMIT = 7.0
NORM_EPS = 1e-6
SUBLN_EPS = 1e-5
LN_X_EPS = 64e-5
N_MOD = 6

LANES = 128
SUBLANES = 8
VMEM_LIMIT_BYTES = 56 * 1024 * 1024

PROJ_ROWS = 512
ATTN_BLOCK = 256
ATTN_KV_BLOCK = 512
ATTN_HEAD_GROUP = 2
RW_CHUNK = 128
RW_BLOCK = 256
OUT_ROWS = 512
EXPERT_ROWS = 512
DISPATCH_TOKENS = 2048
COMBINE_TOKENS = 1024
COMBINE_SUB_ROWS = 256


def _params(*sem):
    return pltpu.CompilerParams(dimension_semantics=sem, vmem_limit_bytes=VMEM_LIMIT_BYTES)


def _bdot(a, b):
    return jnp.dot(a.astype(BF16), b.astype(BF16), preferred_element_type=F32)


def _bdot_nt(a, b):
    return lax.dot_general(a.astype(BF16), b.astype(BF16), (((1,), (1,)), ((), ())),
                           preferred_element_type=F32)


def _bdot_tn(a, b):
    return lax.dot_general(a.astype(BF16), b.astype(BF16), (((0,), (0,)), ((), ())),
                           preferred_element_type=F32)


def _rms(x, w, eps):
    return x * lax.rsqrt(jnp.mean(x * x, axis=-1, keepdims=True) + eps) * w


def _mod_kernel(c_ref, w_ref, b_ref, o_ref):
    c = c_ref[...]
    s = c * jax.nn.sigmoid(c)
    o_ref[...] = _bdot(s, w_ref[...]) + b_ref[...]


def _mod(c, ada_w, ada_b):
    b, d = c.shape
    n = ada_w.shape[1]
    return pl.pallas_call(
        _mod_kernel,
        grid=(n // d,),
        in_specs=[pl.BlockSpec((b, d), lambda j: (0, 0)),
                  pl.BlockSpec((d, d), lambda j: (0, j)),
                  pl.BlockSpec((1, d), lambda j: (0, j))],
        out_specs=pl.BlockSpec((b, d), lambda j: (0, j)),
        out_shape=jax.ShapeDtypeStruct((b, n), F32),
        compiler_params=_params("parallel"),
        name="mod",
    )(c, ada_w, ada_b.reshape(1, n))


def _proj_kernel(x_ref, pos_ref, mod_ref, nw_ref, invf_ref, w_ref, mu_ref,
                 q_ref, k_ref, v_ref, rw_ref, carry_ref):
    ti = pl.program_id(1)

    @pl.when(ti == 0)
    def _():
        carry_ref[...] = jnp.zeros_like(carry_ref)

    x = x_ref[...]
    h = _rms(x, nw_ref[...], NORM_EPS) * (1.0 + mod_ref[1:2, :]) + mod_ref[0:1, :]
    hb = h.astype(BF16)

    ang = pos_ref[...].astype(F32) * invf_ref[...]
    cos, sin = jnp.cos(ang), jnp.sin(ang)
    l64 = lax.broadcasted_iota(jnp.int32, ang.shape, 1) % DA_HEAD_DIM
    half = ROPE_DIM // 2
    c_tab = jnp.where(l64 < ROPE_DIM, cos, 1.0)
    s_lo = jnp.where(l64 < half, -sin, 0.0)
    s_hi = jnp.where((l64 >= half) & (l64 < ROPE_DIM), sin, 0.0)

    def rope(z):
        up = pltpu.roll(z, LANES - half, axis=1)
        dn = pltpu.roll(z, half, axis=1)
        return z * c_tab + up * s_lo + dn * s_hi

    for g in range(DA_WIDTH // LANES):
        sl = slice(g * LANES, (g + 1) * LANES)
        qg = jnp.dot(hb, w_ref[:, sl], preferred_element_type=F32)
        q_ref[:, sl] = (rope(qg) * (DA_HEAD_DIM ** -0.5)).astype(q_ref.dtype)
        kg = jnp.dot(hb, w_ref[:, DA_WIDTH + g * LANES:DA_WIDTH + (g + 1) * LANES],
                     preferred_element_type=F32)
        k_ref[:, sl] = rope(kg).astype(k_ref.dtype)
    v_ref[...] = jnp.dot(hb, w_ref[:, 2 * DA_WIDTH:DA_COLS],
                         preferred_element_type=F32).astype(v_ref.dtype)

    p = jnp.dot(hb, w_ref[:, DA_COLS:], preferred_element_type=F32)
    rows = p.shape[0]
    prev = pltpu.roll(p, 1, axis=0)
    first = lax.broadcasted_iota(jnp.int32, p.shape, 0) == 0
    prev = jnp.where(first, carry_ref[0:1, :], prev)
    rw_ref[...] = p + (prev - p) * mu_ref[...]
    carry_ref[0:1, :] = p[rows - 1:rows, :]


def _proj(x, pos3, mod3, norm_w, invf, w_in_b, mu):
    b, t, d = x.shape
    tm = min(PROJ_ROWS, t)
    n_in = w_in_b.shape[1]
    blk = lambda w: pl.BlockSpec((None, tm, w), lambda bi, ti: (bi, ti, 0))
    full = lambda r, c: pl.BlockSpec((r, c), lambda bi, ti: (0, 0))
    return pl.pallas_call(
        _proj_kernel,
        grid=(b, t // tm),
        in_specs=[blk(d), blk(1),
                  pl.BlockSpec((None, N_MOD, d), lambda bi, ti: (bi, 0, 0)),
                  full(1, d), full(1, LANES), full(d, n_in), full(1, RW_COLS)],
        out_specs=[blk(DA_WIDTH), blk(DA_WIDTH), blk(DA_WIDTH), blk(RW_COLS)],
        out_shape=[jax.ShapeDtypeStruct((b, t, DA_WIDTH), BF16)] * 3
        + [jax.ShapeDtypeStruct((b, t, RW_COLS), F32)],
        scratch_shapes=[pltpu.VMEM((8, RW_COLS), F32)],
        compiler_params=_params("parallel", "arbitrary"),
        name="proj",
    )(x, pos3, mod3, norm_w, invf, w_in_b, mu)


def _attn_kernel(q_ref, k_ref, v_ref, lam_ref, subln_ref, o_ref, m_ref, l_ref, acc_ref,
                 *, lambda_init):
    qi = pl.program_id(2)
    tq = q_ref.shape[0]
    heads = range(ATTN_HEAD_GROUP)
    hs = [slice(h * DA_V_DIM, (h + 1) * DA_V_DIM) for h in heads]
    lane = lax.broadcasted_iota(jnp.int32, (tq, DA_V_DIM), 1)
    qq = []
    for c in hs:
        q = q_ref[:, c]
        zero = jnp.zeros_like(q)
        qq.append(jnp.concatenate([jnp.where(lane < DA_HEAD_DIM, q, zero),
                                   jnp.where(lane >= DA_HEAD_DIM, q, zero)], axis=0))

    m_ref[...] = jnp.full(m_ref.shape, -jnp.inf, F32)
    l_ref[...] = jnp.zeros(l_ref.shape, F32)
    acc_ref[...] = jnp.zeros(acc_ref.shape, F32)
    tk = ATTN_KV_BLOCK if k_ref.shape[0] % ATTN_KV_BLOCK == 0 else tq
    rep = tk // LANES

    def step(j, masked):
        rows = pl.ds(pl.multiple_of(j * tk, tk), tk)
        s = [lax.dot_general(qq[h], k_ref[rows, hs[h]], (((1,), (1,)), ((), ())),
                             preferred_element_type=F32) for h in heads]
        if masked:
            qpos = qi * tq + lax.broadcasted_iota(jnp.int32, s[0].shape, 0) % tq
            kpos = j * tk + lax.broadcasted_iota(jnp.int32, s[0].shape, 1)
            s = [jnp.where(qpos >= kpos, x, -jnp.inf) for x in s]
        for h in heads:
            m_old = m_ref[h]
            m_new = jnp.maximum(m_old, jnp.max(s[h], axis=-1, keepdims=True))
            alpha = jnp.exp(m_old - m_new)
            p = jnp.exp(s[h] - jnp.concatenate([m_new] * rep, axis=1))
            l_ref[h] = alpha * l_ref[h] + jnp.sum(p, axis=-1, keepdims=True)
            acc_ref[h] = alpha * acc_ref[h] + jnp.dot(p.astype(v_ref.dtype), v_ref[rows, hs[h]],
                                                      preferred_element_type=F32)
            m_ref[h] = m_new

    def body(j, carry):
        step(j, False)
        return carry

    n_full = (qi * tq) // tk
    lax.fori_loop(0, n_full, body, 0)
    step(n_full, True)

    lam = (jnp.exp(jnp.sum(lam_ref[0:1, :] * lam_ref[1:2, :], axis=-1, keepdims=True))
           - jnp.exp(jnp.sum(lam_ref[2:3, :] * lam_ref[3:4, :], axis=-1, keepdims=True))
           + lambda_init)
    for h in heads:
        o = acc_ref[h] / l_ref[h]
        d = o[:tq, :] - lam * o[tq:, :]
        o_ref[:, hs[h]] = (_rms(d, subln_ref[...], SUBLN_EPS)
                           * (1.0 - lambda_init)).astype(o_ref.dtype)


def _attn(q, k, v, lam4, subln, lambda_init):
    b, t, _ = q.shape
    tq = min(ATTN_BLOCK, t)
    hg = ATTN_HEAD_GROUP
    gw = hg * DA_V_DIM
    return pl.pallas_call(
        functools.partial(_attn_kernel, lambda_init=lambda_init),
        grid=(b, DA_HEADS // hg, t // tq),
        in_specs=[pl.BlockSpec((None, tq, gw), lambda bi, h, qi: (bi, qi, h)),
                  pl.BlockSpec((None, t, gw), lambda bi, h, qi: (bi, 0, h)),
                  pl.BlockSpec((None, t, gw), lambda bi, h, qi: (bi, 0, h)),
                  pl.BlockSpec((4, DA_HEAD_DIM), lambda bi, h, qi: (0, 0)),
                  pl.BlockSpec((1, DA_V_DIM), lambda bi, h, qi: (0, 0))],
        out_specs=pl.BlockSpec((None, tq, gw), lambda bi, h, qi: (bi, qi, h)),
        out_shape=jax.ShapeDtypeStruct((b, t, DA_WIDTH), BF16),
        scratch_shapes=[pltpu.VMEM((hg, 2 * tq, LANES), F32), pltpu.VMEM((hg, 2 * tq, LANES), F32),
                        pltpu.VMEM((hg, 2 * tq, DA_V_DIM), F32)],
        compiler_params=_params("parallel", "parallel", "arbitrary"),
        name="attn",
    )(q, k, v, lam4, subln)


def _rwkv_kernel(rw_ref, w0_ref, w2_ref, a0_ref, a2_ref, g2_ref, kk_ref, ka_ref, rk_ref,
                 lnw_ref, lnb_ref, o_ref, state_ref, r_s, k_s, v_s, lw_s, kk_s, a_s, g_s, cum_s):
    ti = pl.program_id(1)

    @pl.when(ti == 0)
    def _():
        state_ref[...] = jnp.zeros_like(state_ref)

    w = RW_WIDTH
    rw = rw_ref[...]
    k = rw[:, w:2 * w]
    wl = rw[:, 3 * w:3 * w + DECAY_LORA]
    al = rw[:, 3 * w + DECAY_LORA:3 * w + DECAY_LORA + AAA_LORA]
    gl = rw[:, 3 * w + DECAY_LORA + AAA_LORA:]
    z = -(w0_ref[...] + _bdot(jnp.tanh(wl), w2_ref[...]))
    softplus = jnp.maximum(z, 0.0) + jnp.log(1.0 + jnp.exp(-jnp.abs(z)))
    a = jax.nn.sigmoid(a0_ref[...] + _bdot(al, a2_ref[...]))
    r_s[...] = rw[:, 0:w]
    v_s[...] = rw[:, 2 * w:3 * w]
    lw_s[...] = -jnp.exp(-softplus - 0.5)
    a_s[...] = a
    g_s[...] = _bdot(jax.nn.sigmoid(gl), g2_ref[...])
    kk_s[...] = k * kk_ref[...]
    k_s[...] = k * (1.0 + (a - 1.0) * ka_ref[...])

    c_len = RW_CHUNK
    n = RW_HEAD_DIM
    tb = rw_ref.shape[0]

    br = lax.broadcasted_iota(jnp.int32, (tb, tb), 0)
    bc = lax.broadcasted_iota(jnp.int32, (tb, tb), 1)
    tri = jnp.where((br >= bc) & (br // c_len == bc // c_len), 1.0, 0.0).astype(BF16)
    lw_all = lw_s[...]
    lw_hi = lw_all.astype(BF16)
    rem = lw_all - lw_hi.astype(F32)
    lw_mid = rem.astype(BF16)
    lw_lo = (rem - lw_mid.astype(F32)).astype(BF16)
    cum_s[...] = (jnp.dot(tri, lw_hi, preferred_element_type=F32)
                  + jnp.dot(tri, lw_mid, preferred_element_type=F32)
                  + jnp.dot(tri, lw_lo, preferred_element_type=F32))

    row = lax.broadcasted_iota(jnp.int32, (c_len, 2 * c_len), 0)
    col = lax.broadcasted_iota(jnp.int32, (c_len, 2 * c_len), 1)
    incl2 = row >= col % c_len
    strict2 = row > col % c_len
    eye = jnp.where(lax.broadcasted_iota(jnp.int32, (c_len, c_len), 0)
                    == lax.broadcasted_iota(jnp.int32, (c_len, c_len), 1), 1.0, 0.0).astype(F32)

    def chunk(ci, carry):
        rows = pl.ds(pl.multiple_of(ci * c_len, c_len), c_len)
        heads = range(RW_HEADS)
        sl = [slice(h * n, (h + 1) * n) for h in heads]
        r = [r_s[rows, c] for c in sl]
        kh = [k_s[rows, c] for c in sl]
        v = [v_s[rows, c] for c in sl]
        lw = [lw_s[rows, c] for c in sl]
        cum = [cum_s[rows, c] for c in sl]
        kk = [kk_s[rows, c] for c in sl]
        kk = [x / jnp.maximum(jnp.sqrt(jnp.sum(x * x, axis=-1, keepdims=True)), 1e-12) for x in kk]
        kka = [kk[h] * a_s[rows, sl[h]] for h in heads]
        end = [jnp.sum(x, axis=0, keepdims=True) for x in lw]
        e_neg = [jnp.exp(-x) for x in cum]
        e_end = [jnp.exp(end[h] - cum[h]) for h in heads]
        left = [jnp.concatenate([-kk[h] * jnp.exp(cum[h] - lw[h]), r[h] * jnp.exp(cum[h])], axis=0)
                for h in heads]
        g = [_bdot_nt(left[h], jnp.concatenate([kka[h] * e_neg[h], kh[h] * e_neg[h]], axis=0))
             for h in heads]
        a_a = [jnp.where(strict2, x[:c_len, :], 0.0) for x in g]
        a_r = [jnp.where(incl2, x[c_len:, :], 0.0) for x in g]
        pw = [x[:, :c_len] for x in a_a]
        inv = [eye + x for x in pw]
        for _ in range(c_len.bit_length() - 2):
            pw = [_bdot(x, x) for x in pw]
            inv = [inv[h] + _bdot(inv[h], pw[h]) for h in heads]
        akv = [_bdot(a_a[h][:, c_len:], v[h]) for h in heads]
        s0 = [state_ref[h] for h in heads]
        ls = [_bdot_nt(left[h], s0[h]) for h in heads]
        u = [_bdot(inv[h], ls[h][:c_len, :] + akv[h]) for h in heads]
        uv = [jnp.concatenate([u[h], v[h]], axis=0) for h in heads]
        y = [ls[h][c_len:, :] + _bdot(a_r[h], uv[h]) for h in heads]
        for h in heads:
            state_ref[h] = s0[h] * jnp.exp(end[h]) + _bdot_tn(
                uv[h], jnp.concatenate([kka[h] * e_end[h], kh[h] * e_end[h]], axis=0))
        for h in heads:
            mean = jnp.mean(y[h], axis=-1, keepdims=True)
            yc = y[h] - mean
            var = jnp.mean(yc * yc, axis=-1, keepdims=True)
            yn = yc * lax.rsqrt(var + LN_X_EPS) * lnw_ref[:, sl[h]] + lnb_ref[:, sl[h]]
            bonus = jnp.sum(r[h] * kh[h] * rk_ref[:, sl[h]], axis=-1, keepdims=True) * v[h]
            o_ref[rows, sl[h]] = ((yn + bonus) * g_s[rows, sl[h]]).astype(o_ref.dtype)
        return carry

    lax.fori_loop(0, rw_ref.shape[0] // c_len, chunk, 0, unroll=2)


def _rwkv(rw, w0, w2, a0, a2, g2, k_k, k_a, r_k, ln_w, ln_b):
    b, t, _ = rw.shape
    tb = min(RW_BLOCK, t)
    w = RW_WIDTH
    vec = pl.BlockSpec((1, w), lambda bi, ti: (0, 0))
    mat = lambda r: pl.BlockSpec((r, w), lambda bi, ti: (0, 0))
    return pl.pallas_call(
        _rwkv_kernel,
        grid=(b, t // tb),
        in_specs=[pl.BlockSpec((None, tb, RW_COLS), lambda bi, ti: (bi, ti, 0)),
                  vec, mat(DECAY_LORA), vec, mat(AAA_LORA), mat(GATE_LORA), vec, vec, vec, vec, vec],
        out_specs=pl.BlockSpec((None, tb, w), lambda bi, ti: (bi, ti, 0)),
        out_shape=jax.ShapeDtypeStruct((b, t, w), BF16),
        scratch_shapes=[pltpu.VMEM((RW_HEADS, RW_HEAD_DIM, RW_HEAD_DIM), F32)]
        + [pltpu.VMEM((tb, w), F32)] * 8,
        compiler_params=_params("parallel", "arbitrary"),
        name="rwkv",
    )(rw, w0, w2, a0, a2, g2, k_k, k_a, r_k.reshape(1, w), ln_w, ln_b)


def _pack_rows(x):
    half = x.shape[1] // 2
    hi = pltpu.bitcast(x[:, :half].astype(BF16).astype(F32), jnp.uint32)
    lo = pltpu.bitcast(x[:, half:].astype(BF16).astype(F32), jnp.uint32)
    return hi | (lo >> 16)


def _unpack_rows(u):
    hi = pltpu.bitcast(u & jnp.uint32(0xFFFF0000), F32)
    lo = pltpu.bitcast(u << 16, F32)
    return jnp.concatenate([hi, lo], axis=1)


ROW_SLAB = 4


def _store_rows(ref, u, r0=0):
    n = u.shape[0]
    for c in range(ROW_SLAB):
        ref[pl.ds(r0 * ROW_SLAB + c, n, stride=ROW_SLAB), :] = u[:, c * LANES:(c + 1) * LANES]


def _load_rows(ref, n, r0=0):
    return jnp.concatenate([ref[pl.ds(r0 * ROW_SLAB + c, n, stride=ROW_SLAB), :]
                            for c in range(ROW_SLAB)], axis=1)


def _out_kernel(yda_ref, yrw_ref, x_ref, mod_ref, wo_ref, pmn_ref, pfn_ref, rw_ref, rb_ref,
                x1_ref, h2_ref, idx_ref, wgt_ref):
    y = (jnp.dot(yda_ref[...], wo_ref[0:DA_WIDTH, :], preferred_element_type=F32)
         + jnp.dot(yrw_ref[...], wo_ref[DA_WIDTH:, :], preferred_element_type=F32))
    x1 = x_ref[...] + mod_ref[2:3, :] * _rms(y, pmn_ref[...], NORM_EPS)
    x1_ref[...] = x1
    h2 = _rms(x1, pfn_ref[...], NORM_EPS) * (1.0 + mod_ref[4:5, :]) + mod_ref[3:4, :]
    _store_rows(h2_ref, _pack_rows(h2))

    h_hi = h2.astype(BF16)
    h_lo = (h2 - h_hi.astype(F32)).astype(BF16)
    rw = rw_ref[...]
    w_hi = rw.astype(BF16)
    w_lo = (rw - w_hi.astype(F32)).astype(BF16)
    logits = (jnp.dot(h_hi, w_hi, preferred_element_type=F32)
              + jnp.dot(h_hi, w_lo, preferred_element_type=F32)
              + jnp.dot(h_lo, w_hi, preferred_element_type=F32)) + rb_ref[...]

    lane = lax.broadcasted_iota(jnp.int32, logits.shape, 1)
    slot = lax.broadcasted_iota(jnp.int32, idx_ref.shape, 1)
    idx = jnp.zeros(idx_ref.shape, jnp.int32)
    val = jnp.zeros(idx_ref.shape, F32)
    top = None
    for j in range(TOP_K):
        m = jnp.max(logits, axis=-1, keepdims=True)
        i = jnp.min(jnp.where(logits == m, lane, N_EXPERTS), axis=-1, keepdims=True)
        top = m if top is None else top
        idx = jnp.where(slot == j, i, idx)
        val = jnp.where(slot == j, jnp.exp(m - top), val)
        logits = jnp.where(lane == i, -jnp.inf, logits)
    idx_ref[...] = idx
    wgt_ref[...] = val / jnp.sum(val, axis=-1, keepdims=True)


def _out(y_da, y_rw, x, mod3, w_out_b, post_mix_norm, pre_ffn_norm, router_w, router_b):
    b, t, d = x.shape
    tm = min(OUT_ROWS, t)
    e = router_w.shape[1]
    blk = lambda w: pl.BlockSpec((None, tm, w), lambda bi, ti: (bi, ti, 0))
    full = lambda r, c: pl.BlockSpec((r, c), lambda bi, ti: (0, 0))
    return pl.pallas_call(
        _out_kernel,
        grid=(b, t // tm),
        in_specs=[blk(DA_WIDTH), blk(RW_WIDTH), blk(d),
                  pl.BlockSpec((None, N_MOD, d), lambda bi, ti: (bi, 0, 0)),
                  full(d, d), full(1, d), full(1, d), full(d, e), full(1, e)],
        out_specs=[blk(d),
                   pl.BlockSpec((None, tm * ROW_SLAB, LANES), lambda bi, ti: (bi, ti, 0)),
                   blk(TOP_K), blk(TOP_K)],
        out_shape=[jax.ShapeDtypeStruct((b, t, d), F32),
                   jax.ShapeDtypeStruct((b, t * ROW_SLAB, LANES), jnp.uint32),
                   jax.ShapeDtypeStruct((b, t, TOP_K), jnp.int32),
                   jax.ShapeDtypeStruct((b, t, TOP_K), F32)],
        compiler_params=_params("parallel", "parallel"),
        name="out",
    )(y_da, y_rw, x, mod3, w_out_b, post_mix_norm, pre_ffn_norm, router_w,
      router_b.reshape(1, e))


def _route(top_idx, rows_per_tile, n_tiles):
    e_flat = top_idx.reshape(-1)
    onehot = (e_flat[:, None] == jnp.arange(N_EXPERTS, dtype=jnp.int32)[None, :]).astype(jnp.int32)
    csum = jnp.cumsum(onehot, axis=0)
    counts = csum[-1]
    padded = (counts + rows_per_tile - 1) // rows_per_tile * rows_per_tile
    ends = jnp.cumsum(padded)
    starts = ends - padded
    pos = jnp.sum((csum - onehot + starts[None, :]) * onehot, axis=1)
    n_active = ends[-1] // rows_per_tile
    tile_start = jnp.arange(n_tiles, dtype=jnp.int32) * rows_per_tile
    tile = jnp.minimum(tile_start, ends[-1] - 1)
    tile_expert = jnp.sum((tile[:, None] >= ends[None, :]).astype(jnp.int32), axis=1)
    return pos.astype(jnp.int32), tile_expert.astype(jnp.int32), n_active.reshape(1).astype(jnp.int32)


def _dispatch_kernel(pos_ref, h_ref, xs_in_ref, xs_ref, sem):
    del xs_in_ref
    tb = h_ref.shape[0] // ROW_SLAB

    def issue(t, carry):
        src = h_ref.at[pl.ds(pl.multiple_of(t * ROW_SLAB, ROW_SLAB), ROW_SLAB)]
        for j in range(TOP_K):
            pltpu.make_async_copy(src, xs_ref.at[pos_ref[0, t * TOP_K + j]],
                                  sem).start(priority=j % 2)
        return carry

    lax.fori_loop(0, tb, issue, 0, unroll=8)
    for j in range(TOP_K):
        pltpu.make_async_copy(xs_ref.at[pl.ds(0, tb)], xs_ref.at[pl.ds(0, tb)], sem).wait()


def _dispatch(pos, h2p, n_rows):
    n = h2p.shape[0] // ROW_SLAB
    w = h2p.shape[1]
    tb = min(DISPATCH_TOKENS, n)
    pos3 = pos.reshape(n // tb, 1, tb * TOP_K)
    xs0 = jnp.zeros((n_rows, ROW_SLAB, w), h2p.dtype)
    return pl.pallas_call(
        _dispatch_kernel,
        grid=(n // tb,),
        in_specs=[pl.BlockSpec((None, 1, tb * TOP_K), lambda i: (i, 0, 0),
                               memory_space=pltpu.SMEM),
                  pl.BlockSpec((tb * ROW_SLAB, w), lambda i: (i, 0)),
                  pl.BlockSpec(memory_space=pl.ANY)],
        out_specs=pl.BlockSpec(memory_space=pl.ANY),
        out_shape=jax.ShapeDtypeStruct(xs0.shape, h2p.dtype),
        scratch_shapes=[pltpu.SemaphoreType.DMA(())],
        input_output_aliases={2: 0},
        compiler_params=_params("arbitrary"),
        name="dispatch",
    )(pos3, h2p, xs0)


def _expert_kernel(te_ref, na_ref, xs_ref, w1_ref, b1_ref, w2_ref, b2_ref, ys_ref,
                   w1p_s, b1p_s, w2b_s, act_s):
    i = pl.program_id(0)
    active = i < na_ref[0]
    fresh = jnp.logical_or(i == 0, te_ref[i] != te_ref[jnp.maximum(i - 1, 0)])
    grp = 2 * LANES
    n_grp = w1_ref.shape[1] // grp

    @pl.when(jnp.logical_and(active, fresh))
    def _():
        src = lax.broadcasted_iota(jnp.int32, (grp, grp), 0)
        dst = lax.broadcasted_iota(jnp.int32, (grp, grp), 1)
        perm = jnp.where(src == jnp.where(dst < LANES, 2 * dst, 2 * (dst - LANES) + 1),
                         1.0, 0.0).astype(BF16)
        for g in range(n_grp):
            sl = slice(g * grp, (g + 1) * grp)
            w1p_s[:, sl] = jnp.dot(w1_ref[:, sl].astype(BF16), perm,
                                   preferred_element_type=F32).astype(BF16)
            b = b1_ref[:, sl]
            b_hi = b.astype(BF16)
            b_lo = (b - b_hi.astype(F32)).astype(BF16)
            b1p_s[:, sl] = (jnp.dot(b_hi, perm, preferred_element_type=F32)
                            + jnp.dot(b_lo, perm, preferred_element_type=F32))
        w2b_s[...] = w2_ref[...].astype(BF16)

    @pl.when(active)
    def _():
        x = _unpack_rows(_load_rows(xs_ref, act_s.shape[0])).astype(BF16)
        hid = jnp.dot(x, w1p_s[...], preferred_element_type=F32) + b1p_s[0:1, :]
        for g in range(n_grp):
            glu = jnp.minimum(hid[:, g * grp:g * grp + LANES], SWIGLU_LIMIT)
            lin = jnp.clip(hid[:, g * grp + LANES:(g + 1) * grp], -SWIGLU_LIMIT, SWIGLU_LIMIT)
            act_s[:, g * LANES:(g + 1) * LANES] = (
                glu * jax.nn.sigmoid(SWIGLU_ALPHA * glu) * (lin + 1.0)).astype(BF16)
        y = jnp.dot(act_s[...], w2b_s[...], preferred_element_type=F32) + b2_ref[...]
        _store_rows(ys_ref, _pack_rows(y))

    @pl.when(jnp.logical_not(active))
    def _():
        ys_ref[...] = jnp.zeros_like(ys_ref)


def _experts(tile_expert, n_active, xs, w1, b1, w2, b2):
    n_rows, w = xs.shape[0] // ROW_SLAB, xs.shape[1]
    tm = EXPERT_ROWS
    d, f2 = w1.shape[1], w1.shape[2]
    f = f2 // 2
    wspec = lambda r, c: pl.BlockSpec((None, r, c), lambda i, te, na: (te[i], 0, 0))
    rows = pl.BlockSpec((tm * ROW_SLAB, w), lambda i, te, na: (i, 0))
    return pl.pallas_call(
        _expert_kernel,
        grid_spec=pltpu.PrefetchScalarGridSpec(
            num_scalar_prefetch=2,
            grid=(n_rows // tm,),
            in_specs=[rows, wspec(d, f2), wspec(8, f2), wspec(f, d), wspec(1, d)],
            out_specs=rows,
            scratch_shapes=[pltpu.VMEM((d, f2), BF16), pltpu.VMEM((8, f2), F32),
                            pltpu.VMEM((f, d), BF16), pltpu.VMEM((tm, f), BF16)]),
        out_shape=jax.ShapeDtypeStruct((n_rows * ROW_SLAB, w), jnp.uint32),
        compiler_params=_params("arbitrary"),
        name="expert",
    )(tile_expert, n_active, xs, w1, b1, w2, b2)


def _combine_kernel(pos_ref, ys_ref, wgt_ref, x1_ref, mod_ref, nw_ref, o_ref, buf_ref, sem):
    tc = x1_ref.shape[0]

    def issue(t, carry):
        dst = pl.ds(pl.multiple_of(t * ROW_SLAB, ROW_SLAB), ROW_SLAB)
        for j in range(TOP_K):
            pltpu.make_async_copy(ys_ref.at[pos_ref[0, t * TOP_K + j]], buf_ref.at[j, dst],
                                  sem).start(priority=j % 2)
        return carry

    lax.fori_loop(0, tc, issue, 0, unroll=8)
    for j in range(TOP_K):
        pltpu.make_async_copy(ys_ref.at[pl.ds(0, tc)], ys_ref.at[pl.ds(0, tc)], sem).wait()

    sub = min(COMBINE_SUB_ROWS, tc)

    def mix(i, carry):
        r0 = pl.multiple_of(i * sub, sub)
        wgt = wgt_ref[pl.ds(r0, sub), :]
        acc = jnp.zeros((sub, x1_ref.shape[1]), F32)
        for j in range(TOP_K):
            acc = acc + wgt[:, j:j + 1] * _unpack_rows(_load_rows(buf_ref.at[j], sub, r0))
        o_ref[pl.ds(r0, sub), :] = (x1_ref[pl.ds(r0, sub), :]
                                    + mod_ref[5:6, :] * _rms(acc, nw_ref[...], NORM_EPS))
        return carry

    lax.fori_loop(0, tc // sub, mix, 0)


def _combine(pos, ys, wgt, x1, mod3, post_ffn_norm):
    b, t, d = x1.shape
    tc = min(COMBINE_TOKENS, t)
    nt = t // tc
    pos3 = pos.reshape(b * nt, 1, tc * TOP_K)
    blk = lambda w: pl.BlockSpec((None, tc, w), lambda bi, ti: (bi, ti, 0))
    return pl.pallas_call(
        _combine_kernel,
        grid=(b, nt),
        in_specs=[pl.BlockSpec((None, 1, tc * TOP_K), lambda bi, ti: (bi * nt + ti, 0, 0),
                               memory_space=pltpu.SMEM),
                  pl.BlockSpec(memory_space=pl.ANY),
                  blk(TOP_K), blk(d),
                  pl.BlockSpec((None, N_MOD, d), lambda bi, ti: (bi, 0, 0)),
                  pl.BlockSpec((1, d), lambda bi, ti: (0, 0))],
        out_specs=blk(d),
        out_shape=jax.ShapeDtypeStruct((b, t, d), F32),
        scratch_shapes=[pltpu.VMEM((TOP_K, tc * ROW_SLAB, LANES), jnp.uint32),
                        pltpu.SemaphoreType.DMA(())],
        compiler_params=_params("arbitrary", "arbitrary"),
        name="combine",
    )(pos3, ys.reshape(ys.shape[0] // ROW_SLAB, ROW_SLAB, LANES), wgt, x1, mod3, post_ffn_norm)


def _stages(x, c, positions, ada_w, ada_b, pre_mix_norm, post_mix_norm, pre_ffn_norm,
            post_ffn_norm, w_in, w_out, da_lambda_q1, da_lambda_k1, da_lambda_q2, da_lambda_k2,
            da_subln, rw_mu, rw_w0, rw_w2, rw_a0, rw_a2, rw_g2, rw_k_k, rw_k_a, rw_r_k, rw_ln_w,
            rw_ln_b, router_w, router_b, moe_w1, moe_b1, moe_w2, moe_b2):
    b, t, d = x.shape
    res = {}
    lambda_init = 0.8 - 0.6 * math.exp(-0.3 * 0)
    mod = _mod(c, ada_w[0], ada_b[0])
    res["mod"] = mod
    mod3 = mod.reshape(b, N_MOD, d)
    inv_freq = ROPE_THETA ** (-jnp.arange(0, ROPE_DIM, 2, dtype=F32) / ROPE_DIM)
    invf = jnp.tile(inv_freq, LANES // (ROPE_DIM // 2)).reshape(1, LANES)
    q, k, v, rw = _proj(x, positions.reshape(b, t, 1), mod3, pre_mix_norm, invf,
                        w_in[0].astype(BF16), rw_mu)
    res.update(q=q, k=k, v=v, rw=rw)
    lam4 = jnp.concatenate([da_lambda_q1, da_lambda_k1, da_lambda_q2, da_lambda_k2], axis=0)
    y_da = _attn(q, k, v, lam4, da_subln, lambda_init)
    res["y_da"] = y_da
    y_rw = _rwkv(rw, rw_w0, rw_w2[0], rw_a0, rw_a2[0], rw_g2[0], rw_k_k, rw_k_a, rw_r_k[0],
                 rw_ln_w, rw_ln_b)
    res["y_rw"] = y_rw
    x1, h2p, top_idx, top_w = _out(y_da, y_rw, x, mod3, w_out[0].astype(BF16), post_mix_norm,
                                   pre_ffn_norm, router_w[0], router_b[0])
    res.update(x1=x1, top_idx=top_idx, top_w=top_w)
    n = b * t
    n_tiles = n * TOP_K // EXPERT_ROWS + N_EXPERTS
    pos, tile_expert, n_active = _route(top_idx.reshape(n, TOP_K), EXPERT_ROWS, n_tiles)
    xs = _dispatch(pos, h2p.reshape(n * ROW_SLAB, LANES), n_tiles * EXPERT_ROWS)
    xs = xs.reshape(n_tiles * EXPERT_ROWS * ROW_SLAB, LANES)
    b1 = jnp.broadcast_to(moe_b1[0][:, None, :], (N_EXPERTS, 8, moe_b1.shape[-1]))
    ys = _experts(tile_expert, n_active, xs, moe_w1[0], b1, moe_w2[0], moe_b2[0][:, None, :])
    res["final"] = _combine(pos, ys, top_w, x1, mod3, post_ffn_norm)
    return res


stages = _stages


def kernel(x, c, positions, ada_w, ada_b, pre_mix_norm, post_mix_norm, pre_ffn_norm, post_ffn_norm, w_in, w_out, da_lambda_q1, da_lambda_k1, da_lambda_q2, da_lambda_k2, da_subln, rw_mu, rw_w0, rw_w2, rw_a0, rw_a2, rw_g2, rw_k_k, rw_k_a, rw_r_k, rw_ln_w, rw_ln_b, router_w, router_b, moe_w1, moe_b1, moe_w2, moe_b2):
    res = _stages(x, c, positions, ada_w, ada_b, pre_mix_norm, post_mix_norm, pre_ffn_norm,
                  post_ffn_norm, w_in, w_out, da_lambda_q1, da_lambda_k1, da_lambda_q2,
                  da_lambda_k2, da_subln, rw_mu, rw_w0, rw_w2, rw_a0, rw_a2, rw_g2, rw_k_k,
                  rw_k_a, rw_r_k, rw_ln_w, rw_ln_b, router_w, router_b, moe_w1, moe_b1,
                  moe_w2, moe_b2)
    return res["final"]
```

```python
import functools
import math

import jax
import jax.numpy as jnp
from jax import lax
from jax.experimental import pallas as pl
from jax.experimental.pallas import tpu as pltpu

F32 = jnp.float32
BF16 = jnp.bfloat16

DA_HEADS = 4
DA_HEAD_DIM = 64
DA_V_DIM = 128
DA_WIDTH = 512
RW_HEADS = 8
RW_HEAD_DIM = 64
RW_WIDTH = 512
DECAY_LORA = 64
AAA_LORA = 64
GATE_LORA = 128
DA_COLS = 1536
RW_COLS = 1792
ROPE_THETA = 500000.0
ROPE_DIM = 16
N_EXPERTS = 32
TOP_K = 4
SWIGLU_ALPHA = 1.702
SWIGLU_LIMIT = 7.0
NORM_EPS = 1e-6
SUBLN_EPS = 1e-5
LN_X_EPS = 64e-5
N_MOD = 6

LANES = 128
SUBLANES = 8
VMEM_LIMIT_BYTES = 56 * 1024 * 1024

PROJ_ROWS = 512
ATTN_BLOCK = 256
ATTN_KV_BLOCK = 512
ATTN_HEAD_GROUP = 2
RW_CHUNK = 128
RW_BLOCK = 256
OUT_ROWS = 512
EXPERT_ROWS = 512
DISPATCH_TOKENS = 2048
COMBINE_TOKENS = 1024
COMBINE_SUB_ROWS = 256


def _params(*sem):
    return pltpu.CompilerParams(dimension_semantics=sem, vmem_limit_bytes=VMEM_LIMIT_BYTES)


def _bdot(a, b):
    return jnp.dot(a.astype(BF16), b.astype(BF16), preferred_element_type=F32)


def _bdot_nt(a, b):
    return lax.dot_general(a.astype(BF16), b.astype(BF16), (((1,), (1,)), ((), ())),
                           preferred_element_type=F32)


def _bdot_tn(a, b):
    return lax.dot_general(a.astype(BF16), b.astype(BF16), (((0,), (0,)), ((), ())),
                           preferred_element_type=F32)


def _rms(x, w, eps):
    return x * lax.rsqrt(jnp.mean(x * x, axis=-1, keepdims=True) + eps) * w


def _mod_kernel(c_ref, w_ref, b_ref, o_ref):
    c = c_ref[...]
    s = c * jax.nn.sigmoid(c)
    o_ref[...] = _bdot(s, w_ref[...]) + b_ref[...]


def _mod(c, ada_w, ada_b):
    b, d = c.shape
    n = ada_w.shape[1]
    return pl.pallas_call(
        _mod_kernel,
        grid=(n // d,),
        in_specs=[pl.BlockSpec((b, d), lambda j: (0, 0)),
                  pl.BlockSpec((d, d), lambda j: (0, j)),
                  pl.BlockSpec((1, d), lambda j: (0, j))],
        out_specs=pl.BlockSpec((b, d), lambda j: (0, j)),
        out_shape=jax.ShapeDtypeStruct((b, n), F32),
        compiler_params=_params("parallel"),
        name="mod",
    )(c, ada_w, ada_b.reshape(1, n))


def _proj_kernel(x_ref, pos_ref, mod_ref, nw_ref, invf_ref, w_ref, mu_ref,
                 q_ref, k_ref, v_ref, rw_ref, carry_ref):
    ti = pl.program_id(1)

    @pl.when(ti == 0)
    def _():
        carry_ref[...] = jnp.zeros_like(carry_ref)

    x = x_ref[...]
    h = _rms(x, nw_ref[...], NORM_EPS) * (1.0 + mod_ref[1:2, :]) + mod_ref[0:1, :]
    hb = h.astype(BF16)

    ang = pos_ref[...].astype(F32) * invf_ref[...]
    cos, sin = jnp.cos(ang), jnp.sin(ang)
    l64 = lax.broadcasted_iota(jnp.int32, ang.shape, 1) % DA_HEAD_DIM
    half = ROPE_DIM // 2
    c_tab = jnp.where(l64 < ROPE_DIM, cos, 1.0)
    s_lo = jnp.where(l64 < half, -sin, 0.0)
    s_hi = jnp.where((l64 >= half) & (l64 < ROPE_DIM), sin, 0.0)

    def rope(z):
        up = pltpu.roll(z, LANES - half, axis=1)
        dn = pltpu.roll(z, half, axis=1)
        return z * c_tab + up * s_lo + dn * s_hi

    for g in range(DA_WIDTH // LANES):
        sl = slice(g * LANES, (g + 1) * LANES)
        qg = jnp.dot(hb, w_ref[:, sl], preferred_element_type=F32)
        q_ref[:, sl] = (rope(qg) * (DA_HEAD_DIM ** -0.5)).astype(q_ref.dtype)
        kg = jnp.dot(hb, w_ref[:, DA_WIDTH + g * LANES:DA_WIDTH + (g + 1) * LANES],
                     preferred_element_type=F32)
        k_ref[:, sl] = rope(kg).astype(k_ref.dtype)
    v_ref[...] = jnp.dot(hb, w_ref[:, 2 * DA_WIDTH:DA_COLS],
                         preferred_element_type=F32).astype(v_ref.dtype)

    p = jnp.dot(hb, w_ref[:, DA_COLS:], preferred_element_type=F32)
    rows = p.shape[0]
    prev = pltpu.roll(p, 1, axis=0)
    first = lax.broadcasted_iota(jnp.int32, p.shape, 0) == 0
    prev = jnp.where(first, carry_ref[0:1, :], prev)
    rw_ref[...] = p + (prev - p) * mu_ref[...]
    carry_ref[0:1, :] = p[rows - 1:rows, :]


def _proj(x, pos3, mod3, norm_w, invf, w_in_b, mu):
    b, t, d = x.shape
    tm = min(PROJ_ROWS, t)
    n_in = w_in_b.shape[1]
    blk = lambda w: pl.BlockSpec((None, tm, w), lambda bi, ti: (bi, ti, 0))
    full = lambda r, c: pl.BlockSpec((r, c), lambda bi, ti: (0, 0))
    return pl.pallas_call(
        _proj_kernel,
        grid=(b, t // tm),
        in_specs=[blk(d), blk(1),
                  pl.BlockSpec((None, N_MOD, d), lambda bi, ti: (bi, 0, 0)),
                  full(1, d), full(1, LANES), full(d, n_in), full(1, RW_COLS)],
        out_specs=[blk(DA_WIDTH), blk(DA_WIDTH), blk(DA_WIDTH), blk(RW_COLS)],
        out_shape=[jax.ShapeDtypeStruct((b, t, DA_WIDTH), BF16)] * 3
        + [jax.ShapeDtypeStruct((b, t, RW_COLS), F32)],
        scratch_shapes=[pltpu.VMEM((8, RW_COLS), F32)],
        compiler_params=_params("parallel", "arbitrary"),
        name="proj",
    )(x, pos3, mod3, norm_w, invf, w_in_b, mu)


def _attn_kernel(q_ref, k_ref, v_ref, lam_ref, subln_ref, o_ref, m_ref, l_ref, acc_ref,
                 *, lambda_init):
    qi = pl.program_id(2)
    tq = q_ref.shape[0]
    heads = range(ATTN_HEAD_GROUP)
    hs = [slice(h * DA_V_DIM, (h + 1) * DA_V_DIM) for h in heads]
    lane = lax.broadcasted_iota(jnp.int32, (tq, DA_V_DIM), 1)
    qq = []
    for c in hs:
        q = q_ref[:, c]
        zero = jnp.zeros_like(q)
        qq.append(jnp.concatenate([jnp.where(lane < DA_HEAD_DIM, q, zero),
                                   jnp.where(lane >= DA_HEAD_DIM, q, zero)], axis=0))

    m_ref[...] = jnp.full(m_ref.shape, -jnp.inf, F32)
    l_ref[...] = jnp.zeros(l_ref.shape, F32)
    acc_ref[...] = jnp.zeros(acc_ref.shape, F32)
    tk = ATTN_KV_BLOCK if k_ref.shape[0] % ATTN_KV_BLOCK == 0 else tq
    rep = tk // LANES

    def step(j, masked):
        rows = pl.ds(pl.multiple_of(j * tk, tk), tk)
        s = [lax.dot_general(qq[h], k_ref[rows, hs[h]], (((1,), (1,)), ((), ())),
                             preferred_element_type=F32) for h in heads]
        if masked:
            qpos = qi * tq + lax.broadcasted_iota(jnp.int32, s[0].shape, 0) % tq
            kpos = j * tk + lax.broadcasted_iota(jnp.int32, s[0].shape, 1)
            s = [jnp.where(qpos >= kpos, x, -jnp.inf) for x in s]
        for h in heads:
            m_old = m_ref[h]
            m_new = jnp.maximum(m_old, jnp.max(s[h], axis=-1, keepdims=True))
            alpha = jnp.exp(m_old - m_new)
            p = jnp.exp(s[h] - jnp.concatenate([m_new] * rep, axis=1))
            l_ref[h] = alpha * l_ref[h] + jnp.sum(p, axis=-1, keepdims=True)
            acc_ref[h] = alpha * acc_ref[h] + jnp.dot(p.astype(v_ref.dtype), v_ref[rows, hs[h]],
                                                      preferred_element_type=F32)
            m_ref[h] = m_new

    def body(j, carry):
        step(j, False)
        return carry

    n_full = (qi * tq) // tk
    lax.fori_loop(0, n_full, body, 0)
    step(n_full, True)

    lam = (jnp.exp(jnp.sum(lam_ref[0:1, :] * lam_ref[1:2, :], axis=-1, keepdims=True))
           - jnp.exp(jnp.sum(lam_ref[2:3, :] * lam_ref[3:4, :], axis=-1, keepdims=True))
           + lambda_init)
    for h in heads:
        o = acc_ref[h] / l_ref[h]
        d = o[:tq, :] - lam * o[tq:, :]
        o_ref[:, hs[h]] = (_rms(d, subln_ref[...], SUBLN_EPS)
                           * (1.0 - lambda_init)).astype(o_ref.dtype)


def _attn(q, k, v, lam4, subln, lambda_init):
    b, t, _ = q.shape
    tq = min(ATTN_BLOCK, t)
    hg = ATTN_HEAD_GROUP
    gw = hg * DA_V_DIM
    return pl.pallas_call(
        functools.partial(_attn_kernel, lambda_init=lambda_init),
        grid=(b, DA_HEADS // hg, t // tq),
        in_specs=[pl.BlockSpec((None, tq, gw), lambda bi, h, qi: (bi, qi, h)),
                  pl.BlockSpec((None, t, gw), lambda bi, h, qi: (bi, 0, h)),
                  pl.BlockSpec((None, t, gw), lambda bi, h, qi: (bi, 0, h)),
                  pl.BlockSpec((4, DA_HEAD_DIM), lambda bi, h, qi: (0, 0)),
                  pl.BlockSpec((1, DA_V_DIM), lambda bi, h, qi: (0, 0))],
        out_specs=pl.BlockSpec((None, tq, gw), lambda bi, h, qi: (bi, qi, h)),
        out_shape=jax.ShapeDtypeStruct((b, t, DA_WIDTH), BF16),
        scratch_shapes=[pltpu.VMEM((hg, 2 * tq, LANES), F32), pltpu.VMEM((hg, 2 * tq, LANES), F32),
                        pltpu.VMEM((hg, 2 * tq, DA_V_DIM), F32)],
        compiler_params=_params("parallel", "parallel", "arbitrary"),
        name="attn",
    )(q, k, v, lam4, subln)


def _rwkv_kernel(rw_ref, w0_ref, w2_ref, a0_ref, a2_ref, g2_ref, kk_ref, ka_ref, rk_ref,
                 lnw_ref, lnb_ref, o_ref, state_ref, r_s, k_s, v_s, lw_s, kk_s, a_s, g_s, cum_s):
    ti = pl.program_id(1)

    @pl.when(ti == 0)
    def _():
        state_ref[...] = jnp.zeros_like(state_ref)

    w = RW_WIDTH
    rw = rw_ref[...]
    k = rw[:, w:2 * w]
    wl = rw[:, 3 * w:3 * w + DECAY_LORA]
    al = rw[:, 3 * w + DECAY_LORA:3 * w + DECAY_LORA + AAA_LORA]
    gl = rw[:, 3 * w + DECAY_LORA + AAA_LORA:]
    z = -(w0_ref[...] + _bdot(jnp.tanh(wl), w2_ref[...]))
    softplus = jnp.maximum(z, 0.0) + jnp.log(1.0 + jnp.exp(-jnp.abs(z)))
    a = jax.nn.sigmoid(a0_ref[...] + _bdot(al, a2_ref[...]))
    r_s[...] = rw[:, 0:w]
    v_s[...] = rw[:, 2 * w:3 * w]
    lw_s[...] = -jnp.exp(-softplus - 0.5)
    a_s[...] = a
    g_s[...] = _bdot(jax.nn.sigmoid(gl), g2_ref[...])
    kk_s[...] = k * kk_ref[...]
    k_s[...] = k * (1.0 + (a - 1.0) * ka_ref[...])

    c_len = RW_CHUNK
    n = RW_HEAD_DIM
    tb = rw_ref.shape[0]

    br = lax.broadcasted_iota(jnp.int32, (tb, tb), 0)
    bc = lax.broadcasted_iota(jnp.int32, (tb, tb), 1)
    tri = jnp.where((br >= bc) & (br // c_len == bc // c_len), 1.0, 0.0).astype(BF16)
    lw_all = lw_s[...]
    lw_hi = lw_all.astype(BF16)
    rem = lw_all - lw_hi.astype(F32)
    lw_mid = rem.astype(BF16)
    lw_lo = (rem - lw_mid.astype(F32)).astype(BF16)
    cum_s[...] = (jnp.dot(tri, lw_hi, preferred_element_type=F32)
                  + jnp.dot(tri, lw_mid, preferred_element_type=F32)
                  + jnp.dot(tri, lw_lo, preferred_element_type=F32))

    row = lax.broadcasted_iota(jnp.int32, (c_len, 2 * c_len), 0)
    col = lax.broadcasted_iota(jnp.int32, (c_len, 2 * c_len), 1)
    incl2 = row >= col % c_len
    strict2 = row > col % c_len
    eye = jnp.where(lax.broadcasted_iota(jnp.int32, (c_len, c_len), 0)
                    == lax.broadcasted_iota(jnp.int32, (c_len, c_len), 1), 1.0, 0.0).astype(F32)

    def chunk(ci, carry):
        rows = pl.ds(pl.multiple_of(ci * c_len, c_len), c_len)
        heads = range(RW_HEADS)
        sl = [slice(h * n, (h + 1) * n) for h in heads]
        r = [r_s[rows, c] for c in sl]
        kh = [k_s[rows, c] for c in sl]
        v = [v_s[rows, c] for c in sl]
        lw = [lw_s[rows, c] for c in sl]
        cum = [cum_s[rows, c] for c in sl]
        kk = [kk_s[rows, c] for c in sl]
        kk = [x / jnp.maximum(jnp.sqrt(jnp.sum(x * x, axis=-1, keepdims=True)), 1e-12) for x in kk]
        kka = [kk[h] * a_s[rows, sl[h]] for h in heads]
        end = [jnp.sum(x, axis=0, keepdims=True) for x in lw]
        e_neg = [jnp.exp(-x) for x in cum]
        e_end = [jnp.exp(end[h] - cum[h]) for h in heads]
        left = [jnp.concatenate([-kk[h] * jnp.exp(cum[h] - lw[h]), r[h] * jnp.exp(cum[h])], axis=0)
                for h in heads]
        g = [_bdot_nt(left[h], jnp.concatenate([kka[h] * e_neg[h], kh[h] * e_neg[h]], axis=0))
             for h in heads]
        a_a = [jnp.where(strict2, x[:c_len, :], 0.0) for x in g]
        a_r = [jnp.where(incl2, x[c_len:, :], 0.0) for x in g]
        pw = [x[:, :c_len] for x in a_a]
        inv = [eye + x for x in pw]
        for _ in range(c_len.bit_length() - 2):
            pw = [_bdot(x, x) for x in pw]
            inv = [inv[h] + _bdot(inv[h], pw[h]) for h in heads]
        akv = [_bdot(a_a[h][:, c_len:], v[h]) for h in heads]
        s0 = [state_ref[h] for h in heads]
        ls = [_bdot_nt(left[h], s0[h]) for h in heads]
        u = [_bdot(inv[h], ls[h][:c_len, :] + akv[h]) for h in heads]
        uv = [jnp.concatenate([u[h], v[h]], axis=0) for h in heads]
        y = [ls[h][c_len:, :] + _bdot(a_r[h], uv[h]) for h in heads]
        for h in heads:
            state_ref[h] = s0[h] * jnp.exp(end[h]) + _bdot_tn(
                uv[h], jnp.concatenate([kka[h] * e_end[h], kh[h] * e_end[h]], axis=0))
        for h in heads:
            mean = jnp.mean(y[h], axis=-1, keepdims=True)
            yc = y[h] - mean
            var = jnp.mean(yc * yc, axis=-1, keepdims=True)
            yn = yc * lax.rsqrt(var + LN_X_EPS) * lnw_ref[:, sl[h]] + lnb_ref[:, sl[h]]
            bonus = jnp.sum(r[h] * kh[h] * rk_ref[:, sl[h]], axis=-1, keepdims=True) * v[h]
            o_ref[rows, sl[h]] = ((yn + bonus) * g_s[rows, sl[h]]).astype(o_ref.dtype)
        return carry

    lax.fori_loop(0, rw_ref.shape[0] // c_len, chunk, 0, unroll=2)


def _rwkv(rw, w0, w2, a0, a2, g2, k_k, k_a, r_k, ln_w, ln_b):
    b, t, _ = rw.shape
    tb = min(RW_BLOCK, t)
    w = RW_WIDTH
    vec = pl.BlockSpec((1, w), lambda bi, ti: (0, 0))
    mat = lambda r: pl.BlockSpec((r, w), lambda bi, ti: (0, 0))
    return pl.pallas_call(
        _rwkv_kernel,
        grid=(b, t // tb),
        in_specs=[pl.BlockSpec((None, tb, RW_COLS), lambda bi, ti: (bi, ti, 0)),
                  vec, mat(DECAY_LORA), vec, mat(AAA_LORA), mat(GATE_LORA), vec, vec, vec, vec, vec],
        out_specs=pl.BlockSpec((None, tb, w), lambda bi, ti: (bi, ti, 0)),
        out_shape=jax.ShapeDtypeStruct((b, t, w), BF16),
        scratch_shapes=[pltpu.VMEM((RW_HEADS, RW_HEAD_DIM, RW_HEAD_DIM), F32)]
        + [pltpu.VMEM((tb, w), F32)] * 8,
        compiler_params=_params("parallel", "arbitrary"),
        name="rwkv",
    )(rw, w0, w2, a0, a2, g2, k_k, k_a, r_k.reshape(1, w), ln_w, ln_b)


def _pack_rows(x):
    half = x.shape[1] // 2
    hi = pltpu.bitcast(x[:, :half].astype(BF16).astype(F32), jnp.uint32)
    lo = pltpu.bitcast(x[:, half:].astype(BF16).astype(F32), jnp.uint32)
    return hi | (lo >> 16)


def _unpack_rows(u):
    hi = pltpu.bitcast(u & jnp.uint32(0xFFFF0000), F32)
    lo = pltpu.bitcast(u << 16, F32)
    return jnp.concatenate([hi, lo], axis=1)


ROW_SLAB = 4


def _store_rows(ref, u, r0=0):
    n = u.shape[0]
    for c in range(ROW_SLAB):
        ref[pl.ds(r0 * ROW_SLAB + c, n, stride=ROW_SLAB), :] = u[:, c * LANES:(c + 1) * LANES]


def _load_rows(ref, n, r0=0):
    return jnp.concatenate([ref[pl.ds(r0 * ROW_SLAB + c, n, stride=ROW_SLAB), :]
                            for c in range(ROW_SLAB)], axis=1)


def _out_kernel(yda_ref, yrw_ref, x_ref, mod_ref, wo_ref, pmn_ref, pfn_ref, rw_ref, rb_ref,
                x1_ref, h2_ref, idx_ref, wgt_ref):
    y = (jnp.dot(yda_ref[...], wo_ref[0:DA_WIDTH, :], preferred_element_type=F32)
         + jnp.dot(yrw_ref[...], wo_ref[DA_WIDTH:, :], preferred_element_type=F32))
    x1 = x_ref[...] + mod_ref[2:3, :] * _rms(y, pmn_ref[...], NORM_EPS)
    x1_ref[...] = x1
    h2 = _rms(x1, pfn_ref[...], NORM_EPS) * (1.0 + mod_ref[4:5, :]) + mod_ref[3:4, :]
    _store_rows(h2_ref, _pack_rows(h2))

    h_hi = h2.astype(BF16)
    h_lo = (h2 - h_hi.astype(F32)).astype(BF16)
    rw = rw_ref[...]
    w_hi = rw.astype(BF16)
    w_lo = (rw - w_hi.astype(F32)).astype(BF16)
    logits = (jnp.dot(h_hi, w_hi, preferred_element_type=F32)
              + jnp.dot(h_hi, w_lo, preferred_element_type=F32)
              + jnp.dot(h_lo, w_hi, preferred_element_type=F32)) + rb_ref[...]

    lane = lax.broadcasted_iota(jnp.int32, logits.shape, 1)
    slot = lax.broadcasted_iota(jnp.int32, idx_ref.shape, 1)
    idx = jnp.zeros(idx_ref.shape, jnp.int32)
    val = jnp.zeros(idx_ref.shape, F32)
    top = None
    for j in range(TOP_K):
        m = jnp.max(logits, axis=-1, keepdims=True)
        i = jnp.min(jnp.where(logits == m, lane, N_EXPERTS), axis=-1, keepdims=True)
        top = m if top is None else top
        idx = jnp.where(slot == j, i, idx)
        val = jnp.where(slot == j, jnp.exp(m - top), val)
        logits = jnp.where(lane == i, -jnp.inf, logits)
    idx_ref[...] = idx
    wgt_ref[...] = val / jnp.sum(val, axis=-1, keepdims=True)


def _out(y_da, y_rw, x, mod3, w_out_b, post_mix_norm, pre_ffn_norm, router_w, router_b):
    b, t, d = x.shape
    tm = min(OUT_ROWS, t)
    e = router_w.shape[1]
    blk = lambda w: pl.BlockSpec((None, tm, w), lambda bi, ti: (bi, ti, 0))
    full = lambda r, c: pl.BlockSpec((r, c), lambda bi, ti: (0, 0))
    return pl.pallas_call(
        _out_kernel,
        grid=(b, t // tm),
        in_specs=[blk(DA_WIDTH), blk(RW_WIDTH), blk(d),
                  pl.BlockSpec((None, N_MOD, d), lambda bi, ti: (bi, 0, 0)),
                  full(d, d), full(1, d), full(1, d), full(d, e), full(1, e)],
        out_specs=[blk(d),
                   pl.BlockSpec((None, tm * ROW_SLAB, LANES), lambda bi, ti: (bi, ti, 0)),
                   blk(TOP_K), blk(TOP_K)],
        out_shape=[jax.ShapeDtypeStruct((b, t, d), F32),
                   jax.ShapeDtypeStruct((b, t * ROW_SLAB, LANES), jnp.uint32),
                   jax.ShapeDtypeStruct((b, t, TOP_K), jnp.int32),
                   jax.ShapeDtypeStruct((b, t, TOP_K), F32)],
        compiler_params=_params("parallel", "parallel"),
        name="out",
    )(y_da, y_rw, x, mod3, w_out_b, post_mix_norm, pre_ffn_norm, router_w,
      router_b.reshape(1, e))


def _route(top_idx, rows_per_tile, n_tiles):
    e_flat = top_idx.reshape(-1)
    onehot = (e_flat[:, None] == jnp.arange(N_EXPERTS, dtype=jnp.int32)[None, :]).astype(jnp.int32)
    csum = jnp.cumsum(onehot, axis=0)
    counts = csum[-1]
    padded = (counts + rows_per_tile - 1) // rows_per_tile * rows_per_tile
    ends = jnp.cumsum(padded)
    starts = ends - padded
    pos = jnp.sum((csum - onehot + starts[None, :]) * onehot, axis=1)
    n_active = ends[-1] // rows_per_tile
    tile_start = jnp.arange(n_tiles, dtype=jnp.int32) * rows_per_tile
    tile = jnp.minimum(tile_start, ends[-1] - 1)
    tile_expert = jnp.sum((tile[:, None] >= ends[None, :]).astype(jnp.int32), axis=1)
    experts = jnp.arange(N_EXPERTS, dtype=jnp.int32)
    used = padded > 0
    later = lax.cummin(jnp.where(used, experts, N_EXPERTS), reverse=True)
    following = jnp.concatenate([later[1:], jnp.full((1,), N_EXPERTS, jnp.int32)])
    following = jnp.where(following == N_EXPERTS, -1, following)
    run_index = jnp.cumsum(used.astype(jnp.int32)) - 1
    plan = (tile_expert.astype(jnp.int32), n_active.reshape(1).astype(jnp.int32),
            following[tile_expert].astype(jnp.int32), (run_index[tile_expert] % 2).astype(jnp.int32))
    return pos.astype(jnp.int32), plan


def _dispatch_kernel(pos_ref, h_ref, xs_in_ref, xs_ref, sem):
    del xs_in_ref
    tb = h_ref.shape[0] // ROW_SLAB

    def issue(t, carry):
        src = h_ref.at[pl.ds(pl.multiple_of(t * ROW_SLAB, ROW_SLAB), ROW_SLAB)]
        for j in range(TOP_K):
            pltpu.make_async_copy(src, xs_ref.at[pos_ref[0, t * TOP_K + j]],
                                  sem).start(priority=j % 2)
        return carry

    lax.fori_loop(0, tb, issue, 0, unroll=8)
    for j in range(TOP_K):
        pltpu.make_async_copy(xs_ref.at[pl.ds(0, tb)], xs_ref.at[pl.ds(0, tb)], sem).wait()


def _dispatch(pos, h2p, n_rows):
    n = h2p.shape[0] // ROW_SLAB
    w = h2p.shape[1]
    tb = min(DISPATCH_TOKENS, n)
    pos3 = pos.reshape(n // tb, 1, tb * TOP_K)
    xs0 = jnp.zeros((n_rows, ROW_SLAB, w), h2p.dtype)
    return pl.pallas_call(
        _dispatch_kernel,
        grid=(n // tb,),
        in_specs=[pl.BlockSpec((None, 1, tb * TOP_K), lambda i: (i, 0, 0),
                               memory_space=pltpu.SMEM),
                  pl.BlockSpec((tb * ROW_SLAB, w), lambda i: (i, 0)),
                  pl.BlockSpec(memory_space=pl.ANY)],
        out_specs=pl.BlockSpec(memory_space=pl.ANY),
        out_shape=jax.ShapeDtypeStruct(xs0.shape, h2p.dtype),
        scratch_shapes=[pltpu.SemaphoreType.DMA(())],
        input_output_aliases={2: 0},
        compiler_params=_params("arbitrary"),
        name="dispatch",
    )(pos3, h2p, xs0)


def _expert_kernel(te_ref, na_ref, nx_ref, sl_ref, xs_ref, w1_hbm, b1_ref, w2_hbm, b2_ref, ys_ref,
                   w1p_s, b1p_s, w2b_s, act_s, w1_buf, w2_buf, sem):
    i = pl.program_id(0)
    active = i < na_ref[0]
    expert = te_ref[i]
    fresh = jnp.logical_or(i == 0, expert != te_ref[jnp.maximum(i - 1, 0)])
    slot = sl_ref[i]
    grp = 2 * LANES
    n_grp = w1_buf.shape[2] // grp

    def weight_copies(ex, s):
        return (pltpu.make_async_copy(w1_hbm.at[ex], w1_buf.at[s], sem.at[0, s]),
                pltpu.make_async_copy(w2_hbm.at[ex], w2_buf.at[s], sem.at[1, s]))

    @pl.when(jnp.logical_and(active, i == 0))
    def _():
        for cp in weight_copies(expert, slot):
            cp.start()

    @pl.when(jnp.logical_and(active, fresh))
    def _():
        for cp in weight_copies(expert, slot):
            cp.wait()

        @pl.when(nx_ref[i] >= 0)
        def _():
            for cp in weight_copies(nx_ref[i], 1 - slot):
                cp.start()

        src = lax.broadcasted_iota(jnp.int32, (grp, grp), 0)
        dst = lax.broadcasted_iota(jnp.int32, (grp, grp), 1)
        perm = jnp.where(src == jnp.where(dst < LANES, 2 * dst, 2 * (dst - LANES) + 1),
                         1.0, 0.0).astype(BF16)
        for g in range(n_grp):
            sl = slice(g * grp, (g + 1) * grp)
            w1p_s[:, sl] = jnp.dot(w1_buf[slot, :, sl].astype(BF16), perm,
                                   preferred_element_type=F32).astype(BF16)
            b = b1_ref[:, sl]
            b_hi = b.astype(BF16)
            b_lo = (b - b_hi.astype(F32)).astype(BF16)
            b1p_s[:, sl] = (jnp.dot(b_hi, perm, preferred_element_type=F32)
                            + jnp.dot(b_lo, perm, preferred_element_type=F32))
        w2b_s[...] = w2_buf[slot].astype(BF16)

    @pl.when(active)
    def _():
        x = _unpack_rows(_load_rows(xs_ref, act_s.shape[0])).astype(BF16)
        hid = jnp.dot(x, w1p_s[...], preferred_element_type=F32) + b1p_s[0:1, :]
        for g in range(n_grp):
            glu = jnp.minimum(hid[:, g * grp:g * grp + LANES], SWIGLU_LIMIT)
            lin = jnp.clip(hid[:, g * grp + LANES:(g + 1) * grp], -SWIGLU_LIMIT, SWIGLU_LIMIT)
            act_s[:, g * LANES:(g + 1) * LANES] = (
                glu * jax.nn.sigmoid(SWIGLU_ALPHA * glu) * (lin + 1.0)).astype(BF16)
        y = jnp.dot(act_s[...], w2b_s[...], preferred_element_type=F32) + b2_ref[...]
        _store_rows(ys_ref, _pack_rows(y))

    @pl.when(jnp.logical_not(active))
    def _():
        ys_ref[...] = jnp.zeros_like(ys_ref)


def _experts(plan, xs, w1, b1, w2, b2):
    tile_expert, n_active, next_expert, tile_slot = plan
    n_rows, w = xs.shape[0] // ROW_SLAB, xs.shape[1]
    tm = EXPERT_ROWS
    d, f2 = w1.shape[1], w1.shape[2]
    f = f2 // 2
    wspec = lambda r, c: pl.BlockSpec((None, r, c), lambda i, te, na, nx, sl: (te[i], 0, 0))
    rows = pl.BlockSpec((tm * ROW_SLAB, w), lambda i, te, na, nx, sl: (i, 0))
    hbm = pl.BlockSpec(memory_space=pl.ANY)
    return pl.pallas_call(
        _expert_kernel,
        grid_spec=pltpu.PrefetchScalarGridSpec(
            num_scalar_prefetch=4,
            grid=(n_rows // tm,),
            in_specs=[rows, hbm, wspec(8, f2), hbm, wspec(1, d)],
            out_specs=rows,
            scratch_shapes=[pltpu.VMEM((d, f2), BF16), pltpu.VMEM((8, f2), F32),
                            pltpu.VMEM((f, d), BF16), pltpu.VMEM((tm, f), BF16),
                            pltpu.VMEM((2, d, f2), F32), pltpu.VMEM((2, f, d), F32),
                            pltpu.SemaphoreType.DMA((2, 2))]),
        out_shape=jax.ShapeDtypeStruct((n_rows * ROW_SLAB, w), jnp.uint32),
        compiler_params=_params("arbitrary"),
        name="expert",
    )(tile_expert, n_active, next_expert, tile_slot, xs, w1, b1, w2, b2)


def _combine_kernel(pos_ref, ys_ref, wgt_ref, x1_ref, mod_ref, nw_ref, o_ref, buf_ref, sem):
    tc = x1_ref.shape[0]

    def issue(t, carry):
        dst = pl.ds(pl.multiple_of(t * ROW_SLAB, ROW_SLAB), ROW_SLAB)
        for j in range(TOP_K):
            pltpu.make_async_copy(ys_ref.at[pos_ref[0, t * TOP_K + j]], buf_ref.at[j, dst],
                                  sem).start(priority=j % 2)
        return carry

    lax.fori_loop(0, tc, issue, 0, unroll=8)
    for j in range(TOP_K):
        pltpu.make_async_copy(ys_ref.at[pl.ds(0, tc)], ys_ref.at[pl.ds(0, tc)], sem).wait()

    sub = min(COMBINE_SUB_ROWS, tc)

    def mix(i, carry):
        r0 = pl.multiple_of(i * sub, sub)
        wgt = wgt_ref[pl.ds(r0, sub), :]
        acc = jnp.zeros((sub, x1_ref.shape[1]), F32)
        for j in range(TOP_K):
            acc = acc + wgt[:, j:j + 1] * _unpack_rows(_load_rows(buf_ref.at[j], sub, r0))
        o_ref[pl.ds(r0, sub), :] = (x1_ref[pl.ds(r0, sub), :]
                                    + mod_ref[5:6, :] * _rms(acc, nw_ref[...], NORM_EPS))
        return carry

    lax.fori_loop(0, tc // sub, mix, 0)


def _combine(pos, ys, wgt, x1, mod3, post_ffn_norm):
    b, t, d = x1.shape
    tc = min(COMBINE_TOKENS, t)
    nt = t // tc
    pos3 = pos.reshape(b * nt, 1, tc * TOP_K)
    blk = lambda w: pl.BlockSpec((None, tc, w), lambda bi, ti: (bi, ti, 0))
    return pl.pallas_call(
        _combine_kernel,
        grid=(b, nt),
        in_specs=[pl.BlockSpec((None, 1, tc * TOP_K), lambda bi, ti: (bi * nt + ti, 0, 0),
                               memory_space=pltpu.SMEM),
                  pl.BlockSpec(memory_space=pl.ANY),
                  blk(TOP_K), blk(d),
                  pl.BlockSpec((None, N_MOD, d), lambda bi, ti: (bi, 0, 0)),
                  pl.BlockSpec((1, d), lambda bi, ti: (0, 0))],
        out_specs=blk(d),
        out_shape=jax.ShapeDtypeStruct((b, t, d), F32),
        scratch_shapes=[pltpu.VMEM((TOP_K, tc * ROW_SLAB, LANES), jnp.uint32),
                        pltpu.SemaphoreType.DMA(())],
        compiler_params=_params("arbitrary", "arbitrary"),
        name="combine",
    )(pos3, ys.reshape(ys.shape[0] // ROW_SLAB, ROW_SLAB, LANES), wgt, x1, mod3, post_ffn_norm)


def _stages(x, c, positions, ada_w, ada_b, pre_mix_norm, post_mix_norm, pre_ffn_norm,
            post_ffn_norm, w_in, w_out, da_lambda_q1, da_lambda_k1, da_lambda_q2, da_lambda_k2,
            da_subln, rw_mu, rw_w0, rw_w2, rw_a0, rw_a2, rw_g2, rw_k_k, rw_k_a, rw_r_k, rw_ln_w,
            rw_ln_b, router_w, router_b, moe_w1, moe_b1, moe_w2, moe_b2):
    b, t, d = x.shape
    res = {}
    lambda_init = 0.8 - 0.6 * math.exp(-0.3 * 0)
    mod = _mod(c, ada_w[0], ada_b[0])
    res["mod"] = mod
    mod3 = mod.reshape(b, N_MOD, d)
    inv_freq = ROPE_THETA ** (-jnp.arange(0, ROPE_DIM, 2, dtype=F32) / ROPE_DIM)
    invf = jnp.tile(inv_freq, LANES // (ROPE_DIM // 2)).reshape(1, LANES)
    q, k, v, rw = _proj(x, positions.reshape(b, t, 1), mod3, pre_mix_norm, invf,
                        w_in[0].astype(BF16), rw_mu)
    res.update(q=q, k=k, v=v, rw=rw)
    lam4 = jnp.concatenate([da_lambda_q1, da_lambda_k1, da_lambda_q2, da_lambda_k2], axis=0)
    y_da = _attn(q, k, v, lam4, da_subln, lambda_init)
    res["y_da"] = y_da
    y_rw = _rwkv(rw, rw_w0, rw_w2[0], rw_a0, rw_a2[0], rw_g2[0], rw_k_k, rw_k_a, rw_r_k[0],
                 rw_ln_w, rw_ln_b)
    res["y_rw"] = y_rw
    x1, h2p, top_idx, top_w = _out(y_da, y_rw, x, mod3, w_out[0].astype(BF16), post_mix_norm,
                                   pre_ffn_norm, router_w[0], router_b[0])
    res.update(x1=x1, top_idx=top_idx, top_w=top_w)
    n = b * t
    n_tiles = n * TOP_K // EXPERT_ROWS + N_EXPERTS
    pos, plan = _route(top_idx.reshape(n, TOP_K), EXPERT_ROWS, n_tiles)
    xs = _dispatch(pos, h2p.reshape(n * ROW_SLAB, LANES), n_tiles * EXPERT_ROWS)
    xs = xs.reshape(n_tiles * EXPERT_ROWS * ROW_SLAB, LANES)
    b1 = jnp.broadcast_to(moe_b1[0][:, None, :], (N_EXPERTS, 8, moe_b1.shape[-1]))
    ys = _experts(plan, xs, moe_w1[0], b1, moe_w2[0], moe_b2[0][:, None, :])
    res["final"] = _combine(pos, ys, top_w, x1, mod3, post_ffn_norm)
    return res


stages = _stages


def kernel(x, c, positions, ada_w, ada_b, pre_mix_norm, post_mix_norm, pre_ffn_norm, post_ffn_norm, w_in, w_out, da_lambda_q1, da_lambda_k1, da_lambda_q2, da_lambda_k2, da_subln, rw_mu, rw_w0, rw_w2, rw_a0, rw_a2, rw_g2, rw_k_k, rw_k_a, rw_r_k, rw_ln_w, rw_ln_b, router_w, router_b, moe_w1, moe_b1, moe_w2, moe_b2):
    res = _stages(x, c, positions, ada_w, ada_b, pre_mix_norm, post_mix_norm, pre_ffn_norm,
                  post_ffn_norm, w_in, w_out, da_lambda_q1, da_lambda_k1, da_lambda_q2,
                  da_lambda_k2, da_subln, rw_mu, rw_w0, rw_w2, rw_a0, rw_a2, rw_g2, rw_k_k,
                  rw_k_a, rw_r_k, rw_ln_w, rw_ln_b, router_w, router_b, moe_w1, moe_b1,
                  moe_w2, moe_b2)
    return res["final"]
```

```python
import functools
import math

import jax
import jax.numpy as jnp
from jax import lax
from jax.experimental import pallas as pl
from jax.experimental.pallas import tpu as pltpu

F32 = jnp.float32
BF16 = jnp.bfloat16

DA_HEADS = 4
DA_HEAD_DIM = 64
DA_V_DIM = 128
DA_WIDTH = 512
RW_HEADS = 8
RW_HEAD_DIM = 64
RW_WIDTH = 512
DECAY_LORA = 64
AAA_LORA = 64
GATE_LORA = 128
DA_COLS = 1536
RW_COLS = 1792
ROPE_THETA = 500000.0
ROPE_DIM = 16
N_EXPERTS = 32
TOP_K = 4
SWIGLU_ALPHA = 1.702
SWIGLU_LIMIT = 7.0
NORM_EPS = 1e-6
SUBLN_EPS = 1e-5
LN_X_EPS = 64e-5
N_MOD = 6

LANES = 128
SUBLANES = 8
VMEM_LIMIT_BYTES = 56 * 1024 * 1024

PROJ_ROWS = 512
ATTN_BLOCK = 512
ATTN_KV_BLOCK = 512
ATTN_HEAD_GROUP = 2
RW_CHUNK = 128
RW_BLOCK = 256
OUT_ROWS = 512
EXPERT_ROWS = 512
DISPATCH_TOKENS = 2048
COMBINE_TOKENS = 1024
COMBINE_SUB_ROWS = 256


def _params(*sem):
    return pltpu.CompilerParams(dimension_semantics=sem, vmem_limit_bytes=VMEM_LIMIT_BYTES)


def _bdot(a, b):
    return jnp.dot(a.astype(BF16), b.astype(BF16), preferred_element_type=F32)


def _bdot_nt(a, b):
    return lax.dot_general(a.astype(BF16), b.astype(BF16), (((1,), (1,)), ((), ())),
                           preferred_element_type=F32)


def _bdot_tn(a, b):
    return lax.dot_general(a.astype(BF16), b.astype(BF16), (((0,), (0,)), ((), ())),
                           preferred_element_type=F32)


def _rms(x, w, eps):
    return x * lax.rsqrt(jnp.mean(x * x, axis=-1, keepdims=True) + eps) * w


def _mod_kernel(c_ref, w_ref, b_ref, o_ref):
    c = c_ref[...]
    s = c * jax.nn.sigmoid(c)
    o_ref[...] = _bdot(s, w_ref[...]) + b_ref[...]


def _mod(c, ada_w, ada_b):
    b, d = c.shape
    n = ada_w.shape[1]
    return pl.pallas_call(
        _mod_kernel,
        grid=(n // d,),
        in_specs=[pl.BlockSpec((b, d), lambda j: (0, 0)),
                  pl.BlockSpec((d, d), lambda j: (0, j)),
                  pl.BlockSpec((1, d), lambda j: (0, j))],
        out_specs=pl.BlockSpec((b, d), lambda j: (0, j)),
        out_shape=jax.ShapeDtypeStruct((b, n), F32),
        compiler_params=_params("parallel"),
        name="mod",
    )(c, ada_w, ada_b.reshape(1, n))


def _proj_kernel(x_ref, pos_ref, mod_ref, nw_ref, invf_ref, w_ref, mu_ref,
                 q_ref, k_ref, v_ref, rw_ref, carry_ref):
    ti = pl.program_id(1)

    @pl.when(ti == 0)
    def _():
        carry_ref[...] = jnp.zeros_like(carry_ref)

    x = x_ref[...]
    h = _rms(x, nw_ref[...], NORM_EPS) * (1.0 + mod_ref[1:2, :]) + mod_ref[0:1, :]
    hb = h.astype(BF16)

    ang = pos_ref[...].astype(F32) * invf_ref[...]
    cos, sin = jnp.cos(ang), jnp.sin(ang)
    l64 = lax.broadcasted_iota(jnp.int32, ang.shape, 1) % DA_HEAD_DIM
    half = ROPE_DIM // 2
    c_tab = jnp.where(l64 < ROPE_DIM, cos, 1.0)
    s_lo = jnp.where(l64 < half, -sin, 0.0)
    s_hi = jnp.where((l64 >= half) & (l64 < ROPE_DIM), sin, 0.0)

    def rope(z):
        up = pltpu.roll(z, LANES - half, axis=1)
        dn = pltpu.roll(z, half, axis=1)
        return z * c_tab + up * s_lo + dn * s_hi

    for g in range(DA_WIDTH // LANES):
        sl = slice(g * LANES, (g + 1) * LANES)
        qg = jnp.dot(hb, w_ref[:, sl], preferred_element_type=F32)
        q_ref[:, sl] = (rope(qg) * (DA_HEAD_DIM ** -0.5)).astype(q_ref.dtype)
        kg = jnp.dot(hb, w_ref[:, DA_WIDTH + g * LANES:DA_WIDTH + (g + 1) * LANES],
                     preferred_element_type=F32)
        k_ref[:, sl] = rope(kg).astype(k_ref.dtype)
    v_ref[...] = jnp.dot(hb, w_ref[:, 2 * DA_WIDTH:DA_COLS],
                         preferred_element_type=F32).astype(v_ref.dtype)

    p = jnp.dot(hb, w_ref[:, DA_COLS:], preferred_element_type=F32)
    rows = p.shape[0]
    prev = pltpu.roll(p, 1, axis=0)
    first = lax.broadcasted_iota(jnp.int32, p.shape, 0) == 0
    prev = jnp.where(first, carry_ref[0:1, :], prev)
    rw_ref[...] = p + (prev - p) * mu_ref[...]
    carry_ref[0:1, :] = p[rows - 1:rows, :]


def _proj(x, pos3, mod3, norm_w, invf, w_in_b, mu):
    b, t, d = x.shape
    tm = min(PROJ_ROWS, t)
    n_in = w_in_b.shape[1]
    blk = lambda w: pl.BlockSpec((None, tm, w), lambda bi, ti: (bi, ti, 0))
    full = lambda r, c: pl.BlockSpec((r, c), lambda bi, ti: (0, 0))
    return pl.pallas_call(
        _proj_kernel,
        grid=(b, t // tm),
        in_specs=[blk(d), blk(1),
                  pl.BlockSpec((None, N_MOD, d), lambda bi, ti: (bi, 0, 0)),
                  full(1, d), full(1, LANES), full(d, n_in), full(1, RW_COLS)],
        out_specs=[blk(DA_WIDTH), blk(DA_WIDTH), blk(DA_WIDTH), blk(RW_COLS)],
        out_shape=[jax.ShapeDtypeStruct((b, t, DA_WIDTH), BF16)] * 3
        + [jax.ShapeDtypeStruct((b, t, RW_COLS), F32)],
        scratch_shapes=[pltpu.VMEM((8, RW_COLS), F32)],
        compiler_params=_params("parallel", "arbitrary"),
        name="proj",
    )(x, pos3, mod3, norm_w, invf, w_in_b, mu)


def _attn_kernel(q_ref, k_ref, v_ref, lam_ref, subln_ref, o_ref, m_ref, l_ref, acc_ref,
                 *, lambda_init):
    qi = pl.program_id(2)
    tq = q_ref.shape[0]
    heads = range(ATTN_HEAD_GROUP)
    hs = [slice(h * DA_V_DIM, (h + 1) * DA_V_DIM) for h in heads]
    lane = lax.broadcasted_iota(jnp.int32, (tq, DA_V_DIM), 1)
    qq = []
    for c in hs:
        q = q_ref[:, c]
        zero = jnp.zeros_like(q)
        qq.append(jnp.concatenate([jnp.where(lane < DA_HEAD_DIM, q, zero),
                                   jnp.where(lane >= DA_HEAD_DIM, q, zero)], axis=0))

    m_ref[...] = jnp.full(m_ref.shape, -jnp.inf, F32)
    l_ref[...] = jnp.zeros(l_ref.shape, F32)
    acc_ref[...] = jnp.zeros(acc_ref.shape, F32)
    tk = ATTN_KV_BLOCK if k_ref.shape[0] % ATTN_KV_BLOCK == 0 else tq
    rep = tk // LANES

    def step(j, masked):
        rows = pl.ds(pl.multiple_of(j * tk, tk), tk)
        s = [lax.dot_general(qq[h], k_ref[rows, hs[h]], (((1,), (1,)), ((), ())),
                             preferred_element_type=F32) for h in heads]
        if masked:
            qpos = qi * tq + lax.broadcasted_iota(jnp.int32, s[0].shape, 0) % tq
            kpos = j * tk + lax.broadcasted_iota(jnp.int32, s[0].shape, 1)
            s = [jnp.where(qpos >= kpos, x, -jnp.inf) for x in s]
        for h in heads:
            m_old = m_ref[h]
            m_new = jnp.maximum(m_old, jnp.max(s[h], axis=-1, keepdims=True))
            alpha = jnp.exp(m_old - m_new)
            p = jnp.exp(s[h] - jnp.concatenate([m_new] * rep, axis=1))
            l_ref[h] = alpha * l_ref[h] + jnp.sum(p, axis=-1, keepdims=True)
            acc_ref[h] = alpha * acc_ref[h] + jnp.dot(p.astype(v_ref.dtype), v_ref[rows, hs[h]],
                                                      preferred_element_type=F32)
            m_ref[h] = m_new

    def body(j, carry):
        step(j, False)
        return carry

    n_full = (qi * tq) // tk
    lax.fori_loop(0, n_full, body, 0)
    step(n_full, True)

    lam = (jnp.exp(jnp.sum(lam_ref[0:1, :] * lam_ref[1:2, :], axis=-1, keepdims=True))
           - jnp.exp(jnp.sum(lam_ref[2:3, :] * lam_ref[3:4, :], axis=-1, keepdims=True))
           + lambda_init)
    for h in heads:
        o = acc_ref[h] / l_ref[h]
        d = o[:tq, :] - lam * o[tq:, :]
        o_ref[:, hs[h]] = (_rms(d, subln_ref[...], SUBLN_EPS)
                           * (1.0 - lambda_init)).astype(o_ref.dtype)


def _attn(q, k, v, lam4, subln, lambda_init):
    b, t, _ = q.shape
    tq = min(ATTN_BLOCK, t)
    hg = ATTN_HEAD_GROUP
    gw = hg * DA_V_DIM
    return pl.pallas_call(
        functools.partial(_attn_kernel, lambda_init=lambda_init),
        grid=(b, DA_HEADS // hg, t // tq),
        in_specs=[pl.BlockSpec((None, tq, gw), lambda bi, h, qi: (bi, qi, h)),
                  pl.BlockSpec((None, t, gw), lambda bi, h, qi: (bi, 0, h)),
                  pl.BlockSpec((None, t, gw), lambda bi, h, qi: (bi, 0, h)),
                  pl.BlockSpec((4, DA_HEAD_DIM), lambda bi, h, qi: (0, 0)),
                  pl.BlockSpec((1, DA_V_DIM), lambda bi, h, qi: (0, 0))],
        out_specs=pl.BlockSpec((None, tq, gw), lambda bi, h, qi: (bi, qi, h)),
        out_shape=jax.ShapeDtypeStruct((b, t, DA_WIDTH), BF16),
        scratch_shapes=[pltpu.VMEM((hg, 2 * tq, LANES), F32), pltpu.VMEM((hg, 2 * tq, LANES), F32),
                        pltpu.VMEM((hg, 2 * tq, DA_V_DIM), F32)],
        compiler_params=_params("parallel", "parallel", "arbitrary"),
        name="attn",
    )(q, k, v, lam4, subln)


def _rwkv_kernel(rw_ref, w0_ref, w2_ref, a0_ref, a2_ref, g2_ref, kk_ref, ka_ref, rk_ref,
                 lnw_ref, lnb_ref, o_ref, state_ref, r_s, k_s, v_s, lw_s, kk_s, a_s, g_s, cum_s):
    ti = pl.program_id(1)

    @pl.when(ti == 0)
    def _():
        state_ref[...] = jnp.zeros_like(state_ref)

    w = RW_WIDTH
    rw = rw_ref[...]
    k = rw[:, w:2 * w]
    wl = rw[:, 3 * w:3 * w + DECAY_LORA]
    al = rw[:, 3 * w + DECAY_LORA:3 * w + DECAY_LORA + AAA_LORA]
    gl = rw[:, 3 * w + DECAY_LORA + AAA_LORA:]
    z = -(w0_ref[...] + _bdot(jnp.tanh(wl), w2_ref[...]))
    softplus = jnp.maximum(z, 0.0) + jnp.log(1.0 + jnp.exp(-jnp.abs(z)))
    a = jax.nn.sigmoid(a0_ref[...] + _bdot(al, a2_ref[...]))
    r_s[...] = rw[:, 0:w]
    v_s[...] = rw[:, 2 * w:3 * w]
    lw_s[...] = -jnp.exp(-softplus - 0.5)
    a_s[...] = a
    g_s[...] = _bdot(jax.nn.sigmoid(gl), g2_ref[...])
    kk_s[...] = k * kk_ref[...]
    k_s[...] = k * (1.0 + (a - 1.0) * ka_ref[...])

    c_len = RW_CHUNK
    n = RW_HEAD_DIM
    tb = rw_ref.shape[0]

    br = lax.broadcasted_iota(jnp.int32, (tb, tb), 0)
    bc = lax.broadcasted_iota(jnp.int32, (tb, tb), 1)
    tri = jnp.where((br >= bc) & (br // c_len == bc // c_len), 1.0, 0.0).astype(BF16)
    lw_all = lw_s[...]
    lw_hi = lw_all.astype(BF16)
    rem = lw_all - lw_hi.astype(F32)
    lw_mid = rem.astype(BF16)
    lw_lo = (rem - lw_mid.astype(F32)).astype(BF16)
    cum_s[...] = (jnp.dot(tri, lw_hi, preferred_element_type=F32)
                  + jnp.dot(tri, lw_mid, preferred_element_type=F32)
                  + jnp.dot(tri, lw_lo, preferred_element_type=F32))

    row = lax.broadcasted_iota(jnp.int32, (c_len, 2 * c_len), 0)
    col = lax.broadcasted_iota(jnp.int32, (c_len, 2 * c_len), 1)
    incl2 = row >= col % c_len
    strict2 = row > col % c_len
    eye = jnp.where(lax.broadcasted_iota(jnp.int32, (c_len, c_len), 0)
                    == lax.broadcasted_iota(jnp.int32, (c_len, c_len), 1), 1.0, 0.0).astype(F32)

    def chunk(ci, carry):
        rows = pl.ds(pl.multiple_of(ci * c_len, c_len), c_len)
        heads = range(RW_HEADS)
        sl = [slice(h * n, (h + 1) * n) for h in heads]
        r = [r_s[rows, c] for c in sl]
        kh = [k_s[rows, c] for c in sl]
        v = [v_s[rows, c] for c in sl]
        lw = [lw_s[rows, c] for c in sl]
        cum = [cum_s[rows, c] for c in sl]
        kk = [kk_s[rows, c] for c in sl]
        kk = [x * lax.rsqrt(jnp.maximum(jnp.sum(x * x, axis=-1, keepdims=True), 1e-24)) for x in kk]
        kka = [kk[h] * a_s[rows, sl[h]] for h in heads]
        end = [jnp.sum(x, axis=0, keepdims=True) for x in lw]
        e_neg = [jnp.exp(-x) for x in cum]
        e_end = [jnp.exp(end[h] - cum[h]) for h in heads]
        left = [jnp.concatenate([-kk[h] * jnp.exp(cum[h] - lw[h]), r[h] * jnp.exp(cum[h])], axis=0)
                for h in heads]
        g = [_bdot_nt(left[h], jnp.concatenate([kka[h] * e_neg[h], kh[h] * e_neg[h]], axis=0))
             for h in heads]
        a_a = [jnp.where(strict2, x[:c_len, :], 0.0) for x in g]
        a_r = [jnp.where(incl2, x[c_len:, :], 0.0) for x in g]
        pw = [x[:, :c_len] for x in a_a]
        inv = [eye + x for x in pw]
        for _ in range(c_len.bit_length() - 2):
            pw = [_bdot(x, x) for x in pw]
            inv = [inv[h] + _bdot(inv[h], pw[h]) for h in heads]
        akv = [_bdot(a_a[h][:, c_len:], v[h]) for h in heads]
        s0 = [state_ref[h] for h in heads]
        ls = [_bdot_nt(left[h], s0[h]) for h in heads]
        u = [_bdot(inv[h], ls[h][:c_len, :] + akv[h]) for h in heads]
        uv = [jnp.concatenate([u[h], v[h]], axis=0) for h in heads]
        y = [ls[h][c_len:, :] + _bdot(a_r[h], uv[h]) for h in heads]
        for h in heads:
            state_ref[h] = s0[h] * jnp.exp(end[h]) + _bdot_tn(
                uv[h], jnp.concatenate([kka[h] * e_end[h], kh[h] * e_end[h]], axis=0))
        for h in heads:
            mean = jnp.mean(y[h], axis=-1, keepdims=True)
            yc = y[h] - mean
            var = jnp.mean(yc * yc, axis=-1, keepdims=True)
            yn = yc * lax.rsqrt(var + LN_X_EPS) * lnw_ref[:, sl[h]] + lnb_ref[:, sl[h]]
            bonus = jnp.sum(r[h] * kh[h] * rk_ref[:, sl[h]], axis=-1, keepdims=True) * v[h]
            o_ref[rows, sl[h]] = ((yn + bonus) * g_s[rows, sl[h]]).astype(o_ref.dtype)
        return carry

    lax.fori_loop(0, rw_ref.shape[0] // c_len, chunk, 0, unroll=2)


def _rwkv(rw, w0, w2, a0, a2, g2, k_k, k_a, r_k, ln_w, ln_b):
    b, t, _ = rw.shape
    tb = min(RW_BLOCK, t)
    w = RW_WIDTH
    vec = pl.BlockSpec((1, w), lambda bi, ti: (0, 0))
    mat = lambda r: pl.BlockSpec((r, w), lambda bi, ti: (0, 0))
    return pl.pallas_call(
        _rwkv_kernel,
        grid=(b, t // tb),
        in_specs=[pl.BlockSpec((None, tb, RW_COLS), lambda bi, ti: (bi, ti, 0)),
                  vec, mat(DECAY_LORA), vec, mat(AAA_LORA), mat(GATE_LORA), vec, vec, vec, vec, vec],
        out_specs=pl.BlockSpec((None, tb, w), lambda bi, ti: (bi, ti, 0)),
        out_shape=jax.ShapeDtypeStruct((b, t, w), BF16),
        scratch_shapes=[pltpu.VMEM((RW_HEADS, RW_HEAD_DIM, RW_HEAD_DIM), F32)]
        + [pltpu.VMEM((tb, w), F32)] * 8,
        compiler_params=_params("parallel", "arbitrary"),
        name="rwkv",
    )(rw, w0, w2, a0, a2, g2, k_k, k_a, r_k.reshape(1, w), ln_w, ln_b)


def _pack_rows(x):
    half = x.shape[1] // 2
    hi = pltpu.bitcast(x[:, :half].astype(BF16).astype(F32), jnp.uint32)
    lo = pltpu.bitcast(x[:, half:].astype(BF16).astype(F32), jnp.uint32)
    return hi | (lo >> 16)


def _unpack_rows(u):
    hi = pltpu.bitcast(u & jnp.uint32(0xFFFF0000), F32)
    lo = pltpu.bitcast(u << 16, F32)
    return jnp.concatenate([hi, lo], axis=1)


ROW_SLAB = 4


def _store_rows(ref, u, r0=0):
    n = u.shape[0]
    for c in range(ROW_SLAB):
        ref[pl.ds(r0 * ROW_SLAB + c, n, stride=ROW_SLAB), :] = u[:, c * LANES:(c + 1) * LANES]


def _load_rows(ref, n, r0=0):
    return jnp.concatenate([ref[pl.ds(r0 * ROW_SLAB + c, n, stride=ROW_SLAB), :]
                            for c in range(ROW_SLAB)], axis=1)


def _out_kernel(yda_ref, yrw_ref, x_ref, mod_ref, wo_ref, pmn_ref, pfn_ref, rw_ref, rb_ref,
                x1_ref, h2_ref, idx_ref, wgt_ref):
    y = (jnp.dot(yda_ref[...], wo_ref[0:DA_WIDTH, :], preferred_element_type=F32)
         + jnp.dot(yrw_ref[...], wo_ref[DA_WIDTH:, :], preferred_element_type=F32))
    x1 = x_ref[...] + mod_ref[2:3, :] * _rms(y, pmn_ref[...], NORM_EPS)
    x1_ref[...] = x1
    h2 = _rms(x1, pfn_ref[...], NORM_EPS) * (1.0 + mod_ref[4:5, :]) + mod_ref[3:4, :]
    _store_rows(h2_ref, _pack_rows(h2))

    h_hi = h2.astype(BF16)
    h_lo = (h2 - h_hi.astype(F32)).astype(BF16)
    rw = rw_ref[...]
    w_hi = rw.astype(BF16)
    w_lo = (rw - w_hi.astype(F32)).astype(BF16)
    logits = (jnp.dot(h_hi, w_hi, preferred_element_type=F32)
              + jnp.dot(h_hi, w_lo, preferred_element_type=F32)
              + jnp.dot(h_lo, w_hi, preferred_element_type=F32)) + rb_ref[...]

    lane = lax.broadcasted_iota(jnp.int32, logits.shape, 1)
    slot = lax.broadcasted_iota(jnp.int32, idx_ref.shape, 1)
    idx = jnp.zeros(idx_ref.shape, jnp.int32)
    val = jnp.zeros(idx_ref.shape, F32)
    top = None
    for j in range(TOP_K):
        m = jnp.max(logits, axis=-1, keepdims=True)
        i = jnp.min(jnp.where(logits == m, lane, N_EXPERTS), axis=-1, keepdims=True)
        top = m if top is None else top
        idx = jnp.where(slot == j, i, idx)
        val = jnp.where(slot == j, jnp.exp(m - top), val)
        logits = jnp.where(lane == i, -jnp.inf, logits)
    idx_ref[...] = idx
    wgt_ref[...] = val / jnp.sum(val, axis=-1, keepdims=True)


def _out(y_da, y_rw, x, mod3, w_out_b, post_mix_norm, pre_ffn_norm, router_w, router_b):
    b, t, d = x.shape
    tm = min(OUT_ROWS, t)
    e = router_w.shape[1]
    blk = lambda w: pl.BlockSpec((None, tm, w), lambda bi, ti: (bi, ti, 0))
    full = lambda r, c: pl.BlockSpec((r, c), lambda bi, ti: (0, 0))
    return pl.pallas_call(
        _out_kernel,
        grid=(b, t // tm),
        in_specs=[blk(DA_WIDTH), blk(RW_WIDTH), blk(d),
                  pl.BlockSpec((None, N_MOD, d), lambda bi, ti: (bi, 0, 0)),
                  full(d, d), full(1, d), full(1, d), full(d, e), full(1, e)],
        out_specs=[blk(d),
                   pl.BlockSpec((None, tm * ROW_SLAB, LANES), lambda bi, ti: (bi, ti, 0)),
                   blk(TOP_K), blk(TOP_K)],
        out_shape=[jax.ShapeDtypeStruct((b, t, d), F32),
                   jax.ShapeDtypeStruct((b, t * ROW_SLAB, LANES), jnp.uint32),
                   jax.ShapeDtypeStruct((b, t, TOP_K), jnp.int32),
                   jax.ShapeDtypeStruct((b, t, TOP_K), F32)],
        compiler_params=_params("parallel", "parallel"),
        name="out",
    )(y_da, y_rw, x, mod3, w_out_b, post_mix_norm, pre_ffn_norm, router_w,
      router_b.reshape(1, e))


def _route(top_idx, rows_per_tile, n_tiles):
    e_flat = top_idx.reshape(-1)
    onehot = (e_flat[:, None] == jnp.arange(N_EXPERTS, dtype=jnp.int32)[None, :]).astype(jnp.int32)
    csum = jnp.cumsum(onehot, axis=0)
    counts = csum[-1]
    padded = (counts + rows_per_tile - 1) // rows_per_tile * rows_per_tile
    ends = jnp.cumsum(padded)
    starts = ends - padded
    pos = jnp.sum((csum - onehot + starts[None, :]) * onehot, axis=1)
    n_active = ends[-1] // rows_per_tile
    tile_start = jnp.arange(n_tiles, dtype=jnp.int32) * rows_per_tile
    tile = jnp.minimum(tile_start, ends[-1] - 1)
    tile_expert = jnp.sum((tile[:, None] >= ends[None, :]).astype(jnp.int32), axis=1)
    experts = jnp.arange(N_EXPERTS, dtype=jnp.int32)
    used = padded > 0
    later = lax.cummin(jnp.where(used, experts, N_EXPERTS), reverse=True)
    following = jnp.concatenate([later[1:], jnp.full((1,), N_EXPERTS, jnp.int32)])
    following = jnp.where(following == N_EXPERTS, -1, following)
    run_index = jnp.cumsum(used.astype(jnp.int32)) - 1
    plan = (tile_expert.astype(jnp.int32), n_active.reshape(1).astype(jnp.int32),
            following[tile_expert].astype(jnp.int32), (run_index[tile_expert] % 2).astype(jnp.int32))
    return pos.astype(jnp.int32), plan


def _dispatch_kernel(pos_ref, h_ref, xs_in_ref, xs_ref, sem):
    del xs_in_ref
    tb = h_ref.shape[0] // ROW_SLAB

    def issue(t, carry):
        src = h_ref.at[pl.ds(pl.multiple_of(t * ROW_SLAB, ROW_SLAB), ROW_SLAB)]
        for j in range(TOP_K):
            pltpu.make_async_copy(src, xs_ref.at[pos_ref[0, t * TOP_K + j]],
                                  sem).start(priority=j % 2)
        return carry

    lax.fori_loop(0, tb, issue, 0, unroll=8)
    for j in range(TOP_K):
        pltpu.make_async_copy(xs_ref.at[pl.ds(0, tb)], xs_ref.at[pl.ds(0, tb)], sem).wait()


def _dispatch(pos, h2p, n_rows):
    n = h2p.shape[0] // ROW_SLAB
    w = h2p.shape[1]
    tb = min(DISPATCH_TOKENS, n)
    pos3 = pos.reshape(n // tb, 1, tb * TOP_K)
    xs0 = jnp.zeros((n_rows, ROW_SLAB, w), h2p.dtype)
    return pl.pallas_call(
        _dispatch_kernel,
        grid=(n // tb,),
        in_specs=[pl.BlockSpec((None, 1, tb * TOP_K), lambda i: (i, 0, 0),
                               memory_space=pltpu.SMEM),
                  pl.BlockSpec((tb * ROW_SLAB, w), lambda i: (i, 0)),
                  pl.BlockSpec(memory_space=pl.ANY)],
        out_specs=pl.BlockSpec(memory_space=pl.ANY),
        out_shape=jax.ShapeDtypeStruct(xs0.shape, h2p.dtype),
        scratch_shapes=[pltpu.SemaphoreType.DMA(())],
        input_output_aliases={2: 0},
        compiler_params=_params("arbitrary"),
        name="dispatch",
    )(pos3, h2p, xs0)


def _expert_kernel(te_ref, na_ref, nx_ref, sl_ref, xs_ref, w1_hbm, b1_ref, w2_hbm, b2_ref, ys_ref,
                   w1p_s, b1p_s, w2b_s, act_s, w1_buf, w2_buf, sem):
    i = pl.program_id(0)
    active = i < na_ref[0]
    expert = te_ref[i]
    fresh = jnp.logical_or(i == 0, expert != te_ref[jnp.maximum(i - 1, 0)])
    slot = sl_ref[i]
    grp = 2 * LANES
    n_grp = w1_buf.shape[2] // grp

    def weight_copies(ex, s):
        return (pltpu.make_async_copy(w1_hbm.at[ex], w1_buf.at[s], sem.at[0, s]),
                pltpu.make_async_copy(w2_hbm.at[ex], w2_buf.at[s], sem.at[1, s]))

    @pl.when(jnp.logical_and(active, i == 0))
    def _():
        for cp in weight_copies(expert, slot):
            cp.start()

    @pl.when(jnp.logical_and(active, fresh))
    def _():
        for cp in weight_copies(expert, slot):
            cp.wait()

        @pl.when(nx_ref[i] >= 0)
        def _():
            for cp in weight_copies(nx_ref[i], 1 - slot):
                cp.start()

        src = lax.broadcasted_iota(jnp.int32, (grp, grp), 0)
        dst = lax.broadcasted_iota(jnp.int32, (grp, grp), 1)
        perm = jnp.where(src == jnp.where(dst < LANES, 2 * dst, 2 * (dst - LANES) + 1),
                         1.0, 0.0).astype(BF16)
        for g in range(n_grp):
            sl = slice(g * grp, (g + 1) * grp)
            w1p_s[:, sl] = jnp.dot(w1_buf[slot, :, sl].astype(BF16), perm,
                                   preferred_element_type=F32).astype(BF16)
            b = b1_ref[:, sl]
            b_hi = b.astype(BF16)
            b_lo = (b - b_hi.astype(F32)).astype(BF16)
            b1p_s[:, sl] = (jnp.dot(b_hi, perm, preferred_element_type=F32)
                            + jnp.dot(b_lo, perm, preferred_element_type=F32))
        w2b_s[...] = w2_buf[slot].astype(BF16)

    @pl.when(active)
    def _():
        x = _unpack_rows(_load_rows(xs_ref, act_s.shape[0])).astype(BF16)
        hid = jnp.dot(x, w1p_s[...], preferred_element_type=F32) + b1p_s[0:1, :]
        for g in range(n_grp):
            glu = jnp.minimum(hid[:, g * grp:g * grp + LANES], SWIGLU_LIMIT)
            lin = jnp.clip(hid[:, g * grp + LANES:(g + 1) * grp], -SWIGLU_LIMIT, SWIGLU_LIMIT)
            act_s[:, g * LANES:(g + 1) * LANES] = (
                glu * jax.nn.sigmoid(SWIGLU_ALPHA * glu) * (lin + 1.0)).astype(BF16)
        y = jnp.dot(act_s[...], w2b_s[...], preferred_element_type=F32) + b2_ref[...]
        _store_rows(ys_ref, _pack_rows(y))

    @pl.when(jnp.logical_not(active))
    def _():
        ys_ref[...] = jnp.zeros_like(ys_ref)


def _experts(plan, xs, w1, b1, w2, b2):
    tile_expert, n_active, next_expert, tile_slot = plan
    n_rows, w = xs.shape[0] // ROW_SLAB, xs.shape[1]
    tm = EXPERT_ROWS
    d, f2 = w1.shape[1], w1.shape[2]
    f = f2 // 2
    wspec = lambda r, c: pl.BlockSpec((None, r, c), lambda i, te, na, nx, sl: (te[i], 0, 0))
    rows = pl.BlockSpec((tm * ROW_SLAB, w), lambda i, te, na, nx, sl: (i, 0))
    hbm = pl.BlockSpec(memory_space=pl.ANY)
    return pl.pallas_call(
        _expert_kernel,
        grid_spec=pltpu.PrefetchScalarGridSpec(
            num_scalar_prefetch=4,
            grid=(n_rows // tm,),
            in_specs=[rows, hbm, wspec(8, f2), hbm, wspec(1, d)],
            out_specs=rows,
            scratch_shapes=[pltpu.VMEM((d, f2), BF16), pltpu.VMEM((8, f2), F32),
                            pltpu.VMEM((f, d), BF16), pltpu.VMEM((tm, f), BF16),
                            pltpu.VMEM((2, d, f2), F32), pltpu.VMEM((2, f, d), F32),
                            pltpu.SemaphoreType.DMA((2, 2))]),
        out_shape=jax.ShapeDtypeStruct((n_rows * ROW_SLAB, w), jnp.uint32),
        compiler_params=_params("arbitrary"),
        name="expert",
    )(tile_expert, n_active, next_expert, tile_slot, xs, w1, b1, w2, b2)


def _combine_kernel(pos_ref, ys_ref, wgt_ref, x1_ref, mod_ref, nw_ref, o_ref, buf_ref, sem):
    tc = x1_ref.shape[0]

    def issue(t, carry):
        dst = pl.ds(pl.multiple_of(t * ROW_SLAB, ROW_SLAB), ROW_SLAB)
        for j in range(TOP_K):
            pltpu.make_async_copy(ys_ref.at[pos_ref[0, t * TOP_K + j]], buf_ref.at[j, dst],
                                  sem).start(priority=j % 2)
        return carry

    lax.fori_loop(0, tc, issue, 0, unroll=8)
    for j in range(TOP_K):
        pltpu.make_async_copy(ys_ref.at[pl.ds(0, tc)], ys_ref.at[pl.ds(0, tc)], sem).wait()

    sub = min(COMBINE_SUB_ROWS, tc)

    def mix(i, carry):
        r0 = pl.multiple_of(i * sub, sub)
        wgt = wgt_ref[pl.ds(r0, sub), :]
        acc = jnp.zeros((sub, x1_ref.shape[1]), F32)
        for j in range(TOP_K):
            acc = acc + wgt[:, j:j + 1] * _unpack_rows(_load_rows(buf_ref.at[j], sub, r0))
        o_ref[pl.ds(r0, sub), :] = (x1_ref[pl.ds(r0, sub), :]
                                    + mod_ref[5:6, :] * _rms(acc, nw_ref[...], NORM_EPS))
        return carry

    lax.fori_loop(0, tc // sub, mix, 0)


def _combine(pos, ys, wgt, x1, mod3, post_ffn_norm):
    b, t, d = x1.shape
    tc = min(COMBINE_TOKENS, t)
    nt = t // tc
    pos3 = pos.reshape(b * nt, 1, tc * TOP_K)
    blk = lambda w: pl.BlockSpec((None, tc, w), lambda bi, ti: (bi, ti, 0))
    return pl.pallas_call(
        _combine_kernel,
        grid=(b, nt),
        in_specs=[pl.BlockSpec((None, 1, tc * TOP_K), lambda bi, ti: (bi * nt + ti, 0, 0),
                               memory_space=pltpu.SMEM),
                  pl.BlockSpec(memory_space=pl.ANY),
                  blk(TOP_K), blk(d),
                  pl.BlockSpec((None, N_MOD, d), lambda bi, ti: (bi, 0, 0)),
                  pl.BlockSpec((1, d), lambda bi, ti: (0, 0))],
        out_specs=blk(d),
        out_shape=jax.ShapeDtypeStruct((b, t, d), F32),
        scratch_shapes=[pltpu.VMEM((TOP_K, tc * ROW_SLAB, LANES), jnp.uint32),
                        pltpu.SemaphoreType.DMA(())],
        compiler_params=_params("arbitrary", "arbitrary"),
        name="combine",
    )(pos3, ys.reshape(ys.shape[0] // ROW_SLAB, ROW_SLAB, LANES), wgt, x1, mod3, post_ffn_norm)


def _stages(x, c, positions, ada_w, ada_b, pre_mix_norm, post_mix_norm, pre_ffn_norm,
            post_ffn_norm, w_in, w_out, da_lambda_q1, da_lambda_k1, da_lambda_q2, da_lambda_k2,
            da_subln, rw_mu, rw_w0, rw_w2, rw_a0, rw_a2, rw_g2, rw_k_k, rw_k_a, rw_r_k, rw_ln_w,
            rw_ln_b, router_w, router_b, moe_w1, moe_b1, moe_w2, moe_b2):
    b, t, d = x.shape
    res = {}
    lambda_init = 0.8 - 0.6 * math.exp(-0.3 * 0)
    mod = _mod(c, ada_w[0], ada_b[0])
    res["mod"] = mod
    mod3 = mod.reshape(b, N_MOD, d)
    inv_freq = ROPE_THETA ** (-jnp.arange(0, ROPE_DIM, 2, dtype=F32) / ROPE_DIM)
    invf = jnp.tile(inv_freq, LANES // (ROPE_DIM // 2)).reshape(1, LANES)
    q, k, v, rw = _proj(x, positions.reshape(b, t, 1), mod3, pre_mix_norm, invf,
                        w_in[0].astype(BF16), rw_mu)
    res.update(q=q, k=k, v=v, rw=rw)
    lam4 = jnp.concatenate([da_lambda_q1, da_lambda_k1, da_lambda_q2, da_lambda_k2], axis=0)
    y_da = _attn(q, k, v, lam4, da_subln, lambda_init)
    res["y_da"] = y_da
    y_rw = _rwkv(rw, rw_w0, rw_w2[0], rw_a0, rw_a2[0], rw_g2[0], rw_k_k, rw_k_a, rw_r_k[0],
                 rw_ln_w, rw_ln_b)
    res["y_rw"] = y_rw
    x1, h2p, top_idx, top_w = _out(y_da, y_rw, x, mod3, w_out[0].astype(BF16), post_mix_norm,
                                   pre_ffn_norm, router_w[0], router_b[0])
    res.update(x1=x1, top_idx=top_idx, top_w=top_w)
    n = b * t
    n_tiles = n * TOP_K // EXPERT_ROWS + N_EXPERTS
    pos, plan = _route(top_idx.reshape(n, TOP_K), EXPERT_ROWS, n_tiles)
    xs = _dispatch(pos, h2p.reshape(n * ROW_SLAB, LANES), n_tiles * EXPERT_ROWS)
    xs = xs.reshape(n_tiles * EXPERT_ROWS * ROW_SLAB, LANES)
    b1 = jnp.broadcast_to(moe_b1[0][:, None, :], (N_EXPERTS, 8, moe_b1.shape[-1]))
    ys = _experts(plan, xs, moe_w1[0], b1, moe_w2[0], moe_b2[0][:, None, :])
    res["final"] = _combine(pos, ys, top_w, x1, mod3, post_ffn_norm)
    return res


stages = _stages


def kernel(x, c, positions, ada_w, ada_b, pre_mix_norm, post_mix_norm, pre_ffn_norm, post_ffn_norm, w_in, w_out, da_lambda_q1, da_lambda_k1, da_lambda_q2, da_lambda_k2, da_subln, rw_mu, rw_w0, rw_w2, rw_a0, rw_a2, rw_g2, rw_k_k, rw_k_a, rw_r_k, rw_ln_w, rw_ln_b, router_w, router_b, moe_w1, moe_b1, moe_w2, moe_b2):
    res = _stages(x, c, positions, ada_w, ada_b, pre_mix_norm, post_mix_norm, pre_ffn_norm,
                  post_ffn_norm, w_in, w_out, da_lambda_q1, da_lambda_k1, da_lambda_q2,
                  da_lambda_k2, da_subln, rw_mu, rw_w0, rw_w2, rw_a0, rw_a2, rw_g2, rw_k_k,
                  rw_k_a, rw_r_k, rw_ln_w, rw_ln_b, router_w, router_b, moe_w1, moe_b1,
                  moe_w2, moe_b2)
    return res["final"]
```

```python
import functools
import math

import jax
import jax.numpy as jnp
from jax import lax
from jax.experimental import pallas as pl
from jax.experimental.pallas import tpu as pltpu

F32 = jnp.float32
BF16 = jnp.bfloat16

DA_HEADS = 4
DA_HEAD_DIM = 64
DA_V_DIM = 128
DA_WIDTH = 512
RW_HEADS = 8
RW_HEAD_DIM = 64
RW_WIDTH = 512
DECAY_LORA = 64
AAA_LORA = 64
GATE_LORA = 128
DA_COLS = 1536
RW_COLS = 1792
ROPE_THETA = 500000.0
ROPE_DIM = 16
N_EXPERTS = 32
TOP_K = 4
SWIGLU_ALPHA = 1.702
SWIGLU_LIMIT = 7.0
NORM_EPS = 1e-6
SUBLN_EPS = 1e-5
LN_X_EPS = 64e-5
N_MOD = 6

LANES = 128
SUBLANES = 8
VMEM_LIMIT_BYTES = 56 * 1024 * 1024

PROJ_ROWS = 512
ATTN_BLOCK = 512
ATTN_KV_BLOCK = 512
ATTN_HEAD_GROUP = 2
RW_CHUNK = 128
RW_BLOCK = 256
OUT_ROWS = 512
EXPERT_ROWS = 512
DISPATCH_TOKENS = 2048
COMBINE_TOKENS = 1024
COMBINE_SUB_ROWS = 256


def _params(*sem):
    return pltpu.CompilerParams(dimension_semantics=sem, vmem_limit_bytes=VMEM_LIMIT_BYTES)


def _bdot(a, b):
    return jnp.dot(a.astype(BF16), b.astype(BF16), preferred_element_type=F32)


def _bdot_nt(a, b):
    return lax.dot_general(a.astype(BF16), b.astype(BF16), (((1,), (1,)), ((), ())),
                           preferred_element_type=F32)


def _bdot_tn(a, b):
    return lax.dot_general(a.astype(BF16), b.astype(BF16), (((0,), (0,)), ((), ())),
                           preferred_element_type=F32)


def _rms(x, w, eps):
    return x * lax.rsqrt(jnp.mean(x * x, axis=-1, keepdims=True) + eps) * w


def _mod_kernel(c_ref, w_ref, b_ref, o_ref):
    c = c_ref[...]
    s = c * jax.nn.sigmoid(c)
    o_ref[...] = _bdot(s, w_ref[...]) + b_ref[...]


def _mod(c, ada_w, ada_b):
    b, d = c.shape
    n = ada_w.shape[1]
    return pl.pallas_call(
        _mod_kernel,
        grid=(n // d,),
        in_specs=[pl.BlockSpec((b, d), lambda j: (0, 0)),
                  pl.BlockSpec((d, d), lambda j: (0, j)),
                  pl.BlockSpec((1, d), lambda j: (0, j))],
        out_specs=pl.BlockSpec((b, d), lambda j: (0, j)),
        out_shape=jax.ShapeDtypeStruct((b, n), F32),
        compiler_params=_params("parallel"),
        name="mod",
    )(c, ada_w, ada_b.reshape(1, n))


def _proj_kernel(x_ref, pos_ref, mod_ref, nw_ref, invf_ref, w_ref, mu_ref,
                 q_ref, k_ref, v_ref, rw_ref, carry_ref):
    ti = pl.program_id(1)

    @pl.when(ti == 0)
    def _():
        carry_ref[...] = jnp.zeros_like(carry_ref)

    x = x_ref[...]
    h = _rms(x, nw_ref[...], NORM_EPS) * (1.0 + mod_ref[1:2, :]) + mod_ref[0:1, :]
    hb = h.astype(BF16)

    ang = pos_ref[...].astype(F32) * invf_ref[...]
    cos, sin = jnp.cos(ang), jnp.sin(ang)
    l64 = lax.broadcasted_iota(jnp.int32, ang.shape, 1) % DA_HEAD_DIM
    half = ROPE_DIM // 2
    c_tab = jnp.where(l64 < ROPE_DIM, cos, 1.0)
    s_lo = jnp.where(l64 < half, -sin, 0.0)
    s_hi = jnp.where((l64 >= half) & (l64 < ROPE_DIM), sin, 0.0)

    def rope(z):
        up = pltpu.roll(z, LANES - half, axis=1)
        dn = pltpu.roll(z, half, axis=1)
        return z * c_tab + up * s_lo + dn * s_hi

    for g in range(DA_WIDTH // LANES):
        sl = slice(g * LANES, (g + 1) * LANES)
        qg = jnp.dot(hb, w_ref[:, sl], preferred_element_type=F32)
        q_ref[:, sl] = (rope(qg) * (DA_HEAD_DIM ** -0.5)).astype(q_ref.dtype)
        kg = jnp.dot(hb, w_ref[:, DA_WIDTH + g * LANES:DA_WIDTH + (g + 1) * LANES],
                     preferred_element_type=F32)
        k_ref[:, sl] = rope(kg).astype(k_ref.dtype)
    v_ref[...] = jnp.dot(hb, w_ref[:, 2 * DA_WIDTH:DA_COLS],
                         preferred_element_type=F32).astype(v_ref.dtype)

    p = jnp.dot(hb, w_ref[:, DA_COLS:], preferred_element_type=F32)
    rows = p.shape[0]
    prev = pltpu.roll(p, 1, axis=0)
    first = lax.broadcasted_iota(jnp.int32, p.shape, 0) == 0
    prev = jnp.where(first, carry_ref[0:1, :], prev)
    rw_ref[...] = p + (prev - p) * mu_ref[...]
    carry_ref[0:1, :] = p[rows - 1:rows, :]


def _proj(x, pos3, mod3, norm_w, invf, w_in_b, mu):
    b, t, d = x.shape
    tm = min(PROJ_ROWS, t)
    n_in = w_in_b.shape[1]
    blk = lambda w: pl.BlockSpec((None, tm, w), lambda bi, ti: (bi, ti, 0))
    full = lambda r, c: pl.BlockSpec((r, c), lambda bi, ti: (0, 0))
    return pl.pallas_call(
        _proj_kernel,
        grid=(b, t // tm),
        in_specs=[blk(d), blk(1),
                  pl.BlockSpec((None, N_MOD, d), lambda bi, ti: (bi, 0, 0)),
                  full(1, d), full(1, LANES), full(d, n_in), full(1, RW_COLS)],
        out_specs=[blk(DA_WIDTH), blk(DA_WIDTH), blk(DA_WIDTH), blk(RW_COLS)],
        out_shape=[jax.ShapeDtypeStruct((b, t, DA_WIDTH), BF16)] * 3
        + [jax.ShapeDtypeStruct((b, t, RW_COLS), F32)],
        scratch_shapes=[pltpu.VMEM((8, RW_COLS), F32)],
        compiler_params=_params("parallel", "arbitrary"),
        name="proj",
    )(x, pos3, mod3, norm_w, invf, w_in_b, mu)


def _attn_kernel(q_ref, k_ref, v_ref, lam_ref, subln_ref, o_ref, m_ref, l_ref, acc_ref,
                 *, lambda_init):
    qi = pl.program_id(2)
    tq = q_ref.shape[0]
    heads = range(ATTN_HEAD_GROUP)
    hs = [slice(h * DA_V_DIM, (h + 1) * DA_V_DIM) for h in heads]
    lane = lax.broadcasted_iota(jnp.int32, (tq, DA_V_DIM), 1)
    qq = []
    for c in hs:
        q = q_ref[:, c]
        zero = jnp.zeros_like(q)
        qq.append(jnp.concatenate([jnp.where(lane < DA_HEAD_DIM, q, zero),
                                   jnp.where(lane >= DA_HEAD_DIM, q, zero)], axis=0))

    m_ref[...] = jnp.full(m_ref.shape, -jnp.inf, F32)
    l_ref[...] = jnp.zeros(l_ref.shape, F32)
    acc_ref[...] = jnp.zeros(acc_ref.shape, F32)
    tk = ATTN_KV_BLOCK if k_ref.shape[0] % ATTN_KV_BLOCK == 0 else tq
    rep = tk // LANES

    def step(j, masked):
        rows = pl.ds(pl.multiple_of(j * tk, tk), tk)
        s = [lax.dot_general(qq[h], k_ref[rows, hs[h]], (((1,), (1,)), ((), ())),
                             preferred_element_type=F32) for h in heads]
        if masked:
            qpos = qi * tq + lax.broadcasted_iota(jnp.int32, s[0].shape, 0) % tq
            kpos = j * tk + lax.broadcasted_iota(jnp.int32, s[0].shape, 1)
            s = [jnp.where(qpos >= kpos, x, -jnp.inf) for x in s]
        for h in heads:
            m_old = m_ref[h]
            m_new = jnp.maximum(m_old, jnp.max(s[h], axis=-1, keepdims=True))
            alpha = jnp.exp(m_old - m_new)
            p = jnp.exp(s[h] - jnp.concatenate([m_new] * rep, axis=1))
            l_ref[h] = alpha * l_ref[h] + jnp.sum(p, axis=-1, keepdims=True)
            acc_ref[h] = alpha * acc_ref[h] + jnp.dot(p.astype(v_ref.dtype), v_ref[rows, hs[h]],
                                                      preferred_element_type=F32)
            m_ref[h] = m_new

    def body(j, carry):
        step(j, False)
        return carry

    n_full = (qi * tq) // tk
    lax.fori_loop(0, n_full, body, 0)
    step(n_full, True)

    lam = (jnp.exp(jnp.sum(lam_ref[0:1, :] * lam_ref[1:2, :], axis=-1, keepdims=True))
           - jnp.exp(jnp.sum(lam_ref[2:3, :] * lam_ref[3:4, :], axis=-1, keepdims=True))
           + lambda_init)
    for h in heads:
        o = acc_ref[h] / l_ref[h]
        d = o[:tq, :] - lam * o[tq:, :]
        o_ref[:, hs[h]] = (_rms(d, subln_ref[...], SUBLN_EPS)
                           * (1.0 - lambda_init)).astype(o_ref.dtype)


def _attn(q, k, v, lam4, subln, lambda_init):
    b, t, _ = q.shape
    tq = min(ATTN_BLOCK, t)
    hg = ATTN_HEAD_GROUP
    gw = hg * DA_V_DIM
    return pl.pallas_call(
        functools.partial(_attn_kernel, lambda_init=lambda_init),
        grid=(b, DA_HEADS // hg, t // tq),
        in_specs=[pl.BlockSpec((None, tq, gw), lambda bi, h, qi: (bi, qi, h)),
                  pl.BlockSpec((None, t, gw), lambda bi, h, qi: (bi, 0, h)),
                  pl.BlockSpec((None, t, gw), lambda bi, h, qi: (bi, 0, h)),
                  pl.BlockSpec((4, DA_HEAD_DIM), lambda bi, h, qi: (0, 0)),
                  pl.BlockSpec((1, DA_V_DIM), lambda bi, h, qi: (0, 0))],
        out_specs=pl.BlockSpec((None, tq, gw), lambda bi, h, qi: (bi, qi, h)),
        out_shape=jax.ShapeDtypeStruct((b, t, DA_WIDTH), BF16),
        scratch_shapes=[pltpu.VMEM((hg, 2 * tq, LANES), F32), pltpu.VMEM((hg, 2 * tq, LANES), F32),
                        pltpu.VMEM((hg, 2 * tq, DA_V_DIM), F32)],
        compiler_params=_params("parallel", "parallel", "arbitrary"),
        name="attn",
    )(q, k, v, lam4, subln)


def _rwkv_kernel(rw_ref, w0_ref, w2_ref, a0_ref, a2_ref, g2_ref, kk_ref, ka_ref, rk_ref,
                 lnw_ref, lnb_ref, o_ref, state_ref, r_s, k_s, v_s, lw_s, kk_s, a_s, g_s, cum_s):
    ti = pl.program_id(1)

    @pl.when(ti == 0)
    def _():
        state_ref[...] = jnp.zeros_like(state_ref)

    w = RW_WIDTH
    rw = rw_ref[...]
    k = rw[:, w:2 * w]
    wl = rw[:, 3 * w:3 * w + DECAY_LORA]
    al = rw[:, 3 * w + DECAY_LORA:3 * w + DECAY_LORA + AAA_LORA]
    gl = rw[:, 3 * w + DECAY_LORA + AAA_LORA:]
    z = -(w0_ref[...] + _bdot(jnp.tanh(wl), w2_ref[...]))
    softplus = jnp.maximum(z, 0.0) + jnp.log(1.0 + jnp.exp(-jnp.abs(z)))
    a = jax.nn.sigmoid(a0_ref[...] + _bdot(al, a2_ref[...]))
    r_s[...] = rw[:, 0:w]
    v_s[...] = rw[:, 2 * w:3 * w]
    lw_s[...] = -jnp.exp(-softplus - 0.5)
    a_s[...] = a
    g_s[...] = _bdot(jax.nn.sigmoid(gl), g2_ref[...])
    kk_s[...] = k * kk_ref[...]
    k_s[...] = k * (1.0 + (a - 1.0) * ka_ref[...])

    c_len = RW_CHUNK
    n = RW_HEAD_DIM
    tb = rw_ref.shape[0]

    br = lax.broadcasted_iota(jnp.int32, (tb, tb), 0)
    bc = lax.broadcasted_iota(jnp.int32, (tb, tb), 1)
    tri = jnp.where((br >= bc) & (br // c_len == bc // c_len), 1.0, 0.0).astype(BF16)
    lw_all = lw_s[...]
    lw_hi = lw_all.astype(BF16)
    rem = lw_all - lw_hi.astype(F32)
    lw_mid = rem.astype(BF16)
    lw_lo = (rem - lw_mid.astype(F32)).astype(BF16)
    cum_s[...] = (jnp.dot(tri, lw_hi, preferred_element_type=F32)
                  + jnp.dot(tri, lw_mid, preferred_element_type=F32)
                  + jnp.dot(tri, lw_lo, preferred_element_type=F32))

    row = lax.broadcasted_iota(jnp.int32, (c_len, 2 * c_len), 0)
    col = lax.broadcasted_iota(jnp.int32, (c_len, 2 * c_len), 1)
    incl2 = row >= col % c_len
    strict2 = row > col % c_len
    eye = jnp.where(lax.broadcasted_iota(jnp.int32, (c_len, c_len), 0)
                    == lax.broadcasted_iota(jnp.int32, (c_len, c_len), 1), 1.0, 0.0).astype(F32)

    def chunk(ci, carry):
        rows = pl.ds(pl.multiple_of(ci * c_len, c_len), c_len)
        heads = range(RW_HEADS)
        sl = [slice(h * n, (h + 1) * n) for h in heads]
        r = [r_s[rows, c] for c in sl]
        kh = [k_s[rows, c] for c in sl]
        v = [v_s[rows, c] for c in sl]
        lw = [lw_s[rows, c] for c in sl]
        cum = [cum_s[rows, c] for c in sl]
        kk = [kk_s[rows, c] for c in sl]
        kk = [x * lax.rsqrt(jnp.maximum(jnp.sum(x * x, axis=-1, keepdims=True), 1e-24)) for x in kk]
        kka = [kk[h] * a_s[rows, sl[h]] for h in heads]
        end = [jnp.sum(x, axis=0, keepdims=True) for x in lw]
        e_neg = [jnp.exp(-x) for x in cum]
        e_end = [jnp.exp(end[h] - cum[h]) for h in heads]
        left = [jnp.concatenate([-kk[h] * jnp.exp(cum[h] - lw[h]), r[h] * jnp.exp(cum[h])], axis=0)
                for h in heads]
        g = [_bdot_nt(left[h], jnp.concatenate([kka[h] * e_neg[h], kh[h] * e_neg[h]], axis=0))
             for h in heads]
        a_a = [jnp.where(strict2, x[:c_len, :], 0.0) for x in g]
        a_r = [jnp.where(incl2, x[c_len:, :], 0.0) for x in g]
        pw = [x[:, :c_len] for x in a_a]
        inv = [eye + x for x in pw]
        for _ in range(c_len.bit_length() - 2):
            pw = [_bdot(x, x) for x in pw]
            inv = [inv[h] + _bdot(inv[h], pw[h]) for h in heads]
        akv = [_bdot(a_a[h][:, c_len:], v[h]) for h in heads]
        s0 = [state_ref[h] for h in heads]
        ls = [_bdot_nt(left[h], s0[h]) for h in heads]
        u = [_bdot(inv[h], ls[h][:c_len, :] + akv[h]) for h in heads]
        uv = [jnp.concatenate([u[h], v[h]], axis=0) for h in heads]
        y = [ls[h][c_len:, :] + _bdot(a_r[h], uv[h]) for h in heads]
        for h in heads:
            state_ref[h] = s0[h] * jnp.exp(end[h]) + _bdot_tn(
                uv[h], jnp.concatenate([kka[h] * e_end[h], kh[h] * e_end[h]], axis=0))
        for h in heads:
            mean = jnp.mean(y[h], axis=-1, keepdims=True)
            yc = y[h] - mean
            var = jnp.mean(yc * yc, axis=-1, keepdims=True)
            yn = yc * lax.rsqrt(var + LN_X_EPS) * lnw_ref[:, sl[h]] + lnb_ref[:, sl[h]]
            bonus = jnp.sum(r[h] * kh[h] * rk_ref[:, sl[h]], axis=-1, keepdims=True) * v[h]
            o_ref[rows, sl[h]] = ((yn + bonus) * g_s[rows, sl[h]]).astype(o_ref.dtype)
        return carry

    lax.fori_loop(0, rw_ref.shape[0] // c_len, chunk, 0, unroll=2)


def _rwkv(rw, w0, w2, a0, a2, g2, k_k, k_a, r_k, ln_w, ln_b):
    b, t, _ = rw.shape
    tb = min(RW_BLOCK, t)
    w = RW_WIDTH
    vec = pl.BlockSpec((1, w), lambda bi, ti: (0, 0))
    mat = lambda r: pl.BlockSpec((r, w), lambda bi, ti: (0, 0))
    return pl.pallas_call(
        _rwkv_kernel,
        grid=(b, t // tb),
        in_specs=[pl.BlockSpec((None, tb, RW_COLS), lambda bi, ti: (bi, ti, 0)),
                  vec, mat(DECAY_LORA), vec, mat(AAA_LORA), mat(GATE_LORA), vec, vec, vec, vec, vec],
        out_specs=pl.BlockSpec((None, tb, w), lambda bi, ti: (bi, ti, 0)),
        out_shape=jax.ShapeDtypeStruct((b, t, w), BF16),
        scratch_shapes=[pltpu.VMEM((RW_HEADS, RW_HEAD_DIM, RW_HEAD_DIM), F32)]
        + [pltpu.VMEM((tb, w), F32)] * 8,
        compiler_params=_params("parallel", "arbitrary"),
        name="rwkv",
    )(rw, w0, w2, a0, a2, g2, k_k, k_a, r_k.reshape(1, w), ln_w, ln_b)


def _pack_rows(x):
    half = x.shape[1] // 2
    hi = pltpu.bitcast(x[:, :half].astype(BF16).astype(F32), jnp.uint32)
    lo = pltpu.bitcast(x[:, half:].astype(BF16).astype(F32), jnp.uint32)
    return hi | (lo >> 16)


def _unpack_rows(u):
    hi = pltpu.bitcast(u & jnp.uint32(0xFFFF0000), F32)
    lo = pltpu.bitcast(u << 16, F32)
    return jnp.concatenate([hi, lo], axis=1)


ROW_SLAB = 4


def _store_rows(ref, u, r0=0):
    n = u.shape[0]
    for c in range(ROW_SLAB):
        ref[pl.ds(r0 * ROW_SLAB + c, n, stride=ROW_SLAB), :] = u[:, c * LANES:(c + 1) * LANES]


def _load_rows(ref, n, r0=0):
    return jnp.concatenate([ref[pl.ds(r0 * ROW_SLAB + c, n, stride=ROW_SLAB), :]
                            for c in range(ROW_SLAB)], axis=1)


def _out_kernel(yda_ref, yrw_ref, x_ref, mod_ref, wo_ref, pmn_ref, pfn_ref, rw_ref, rb_ref,
                x1_ref, h2_ref, idx_ref, wgt_ref):
    y = (jnp.dot(yda_ref[...], wo_ref[0:DA_WIDTH, :], preferred_element_type=F32)
         + jnp.dot(yrw_ref[...], wo_ref[DA_WIDTH:, :], preferred_element_type=F32))
    x1 = x_ref[...] + mod_ref[2:3, :] * _rms(y, pmn_ref[...], NORM_EPS)
    x1_ref[...] = x1
    h2 = _rms(x1, pfn_ref[...], NORM_EPS) * (1.0 + mod_ref[4:5, :]) + mod_ref[3:4, :]
    _store_rows(h2_ref, _pack_rows(h2))

    h_hi = h2.astype(BF16)
    h_lo = (h2 - h_hi.astype(F32)).astype(BF16)
    rw = rw_ref[...]
    w_hi = rw.astype(BF16)
    w_lo = (rw - w_hi.astype(F32)).astype(BF16)
    logits = (jnp.dot(h_hi, w_hi, preferred_element_type=F32)
              + jnp.dot(h_hi, w_lo, preferred_element_type=F32)
              + jnp.dot(h_lo, w_hi, preferred_element_type=F32)) + rb_ref[...]

    lane = lax.broadcasted_iota(jnp.int32, logits.shape, 1)
    slot = lax.broadcasted_iota(jnp.int32, idx_ref.shape, 1)
    idx = jnp.zeros(idx_ref.shape, jnp.int32)
    val = jnp.zeros(idx_ref.shape, F32)
    top = None
    for j in range(TOP_K):
        m = jnp.max(logits, axis=-1, keepdims=True)
        i = jnp.min(jnp.where(logits == m, lane, N_EXPERTS), axis=-1, keepdims=True)
        top = m if top is None else top
        idx = jnp.where(slot == j, i, idx)
        val = jnp.where(slot == j, jnp.exp(m - top), val)
        logits = jnp.where(lane == i, -jnp.inf, logits)
    idx_ref[...] = idx
    wgt_ref[...] = val / jnp.sum(val, axis=-1, keepdims=True)


def _out(y_da, y_rw, x, mod3, w_out_b, post_mix_norm, pre_ffn_norm, router_w, router_b):
    b, t, d = x.shape
    tm = min(OUT_ROWS, t)
    e = router_w.shape[1]
    blk = lambda w: pl.BlockSpec((None, tm, w), lambda bi, ti: (bi, ti, 0))
    full = lambda r, c: pl.BlockSpec((r, c), lambda bi, ti: (0, 0))
    return pl.pallas_call(
        _out_kernel,
        grid=(b, t // tm),
        in_specs=[blk(DA_WIDTH), blk(RW_WIDTH), blk(d),
                  pl.BlockSpec((None, N_MOD, d), lambda bi, ti: (bi, 0, 0)),
                  full(d, d), full(1, d), full(1, d), full(d, e), full(1, e)],
        out_specs=[blk(d),
                   pl.BlockSpec((None, tm * ROW_SLAB, LANES), lambda bi, ti: (bi, ti, 0)),
                   blk(TOP_K), blk(TOP_K)],
        out_shape=[jax.ShapeDtypeStruct((b, t, d), F32),
                   jax.ShapeDtypeStruct((b, t * ROW_SLAB, LANES), jnp.uint32),
                   jax.ShapeDtypeStruct((b, t, TOP_K), jnp.int32),
                   jax.ShapeDtypeStruct((b, t, TOP_K), F32)],
        compiler_params=_params("parallel", "parallel"),
        name="out",
    )(y_da, y_rw, x, mod3, w_out_b, post_mix_norm, pre_ffn_norm, router_w,
      router_b.reshape(1, e))


def _route(top_idx, rows_per_tile, n_tiles):
    e_flat = top_idx.reshape(-1)
    onehot = (e_flat[:, None] == jnp.arange(N_EXPERTS, dtype=jnp.int32)[None, :]).astype(jnp.int32)
    csum = jnp.cumsum(onehot, axis=0)
    counts = csum[-1]
    padded = (counts + rows_per_tile - 1) // rows_per_tile * rows_per_tile
    ends = jnp.cumsum(padded)
    starts = ends - padded
    pos = jnp.sum((csum - onehot + starts[None, :]) * onehot, axis=1)
    n_active = ends[-1] // rows_per_tile
    tile_start = jnp.arange(n_tiles, dtype=jnp.int32) * rows_per_tile
    tile = jnp.minimum(tile_start, ends[-1] - 1)
    tile_expert = jnp.sum((tile[:, None] >= ends[None, :]).astype(jnp.int32), axis=1)
    experts = jnp.arange(N_EXPERTS, dtype=jnp.int32)
    used = padded > 0
    later = lax.cummin(jnp.where(used, experts, N_EXPERTS), reverse=True)
    following = jnp.concatenate([later[1:], jnp.full((1,), N_EXPERTS, jnp.int32)])
    following = jnp.where(following == N_EXPERTS, -1, following)
    run_index = jnp.cumsum(used.astype(jnp.int32)) - 1
    plan = (tile_expert.astype(jnp.int32), n_active.reshape(1).astype(jnp.int32),
            following[tile_expert].astype(jnp.int32), (run_index[tile_expert] % 2).astype(jnp.int32))
    return pos.astype(jnp.int32), plan


def _dispatch_kernel(pos_ref, h_ref, xs_in_ref, xs_ref, sem):
    del xs_in_ref
    tb = h_ref.shape[0] // ROW_SLAB

    def issue(t, carry):
        src = h_ref.at[pl.ds(pl.multiple_of(t * ROW_SLAB, ROW_SLAB), ROW_SLAB)]
        for j in range(TOP_K):
            pltpu.make_async_copy(src, xs_ref.at[pos_ref[0, t * TOP_K + j]],
                                  sem).start(priority=j % 2)
        return carry

    lax.fori_loop(0, tb, issue, 0, unroll=8)
    for j in range(TOP_K):
        pltpu.make_async_copy(xs_ref.at[pl.ds(0, tb)], xs_ref.at[pl.ds(0, tb)], sem).wait()


def _dispatch(pos, h2p, n_rows):
    n = h2p.shape[0] // ROW_SLAB
    w = h2p.shape[1]
    tb = min(DISPATCH_TOKENS, n)
    pos3 = pos.reshape(n // tb, 1, tb * TOP_K)
    xs0 = jnp.zeros((n_rows, ROW_SLAB, w), h2p.dtype)
    return pl.pallas_call(
        _dispatch_kernel,
        grid=(n // tb,),
        in_specs=[pl.BlockSpec((None, 1, tb * TOP_K), lambda i: (i, 0, 0),
                               memory_space=pltpu.SMEM),
                  pl.BlockSpec((tb * ROW_SLAB, w), lambda i: (i, 0)),
                  pl.BlockSpec(memory_space=pl.ANY)],
        out_specs=pl.BlockSpec(memory_space=pl.ANY),
        out_shape=jax.ShapeDtypeStruct(xs0.shape, h2p.dtype),
        scratch_shapes=[pltpu.SemaphoreType.DMA(())],
        input_output_aliases={2: 0},
        compiler_params=_params("arbitrary"),
        name="dispatch",
    )(pos3, h2p, xs0)


def _expert_kernel(te_ref, na_ref, nx_ref, sl_ref, xs_ref, w1_hbm, b1_ref, w2_hbm, b2_ref, ys_ref,
                   w1p_s, b1p_s, w2b_s, act_s, w1_buf, w2_buf, sem):
    i = pl.program_id(0)
    active = i < na_ref[0]
    expert = te_ref[i]
    fresh = jnp.logical_or(i == 0, expert != te_ref[jnp.maximum(i - 1, 0)])
    slot = sl_ref[i]
    grp = 2 * LANES
    n_grp = w1_buf.shape[2] // grp

    def weight_copies(ex, s):
        return (pltpu.make_async_copy(w1_hbm.at[ex], w1_buf.at[s], sem.at[0, s]),
                pltpu.make_async_copy(w2_hbm.at[ex], w2_buf.at[s], sem.at[1, s]))

    @pl.when(jnp.logical_and(active, i == 0))
    def _():
        for cp in weight_copies(expert, slot):
            cp.start()

    @pl.when(jnp.logical_and(active, fresh))
    def _():
        for cp in weight_copies(expert, slot):
            cp.wait()

        @pl.when(nx_ref[i] >= 0)
        def _():
            for cp in weight_copies(nx_ref[i], 1 - slot):
                cp.start()

        src = lax.broadcasted_iota(jnp.int32, (grp, grp), 0)
        dst = lax.broadcasted_iota(jnp.int32, (grp, grp), 1)
        perm = jnp.where(src == jnp.where(dst < LANES, 2 * dst, 2 * (dst - LANES) + 1),
                         1.0, 0.0).astype(BF16)
        for g in range(n_grp):
            sl = slice(g * grp, (g + 1) * grp)
            w1p_s[:, sl] = jnp.dot(w1_buf[slot, :, sl].astype(BF16), perm,
                                   preferred_element_type=F32).astype(BF16)
            b = b1_ref[:, sl]
            b_hi = b.astype(BF16)
            b_lo = (b - b_hi.astype(F32)).astype(BF16)
            b1p_s[:, sl] = (jnp.dot(b_hi, perm, preferred_element_type=F32)
                            + jnp.dot(b_lo, perm, preferred_element_type=F32))
        w2b_s[...] = w2_buf[slot].astype(BF16)

    @pl.when(active)
    def _():
        x = _unpack_rows(_load_rows(xs_ref, act_s.shape[0])).astype(BF16)
        hid = jnp.dot(x, w1p_s[...], preferred_element_type=F32) + b1p_s[0:1, :]
        for g in range(n_grp):
            glu = jnp.minimum(hid[:, g * grp:g * grp + LANES], SWIGLU_LIMIT)
            lin = jnp.clip(hid[:, g * grp + LANES:(g + 1) * grp], -SWIGLU_LIMIT, SWIGLU_LIMIT)
            act_s[:, g * LANES:(g + 1) * LANES] = (
                glu * jax.nn.sigmoid(SWIGLU_ALPHA * glu) * (lin + 1.0)).astype(BF16)
        y = jnp.dot(act_s[...], w2b_s[...], preferred_element_type=F32) + b2_ref[...]
        _store_rows(ys_ref, _pack_rows(y))

    @pl.when(jnp.logical_not(active))
    def _():
        ys_ref[...] = jnp.zeros_like(ys_ref)


def _experts(plan, xs, w1, b1, w2, b2):
    tile_expert, n_active, next_expert, tile_slot = plan
    n_rows, w = xs.shape[0] // ROW_SLAB, xs.shape[1]
    tm = EXPERT_ROWS
    d, f2 = w1.shape[1], w1.shape[2]
    f = f2 // 2
    wspec = lambda r, c: pl.BlockSpec((None, r, c), lambda i, te, na, nx, sl: (te[i], 0, 0))
    rows = pl.BlockSpec((tm * ROW_SLAB, w), lambda i, te, na, nx, sl: (i, 0))
    hbm = pl.BlockSpec(memory_space=pl.ANY)
    return pl.pallas_call(
        _expert_kernel,
        grid_spec=pltpu.PrefetchScalarGridSpec(
            num_scalar_prefetch=4,
            grid=(n_rows // tm,),
            in_specs=[rows, hbm, wspec(8, f2), hbm, wspec(1, d)],
            out_specs=rows,
            scratch_shapes=[pltpu.VMEM((d, f2), BF16), pltpu.VMEM((8, f2), F32),
                            pltpu.VMEM((f, d), BF16), pltpu.VMEM((tm, f), BF16),
                            pltpu.VMEM((2, d, f2), F32), pltpu.VMEM((2, f, d), F32),
                            pltpu.SemaphoreType.DMA((2, 2))]),
        out_shape=jax.ShapeDtypeStruct((n_rows * ROW_SLAB, w), jnp.uint32),
        compiler_params=_params("arbitrary"),
        name="expert",
    )(tile_expert, n_active, next_expert, tile_slot, xs, w1, b1, w2, b2)


def _combine_kernel(pos_ref, nxt_ref, ys_ref, wgt_ref, x1_ref, mod_ref, nw_ref, o_ref, buf_ref,
                    sem):
    tc = x1_ref.shape[0]
    step = pl.program_id(0) * pl.num_programs(1) + pl.program_id(1)
    last = pl.num_programs(0) * pl.num_programs(1) - 1
    slot = step % 2

    def gather(p_ref, s):
        def issue(t, carry):
            dst = pl.ds(pl.multiple_of(t * ROW_SLAB, ROW_SLAB), ROW_SLAB)
            for j in range(TOP_K):
                pltpu.make_async_copy(ys_ref.at[p_ref[0, t * TOP_K + j]], buf_ref.at[s, j, dst],
                                      sem.at[s]).start(priority=j % 2)
            return carry

        lax.fori_loop(0, tc, issue, 0, unroll=8)

    @pl.when(step == 0)
    def _():
        gather(pos_ref, slot)

    @pl.when(step < last)
    def _():
        gather(nxt_ref, 1 - slot)

    for j in range(TOP_K):
        pltpu.make_async_copy(ys_ref.at[pl.ds(0, tc)], ys_ref.at[pl.ds(0, tc)], sem.at[slot]).wait()

    sub = min(COMBINE_SUB_ROWS, tc)

    def mix(i, carry):
        r0 = pl.multiple_of(i * sub, sub)
        wgt = wgt_ref[pl.ds(r0, sub), :]
        acc = jnp.zeros((sub, x1_ref.shape[1]), F32)
        for j in range(TOP_K):
            acc = acc + wgt[:, j:j + 1] * _unpack_rows(_load_rows(buf_ref.at[slot, j], sub, r0))
        o_ref[pl.ds(r0, sub), :] = (x1_ref[pl.ds(r0, sub), :]
                                    + mod_ref[5:6, :] * _rms(acc, nw_ref[...], NORM_EPS))
        return carry

    lax.fori_loop(0, tc // sub, mix, 0)


def _combine(pos, ys, wgt, x1, mod3, post_ffn_norm):
    b, t, d = x1.shape
    tc = min(COMBINE_TOKENS, t)
    nt = t // tc
    pos3 = pos.reshape(b * nt, 1, tc * TOP_K)
    blk = lambda w: pl.BlockSpec((None, tc, w), lambda bi, ti: (bi, ti, 0))
    return pl.pallas_call(
        _combine_kernel,
        grid=(b, nt),
        in_specs=[pl.BlockSpec((None, 1, tc * TOP_K), lambda bi, ti: (bi * nt + ti, 0, 0),
                               memory_space=pltpu.SMEM),
                  pl.BlockSpec((None, 1, tc * TOP_K),
                               lambda bi, ti: (jnp.minimum(bi * nt + ti + 1, b * nt - 1), 0, 0),
                               memory_space=pltpu.SMEM),
                  pl.BlockSpec(memory_space=pl.ANY),
                  blk(TOP_K), blk(d),
                  pl.BlockSpec((None, N_MOD, d), lambda bi, ti: (bi, 0, 0)),
                  pl.BlockSpec((1, d), lambda bi, ti: (0, 0))],
        out_specs=blk(d),
        out_shape=jax.ShapeDtypeStruct((b, t, d), F32),
        scratch_shapes=[pltpu.VMEM((2, TOP_K, tc * ROW_SLAB, LANES), jnp.uint32),
                        pltpu.SemaphoreType.DMA((2,))],
        compiler_params=_params("arbitrary", "arbitrary"),
        name="combine",
    )(pos3, pos3, ys.reshape(ys.shape[0] // ROW_SLAB, ROW_SLAB, LANES), wgt, x1, mod3,
      post_ffn_norm)


def _stages(x, c, positions, ada_w, ada_b, pre_mix_norm, post_mix_norm, pre_ffn_norm,
            post_ffn_norm, w_in, w_out, da_lambda_q1, da_lambda_k1, da_lambda_q2, da_lambda_k2,
            da_subln, rw_mu, rw_w0, rw_w2, rw_a0, rw_a2, rw_g2, rw_k_k, rw_k_a, rw_r_k, rw_ln_w,
            rw_ln_b, router_w, router_b, moe_w1, moe_b1, moe_w2, moe_b2):
    b, t, d = x.shape
    res = {}
    lambda_init = 0.8 - 0.6 * math.exp(-0.3 * 0)
    mod = _mod(c, ada_w[0], ada_b[0])
    res["mod"] = mod
    mod3 = mod.reshape(b, N_MOD, d)
    inv_freq = ROPE_THETA ** (-jnp.arange(0, ROPE_DIM, 2, dtype=F32) / ROPE_DIM)
    invf = jnp.tile(inv_freq, LANES // (ROPE_DIM // 2)).reshape(1, LANES)
    q, k, v, rw = _proj(x, positions.reshape(b, t, 1), mod3, pre_mix_norm, invf,
                        w_in[0].astype(BF16), rw_mu)
    res.update(q=q, k=k, v=v, rw=rw)
    lam4 = jnp.concatenate([da_lambda_q1, da_lambda_k1, da_lambda_q2, da_lambda_k2], axis=0)
    y_da = _attn(q, k, v, lam4, da_subln, lambda_init)
    res["y_da"] = y_da
    y_rw = _rwkv(rw, rw_w0, rw_w2[0], rw_a0, rw_a2[0], rw_g2[0], rw_k_k, rw_k_a, rw_r_k[0],
                 rw_ln_w, rw_ln_b)
    res["y_rw"] = y_rw
    x1, h2p, top_idx, top_w = _out(y_da, y_rw, x, mod3, w_out[0].astype(BF16), post_mix_norm,
                                   pre_ffn_norm, router_w[0], router_b[0])
    res.update(x1=x1, top_idx=top_idx, top_w=top_w)
    n = b * t
    n_tiles = n * TOP_K // EXPERT_ROWS + N_EXPERTS
    pos, plan = _route(top_idx.reshape(n, TOP_K), EXPERT_ROWS, n_tiles)
    xs = _dispatch(pos, h2p.reshape(n * ROW_SLAB, LANES), n_tiles * EXPERT_ROWS)
    xs = xs.reshape(n_tiles * EXPERT_ROWS * ROW_SLAB, LANES)
    b1 = jnp.broadcast_to(moe_b1[0][:, None, :], (N_EXPERTS, 8, moe_b1.shape[-1]))
    ys = _experts(plan, xs, moe_w1[0], b1, moe_w2[0], moe_b2[0][:, None, :])
    res["final"] = _combine(pos, ys, top_w, x1, mod3, post_ffn_norm)
    return res


stages = _stages


def kernel(x, c, positions, ada_w, ada_b, pre_mix_norm, post_mix_norm, pre_ffn_norm, post_ffn_norm, w_in, w_out, da_lambda_q1, da_lambda_k1, da_lambda_q2, da_lambda_k2, da_subln, rw_mu, rw_w0, rw_w2, rw_a0, rw_a2, rw_g2, rw_k_k, rw_k_a, rw_r_k, rw_ln_w, rw_ln_b, router_w, router_b, moe_w1, moe_b1, moe_w2, moe_b2):
    res = _stages(x, c, positions, ada_w, ada_b, pre_mix_norm, post_mix_norm, pre_ffn_norm,
                  post_ffn_norm, w_in, w_out, da_lambda_q1, da_lambda_k1, da_lambda_q2,
                  da_lambda_k2, da_subln, rw_mu, rw_w0, rw_w2, rw_a0, rw_a2, rw_g2, rw_k_k,
                  rw_k_a, rw_r_k, rw_ln_w, rw_ln_b, router_w, router_b, moe_w1, moe_b1,
                  moe_w2, moe_b2)
    return res["final"]
```

```python
import functools
import math

import jax
import jax.numpy as jnp
from jax import lax
from jax.experimental import pallas as pl
from jax.experimental.pallas import tpu as pltpu

F32 = jnp.float32
BF16 = jnp.bfloat16

DA_HEADS = 4
DA_HEAD_DIM = 64
DA_V_DIM = 128
DA_WIDTH = 512
RW_HEADS = 8
RW_HEAD_DIM = 64
RW_WIDTH = 512
DECAY_LORA = 64
AAA_LORA = 64
GATE_LORA = 128
DA_COLS = 1536
RW_COLS = 1792
ROPE_THETA = 500000.0
ROPE_DIM = 16
N_EXPERTS = 32
TOP_K = 4
SWIGLU_ALPHA = 1.702
SWIGLU_LIMIT = 7.0
NORM_EPS = 1e-6
SUBLN_EPS = 1e-5
LN_X_EPS = 64e-5
N_MOD = 6

LANES = 128
SUBLANES = 8
VMEM_LIMIT_BYTES = 56 * 1024 * 1024

PROJ_ROWS = 512
ATTN_BLOCK = 512
ATTN_KV_BLOCK = 512
ATTN_HEAD_GROUP = 2
RW_CHUNK = 128
RW_BLOCK = 256
OUT_ROWS = 512
EXPERT_ROWS = 512
DISPATCH_TOKENS = 2048
COMBINE_TOKENS = 1024
COMBINE_SUB_ROWS = 256


def _params(*sem):
    return pltpu.CompilerParams(dimension_semantics=sem, vmem_limit_bytes=VMEM_LIMIT_BYTES)


def _bdot(a, b):
    return jnp.dot(a.astype(BF16), b.astype(BF16), preferred_element_type=F32)


def _bdot_nt(a, b):
    return lax.dot_general(a.astype(BF16), b.astype(BF16), (((1,), (1,)), ((), ())),
                           preferred_element_type=F32)


def _bdot_tn(a, b):
    return lax.dot_general(a.astype(BF16), b.astype(BF16), (((0,), (0,)), ((), ())),
                           preferred_element_type=F32)


def _rms(x, w, eps):
    return x * lax.rsqrt(jnp.mean(x * x, axis=-1, keepdims=True) + eps) * w


def _mod_kernel(c_ref, w_ref, b_ref, o_ref):
    c = c_ref[...]
    s = c * jax.nn.sigmoid(c)
    o_ref[...] = _bdot(s, w_ref[...]) + b_ref[...]


def _mod(c, ada_w, ada_b):
    b, d = c.shape
    n = ada_w.shape[1]
    return pl.pallas_call(
        _mod_kernel,
        grid=(n // d,),
        in_specs=[pl.BlockSpec((b, d), lambda j: (0, 0)),
                  pl.BlockSpec((d, d), lambda j: (0, j)),
                  pl.BlockSpec((1, d), lambda j: (0, j))],
        out_specs=pl.BlockSpec((b, d), lambda j: (0, j)),
        out_shape=jax.ShapeDtypeStruct((b, n), F32),
        compiler_params=_params("parallel"),
        name="mod",
    )(c, ada_w, ada_b.reshape(1, n))


def _proj_kernel(x_ref, pos_ref, mod_ref, nw_ref, invf_ref, w_ref, mu_ref,
                 q_ref, k_ref, v_ref, rw_ref, carry_ref):
    ti = pl.program_id(1)

    @pl.when(ti == 0)
    def _():
        carry_ref[...] = jnp.zeros_like(carry_ref)

    x = x_ref[...]
    h = _rms(x, nw_ref[...], NORM_EPS) * (1.0 + mod_ref[1:2, :]) + mod_ref[0:1, :]
    hb = h.astype(BF16)

    ang = pos_ref[...].astype(F32) * invf_ref[...]
    cos, sin = jnp.cos(ang), jnp.sin(ang)
    l64 = lax.broadcasted_iota(jnp.int32, ang.shape, 1) % DA_HEAD_DIM
    half = ROPE_DIM // 2
    c_tab = jnp.where(l64 < ROPE_DIM, cos, 1.0)
    s_lo = jnp.where(l64 < half, -sin, 0.0)
    s_hi = jnp.where((l64 >= half) & (l64 < ROPE_DIM), sin, 0.0)

    def rope(z):
        up = pltpu.roll(z, LANES - half, axis=1)
        dn = pltpu.roll(z, half, axis=1)
        return z * c_tab + up * s_lo + dn * s_hi

    for g in range(DA_WIDTH // LANES):
        sl = slice(g * LANES, (g + 1) * LANES)
        qg = jnp.dot(hb, w_ref[:, sl], preferred_element_type=F32)
        q_ref[:, sl] = (rope(qg) * (DA_HEAD_DIM ** -0.5)).astype(q_ref.dtype)
        kg = jnp.dot(hb, w_ref[:, DA_WIDTH + g * LANES:DA_WIDTH + (g + 1) * LANES],
                     preferred_element_type=F32)
        k_ref[:, sl] = rope(kg).astype(k_ref.dtype)
    v_ref[...] = jnp.dot(hb, w_ref[:, 2 * DA_WIDTH:DA_COLS],
                         preferred_element_type=F32).astype(v_ref.dtype)

    p = jnp.dot(hb, w_ref[:, DA_COLS:], preferred_element_type=F32)
    rows = p.shape[0]
    prev = pltpu.roll(p, 1, axis=0)
    first = lax.broadcasted_iota(jnp.int32, p.shape, 0) == 0
    prev = jnp.where(first, carry_ref[0:1, :], prev)
    rw_ref[...] = p + (prev - p) * mu_ref[...]
    carry_ref[0:1, :] = p[rows - 1:rows, :]


def _proj(x, pos3, mod3, norm_w, invf, w_in_b, mu):
    b, t, d = x.shape
    tm = min(PROJ_ROWS, t)
    n_in = w_in_b.shape[1]
    blk = lambda w: pl.BlockSpec((None, tm, w), lambda bi, ti: (bi, ti, 0))
    full = lambda r, c: pl.BlockSpec((r, c), lambda bi, ti: (0, 0))
    return pl.pallas_call(
        _proj_kernel,
        grid=(b, t // tm),
        in_specs=[blk(d), blk(1),
                  pl.BlockSpec((None, N_MOD, d), lambda bi, ti: (bi, 0, 0)),
                  full(1, d), full(1, LANES), full(d, n_in), full(1, RW_COLS)],
        out_specs=[blk(DA_WIDTH), blk(DA_WIDTH), blk(DA_WIDTH), blk(RW_COLS)],
        out_shape=[jax.ShapeDtypeStruct((b, t, DA_WIDTH), BF16)] * 3
        + [jax.ShapeDtypeStruct((b, t, RW_COLS), F32)],
        scratch_shapes=[pltpu.VMEM((8, RW_COLS), F32)],
        compiler_params=_params("parallel", "arbitrary"),
        name="proj",
    )(x, pos3, mod3, norm_w, invf, w_in_b, mu)


def _attn_kernel(q_ref, k_ref, v_ref, lam_ref, subln_ref, o_ref, m_ref, l_ref, acc_ref,
                 *, lambda_init):
    qi = pl.program_id(2)
    tq = q_ref.shape[0]
    heads = range(ATTN_HEAD_GROUP)
    hs = [slice(h * DA_V_DIM, (h + 1) * DA_V_DIM) for h in heads]
    lane = lax.broadcasted_iota(jnp.int32, (tq, DA_V_DIM), 1)
    qq = []
    for c in hs:
        q = q_ref[:, c]
        zero = jnp.zeros_like(q)
        qq.append(jnp.concatenate([jnp.where(lane < DA_HEAD_DIM, q, zero),
                                   jnp.where(lane >= DA_HEAD_DIM, q, zero)], axis=0))

    m_ref[...] = jnp.full(m_ref.shape, -jnp.inf, F32)
    l_ref[...] = jnp.zeros(l_ref.shape, F32)
    acc_ref[...] = jnp.zeros(acc_ref.shape, F32)
    tk = ATTN_KV_BLOCK if k_ref.shape[0] % ATTN_KV_BLOCK == 0 else tq
    rep = tk // LANES

    def step(j, masked):
        rows = pl.ds(pl.multiple_of(j * tk, tk), tk)
        s = [lax.dot_general(qq[h], k_ref[rows, hs[h]], (((1,), (1,)), ((), ())),
                             preferred_element_type=F32) for h in heads]
        if masked:
            qpos = qi * tq + lax.broadcasted_iota(jnp.int32, s[0].shape, 0) % tq
            kpos = j * tk + lax.broadcasted_iota(jnp.int32, s[0].shape, 1)
            s = [jnp.where(qpos >= kpos, x, -jnp.inf) for x in s]
        for h in heads:
            m_old = m_ref[h]
            m_new = jnp.maximum(m_old, jnp.max(s[h], axis=-1, keepdims=True))
            alpha = jnp.exp(m_old - m_new)
            p = jnp.exp(s[h] - jnp.concatenate([m_new] * rep, axis=1))
            l_ref[h] = alpha * l_ref[h] + jnp.sum(p, axis=-1, keepdims=True)
            acc_ref[h] = alpha * acc_ref[h] + jnp.dot(p.astype(v_ref.dtype), v_ref[rows, hs[h]],
                                                      preferred_element_type=F32)
            m_ref[h] = m_new

    def body(j, carry):
        step(j, False)
        return carry

    n_full = (qi * tq) // tk
    lax.fori_loop(0, n_full, body, 0)
    step(n_full, True)

    lam = (jnp.exp(jnp.sum(lam_ref[0:1, :] * lam_ref[1:2, :], axis=-1, keepdims=True))
           - jnp.exp(jnp.sum(lam_ref[2:3, :] * lam_ref[3:4, :], axis=-1, keepdims=True))
           + lambda_init)
    for h in heads:
        o = acc_ref[h] / l_ref[h]
        d = o[:tq, :] - lam * o[tq:, :]
        o_ref[:, hs[h]] = (_rms(d, subln_ref[...], SUBLN_EPS)
                           * (1.0 - lambda_init)).astype(o_ref.dtype)


def _attn(q, k, v, lam4, subln, lambda_init):
    b, t, _ = q.shape
    tq = min(ATTN_BLOCK, t)
    hg = ATTN_HEAD_GROUP
    gw = hg * DA_V_DIM
    return pl.pallas_call(
        functools.partial(_attn_kernel, lambda_init=lambda_init),
        grid=(b, DA_HEADS // hg, t // tq),
        in_specs=[pl.BlockSpec((None, tq, gw), lambda bi, h, qi: (bi, qi, h)),
                  pl.BlockSpec((None, t, gw), lambda bi, h, qi: (bi, 0, h)),
                  pl.BlockSpec((None, t, gw), lambda bi, h, qi: (bi, 0, h)),
                  pl.BlockSpec((4, DA_HEAD_DIM), lambda bi, h, qi: (0, 0)),
                  pl.BlockSpec((1, DA_V_DIM), lambda bi, h, qi: (0, 0))],
        out_specs=pl.BlockSpec((None, tq, gw), lambda bi, h, qi: (bi, qi, h)),
        out_shape=jax.ShapeDtypeStruct((b, t, DA_WIDTH), BF16),
        scratch_shapes=[pltpu.VMEM((hg, 2 * tq, LANES), F32), pltpu.VMEM((hg, 2 * tq, LANES), F32),
                        pltpu.VMEM((hg, 2 * tq, DA_V_DIM), F32)],
        compiler_params=_params("parallel", "parallel", "arbitrary"),
        name="attn",
    )(q, k, v, lam4, subln)


def _rwkv_kernel(rw_ref, w0_ref, w2_ref, a0_ref, a2_ref, g2_ref, kk_ref, ka_ref, rk_ref,
                 lnw_ref, lnb_ref, o_ref, state_ref, r_s, k_s, v_s, lw_s, kk_s, a_s, g_s, cum_s):
    ti = pl.program_id(1)

    @pl.when(ti == 0)
    def _():
        state_ref[...] = jnp.zeros_like(state_ref)

    w = RW_WIDTH
    rw = rw_ref[...]
    k = rw[:, w:2 * w]
    wl = rw[:, 3 * w:3 * w + DECAY_LORA]
    al = rw[:, 3 * w + DECAY_LORA:3 * w + DECAY_LORA + AAA_LORA]
    gl = rw[:, 3 * w + DECAY_LORA + AAA_LORA:]
    z = -(w0_ref[...] + _bdot(jnp.tanh(wl), w2_ref[...]))
    softplus = jnp.maximum(z, 0.0) + jnp.log(1.0 + jnp.exp(-jnp.abs(z)))
    a = jax.nn.sigmoid(a0_ref[...] + _bdot(al, a2_ref[...]))
    r_s[...] = rw[:, 0:w]
    v_s[...] = rw[:, 2 * w:3 * w]
    lw_s[...] = -jnp.exp(-softplus - 0.5)
    a_s[...] = a
    g_s[...] = _bdot(jax.nn.sigmoid(gl), g2_ref[...])
    kk_s[...] = k * kk_ref[...]
    k_s[...] = k * (1.0 + (a - 1.0) * ka_ref[...])

    c_len = RW_CHUNK
    n = RW_HEAD_DIM
    tb = rw_ref.shape[0]

    br = lax.broadcasted_iota(jnp.int32, (tb, tb), 0)
    bc = lax.broadcasted_iota(jnp.int32, (tb, tb), 1)
    tri = jnp.where((br >= bc) & (br // c_len == bc // c_len), 1.0, 0.0).astype(BF16)
    lw_all = lw_s[...]
    lw_hi = lw_all.astype(BF16)
    rem = lw_all - lw_hi.astype(F32)
    lw_mid = rem.astype(BF16)
    lw_lo = (rem - lw_mid.astype(F32)).astype(BF16)
    cum_s[...] = (jnp.dot(tri, lw_hi, preferred_element_type=F32)
                  + jnp.dot(tri, lw_mid, preferred_element_type=F32)
                  + jnp.dot(tri, lw_lo, preferred_element_type=F32))

    row = lax.broadcasted_iota(jnp.int32, (c_len, 2 * c_len), 0)
    col = lax.broadcasted_iota(jnp.int32, (c_len, 2 * c_len), 1)
    incl2 = row >= col % c_len
    strict2 = row > col % c_len
    eye = jnp.where(lax.broadcasted_iota(jnp.int32, (c_len, c_len), 0)
                    == lax.broadcasted_iota(jnp.int32, (c_len, c_len), 1), 1.0, 0.0).astype(F32)

    def chunk(ci, carry):
        rows = pl.ds(pl.multiple_of(ci * c_len, c_len), c_len)
        heads = range(RW_HEADS)
        sl = [slice(h * n, (h + 1) * n) for h in heads]
        r = [r_s[rows, c] for c in sl]
        kh = [k_s[rows, c] for c in sl]
        v = [v_s[rows, c] for c in sl]
        lw = [lw_s[rows, c] for c in sl]
        cum = [cum_s[rows, c] for c in sl]
        kk = [kk_s[rows, c] for c in sl]
        kk = [x * lax.rsqrt(jnp.maximum(jnp.sum(x * x, axis=-1, keepdims=True), 1e-24)) for x in kk]
        kka = [kk[h] * a_s[rows, sl[h]] for h in heads]
        end = [jnp.sum(x, axis=0, keepdims=True) for x in lw]
        e_neg = [jnp.exp(-x) for x in cum]
        e_end = [jnp.exp(end[h] - cum[h]) for h in heads]
        left = [jnp.concatenate([-kk[h] * jnp.exp(cum[h] - lw[h]), r[h] * jnp.exp(cum[h])], axis=0)
                for h in heads]
        g = [_bdot_nt(left[h], jnp.concatenate([kka[h] * e_neg[h], kh[h] * e_neg[h]], axis=0))
             for h in heads]
        a_a = [jnp.where(strict2, x[:c_len, :], 0.0) for x in g]
        a_r = [jnp.where(incl2, x[c_len:, :], 0.0) for x in g]
        pw = [x[:, :c_len] for x in a_a]
        inv = [eye + x for x in pw]
        for _ in range(c_len.bit_length() - 2):
            pw = [_bdot(x, x) for x in pw]
            inv = [inv[h] + _bdot(inv[h], pw[h]) for h in heads]
        akv = [_bdot(a_a[h][:, c_len:], v[h]) for h in heads]
        s0 = [state_ref[h] for h in heads]
        ls = [_bdot_nt(left[h], s0[h]) for h in heads]
        u = [_bdot(inv[h], ls[h][:c_len, :] + akv[h]) for h in heads]
        uv = [jnp.concatenate([u[h], v[h]], axis=0) for h in heads]
        y = [ls[h][c_len:, :] + _bdot(a_r[h], uv[h]) for h in heads]
        for h in heads:
            state_ref[h] = s0[h] * jnp.exp(end[h]) + _bdot_tn(
                uv[h], jnp.concatenate([kka[h] * e_end[h], kh[h] * e_end[h]], axis=0))
        for h in heads:
            mean = jnp.mean(y[h], axis=-1, keepdims=True)
            yc = y[h] - mean
            var = jnp.mean(yc * yc, axis=-1, keepdims=True)
            yn = yc * lax.rsqrt(var + LN_X_EPS) * lnw_ref[:, sl[h]] + lnb_ref[:, sl[h]]
            bonus = jnp.sum(r[h] * kh[h] * rk_ref[:, sl[h]], axis=-1, keepdims=True) * v[h]
            o_ref[rows, sl[h]] = ((yn + bonus) * g_s[rows, sl[h]]).astype(o_ref.dtype)
        return carry

    lax.fori_loop(0, rw_ref.shape[0] // c_len, chunk, 0, unroll=2)


def _rwkv(rw, w0, w2, a0, a2, g2, k_k, k_a, r_k, ln_w, ln_b):
    b, t, _ = rw.shape
    tb = min(RW_BLOCK, t)
    w = RW_WIDTH
    vec = pl.BlockSpec((1, w), lambda bi, ti: (0, 0))
    mat = lambda r: pl.BlockSpec((r, w), lambda bi, ti: (0, 0))
    return pl.pallas_call(
        _rwkv_kernel,
        grid=(b, t // tb),
        in_specs=[pl.BlockSpec((None, tb, RW_COLS), lambda bi, ti: (bi, ti, 0)),
                  vec, mat(DECAY_LORA), vec, mat(AAA_LORA), mat(GATE_LORA), vec, vec, vec, vec, vec],
        out_specs=pl.BlockSpec((None, tb, w), lambda bi, ti: (bi, ti, 0)),
        out_shape=jax.ShapeDtypeStruct((b, t, w), BF16),
        scratch_shapes=[pltpu.VMEM((RW_HEADS, RW_HEAD_DIM, RW_HEAD_DIM), F32)]
        + [pltpu.VMEM((tb, w), F32)] * 8,
        compiler_params=_params("parallel", "arbitrary"),
        name="rwkv",
    )(rw, w0, w2, a0, a2, g2, k_k, k_a, r_k.reshape(1, w), ln_w, ln_b)


def _pack_rows(x):
    half = x.shape[1] // 2
    hi = pltpu.bitcast(x[:, :half].astype(BF16).astype(F32), jnp.uint32)
    lo = pltpu.bitcast(x[:, half:].astype(BF16).astype(F32), jnp.uint32)
    return hi | (lo >> 16)


def _unpack_rows(u):
    hi = pltpu.bitcast(u & jnp.uint32(0xFFFF0000), F32)
    lo = pltpu.bitcast(u << 16, F32)
    return jnp.concatenate([hi, lo], axis=1)


ROW_SLAB = 4


def _store_rows(ref, u, r0=0):
    n = u.shape[0]
    for c in range(ROW_SLAB):
        ref[pl.ds(r0 * ROW_SLAB + c, n, stride=ROW_SLAB), :] = u[:, c * LANES:(c + 1) * LANES]


def _load_rows(ref, n, r0=0):
    return jnp.concatenate([ref[pl.ds(r0 * ROW_SLAB + c, n, stride=ROW_SLAB), :]
                            for c in range(ROW_SLAB)], axis=1)


def _out_kernel(yda_ref, yrw_ref, x_ref, mod_ref, wo_ref, pmn_ref, pfn_ref, rw_ref, rb_ref,
                x1_ref, h2_ref, idx_ref, wgt_ref):
    y = (jnp.dot(yda_ref[...], wo_ref[0:DA_WIDTH, :], preferred_element_type=F32)
         + jnp.dot(yrw_ref[...], wo_ref[DA_WIDTH:, :], preferred_element_type=F32))
    x1 = x_ref[...] + mod_ref[2:3, :] * _rms(y, pmn_ref[...], NORM_EPS)
    x1_ref[...] = x1
    h2 = _rms(x1, pfn_ref[...], NORM_EPS) * (1.0 + mod_ref[4:5, :]) + mod_ref[3:4, :]
    _store_rows(h2_ref, _pack_rows(h2))

    h_hi = h2.astype(BF16)
    h_lo = (h2 - h_hi.astype(F32)).astype(BF16)
    rw = rw_ref[...]
    w_hi = rw.astype(BF16)
    w_lo = (rw - w_hi.astype(F32)).astype(BF16)
    logits = (jnp.dot(h_hi, w_hi, preferred_element_type=F32)
              + jnp.dot(h_hi, w_lo, preferred_element_type=F32)
              + jnp.dot(h_lo, w_hi, preferred_element_type=F32)) + rb_ref[...]

    lane = lax.broadcasted_iota(jnp.int32, logits.shape, 1)
    slot = lax.broadcasted_iota(jnp.int32, idx_ref.shape, 1)
    idx = jnp.zeros(idx_ref.shape, jnp.int32)
    val = jnp.zeros(idx_ref.shape, F32)
    top = None
    for j in range(TOP_K):
        m = jnp.max(logits, axis=-1, keepdims=True)
        i = jnp.min(jnp.where(logits == m, lane, N_EXPERTS), axis=-1, keepdims=True)
        top = m if top is None else top
        idx = jnp.where(slot == j, i, idx)
        val = jnp.where(slot == j, jnp.exp(m - top), val)
        logits = jnp.where(lane == i, -jnp.inf, logits)
    idx_ref[...] = idx
    wgt_ref[...] = val / jnp.sum(val, axis=-1, keepdims=True)


def _out(y_da, y_rw, x, mod3, w_out_b, post_mix_norm, pre_ffn_norm, router_w, router_b):
    b, t, d = x.shape
    tm = min(OUT_ROWS, t)
    e = router_w.shape[1]
    blk = lambda w: pl.BlockSpec((None, tm, w), lambda bi, ti: (bi, ti, 0))
    full = lambda r, c: pl.BlockSpec((r, c), lambda bi, ti: (0, 0))
    return pl.pallas_call(
        _out_kernel,
        grid=(b, t // tm),
        in_specs=[blk(DA_WIDTH), blk(RW_WIDTH), blk(d),
                  pl.BlockSpec((None, N_MOD, d), lambda bi, ti: (bi, 0, 0)),
                  full(d, d), full(1, d), full(1, d), full(d, e), full(1, e)],
        out_specs=[blk(d),
                   pl.BlockSpec((None, tm * ROW_SLAB, LANES), lambda bi, ti: (bi, ti, 0)),
                   blk(TOP_K), blk(TOP_K)],
        out_shape=[jax.ShapeDtypeStruct((b, t, d), F32),
                   jax.ShapeDtypeStruct((b, t * ROW_SLAB, LANES), jnp.uint32),
                   jax.ShapeDtypeStruct((b, t, TOP_K), jnp.int32),
                   jax.ShapeDtypeStruct((b, t, TOP_K), F32)],
        compiler_params=_params("parallel", "parallel"),
        name="out",
    )(y_da, y_rw, x, mod3, w_out_b, post_mix_norm, pre_ffn_norm, router_w,
      router_b.reshape(1, e))


def _route(top_idx, rows_per_tile, n_tiles):
    experts = jnp.arange(N_EXPERTS, dtype=jnp.int32)
    chosen = (top_idx[:, :, None] == experts[None, None, :]).astype(jnp.int32)
    member = jnp.sum(chosen, axis=1)
    csum = jnp.cumsum(member, axis=0)
    counts = csum[-1]
    padded = (counts + rows_per_tile - 1) // rows_per_tile * rows_per_tile
    ends = jnp.cumsum(padded)
    starts = ends - padded
    pos = jnp.sum((csum - member + starts[None, :])[:, None, :] * chosen, axis=2).reshape(-1)
    n_active = ends[-1] // rows_per_tile
    tile_start = jnp.arange(n_tiles, dtype=jnp.int32) * rows_per_tile
    tile = jnp.minimum(tile_start, ends[-1] - 1)
    tile_expert = jnp.sum((tile[:, None] >= ends[None, :]).astype(jnp.int32), axis=1)
    used = padded > 0
    later = lax.cummin(jnp.where(used, experts, N_EXPERTS), reverse=True)
    following = jnp.concatenate([later[1:], jnp.full((1,), N_EXPERTS, jnp.int32)])
    following = jnp.where(following == N_EXPERTS, -1, following)
    run_index = jnp.cumsum(used.astype(jnp.int32)) - 1
    plan = (tile_expert.astype(jnp.int32), n_active.reshape(1).astype(jnp.int32),
            following[tile_expert].astype(jnp.int32), (run_index[tile_expert] % 2).astype(jnp.int32))
    return pos.astype(jnp.int32), plan


def _dispatch_kernel(pos_ref, h_ref, xs_in_ref, xs_ref, sem):
    del xs_in_ref
    tb = h_ref.shape[0] // ROW_SLAB

    def issue(t, carry):
        src = h_ref.at[pl.ds(pl.multiple_of(t * ROW_SLAB, ROW_SLAB), ROW_SLAB)]
        for j in range(TOP_K):
            pltpu.make_async_copy(src, xs_ref.at[pos_ref[0, t * TOP_K + j]],
                                  sem).start(priority=j % 2)
        return carry

    lax.fori_loop(0, tb, issue, 0, unroll=8)
    for j in range(TOP_K):
        pltpu.make_async_copy(xs_ref.at[pl.ds(0, tb)], xs_ref.at[pl.ds(0, tb)], sem).wait()


def _dispatch(pos, h2p, n_rows):
    n = h2p.shape[0] // ROW_SLAB
    w = h2p.shape[1]
    tb = min(DISPATCH_TOKENS, n)
    pos3 = pos.reshape(n // tb, 1, tb * TOP_K)
    xs0 = jnp.zeros((n_rows, ROW_SLAB, w), h2p.dtype)
    return pl.pallas_call(
        _dispatch_kernel,
        grid=(n // tb,),
        in_specs=[pl.BlockSpec((None, 1, tb * TOP_K), lambda i: (i, 0, 0),
                               memory_space=pltpu.SMEM),
                  pl.BlockSpec((tb * ROW_SLAB, w), lambda i: (i, 0)),
                  pl.BlockSpec(memory_space=pl.ANY)],
        out_specs=pl.BlockSpec(memory_space=pl.ANY),
        out_shape=jax.ShapeDtypeStruct(xs0.shape, h2p.dtype),
        scratch_shapes=[pltpu.SemaphoreType.DMA(())],
        input_output_aliases={2: 0},
        compiler_params=_params("arbitrary"),
        name="dispatch",
    )(pos3, h2p, xs0)


def _expert_kernel(te_ref, na_ref, nx_ref, sl_ref, xs_ref, w1_hbm, b1_ref, w2_hbm, b2_ref, ys_ref,
                   w1p_s, b1p_s, w2b_s, act_s, w1_buf, w2_buf, sem):
    i = pl.program_id(0)
    active = i < na_ref[0]
    expert = te_ref[i]
    fresh = jnp.logical_or(i == 0, expert != te_ref[jnp.maximum(i - 1, 0)])
    slot = sl_ref[i]
    grp = 2 * LANES
    n_grp = w1_buf.shape[2] // grp

    def weight_copies(ex, s):
        return (pltpu.make_async_copy(w1_hbm.at[ex], w1_buf.at[s], sem.at[0, s]),
                pltpu.make_async_copy(w2_hbm.at[ex], w2_buf.at[s], sem.at[1, s]))

    @pl.when(jnp.logical_and(active, i == 0))
    def _():
        for cp in weight_copies(expert, slot):
            cp.start()

    @pl.when(jnp.logical_and(active, fresh))
    def _():
        for cp in weight_copies(expert, slot):
            cp.wait()

        @pl.when(nx_ref[i] >= 0)
        def _():
            for cp in weight_copies(nx_ref[i], 1 - slot):
                cp.start()

        src = lax.broadcasted_iota(jnp.int32, (grp, grp), 0)
        dst = lax.broadcasted_iota(jnp.int32, (grp, grp), 1)
        perm = jnp.where(src == jnp.where(dst < LANES, 2 * dst, 2 * (dst - LANES) + 1),
                         1.0, 0.0).astype(BF16)
        for g in range(n_grp):
            sl = slice(g * grp, (g + 1) * grp)
            w1p_s[:, sl] = jnp.dot(w1_buf[slot, :, sl].astype(BF16), perm,
                                   preferred_element_type=F32).astype(BF16)
            b = b1_ref[:, sl]
            b_hi = b.astype(BF16)
            b_lo = (b - b_hi.astype(F32)).astype(BF16)
            b1p_s[:, sl] = (jnp.dot(b_hi, perm, preferred_element_type=F32)
                            + jnp.dot(b_lo, perm, preferred_element_type=F32))
        w2b_s[...] = w2_buf[slot].astype(BF16)

    @pl.when(active)
    def _():
        x = _unpack_rows(_load_rows(xs_ref, act_s.shape[0])).astype(BF16)
        hid = jnp.dot(x, w1p_s[...], preferred_element_type=F32) + b1p_s[0:1, :]
        for g in range(n_grp):
            glu = jnp.minimum(hid[:, g * grp:g * grp + LANES], SWIGLU_LIMIT)
            lin = jnp.clip(hid[:, g * grp + LANES:(g + 1) * grp], -SWIGLU_LIMIT, SWIGLU_LIMIT)
            act_s[:, g * LANES:(g + 1) * LANES] = (
                glu * jax.nn.sigmoid(SWIGLU_ALPHA * glu) * (lin + 1.0)).astype(BF16)
        y = jnp.dot(act_s[...], w2b_s[...], preferred_element_type=F32) + b2_ref[...]
        _store_rows(ys_ref, _pack_rows(y))

    @pl.when(jnp.logical_not(active))
    def _():
        ys_ref[...] = jnp.zeros_like(ys_ref)


def _experts(plan, xs, w1, b1, w2, b2):
    tile_expert, n_active, next_expert, tile_slot = plan
    n_rows, w = xs.shape[0] // ROW_SLAB, xs.shape[1]
    tm = EXPERT_ROWS
    d, f2 = w1.shape[1], w1.shape[2]
    f = f2 // 2
    wspec = lambda r, c: pl.BlockSpec((None, r, c), lambda i, te, na, nx, sl: (te[i], 0, 0))
    rows = pl.BlockSpec((tm * ROW_SLAB, w), lambda i, te, na, nx, sl: (i, 0))
    hbm = pl.BlockSpec(memory_space=pl.ANY)
    return pl.pallas_call(
        _expert_kernel,
        grid_spec=pltpu.PrefetchScalarGridSpec(
            num_scalar_prefetch=4,
            grid=(n_rows // tm,),
            in_specs=[rows, hbm, wspec(8, f2), hbm, wspec(1, d)],
            out_specs=rows,
            scratch_shapes=[pltpu.VMEM((d, f2), BF16), pltpu.VMEM((8, f2), F32),
                            pltpu.VMEM((f, d), BF16), pltpu.VMEM((tm, f), BF16),
                            pltpu.VMEM((2, d, f2), F32), pltpu.VMEM((2, f, d), F32),
                            pltpu.SemaphoreType.DMA((2, 2))]),
        out_shape=jax.ShapeDtypeStruct((n_rows * ROW_SLAB, w), jnp.uint32),
        compiler_params=_params("arbitrary"),
        name="expert",
    )(tile_expert, n_active, next_expert, tile_slot, xs, w1, b1, w2, b2)


def _combine_kernel(pos_ref, nxt_ref, ys_ref, wgt_ref, x1_ref, mod_ref, nw_ref, o_ref, buf_ref,
                    sem):
    tc = x1_ref.shape[0]
    step = pl.program_id(0) * pl.num_programs(1) + pl.program_id(1)
    last = pl.num_programs(0) * pl.num_programs(1) - 1
    slot = step % 2

    def gather(p_ref, s):
        def issue(t, carry):
            dst = pl.ds(pl.multiple_of(t * ROW_SLAB, ROW_SLAB), ROW_SLAB)
            for j in range(TOP_K):
                pltpu.make_async_copy(ys_ref.at[p_ref[0, t * TOP_K + j]], buf_ref.at[s, j, dst],
                                      sem.at[s]).start(priority=j % 2)
            return carry

        lax.fori_loop(0, tc, issue, 0, unroll=8)

    @pl.when(step == 0)
    def _():
        gather(pos_ref, slot)

    @pl.when(step < last)
    def _():
        gather(nxt_ref, 1 - slot)

    for j in range(TOP_K):
        pltpu.make_async_copy(ys_ref.at[pl.ds(0, tc)], ys_ref.at[pl.ds(0, tc)], sem.at[slot]).wait()

    sub = min(COMBINE_SUB_ROWS, tc)

    def mix(i, carry):
        r0 = pl.multiple_of(i * sub, sub)
        wgt = wgt_ref[pl.ds(r0, sub), :]
        acc = jnp.zeros((sub, x1_ref.shape[1]), F32)
        for j in range(TOP_K):
            acc = acc + wgt[:, j:j + 1] * _unpack_rows(_load_rows(buf_ref.at[slot, j], sub, r0))
        o_ref[pl.ds(r0, sub), :] = (x1_ref[pl.ds(r0, sub), :]
                                    + mod_ref[5:6, :] * _rms(acc, nw_ref[...], NORM_EPS))
        return carry

    lax.fori_loop(0, tc // sub, mix, 0)


def _combine(pos, ys, wgt, x1, mod3, post_ffn_norm):
    b, t, d = x1.shape
    tc = min(COMBINE_TOKENS, t)
    nt = t // tc
    pos3 = pos.reshape(b * nt, 1, tc * TOP_K)
    blk = lambda w: pl.BlockSpec((None, tc, w), lambda bi, ti: (bi, ti, 0))
    return pl.pallas_call(
        _combine_kernel,
        grid=(b, nt),
        in_specs=[pl.BlockSpec((None, 1, tc * TOP_K), lambda bi, ti: (bi * nt + ti, 0, 0),
                               memory_space=pltpu.SMEM),
                  pl.BlockSpec((None, 1, tc * TOP_K),
                               lambda bi, ti: (jnp.minimum(bi * nt + ti + 1, b * nt - 1), 0, 0),
                               memory_space=pltpu.SMEM),
                  pl.BlockSpec(memory_space=pl.ANY),
                  blk(TOP_K), blk(d),
                  pl.BlockSpec((None, N_MOD, d), lambda bi, ti: (bi, 0, 0)),
                  pl.BlockSpec((1, d), lambda bi, ti: (0, 0))],
        out_specs=blk(d),
        out_shape=jax.ShapeDtypeStruct((b, t, d), F32),
        scratch_shapes=[pltpu.VMEM((2, TOP_K, tc * ROW_SLAB, LANES), jnp.uint32),
                        pltpu.SemaphoreType.DMA((2,))],
        compiler_params=_params("arbitrary", "arbitrary"),
        name="combine",
    )(pos3, pos3, ys.reshape(ys.shape[0] // ROW_SLAB, ROW_SLAB, LANES), wgt, x1, mod3,
      post_ffn_norm)


def _stages(x, c, positions, ada_w, ada_b, pre_mix_norm, post_mix_norm, pre_ffn_norm,
            post_ffn_norm, w_in, w_out, da_lambda_q1, da_lambda_k1, da_lambda_q2, da_lambda_k2,
            da_subln, rw_mu, rw_w0, rw_w2, rw_a0, rw_a2, rw_g2, rw_k_k, rw_k_a, rw_r_k, rw_ln_w,
            rw_ln_b, router_w, router_b, moe_w1, moe_b1, moe_w2, moe_b2):
    b, t, d = x.shape
    res = {}
    lambda_init = 0.8 - 0.6 * math.exp(-0.3 * 0)
    mod = _mod(c, ada_w[0], ada_b[0])
    res["mod"] = mod
    mod3 = mod.reshape(b, N_MOD, d)
    inv_freq = ROPE_THETA ** (-jnp.arange(0, ROPE_DIM, 2, dtype=F32) / ROPE_DIM)
    invf = jnp.tile(inv_freq, LANES // (ROPE_DIM // 2)).reshape(1, LANES)
    q, k, v, rw = _proj(x, positions.reshape(b, t, 1), mod3, pre_mix_norm, invf,
                        w_in[0].astype(BF16), rw_mu)
    res.update(q=q, k=k, v=v, rw=rw)
    lam4 = jnp.concatenate([da_lambda_q1, da_lambda_k1, da_lambda_q2, da_lambda_k2], axis=0)
    y_da = _attn(q, k, v, lam4, da_subln, lambda_init)
    res["y_da"] = y_da
    y_rw = _rwkv(rw, rw_w0, rw_w2[0], rw_a0, rw_a2[0], rw_g2[0], rw_k_k, rw_k_a, rw_r_k[0],
                 rw_ln_w, rw_ln_b)
    res["y_rw"] = y_rw
    x1, h2p, top_idx, top_w = _out(y_da, y_rw, x, mod3, w_out[0].astype(BF16), post_mix_norm,
                                   pre_ffn_norm, router_w[0], router_b[0])
    res.update(x1=x1, top_idx=top_idx, top_w=top_w)
    n = b * t
    n_tiles = n * TOP_K // EXPERT_ROWS + N_EXPERTS
    pos, plan = _route(top_idx.reshape(n, TOP_K), EXPERT_ROWS, n_tiles)
    xs = _dispatch(pos, h2p.reshape(n * ROW_SLAB, LANES), n_tiles * EXPERT_ROWS)
    xs = xs.reshape(n_tiles * EXPERT_ROWS * ROW_SLAB, LANES)
    b1 = jnp.broadcast_to(moe_b1[0][:, None, :], (N_EXPERTS, 8, moe_b1.shape[-1]))
    ys = _experts(plan, xs, moe_w1[0], b1, moe_w2[0], moe_b2[0][:, None, :])
    res["final"] = _combine(pos, ys, top_w, x1, mod3, post_ffn_norm)
    return res


stages = _stages


def kernel(x, c, positions, ada_w, ada_b, pre_mix_norm, post_mix_norm, pre_ffn_norm, post_ffn_norm, w_in, w_out, da_lambda_q1, da_lambda_k1, da_lambda_q2, da_lambda_k2, da_subln, rw_mu, rw_w0, rw_w2, rw_a0, rw_a2, rw_g2, rw_k_k, rw_k_a, rw_r_k, rw_ln_w, rw_ln_b, router_w, router_b, moe_w1, moe_b1, moe_w2, moe_b2):
    res = _stages(x, c, positions, ada_w, ada_b, pre_mix_norm, post_mix_norm, pre_ffn_norm,
                  post_ffn_norm, w_in, w_out, da_lambda_q1, da_lambda_k1, da_lambda_q2,
                  da_lambda_k2, da_subln, rw_mu, rw_w0, rw_w2, rw_a0, rw_a2, rw_g2, rw_k_k,
                  rw_k_a, rw_r_k, rw_ln_w, rw_ln_b, router_w, router_b, moe_w1, moe_b1,
                  moe_w2, moe_b2)
    return res["final"]
```

```python
import functools
import math

import jax
import jax.numpy as jnp
from jax import lax
from jax.experimental import pallas as pl
from jax.experimental.pallas import tpu as pltpu

F32 = jnp.float32
BF16 = jnp.bfloat16

DA_HEADS = 4
DA_HEAD_DIM = 64
DA_V_DIM = 128
DA_WIDTH = 512
RW_HEADS = 8
RW_HEAD_DIM = 64
RW_WIDTH = 512
DECAY_LORA = 64
AAA_LORA = 64
GATE_LORA = 128
DA_COLS = 1536
RW_COLS = 1792
ROPE_THETA = 500000.0
ROPE_DIM = 16
N_EXPERTS = 32
TOP_K = 4
SWIGLU_ALPHA = 1.702
SWIGLU_LIMIT = 7.0
NORM_EPS = 1e-6
SUBLN_EPS = 1e-5
LN_X_EPS = 64e-5
N_MOD = 6

LANES = 128
SUBLANES = 8
VMEM_LIMIT_BYTES = 56 * 1024 * 1024

PROJ_ROWS = 512
ATTN_BLOCK = 512
ATTN_KV_BLOCK = 512
ATTN_HEAD_GROUP = 2
RW_CHUNK = 128
RW_BLOCK = 256
OUT_ROWS = 512
EXPERT_ROWS = 512
DISPATCH_TOKENS = 2048
COMBINE_TOKENS = 1024
COMBINE_SUB_ROWS = 256


def _params(*sem):
    return pltpu.CompilerParams(dimension_semantics=sem, vmem_limit_bytes=VMEM_LIMIT_BYTES)


def _bdot(a, b):
    return jnp.dot(a.astype(BF16), b.astype(BF16), preferred_element_type=F32)


def _bdot_nt(a, b):
    return lax.dot_general(a.astype(BF16), b.astype(BF16), (((1,), (1,)), ((), ())),
                           preferred_element_type=F32)


def _bdot_tn(a, b):
    return lax.dot_general(a.astype(BF16), b.astype(BF16), (((0,), (0,)), ((), ())),
                           preferred_element_type=F32)


def _rms(x, w, eps):
    return x * lax.rsqrt(jnp.mean(x * x, axis=-1, keepdims=True) + eps) * w


def _mod_kernel(c_ref, w_ref, b_ref, o_ref):
    c = c_ref[...]
    s = c * jax.nn.sigmoid(c)
    o_ref[...] = _bdot(s, w_ref[...]) + b_ref[...]


def _mod(c, ada_w, ada_b):
    b, d = c.shape
    n = ada_w.shape[1]
    return pl.pallas_call(
        _mod_kernel,
        grid=(n // d,),
        in_specs=[pl.BlockSpec((b, d), lambda j: (0, 0)),
                  pl.BlockSpec((d, d), lambda j: (0, j)),
                  pl.BlockSpec((1, d), lambda j: (0, j))],
        out_specs=pl.BlockSpec((b, d), lambda j: (0, j)),
        out_shape=jax.ShapeDtypeStruct((b, n), F32),
        compiler_params=_params("parallel"),
        name="mod",
    )(c, ada_w, ada_b.reshape(1, n))


def _proj_kernel(x_ref, pos_ref, mod_ref, nw_ref, invf_ref, w_ref, mu_ref,
                 q_ref, k_ref, v_ref, rw_ref, carry_ref):
    ti = pl.program_id(1)

    @pl.when(ti == 0)
    def _():
        carry_ref[...] = jnp.zeros_like(carry_ref)

    x = x_ref[...]
    h = _rms(x, nw_ref[...], NORM_EPS) * (1.0 + mod_ref[1:2, :]) + mod_ref[0:1, :]
    hb = h.astype(BF16)

    ang = pos_ref[...].astype(F32) * invf_ref[...]
    cos, sin = jnp.cos(ang), jnp.sin(ang)
    l64 = lax.broadcasted_iota(jnp.int32, ang.shape, 1) % DA_HEAD_DIM
    half = ROPE_DIM // 2
    c_tab = jnp.where(l64 < ROPE_DIM, cos, 1.0)
    s_lo = jnp.where(l64 < half, -sin, 0.0)
    s_hi = jnp.where((l64 >= half) & (l64 < ROPE_DIM), sin, 0.0)

    def rope(z):
        up = pltpu.roll(z, LANES - half, axis=1)
        dn = pltpu.roll(z, half, axis=1)
        return z * c_tab + up * s_lo + dn * s_hi

    for g in range(DA_WIDTH // LANES):
        sl = slice(g * LANES, (g + 1) * LANES)
        qg = jnp.dot(hb, w_ref[:, sl], preferred_element_type=F32)
        q_ref[:, sl] = (rope(qg) * (DA_HEAD_DIM ** -0.5)).astype(q_ref.dtype)
        kg = jnp.dot(hb, w_ref[:, DA_WIDTH + g * LANES:DA_WIDTH + (g + 1) * LANES],
                     preferred_element_type=F32)
        k_ref[:, sl] = rope(kg).astype(k_ref.dtype)
    v_ref[...] = jnp.dot(hb, w_ref[:, 2 * DA_WIDTH:DA_COLS],
                         preferred_element_type=F32).astype(v_ref.dtype)

    p = jnp.dot(hb, w_ref[:, DA_COLS:], preferred_element_type=F32)
    rows = p.shape[0]
    prev = pltpu.roll(p, 1, axis=0)
    first = lax.broadcasted_iota(jnp.int32, p.shape, 0) == 0
    prev = jnp.where(first, carry_ref[0:1, :], prev)
    rw_ref[...] = p + (prev - p) * mu_ref[...]
    carry_ref[0:1, :] = p[rows - 1:rows, :]


def _proj(x, pos3, mod3, norm_w, invf, w_in_b, mu):
    b, t, d = x.shape
    tm = min(PROJ_ROWS, t)
    n_in = w_in_b.shape[1]
    blk = lambda w: pl.BlockSpec((None, tm, w), lambda bi, ti: (bi, ti, 0))
    full = lambda r, c: pl.BlockSpec((r, c), lambda bi, ti: (0, 0))
    return pl.pallas_call(
        _proj_kernel,
        grid=(b, t // tm),
        in_specs=[blk(d), blk(1),
                  pl.BlockSpec((None, N_MOD, d), lambda bi, ti: (bi, 0, 0)),
                  full(1, d), full(1, LANES), full(d, n_in), full(1, RW_COLS)],
        out_specs=[blk(DA_WIDTH), blk(DA_WIDTH), blk(DA_WIDTH), blk(RW_COLS)],
        out_shape=[jax.ShapeDtypeStruct((b, t, DA_WIDTH), BF16)] * 3
        + [jax.ShapeDtypeStruct((b, t, RW_COLS), F32)],
        scratch_shapes=[pltpu.VMEM((8, RW_COLS), F32)],
        compiler_params=_params("parallel", "arbitrary"),
        name="proj",
    )(x, pos3, mod3, norm_w, invf, w_in_b, mu)


def _attn_kernel(q_ref, k_ref, v_ref, lam_ref, subln_ref, o_ref, m_ref, l_ref, acc_ref,
                 *, lambda_init):
    qi = pl.program_id(2)
    tq = q_ref.shape[0]
    heads = range(ATTN_HEAD_GROUP)
    hs = [slice(h * DA_V_DIM, (h + 1) * DA_V_DIM) for h in heads]
    lane = lax.broadcasted_iota(jnp.int32, (tq, DA_V_DIM), 1)
    qq = []
    for c in hs:
        q = q_ref[:, c]
        zero = jnp.zeros_like(q)
        qq.append(jnp.concatenate([jnp.where(lane < DA_HEAD_DIM, q, zero),
                                   jnp.where(lane >= DA_HEAD_DIM, q, zero)], axis=0))

    m_ref[...] = jnp.full(m_ref.shape, -jnp.inf, F32)
    l_ref[...] = jnp.zeros(l_ref.shape, F32)
    acc_ref[...] = jnp.zeros(acc_ref.shape, F32)
    tk = ATTN_KV_BLOCK if k_ref.shape[0] % ATTN_KV_BLOCK == 0 else tq
    rep = tk // LANES

    def step(j, masked):
        rows = pl.ds(pl.multiple_of(j * tk, tk), tk)
        s = [lax.dot_general(qq[h], k_ref[rows, hs[h]], (((1,), (1,)), ((), ())),
                             preferred_element_type=F32) for h in heads]
        if masked:
            qpos = qi * tq + lax.broadcasted_iota(jnp.int32, s[0].shape, 0) % tq
            kpos = j * tk + lax.broadcasted_iota(jnp.int32, s[0].shape, 1)
            s = [jnp.where(qpos >= kpos, x, -jnp.inf) for x in s]
        for h in heads:
            m_old = m_ref[h]
            m_new = jnp.maximum(m_old, jnp.max(s[h], axis=-1, keepdims=True))
            alpha = jnp.exp(m_old - m_new)
            p = jnp.exp(s[h] - jnp.concatenate([m_new] * rep, axis=1))
            l_ref[h] = alpha * l_ref[h] + jnp.sum(p, axis=-1, keepdims=True)
            acc_ref[h] = alpha * acc_ref[h] + jnp.dot(p.astype(v_ref.dtype), v_ref[rows, hs[h]],
                                                      preferred_element_type=F32)
            m_ref[h] = m_new

    def body(j, carry):
        step(j, False)
        return carry

    n_full = (qi * tq) // tk
    lax.fori_loop(0, n_full, body, 0)
    step(n_full, True)

    lam = (jnp.exp(jnp.sum(lam_ref[0:1, :] * lam_ref[1:2, :], axis=-1, keepdims=True))
           - jnp.exp(jnp.sum(lam_ref[2:3, :] * lam_ref[3:4, :], axis=-1, keepdims=True))
           + lambda_init)
    for h in heads:
        o = acc_ref[h] / l_ref[h]
        d = o[:tq, :] - lam * o[tq:, :]
        o_ref[:, hs[h]] = (_rms(d, subln_ref[...], SUBLN_EPS)
                           * (1.0 - lambda_init)).astype(o_ref.dtype)


def _attn(q, k, v, lam4, subln, lambda_init):
    b, t, _ = q.shape
    tq = min(ATTN_BLOCK, t)
    hg = ATTN_HEAD_GROUP
    gw = hg * DA_V_DIM
    return pl.pallas_call(
        functools.partial(_attn_kernel, lambda_init=lambda_init),
        grid=(b, DA_HEADS // hg, t // tq),
        in_specs=[pl.BlockSpec((None, tq, gw), lambda bi, h, qi: (bi, qi, h)),
                  pl.BlockSpec((None, t, gw), lambda bi, h, qi: (bi, 0, h)),
                  pl.BlockSpec((None, t, gw), lambda bi, h, qi: (bi, 0, h)),
                  pl.BlockSpec((4, DA_HEAD_DIM), lambda bi, h, qi: (0, 0)),
                  pl.BlockSpec((1, DA_V_DIM), lambda bi, h, qi: (0, 0))],
        out_specs=pl.BlockSpec((None, tq, gw), lambda bi, h, qi: (bi, qi, h)),
        out_shape=jax.ShapeDtypeStruct((b, t, DA_WIDTH), BF16),
        scratch_shapes=[pltpu.VMEM((hg, 2 * tq, LANES), F32), pltpu.VMEM((hg, 2 * tq, LANES), F32),
                        pltpu.VMEM((hg, 2 * tq, DA_V_DIM), F32)],
        compiler_params=_params("parallel", "parallel", "arbitrary"),
        name="attn",
    )(q, k, v, lam4, subln)


def _rwkv_kernel(rw_ref, w0_ref, w2_ref, a0_ref, a2_ref, g2_ref, kk_ref, ka_ref, rk_ref,
                 lnw_ref, lnb_ref, o_ref, state_ref, r_s, k_s, v_s, lw_s, kk_s, a_s, g_s, cum_s):
    ti = pl.program_id(1)

    @pl.when(ti == 0)
    def _():
        state_ref[...] = jnp.zeros_like(state_ref)

    w = RW_WIDTH
    rw = rw_ref[...]
    k = rw[:, w:2 * w]
    wl = rw[:, 3 * w:3 * w + DECAY_LORA]
    al = rw[:, 3 * w + DECAY_LORA:3 * w + DECAY_LORA + AAA_LORA]
    gl = rw[:, 3 * w + DECAY_LORA + AAA_LORA:]
    z = -(w0_ref[...] + _bdot(jnp.tanh(wl), w2_ref[...]))
    softplus = jnp.maximum(z, 0.0) + jnp.log(1.0 + jnp.exp(-jnp.abs(z)))
    a = jax.nn.sigmoid(a0_ref[...] + _bdot(al, a2_ref[...]))
    r_s[...] = rw[:, 0:w]
    v_s[...] = rw[:, 2 * w:3 * w]
    lw_s[...] = -jnp.exp(-softplus - 0.5)
    a_s[...] = a
    g_s[...] = _bdot(jax.nn.sigmoid(gl), g2_ref[...])
    kk_s[...] = k * kk_ref[...]
    k_s[...] = k * (1.0 + (a - 1.0) * ka_ref[...])

    c_len = RW_CHUNK
    n = RW_HEAD_DIM
    tb = rw_ref.shape[0]

    br = lax.broadcasted_iota(jnp.int32, (tb, tb), 0)
    bc = lax.broadcasted_iota(jnp.int32, (tb, tb), 1)
    tri = jnp.where((br >= bc) & (br // c_len == bc // c_len), 1.0, 0.0).astype(BF16)
    lw_all = lw_s[...]
    lw_hi = lw_all.astype(BF16)
    rem = lw_all - lw_hi.astype(F32)
    lw_mid = rem.astype(BF16)
    lw_lo = (rem - lw_mid.astype(F32)).astype(BF16)
    cum_s[...] = (jnp.dot(tri, lw_hi, preferred_element_type=F32)
                  + jnp.dot(tri, lw_mid, preferred_element_type=F32)
                  + jnp.dot(tri, lw_lo, preferred_element_type=F32))

    row = lax.broadcasted_iota(jnp.int32, (c_len, 2 * c_len), 0)
    col = lax.broadcasted_iota(jnp.int32, (c_len, 2 * c_len), 1)
    incl2 = row >= col % c_len
    strict2 = row > col % c_len
    eye = jnp.where(lax.broadcasted_iota(jnp.int32, (c_len, c_len), 0)
                    == lax.broadcasted_iota(jnp.int32, (c_len, c_len), 1), 1.0, 0.0).astype(F32)

    def chunk(ci, carry):
        rows = pl.ds(pl.multiple_of(ci * c_len, c_len), c_len)
        heads = range(RW_HEADS)
        sl = [slice(h * n, (h + 1) * n) for h in heads]
        r = [r_s[rows, c] for c in sl]
        kh = [k_s[rows, c] for c in sl]
        v = [v_s[rows, c] for c in sl]
        lw = [lw_s[rows, c] for c in sl]
        cum = [cum_s[rows, c] for c in sl]
        kk = [kk_s[rows, c] for c in sl]
        kk = [x * lax.rsqrt(jnp.maximum(jnp.sum(x * x, axis=-1, keepdims=True), 1e-24)) for x in kk]
        kka = [kk[h] * a_s[rows, sl[h]] for h in heads]
        end = [jnp.sum(x, axis=0, keepdims=True) for x in lw]
        e_neg = [jnp.exp(-x) for x in cum]
        e_end = [jnp.exp(end[h] - cum[h]) for h in heads]
        left = [jnp.concatenate([-kk[h] * jnp.exp(cum[h] - lw[h]), r[h] * jnp.exp(cum[h])], axis=0)
                for h in heads]
        g = [_bdot_nt(left[h], jnp.concatenate([kka[h] * e_neg[h], kh[h] * e_neg[h]], axis=0))
             for h in heads]
        a_a = [jnp.where(strict2, x[:c_len, :], 0.0) for x in g]
        a_r = [jnp.where(incl2, x[c_len:, :], 0.0) for x in g]
        pw = [x[:, :c_len] for x in a_a]
        inv = [eye + x for x in pw]
        for _ in range(c_len.bit_length() - 2):
            pw = [_bdot(x, x) for x in pw]
            inv = [inv[h] + _bdot(inv[h], pw[h]) for h in heads]
        akv = [_bdot(a_a[h][:, c_len:], v[h]) for h in heads]
        s0 = [state_ref[h] for h in heads]
        ls = [_bdot_nt(left[h], s0[h]) for h in heads]
        u = [_bdot(inv[h], ls[h][:c_len, :] + akv[h]) for h in heads]
        uv = [jnp.concatenate([u[h], v[h]], axis=0) for h in heads]
        y = [ls[h][c_len:, :] + _bdot(a_r[h], uv[h]) for h in heads]
        for h in heads:
            state_ref[h] = s0[h] * jnp.exp(end[h]) + _bdot_tn(
                uv[h], jnp.concatenate([kka[h] * e_end[h], kh[h] * e_end[h]], axis=0))
        for h in heads:
            mean = jnp.mean(y[h], axis=-1, keepdims=True)
            yc = y[h] - mean
            var = jnp.mean(yc * yc, axis=-1, keepdims=True)
            yn = yc * lax.rsqrt(var + LN_X_EPS) * lnw_ref[:, sl[h]] + lnb_ref[:, sl[h]]
            bonus = jnp.sum(r[h] * kh[h] * rk_ref[:, sl[h]], axis=-1, keepdims=True) * v[h]
            o_ref[rows, sl[h]] = ((yn + bonus) * g_s[rows, sl[h]]).astype(o_ref.dtype)
        return carry

    lax.fori_loop(0, rw_ref.shape[0] // c_len, chunk, 0, unroll=2)


def _rwkv(rw, w0, w2, a0, a2, g2, k_k, k_a, r_k, ln_w, ln_b):
    b, t, _ = rw.shape
    tb = min(RW_BLOCK, t)
    w = RW_WIDTH
    vec = pl.BlockSpec((1, w), lambda bi, ti: (0, 0))
    mat = lambda r: pl.BlockSpec((r, w), lambda bi, ti: (0, 0))
    return pl.pallas_call(
        _rwkv_kernel,
        grid=(b, t // tb),
        in_specs=[pl.BlockSpec((None, tb, RW_COLS), lambda bi, ti: (bi, ti, 0)),
                  vec, mat(DECAY_LORA), vec, mat(AAA_LORA), mat(GATE_LORA), vec, vec, vec, vec, vec],
        out_specs=pl.BlockSpec((None, tb, w), lambda bi, ti: (bi, ti, 0)),
        out_shape=jax.ShapeDtypeStruct((b, t, w), BF16),
        scratch_shapes=[pltpu.VMEM((RW_HEADS, RW_HEAD_DIM, RW_HEAD_DIM), F32)]
        + [pltpu.VMEM((tb, w), F32)] * 8,
        compiler_params=_params("parallel", "arbitrary"),
        name="rwkv",
    )(rw, w0, w2, a0, a2, g2, k_k, k_a, r_k.reshape(1, w), ln_w, ln_b)


def _pack_rows(x):
    half = x.shape[1] // 2
    hi = pltpu.bitcast(x[:, :half].astype(BF16).astype(F32), jnp.uint32)
    lo = pltpu.bitcast(x[:, half:].astype(BF16).astype(F32), jnp.uint32)
    return hi | (lo >> 16)


def _unpack_rows(u):
    hi = pltpu.bitcast(u & jnp.uint32(0xFFFF0000), F32)
    lo = pltpu.bitcast(u << 16, F32)
    return jnp.concatenate([hi, lo], axis=1)


ROW_SLAB = 4


def _store_rows(ref, u, r0=0):
    n = u.shape[0]
    for c in range(ROW_SLAB):
        ref[pl.ds(r0 * ROW_SLAB + c, n, stride=ROW_SLAB), :] = u[:, c * LANES:(c + 1) * LANES]


def _load_rows(ref, n, r0=0):
    return jnp.concatenate([ref[pl.ds(r0 * ROW_SLAB + c, n, stride=ROW_SLAB), :]
                            for c in range(ROW_SLAB)], axis=1)


def _out_kernel(yda_ref, yrw_ref, x_ref, mod_ref, wo_ref, pmn_ref, pfn_ref, rw_ref, rb_ref,
                x1_ref, h2_ref, idx_ref, wgt_ref):
    y = (jnp.dot(yda_ref[...], wo_ref[0:DA_WIDTH, :], preferred_element_type=F32)
         + jnp.dot(yrw_ref[...], wo_ref[DA_WIDTH:, :], preferred_element_type=F32))
    x1 = x_ref[...] + mod_ref[2:3, :] * _rms(y, pmn_ref[...], NORM_EPS)
    x1_ref[...] = x1
    h2 = _rms(x1, pfn_ref[...], NORM_EPS) * (1.0 + mod_ref[4:5, :]) + mod_ref[3:4, :]
    _store_rows(h2_ref, _pack_rows(h2))

    h_hi = h2.astype(BF16)
    h_lo = (h2 - h_hi.astype(F32)).astype(BF16)
    rw = rw_ref[...]
    w_hi = rw.astype(BF16)
    w_lo = (rw - w_hi.astype(F32)).astype(BF16)
    logits = (jnp.dot(h_hi, w_hi, preferred_element_type=F32)
              + jnp.dot(h_hi, w_lo, preferred_element_type=F32)
              + jnp.dot(h_lo, w_hi, preferred_element_type=F32)) + rb_ref[...]

    lane = lax.broadcasted_iota(jnp.int32, logits.shape, 1)
    slot = lax.broadcasted_iota(jnp.int32, idx_ref.shape, 1)
    idx = jnp.zeros(idx_ref.shape, jnp.int32)
    val = jnp.zeros(idx_ref.shape, F32)
    top = None
    for j in range(TOP_K):
        m = jnp.max(logits, axis=-1, keepdims=True)
        i = jnp.min(jnp.where(logits == m, lane, N_EXPERTS), axis=-1, keepdims=True)
        top = m if top is None else top
        idx = jnp.where(slot == j, i, idx)
        val = jnp.where(slot == j, jnp.exp(m - top), val)
        logits = jnp.where(lane == i, -jnp.inf, logits)
    idx_ref[...] = idx
    wgt_ref[...] = val / jnp.sum(val, axis=-1, keepdims=True)


def _out(y_da, y_rw, x, mod3, w_out_b, post_mix_norm, pre_ffn_norm, router_w, router_b):
    b, t, d = x.shape
    tm = min(OUT_ROWS, t)
    e = router_w.shape[1]
    blk = lambda w: pl.BlockSpec((None, tm, w), lambda bi, ti: (bi, ti, 0))
    full = lambda r, c: pl.BlockSpec((r, c), lambda bi, ti: (0, 0))
    return pl.pallas_call(
        _out_kernel,
        grid=(b, t // tm),
        in_specs=[blk(DA_WIDTH), blk(RW_WIDTH), blk(d),
                  pl.BlockSpec((None, N_MOD, d), lambda bi, ti: (bi, 0, 0)),
                  full(d, d), full(1, d), full(1, d), full(d, e), full(1, e)],
        out_specs=[blk(d),
                   pl.BlockSpec((None, tm * ROW_SLAB, LANES), lambda bi, ti: (bi, ti, 0)),
                   blk(TOP_K), blk(TOP_K)],
        out_shape=[jax.ShapeDtypeStruct((b, t, d), F32),
                   jax.ShapeDtypeStruct((b, t * ROW_SLAB, LANES), jnp.uint32),
                   jax.ShapeDtypeStruct((b, t, TOP_K), jnp.int32),
                   jax.ShapeDtypeStruct((b, t, TOP_K), F32)],
        compiler_params=_params("parallel", "parallel"),
        name="out",
    )(y_da, y_rw, x, mod3, w_out_b, post_mix_norm, pre_ffn_norm, router_w,
      router_b.reshape(1, e))


def _route(top_idx, rows_per_tile, n_tiles):
    experts = jnp.arange(N_EXPERTS, dtype=jnp.int32)
    chosen = (top_idx[:, :, None] == experts[None, None, :]).astype(jnp.int32)
    member = jnp.sum(chosen, axis=1)
    csum = jnp.cumsum(member, axis=0)
    counts = csum[-1]
    padded = (counts + rows_per_tile - 1) // rows_per_tile * rows_per_tile
    ends = jnp.cumsum(padded)
    starts = ends - padded
    pos = jnp.sum((csum - member + starts[None, :])[:, None, :] * chosen, axis=2).reshape(-1)
    n_active = ends[-1] // rows_per_tile
    tile_start = jnp.arange(n_tiles, dtype=jnp.int32) * rows_per_tile
    tile = jnp.minimum(tile_start, ends[-1] - 1)
    tile_expert = jnp.sum((tile[:, None] >= ends[None, :]).astype(jnp.int32), axis=1)
    used = padded > 0
    later = lax.cummin(jnp.where(used, experts, N_EXPERTS), reverse=True)
    following = jnp.concatenate([later[1:], jnp.full((1,), N_EXPERTS, jnp.int32)])
    following = jnp.where(following == N_EXPERTS, -1, following)
    run_index = jnp.cumsum(used.astype(jnp.int32)) - 1
    plan = (tile_expert.astype(jnp.int32), n_active.reshape(1).astype(jnp.int32),
            following[tile_expert].astype(jnp.int32), (run_index[tile_expert] % 2).astype(jnp.int32))
    pads = ((starts + counts).astype(jnp.int32), (padded - counts).astype(jnp.int32))
    return pos.astype(jnp.int32), pads, plan


def _dispatch_kernel(ps_ref, pl_ref, pos_ref, h_ref, xs_ref, zero_ref, sem, pad_sem):
    tb = h_ref.shape[0] // ROW_SLAB

    def issue(t, carry):
        src = h_ref.at[pl.ds(pl.multiple_of(t * ROW_SLAB, ROW_SLAB), ROW_SLAB)]
        for j in range(TOP_K):
            pltpu.make_async_copy(src, xs_ref.at[pos_ref[0, t * TOP_K + j]],
                                  sem).start(priority=j % 2)
        return carry

    lax.fori_loop(0, tb, issue, 0, unroll=8)

    @pl.when(pl.program_id(0) == pl.num_programs(0) - 1)
    def _():
        zero_ref[...] = jnp.zeros_like(zero_ref)
        for e in range(N_EXPERTS):
            def pad_copy(r, e=e):
                return pltpu.make_async_copy(zero_ref, xs_ref.at[ps_ref[e] + r], pad_sem)

            def start(r, carry, pad_copy=pad_copy):
                pad_copy(r).start()
                return carry

            def drain(r, carry, pad_copy=pad_copy):
                pad_copy(r).wait()
                return carry

            lax.fori_loop(0, pl_ref[e], start, 0)
            lax.fori_loop(0, pl_ref[e], drain, 0)

    for j in range(TOP_K):
        pltpu.make_async_copy(xs_ref.at[pl.ds(0, tb)], xs_ref.at[pl.ds(0, tb)], sem).wait()


def _dispatch(pos, pads, h2p, n_rows):
    pad_start, pad_len = pads
    n = h2p.shape[0] // ROW_SLAB
    w = h2p.shape[1]
    tb = min(DISPATCH_TOKENS, n)
    pos3 = pos.reshape(n // tb, 1, tb * TOP_K)
    return pl.pallas_call(
        _dispatch_kernel,
        grid_spec=pltpu.PrefetchScalarGridSpec(
            num_scalar_prefetch=2,
            grid=(n // tb,),
            in_specs=[pl.BlockSpec((None, 1, tb * TOP_K), lambda i, ps, pn: (i, 0, 0),
                                   memory_space=pltpu.SMEM),
                      pl.BlockSpec((tb * ROW_SLAB, w), lambda i, ps, pn: (i, 0))],
            out_specs=pl.BlockSpec(memory_space=pl.ANY),
            scratch_shapes=[pltpu.VMEM((ROW_SLAB, w), h2p.dtype),
                            pltpu.SemaphoreType.DMA(()), pltpu.SemaphoreType.DMA(())]),
        out_shape=jax.ShapeDtypeStruct((n_rows, ROW_SLAB, w), h2p.dtype),
        compiler_params=_params("arbitrary"),
        name="dispatch",
    )(pad_start, pad_len, pos3, h2p)


def _expert_kernel(te_ref, na_ref, nx_ref, sl_ref, xs_ref, w1_hbm, b1_ref, w2_hbm, b2_ref, ys_ref,
                   w1p_s, b1p_s, w2b_s, act_s, w1_buf, w2_buf, sem):
    i = pl.program_id(0)
    active = i < na_ref[0]
    expert = te_ref[i]
    fresh = jnp.logical_or(i == 0, expert != te_ref[jnp.maximum(i - 1, 0)])
    slot = sl_ref[i]
    grp = 2 * LANES
    n_grp = w1_buf.shape[2] // grp

    def weight_copies(ex, s):
        return (pltpu.make_async_copy(w1_hbm.at[ex], w1_buf.at[s], sem.at[0, s]),
                pltpu.make_async_copy(w2_hbm.at[ex], w2_buf.at[s], sem.at[1, s]))

    @pl.when(jnp.logical_and(active, i == 0))
    def _():
        for cp in weight_copies(expert, slot):
            cp.start()

    @pl.when(jnp.logical_and(active, fresh))
    def _():
        for cp in weight_copies(expert, slot):
            cp.wait()

        @pl.when(nx_ref[i] >= 0)
        def _():
            for cp in weight_copies(nx_ref[i], 1 - slot):
                cp.start()

        src = lax.broadcasted_iota(jnp.int32, (grp, grp), 0)
        dst = lax.broadcasted_iota(jnp.int32, (grp, grp), 1)
        perm = jnp.where(src == jnp.where(dst < LANES, 2 * dst, 2 * (dst - LANES) + 1),
                         1.0, 0.0).astype(BF16)
        for g in range(n_grp):
            sl = slice(g * grp, (g + 1) * grp)
            w1p_s[:, sl] = jnp.dot(w1_buf[slot, :, sl].astype(BF16), perm,
                                   preferred_element_type=F32).astype(BF16)
            b = b1_ref[:, sl]
            b_hi = b.astype(BF16)
            b_lo = (b - b_hi.astype(F32)).astype(BF16)
            b1p_s[:, sl] = (jnp.dot(b_hi, perm, preferred_element_type=F32)
                            + jnp.dot(b_lo, perm, preferred_element_type=F32))
        w2b_s[...] = w2_buf[slot].astype(BF16)

    @pl.when(active)
    def _():
        x = _unpack_rows(_load_rows(xs_ref, act_s.shape[0])).astype(BF16)
        hid = jnp.dot(x, w1p_s[...], preferred_element_type=F32) + b1p_s[0:1, :]
        for g in range(n_grp):
            glu = jnp.minimum(hid[:, g * grp:g * grp + LANES], SWIGLU_LIMIT)
            lin = jnp.clip(hid[:, g * grp + LANES:(g + 1) * grp], -SWIGLU_LIMIT, SWIGLU_LIMIT)
            act_s[:, g * LANES:(g + 1) * LANES] = (
                glu * jax.nn.sigmoid(SWIGLU_ALPHA * glu) * (lin + 1.0)).astype(BF16)
        y = jnp.dot(act_s[...], w2b_s[...], preferred_element_type=F32) + b2_ref[...]
        _store_rows(ys_ref, _pack_rows(y))

    @pl.when(jnp.logical_not(active))
    def _():
        ys_ref[...] = jnp.zeros_like(ys_ref)


def _experts(plan, xs, w1, b1, w2, b2):
    tile_expert, n_active, next_expert, tile_slot = plan
    n_rows, w = xs.shape[0] // ROW_SLAB, xs.shape[1]
    tm = EXPERT_ROWS
    d, f2 = w1.shape[1], w1.shape[2]
    f = f2 // 2
    wspec = lambda r, c: pl.BlockSpec((None, r, c), lambda i, te, na, nx, sl: (te[i], 0, 0))
    rows = pl.BlockSpec((tm * ROW_SLAB, w), lambda i, te, na, nx, sl: (i, 0))
    rows_in = pl.BlockSpec((tm * ROW_SLAB, w),
                           lambda i, te, na, nx, sl: (jnp.minimum(i, na[0] - 1), 0))
    hbm = pl.BlockSpec(memory_space=pl.ANY)
    return pl.pallas_call(
        _expert_kernel,
        grid_spec=pltpu.PrefetchScalarGridSpec(
            num_scalar_prefetch=4,
            grid=(n_rows // tm,),
            in_specs=[rows_in, hbm, wspec(8, f2), hbm, wspec(1, d)],
            out_specs=rows,
            scratch_shapes=[pltpu.VMEM((d, f2), BF16), pltpu.VMEM((8, f2), F32),
                            pltpu.VMEM((f, d), BF16), pltpu.VMEM((tm, f), BF16),
                            pltpu.VMEM((2, d, f2), F32), pltpu.VMEM((2, f, d), F32),
                            pltpu.SemaphoreType.DMA((2, 2))]),
        out_shape=jax.ShapeDtypeStruct((n_rows * ROW_SLAB, w), jnp.uint32),
        compiler_params=_params("arbitrary"),
        name="expert",
    )(tile_expert, n_active, next_expert, tile_slot, xs, w1, b1, w2, b2)


def _combine_kernel(pos_ref, nxt_ref, ys_ref, wgt_ref, x1_ref, mod_ref, nw_ref, o_ref, buf_ref,
                    sem):
    tc = x1_ref.shape[0]
    step = pl.program_id(0) * pl.num_programs(1) + pl.program_id(1)
    last = pl.num_programs(0) * pl.num_programs(1) - 1
    slot = step % 2

    def gather(p_ref, s):
        def issue(t, carry):
            dst = pl.ds(pl.multiple_of(t * ROW_SLAB, ROW_SLAB), ROW_SLAB)
            for j in range(TOP_K):
                pltpu.make_async_copy(ys_ref.at[p_ref[0, t * TOP_K + j]], buf_ref.at[s, j, dst],
                                      sem.at[s]).start(priority=j % 2)
            return carry

        lax.fori_loop(0, tc, issue, 0, unroll=8)

    @pl.when(step == 0)
    def _():
        gather(pos_ref, slot)

    @pl.when(step < last)
    def _():
        gather(nxt_ref, 1 - slot)

    for j in range(TOP_K):
        pltpu.make_async_copy(ys_ref.at[pl.ds(0, tc)], ys_ref.at[pl.ds(0, tc)], sem.at[slot]).wait()

    sub = min(COMBINE_SUB_ROWS, tc)

    def mix(i, carry):
        r0 = pl.multiple_of(i * sub, sub)
        wgt = wgt_ref[pl.ds(r0, sub), :]
        acc = jnp.zeros((sub, x1_ref.shape[1]), F32)
        for j in range(TOP_K):
            acc = acc + wgt[:, j:j + 1] * _unpack_rows(_load_rows(buf_ref.at[slot, j], sub, r0))
        o_ref[pl.ds(r0, sub), :] = (x1_ref[pl.ds(r0, sub), :]
                                    + mod_ref[5:6, :] * _rms(acc, nw_ref[...], NORM_EPS))
        return carry

    lax.fori_loop(0, tc // sub, mix, 0)


def _combine(pos, ys, wgt, x1, mod3, post_ffn_norm):
    b, t, d = x1.shape
    tc = min(COMBINE_TOKENS, t)
    nt = t // tc
    pos3 = pos.reshape(b * nt, 1, tc * TOP_K)
    blk = lambda w: pl.BlockSpec((None, tc, w), lambda bi, ti: (bi, ti, 0))
    return pl.pallas_call(
        _combine_kernel,
        grid=(b, nt),
        in_specs=[pl.BlockSpec((None, 1, tc * TOP_K), lambda bi, ti: (bi * nt + ti, 0, 0),
                               memory_space=pltpu.SMEM),
                  pl.BlockSpec((None, 1, tc * TOP_K),
                               lambda bi, ti: (jnp.minimum(bi * nt + ti + 1, b * nt - 1), 0, 0),
                               memory_space=pltpu.SMEM),
                  pl.BlockSpec(memory_space=pl.ANY),
                  blk(TOP_K), blk(d),
                  pl.BlockSpec((None, N_MOD, d), lambda bi, ti: (bi, 0, 0)),
                  pl.BlockSpec((1, d), lambda bi, ti: (0, 0))],
        out_specs=blk(d),
        out_shape=jax.ShapeDtypeStruct((b, t, d), F32),
        scratch_shapes=[pltpu.VMEM((2, TOP_K, tc * ROW_SLAB, LANES), jnp.uint32),
                        pltpu.SemaphoreType.DMA((2,))],
        compiler_params=_params("arbitrary", "arbitrary"),
        name="combine",
    )(pos3, pos3, ys.reshape(ys.shape[0] // ROW_SLAB, ROW_SLAB, LANES), wgt, x1, mod3,
      post_ffn_norm)


def _stages(x, c, positions, ada_w, ada_b, pre_mix_norm, post_mix_norm, pre_ffn_norm,
            post_ffn_norm, w_in, w_out, da_lambda_q1, da_lambda_k1, da_lambda_q2, da_lambda_k2,
            da_subln, rw_mu, rw_w0, rw_w2, rw_a0, rw_a2, rw_g2, rw_k_k, rw_k_a, rw_r_k, rw_ln_w,
            rw_ln_b, router_w, router_b, moe_w1, moe_b1, moe_w2, moe_b2):
    b, t, d = x.shape
    res = {}
    lambda_init = 0.8 - 0.6 * math.exp(-0.3 * 0)
    mod = _mod(c, ada_w[0], ada_b[0])
    res["mod"] = mod
    mod3 = mod.reshape(b, N_MOD, d)
    inv_freq = ROPE_THETA ** (-jnp.arange(0, ROPE_DIM, 2, dtype=F32) / ROPE_DIM)
    invf = jnp.tile(inv_freq, LANES // (ROPE_DIM // 2)).reshape(1, LANES)
    q, k, v, rw = _proj(x, positions.reshape(b, t, 1), mod3, pre_mix_norm, invf,
                        w_in[0].astype(BF16), rw_mu)
    res.update(q=q, k=k, v=v, rw=rw)
    lam4 = jnp.concatenate([da_lambda_q1, da_lambda_k1, da_lambda_q2, da_lambda_k2], axis=0)
    y_da = _attn(q, k, v, lam4, da_subln, lambda_init)
    res["y_da"] = y_da
    y_rw = _rwkv(rw, rw_w0, rw_w2[0], rw_a0, rw_a2[0], rw_g2[0], rw_k_k, rw_k_a, rw_r_k[0],
                 rw_ln_w, rw_ln_b)
    res["y_rw"] = y_rw
    x1, h2p, top_idx, top_w = _out(y_da, y_rw, x, mod3, w_out[0].astype(BF16), post_mix_norm,
                                   pre_ffn_norm, router_w[0], router_b[0])
    res.update(x1=x1, top_idx=top_idx, top_w=top_w)
    n = b * t
    n_tiles = n * TOP_K // EXPERT_ROWS + N_EXPERTS
    pos, pads, plan = _route(top_idx.reshape(n, TOP_K), EXPERT_ROWS, n_tiles)
    xs = _dispatch(pos, pads, h2p.reshape(n * ROW_SLAB, LANES), n_tiles * EXPERT_ROWS)
    xs = xs.reshape(n_tiles * EXPERT_ROWS * ROW_SLAB, LANES)
    b1 = jnp.broadcast_to(moe_b1[0][:, None, :], (N_EXPERTS, 8, moe_b1.shape[-1]))
    ys = _experts(plan, xs, moe_w1[0], b1, moe_w2[0], moe_b2[0][:, None, :])
    res["final"] = _combine(pos, ys, top_w, x1, mod3, post_ffn_norm)
    return res


stages = _stages


def kernel(x, c, positions, ada_w, ada_b, pre_mix_norm, post_mix_norm, pre_ffn_norm, post_ffn_norm, w_in, w_out, da_lambda_q1, da_lambda_k1, da_lambda_q2, da_lambda_k2, da_subln, rw_mu, rw_w0, rw_w2, rw_a0, rw_a2, rw_g2, rw_k_k, rw_k_a, rw_r_k, rw_ln_w, rw_ln_b, router_w, router_b, moe_w1, moe_b1, moe_w2, moe_b2):
    res = _stages(x, c, positions, ada_w, ada_b, pre_mix_norm, post_mix_norm, pre_ffn_norm,
                  post_ffn_norm, w_in, w_out, da_lambda_q1, da_lambda_k1, da_lambda_q2,
                  da_lambda_k2, da_subln, rw_mu, rw_w0, rw_w2, rw_a0, rw_a2, rw_g2, rw_k_k,
                  rw_k_a, rw_r_k, rw_ln_w, rw_ln_b, router_w, router_b, moe_w1, moe_b1,
                  moe_w2, moe_b2)
    return res["final"]
```

```python
import functools
import math

import jax
import jax.numpy as jnp
from jax import lax
from jax.experimental import pallas as pl
from jax.experimental.pallas import tpu as pltpu

F32 = jnp.float32
BF16 = jnp.bfloat16

DA_HEADS = 4
DA_HEAD_DIM = 64
DA_V_DIM = 128
DA_WIDTH = 512
RW_HEADS = 8
RW_HEAD_DIM = 64
RW_WIDTH = 512
DECAY_LORA = 64
AAA_LORA = 64
GATE_LORA = 128
DA_COLS = 1536
RW_COLS = 1792
ROPE_THETA = 500000.0
ROPE_DIM = 16
N_EXPERTS = 32
TOP_K = 4
SWIGLU_ALPHA = 1.702
SWIGLU_LIMIT = 7.0
NORM_EPS = 1e-6
SUBLN_EPS = 1e-5
LN_X_EPS = 64e-5
N_MOD = 6

LANES = 128
SUBLANES = 8
VMEM_LIMIT_BYTES = 56 * 1024 * 1024

PROJ_ROWS = 512
ATTN_BLOCK = 512
ATTN_KV_BLOCK = 512
ATTN_HEAD_GROUP = 2
RW_CHUNK = 128
RW_BLOCK = 256
OUT_ROWS = 512
EXPERT_ROWS = 512
DISPATCH_TOKENS = 2048
COMBINE_TOKENS = 1024
COMBINE_SUB_ROWS = 256


def _params(*sem):
    return pltpu.CompilerParams(dimension_semantics=sem, vmem_limit_bytes=VMEM_LIMIT_BYTES)


def _bdot(a, b):
    return jnp.dot(a.astype(BF16), b.astype(BF16), preferred_element_type=F32)


def _bdot_nt(a, b):
    return lax.dot_general(a.astype(BF16), b.astype(BF16), (((1,), (1,)), ((), ())),
                           preferred_element_type=F32)


def _bdot_tn(a, b):
    return lax.dot_general(a.astype(BF16), b.astype(BF16), (((0,), (0,)), ((), ())),
                           preferred_element_type=F32)


def _rms(x, w, eps):
    return x * lax.rsqrt(jnp.mean(x * x, axis=-1, keepdims=True) + eps) * w


def _mod_kernel(c_ref, w_ref, b_ref, o_ref):
    c = c_ref[...]
    s = c * jax.nn.sigmoid(c)
    o_ref[...] = _bdot(s, w_ref[...]) + b_ref[...]


def _mod(c, ada_w, ada_b):
    b, d = c.shape
    n = ada_w.shape[1]
    return pl.pallas_call(
        _mod_kernel,
        grid=(n // d,),
        in_specs=[pl.BlockSpec((b, d), lambda j: (0, 0)),
                  pl.BlockSpec((d, d), lambda j: (0, j)),
                  pl.BlockSpec((1, d), lambda j: (0, j))],
        out_specs=pl.BlockSpec((b, d), lambda j: (0, j)),
        out_shape=jax.ShapeDtypeStruct((b, n), F32),
        compiler_params=_params("parallel"),
        name="mod",
    )(c, ada_w, ada_b.reshape(1, n))


def _proj_kernel(x_ref, pos_ref, mod_ref, nw_ref, invf_ref, w_ref, mu_ref,
                 q_ref, k_ref, v_ref, rw_ref, carry_ref):
    ti = pl.program_id(1)

    @pl.when(ti == 0)
    def _():
        carry_ref[...] = jnp.zeros_like(carry_ref)

    x = x_ref[...]
    h = _rms(x, nw_ref[...], NORM_EPS) * (1.0 + mod_ref[1:2, :]) + mod_ref[0:1, :]
    hb = h.astype(BF16)

    ang = pos_ref[...].astype(F32) * invf_ref[...]
    cos, sin = jnp.cos(ang), jnp.sin(ang)
    l64 = lax.broadcasted_iota(jnp.int32, ang.shape, 1) % DA_HEAD_DIM
    half = ROPE_DIM // 2
    c_tab = jnp.where(l64 < ROPE_DIM, cos, 1.0)
    s_lo = jnp.where(l64 < half, -sin, 0.0)
    s_hi = jnp.where((l64 >= half) & (l64 < ROPE_DIM), sin, 0.0)

    def rope(z):
        up = pltpu.roll(z, LANES - half, axis=1)
        dn = pltpu.roll(z, half, axis=1)
        return z * c_tab + up * s_lo + dn * s_hi

    for g in range(DA_WIDTH // LANES):
        sl = slice(g * LANES, (g + 1) * LANES)
        qg = jnp.dot(hb, w_ref[:, sl], preferred_element_type=F32)
        q_ref[:, sl] = (rope(qg) * (DA_HEAD_DIM ** -0.5)).astype(q_ref.dtype)
        kg = jnp.dot(hb, w_ref[:, DA_WIDTH + g * LANES:DA_WIDTH + (g + 1) * LANES],
                     preferred_element_type=F32)
        k_ref[:, sl] = rope(kg).astype(k_ref.dtype)
    v_ref[...] = jnp.dot(hb, w_ref[:, 2 * DA_WIDTH:DA_COLS],
                         preferred_element_type=F32).astype(v_ref.dtype)

    p = jnp.dot(hb, w_ref[:, DA_COLS:], preferred_element_type=F32)
    rows = p.shape[0]
    prev = pltpu.roll(p, 1, axis=0)
    first = lax.broadcasted_iota(jnp.int32, p.shape, 0) == 0
    prev = jnp.where(first, carry_ref[0:1, :], prev)
    rw_ref[...] = p + (prev - p) * mu_ref[...]
    carry_ref[0:1, :] = p[rows - 1:rows, :]


def _proj(x, pos3, mod3, norm_w, invf, w_in_b, mu):
    b, t, d = x.shape
    tm = min(PROJ_ROWS, t)
    n_in = w_in_b.shape[1]
    blk = lambda w: pl.BlockSpec((None, tm, w), lambda bi, ti: (bi, ti, 0))
    full = lambda r, c: pl.BlockSpec((r, c), lambda bi, ti: (0, 0))
    return pl.pallas_call(
        _proj_kernel,
        grid=(b, t // tm),
        in_specs=[blk(d), blk(1),
                  pl.BlockSpec((None, N_MOD, d), lambda bi, ti: (bi, 0, 0)),
                  full(1, d), full(1, LANES), full(d, n_in), full(1, RW_COLS)],
        out_specs=[blk(DA_WIDTH), blk(DA_WIDTH), blk(DA_WIDTH), blk(RW_COLS)],
        out_shape=[jax.ShapeDtypeStruct((b, t, DA_WIDTH), BF16)] * 3
        + [jax.ShapeDtypeStruct((b, t, RW_COLS), F32)],
        scratch_shapes=[pltpu.VMEM((8, RW_COLS), F32)],
        compiler_params=_params("parallel", "arbitrary"),
        name="proj",
    )(x, pos3, mod3, norm_w, invf, w_in_b, mu)


def _attn_kernel(q_ref, k_ref, v_ref, lam_ref, subln_ref, o_ref, m_ref, l_ref, acc_ref,
                 *, lambda_init):
    qi = pl.program_id(2)
    tq = q_ref.shape[0]
    heads = range(ATTN_HEAD_GROUP)
    hs = [slice(h * DA_V_DIM, (h + 1) * DA_V_DIM) for h in heads]
    lane = lax.broadcasted_iota(jnp.int32, (tq, DA_V_DIM), 1)
    qq = []
    for c in hs:
        q = q_ref[:, c]
        zero = jnp.zeros_like(q)
        qq.append(jnp.concatenate([jnp.where(lane < DA_HEAD_DIM, q, zero),
                                   jnp.where(lane >= DA_HEAD_DIM, q, zero)], axis=0))

    m_ref[...] = jnp.full(m_ref.shape, -jnp.inf, F32)
    l_ref[...] = jnp.zeros(l_ref.shape, F32)
    acc_ref[...] = jnp.zeros(acc_ref.shape, F32)
    tk = ATTN_KV_BLOCK if k_ref.shape[0] % ATTN_KV_BLOCK == 0 else tq
    rep = tk // LANES

    def step(j, masked):
        rows = pl.ds(pl.multiple_of(j * tk, tk), tk)
        s = [lax.dot_general(qq[h], k_ref[rows, hs[h]], (((1,), (1,)), ((), ())),
                             preferred_element_type=F32) for h in heads]
        if masked:
            qpos = qi * tq + lax.broadcasted_iota(jnp.int32, s[0].shape, 0) % tq
            kpos = j * tk + lax.broadcasted_iota(jnp.int32, s[0].shape, 1)
            s = [jnp.where(qpos >= kpos, x, -jnp.inf) for x in s]
        for h in heads:
            m_old = m_ref[h]
            m_new = jnp.maximum(m_old, jnp.max(s[h], axis=-1, keepdims=True))
            alpha = jnp.exp(m_old - m_new)
            p = jnp.exp(s[h] - jnp.concatenate([m_new] * rep, axis=1))
            l_ref[h] = alpha * l_ref[h] + jnp.sum(p, axis=-1, keepdims=True)
            acc_ref[h] = alpha * acc_ref[h] + jnp.dot(p.astype(v_ref.dtype), v_ref[rows, hs[h]],
                                                      preferred_element_type=F32)
            m_ref[h] = m_new

    def body(j, carry):
        step(j, False)
        return carry

    n_full = (qi * tq) // tk
    lax.fori_loop(0, n_full, body, 0)
    step(n_full, True)

    lam = (jnp.exp(jnp.sum(lam_ref[0:1, :] * lam_ref[1:2, :], axis=-1, keepdims=True))
           - jnp.exp(jnp.sum(lam_ref[2:3, :] * lam_ref[3:4, :], axis=-1, keepdims=True))
           + lambda_init)
    for h in heads:
        o = acc_ref[h] / l_ref[h]
        d = o[:tq, :] - lam * o[tq:, :]
        o_ref[:, hs[h]] = (_rms(d, subln_ref[...], SUBLN_EPS)
                           * (1.0 - lambda_init)).astype(o_ref.dtype)


def _attn(q, k, v, lam4, subln, lambda_init):
    b, t, _ = q.shape
    tq = min(ATTN_BLOCK, t)
    hg = ATTN_HEAD_GROUP
    gw = hg * DA_V_DIM
    return pl.pallas_call(
        functools.partial(_attn_kernel, lambda_init=lambda_init),
        grid=(b, DA_HEADS // hg, t // tq),
        in_specs=[pl.BlockSpec((None, tq, gw), lambda bi, h, qi: (bi, qi, h)),
                  pl.BlockSpec((None, t, gw), lambda bi, h, qi: (bi, 0, h)),
                  pl.BlockSpec((None, t, gw), lambda bi, h, qi: (bi, 0, h)),
                  pl.BlockSpec((4, DA_HEAD_DIM), lambda bi, h, qi: (0, 0)),
                  pl.BlockSpec((1, DA_V_DIM), lambda bi, h, qi: (0, 0))],
        out_specs=pl.BlockSpec((None, tq, gw), lambda bi, h, qi: (bi, qi, h)),
        out_shape=jax.ShapeDtypeStruct((b, t, DA_WIDTH), BF16),
        scratch_shapes=[pltpu.VMEM((hg, 2 * tq, LANES), F32), pltpu.VMEM((hg, 2 * tq, LANES), F32),
                        pltpu.VMEM((hg, 2 * tq, DA_V_DIM), F32)],
        compiler_params=_params("parallel", "parallel", "arbitrary"),
        name="attn",
    )(q, k, v, lam4, subln)


def _rwkv_kernel(rw_ref, w0_ref, w2_ref, a0_ref, a2_ref, g2_ref, kk_ref, ka_ref, rk_ref,
                 lnw_ref, lnb_ref, o_ref, state_ref, r_s, k_s, v_s, lw_s, kk_s, a_s, g_s, cum_s):
    ti = pl.program_id(1)

    @pl.when(ti == 0)
    def _():
        state_ref[...] = jnp.zeros_like(state_ref)

    w = RW_WIDTH
    rw = rw_ref[...]
    k = rw[:, w:2 * w]
    wl = rw[:, 3 * w:3 * w + DECAY_LORA]
    al = rw[:, 3 * w + DECAY_LORA:3 * w + DECAY_LORA + AAA_LORA]
    gl = rw[:, 3 * w + DECAY_LORA + AAA_LORA:]
    z = -(w0_ref[...] + _bdot(jnp.tanh(wl), w2_ref[...]))
    softplus = jnp.maximum(z, 0.0) + jnp.log(1.0 + jnp.exp(-jnp.abs(z)))
    a = jax.nn.sigmoid(a0_ref[...] + _bdot(al, a2_ref[...]))
    r_s[...] = rw[:, 0:w]
    v_s[...] = rw[:, 2 * w:3 * w]
    lw_s[...] = -jnp.exp(-softplus - 0.5)
    a_s[...] = a
    g_s[...] = _bdot(jax.nn.sigmoid(gl), g2_ref[...])
    kk_s[...] = k * kk_ref[...]
    k_s[...] = k * (1.0 + (a - 1.0) * ka_ref[...])

    c_len = RW_CHUNK
    n = RW_HEAD_DIM
    tb = rw_ref.shape[0]

    br = lax.broadcasted_iota(jnp.int32, (tb, tb), 0)
    bc = lax.broadcasted_iota(jnp.int32, (tb, tb), 1)
    tri = jnp.where((br >= bc) & (br // c_len == bc // c_len), 1.0, 0.0).astype(BF16)
    lw_all = lw_s[...]
    lw_hi = lw_all.astype(BF16)
    rem = lw_all - lw_hi.astype(F32)
    lw_mid = rem.astype(BF16)
    lw_lo = (rem - lw_mid.astype(F32)).astype(BF16)
    cum_s[...] = (jnp.dot(tri, lw_hi, preferred_element_type=F32)
                  + jnp.dot(tri, lw_mid, preferred_element_type=F32)
                  + jnp.dot(tri, lw_lo, preferred_element_type=F32))

    row = lax.broadcasted_iota(jnp.int32, (c_len, 2 * c_len), 0)
    col = lax.broadcasted_iota(jnp.int32, (c_len, 2 * c_len), 1)
    incl2 = row >= col % c_len
    strict2 = row > col % c_len
    eye = jnp.where(lax.broadcasted_iota(jnp.int32, (c_len, c_len), 0)
                    == lax.broadcasted_iota(jnp.int32, (c_len, c_len), 1), 1.0, 0.0).astype(F32)

    def chunk(ci, carry):
        rows = pl.ds(pl.multiple_of(ci * c_len, c_len), c_len)
        heads = range(RW_HEADS)
        sl = [slice(h * n, (h + 1) * n) for h in heads]
        r = [r_s[rows, c] for c in sl]
        kh = [k_s[rows, c] for c in sl]
        v = [v_s[rows, c] for c in sl]
        lw = [lw_s[rows, c] for c in sl]
        cum = [cum_s[rows, c] for c in sl]
        kk = [kk_s[rows, c] for c in sl]
        kk = [x * lax.rsqrt(jnp.maximum(jnp.sum(x * x, axis=-1, keepdims=True), 1e-24)) for x in kk]
        kka = [kk[h] * a_s[rows, sl[h]] for h in heads]
        end = [jnp.sum(x, axis=0, keepdims=True) for x in lw]
        e_neg = [jnp.exp(-x) for x in cum]
        e_end = [jnp.exp(end[h] - cum[h]) for h in heads]
        left = [jnp.concatenate([-kk[h] * jnp.exp(cum[h] - lw[h]), r[h] * jnp.exp(cum[h])], axis=0)
                for h in heads]
        g = [_bdot_nt(left[h], jnp.concatenate([kka[h] * e_neg[h], kh[h] * e_neg[h]], axis=0))
             for h in heads]
        a_a = [jnp.where(strict2, x[:c_len, :], 0.0) for x in g]
        a_r = [jnp.where(incl2, x[c_len:, :], 0.0) for x in g]
        pw = [x[:, :c_len] for x in a_a]
        inv = [eye + x for x in pw]
        for _ in range(c_len.bit_length() - 2):
            pw = [_bdot(x, x) for x in pw]
            inv = [inv[h] + _bdot(inv[h], pw[h]) for h in heads]
        akv = [_bdot(a_a[h][:, c_len:], v[h]) for h in heads]
        s0 = [state_ref[h] for h in heads]
        ls = [_bdot_nt(left[h], s0[h]) for h in heads]
        u = [_bdot(inv[h], ls[h][:c_len, :] + akv[h]) for h in heads]
        uv = [jnp.concatenate([u[h], v[h]], axis=0) for h in heads]
        y = [ls[h][c_len:, :] + _bdot(a_r[h], uv[h]) for h in heads]
        for h in heads:
            state_ref[h] = s0[h] * jnp.exp(end[h]) + _bdot_tn(
                uv[h], jnp.concatenate([kka[h] * e_end[h], kh[h] * e_end[h]], axis=0))
        for h in heads:
            mean = jnp.mean(y[h], axis=-1, keepdims=True)
            yc = y[h] - mean
            var = jnp.mean(yc * yc, axis=-1, keepdims=True)
            yn = yc * lax.rsqrt(var + LN_X_EPS) * lnw_ref[:, sl[h]] + lnb_ref[:, sl[h]]
            bonus = jnp.sum(r[h] * kh[h] * rk_ref[:, sl[h]], axis=-1, keepdims=True) * v[h]
            o_ref[rows, sl[h]] = ((yn + bonus) * g_s[rows, sl[h]]).astype(o_ref.dtype)
        return carry

    lax.fori_loop(0, rw_ref.shape[0] // c_len, chunk, 0, unroll=2)


def _rwkv(rw, w0, w2, a0, a2, g2, k_k, k_a, r_k, ln_w, ln_b):
    b, t, _ = rw.shape
    tb = min(RW_BLOCK, t)
    w = RW_WIDTH
    vec = pl.BlockSpec((1, w), lambda bi, ti: (0, 0))
    mat = lambda r: pl.BlockSpec((r, w), lambda bi, ti: (0, 0))
    return pl.pallas_call(
        _rwkv_kernel,
        grid=(b, t // tb),
        in_specs=[pl.BlockSpec((None, tb, RW_COLS), lambda bi, ti: (bi, ti, 0)),
                  vec, mat(DECAY_LORA), vec, mat(AAA_LORA), mat(GATE_LORA), vec, vec, vec, vec, vec],
        out_specs=pl.BlockSpec((None, tb, w), lambda bi, ti: (bi, ti, 0)),
        out_shape=jax.ShapeDtypeStruct((b, t, w), BF16),
        scratch_shapes=[pltpu.VMEM((RW_HEADS, RW_HEAD_DIM, RW_HEAD_DIM), F32)]
        + [pltpu.VMEM((tb, w), F32)] * 8,
        compiler_params=_params("parallel", "arbitrary"),
        name="rwkv",
    )(rw, w0, w2, a0, a2, g2, k_k, k_a, r_k.reshape(1, w), ln_w, ln_b)


def _pack_rows(x):
    half = x.shape[1] // 2
    hi = pltpu.bitcast(x[:, :half].astype(BF16).astype(F32), jnp.uint32)
    lo = pltpu.bitcast(x[:, half:].astype(BF16).astype(F32), jnp.uint32)
    return hi | (lo >> 16)


def _unpack_rows(u):
    hi = pltpu.bitcast(u & jnp.uint32(0xFFFF0000), F32)
    lo = pltpu.bitcast(u << 16, F32)
    return jnp.concatenate([hi, lo], axis=1)


ROW_SLAB = 4


def _store_rows(ref, u, r0=0):
    n = u.shape[0]
    for c in range(ROW_SLAB):
        ref[pl.ds(r0 * ROW_SLAB + c, n, stride=ROW_SLAB), :] = u[:, c * LANES:(c + 1) * LANES]


def _load_rows(ref, n, r0=0):
    return jnp.concatenate([ref[pl.ds(r0 * ROW_SLAB + c, n, stride=ROW_SLAB), :]
                            for c in range(ROW_SLAB)], axis=1)


def _out_kernel(yda_ref, yrw_ref, x_ref, mod_ref, wo_ref, pmn_ref, pfn_ref, rw_ref, rb_ref,
                x1_ref, h2_ref, idx_ref, wgt_ref):
    y = (jnp.dot(yda_ref[...], wo_ref[0:DA_WIDTH, :], preferred_element_type=F32)
         + jnp.dot(yrw_ref[...], wo_ref[DA_WIDTH:, :], preferred_element_type=F32))
    x1 = x_ref[...] + mod_ref[2:3, :] * _rms(y, pmn_ref[...], NORM_EPS)
    x1_ref[...] = x1
    h2 = _rms(x1, pfn_ref[...], NORM_EPS) * (1.0 + mod_ref[4:5, :]) + mod_ref[3:4, :]
    _store_rows(h2_ref, _pack_rows(h2))

    h_hi = h2.astype(BF16)
    h_lo = (h2 - h_hi.astype(F32)).astype(BF16)
    rw = rw_ref[...]
    w_hi = rw.astype(BF16)
    w_lo = (rw - w_hi.astype(F32)).astype(BF16)
    logits = (jnp.dot(h_hi, w_hi, preferred_element_type=F32)
              + jnp.dot(h_hi, w_lo, preferred_element_type=F32)
              + jnp.dot(h_lo, w_hi, preferred_element_type=F32)) + rb_ref[...]

    lane = lax.broadcasted_iota(jnp.int32, logits.shape, 1)
    slot = lax.broadcasted_iota(jnp.int32, idx_ref.shape, 1)
    idx = jnp.zeros(idx_ref.shape, jnp.int32)
    val = jnp.zeros(idx_ref.shape, F32)
    top = None
    for j in range(TOP_K):
        m = jnp.max(logits, axis=-1, keepdims=True)
        i = jnp.min(jnp.where(logits == m, lane, N_EXPERTS), axis=-1, keepdims=True)
        top = m if top is None else top
        idx = jnp.where(slot == j, i, idx)
        val = jnp.where(slot == j, jnp.exp(m - top), val)
        logits = jnp.where(lane == i, -jnp.inf, logits)
    idx_ref[...] = idx
    wgt_ref[...] = val / jnp.sum(val, axis=-1, keepdims=True)


def _out(y_da, y_rw, x, mod3, w_out_b, post_mix_norm, pre_ffn_norm, router_w, router_b):
    b, t, d = x.shape
    tm = min(OUT_ROWS, t)
    e = router_w.shape[1]
    blk = lambda w: pl.BlockSpec((None, tm, w), lambda bi, ti: (bi, ti, 0))
    full = lambda r, c: pl.BlockSpec((r, c), lambda bi, ti: (0, 0))
    return pl.pallas_call(
        _out_kernel,
        grid=(b, t // tm),
        in_specs=[blk(DA_WIDTH), blk(RW_WIDTH), blk(d),
                  pl.BlockSpec((None, N_MOD, d), lambda bi, ti: (bi, 0, 0)),
                  full(d, d), full(1, d), full(1, d), full(d, e), full(1, e)],
        out_specs=[blk(d),
                   pl.BlockSpec((None, tm * ROW_SLAB, LANES), lambda bi, ti: (bi, ti, 0)),
                   blk(TOP_K), blk(TOP_K)],
        out_shape=[jax.ShapeDtypeStruct((b, t, d), F32),
                   jax.ShapeDtypeStruct((b, t * ROW_SLAB, LANES), jnp.uint32),
                   jax.ShapeDtypeStruct((b, t, TOP_K), jnp.int32),
                   jax.ShapeDtypeStruct((b, t, TOP_K), F32)],
        compiler_params=_params("parallel", "parallel"),
        name="out",
    )(y_da, y_rw, x, mod3, w_out_b, post_mix_norm, pre_ffn_norm, router_w,
      router_b.reshape(1, e))


def _route(top_idx, rows_per_tile, n_tiles):
    experts = jnp.arange(N_EXPERTS, dtype=jnp.int32)
    chosen = (top_idx[:, :, None] == experts[None, None, :]).astype(jnp.int32)
    member = jnp.sum(chosen, axis=1)
    csum = jnp.cumsum(member, axis=0)
    counts = csum[-1]
    padded = (counts + rows_per_tile - 1) // rows_per_tile * rows_per_tile
    ends = jnp.cumsum(padded)
    starts = ends - padded
    pos = jnp.sum((csum - member + starts[None, :])[:, None, :] * chosen, axis=2).reshape(-1)
    n_active = ends[-1] // rows_per_tile
    tile_start = jnp.arange(n_tiles, dtype=jnp.int32) * rows_per_tile
    tile = jnp.minimum(tile_start, ends[-1] - 1)
    tile_expert = jnp.sum((tile[:, None] >= ends[None, :]).astype(jnp.int32), axis=1)
    used = padded > 0
    later = lax.cummin(jnp.where(used, experts, N_EXPERTS), reverse=True)
    following = jnp.concatenate([later[1:], jnp.full((1,), N_EXPERTS, jnp.int32)])
    following = jnp.where(following == N_EXPERTS, -1, following)
    run_index = jnp.cumsum(used.astype(jnp.int32)) - 1
    plan = (tile_expert.astype(jnp.int32), n_active.reshape(1).astype(jnp.int32),
            following[tile_expert].astype(jnp.int32), (run_index[tile_expert] % 2).astype(jnp.int32))
    pads = ((starts + counts).astype(jnp.int32), (padded - counts).astype(jnp.int32))
    return pos.astype(jnp.int32), pads, plan


def _dispatch_kernel(ps_ref, pl_ref, pos_ref, h_ref, xs_ref, zero_ref, sem, pad_sem):
    tb = h_ref.shape[0] // ROW_SLAB

    def issue(t, carry):
        src = h_ref.at[pl.ds(pl.multiple_of(t * ROW_SLAB, ROW_SLAB), ROW_SLAB)]
        for j in range(TOP_K):
            pltpu.make_async_copy(src, xs_ref.at[pos_ref[0, t * TOP_K + j]],
                                  sem).start(priority=j % 2)
        return carry

    lax.fori_loop(0, tb, issue, 0, unroll=8)

    @pl.when(pl.program_id(0) == pl.num_programs(0) - 1)
    def _():
        zero_ref[...] = jnp.zeros_like(zero_ref)
        def pad_copy(e, r):
            return pltpu.make_async_copy(zero_ref, xs_ref.at[ps_ref[e] + r], pad_sem)

        for e in range(N_EXPERTS):
            def start(r, carry, e=e):
                pad_copy(e, r).start()
                return carry

            lax.fori_loop(0, pl_ref[e], start, 0)
        for e in range(N_EXPERTS):
            def drain(r, carry, e=e):
                pad_copy(e, r).wait()
                return carry

            lax.fori_loop(0, pl_ref[e], drain, 0)

    for j in range(TOP_K):
        pltpu.make_async_copy(xs_ref.at[pl.ds(0, tb)], xs_ref.at[pl.ds(0, tb)], sem).wait()


def _dispatch(pos, pads, h2p, n_rows):
    pad_start, pad_len = pads
    n = h2p.shape[0] // ROW_SLAB
    w = h2p.shape[1]
    tb = min(DISPATCH_TOKENS, n)
    pos3 = pos.reshape(n // tb, 1, tb * TOP_K)
    return pl.pallas_call(
        _dispatch_kernel,
        grid_spec=pltpu.PrefetchScalarGridSpec(
            num_scalar_prefetch=2,
            grid=(n // tb,),
            in_specs=[pl.BlockSpec((None, 1, tb * TOP_K), lambda i, ps, pn: (i, 0, 0),
                                   memory_space=pltpu.SMEM),
                      pl.BlockSpec((tb * ROW_SLAB, w), lambda i, ps, pn: (i, 0))],
            out_specs=pl.BlockSpec(memory_space=pl.ANY),
            scratch_shapes=[pltpu.VMEM((ROW_SLAB, w), h2p.dtype),
                            pltpu.SemaphoreType.DMA(()), pltpu.SemaphoreType.DMA(())]),
        out_shape=jax.ShapeDtypeStruct((n_rows, ROW_SLAB, w), h2p.dtype),
        compiler_params=_params("arbitrary"),
        name="dispatch",
    )(pad_start, pad_len, pos3, h2p)


def _expert_kernel(te_ref, na_ref, nx_ref, sl_ref, xs_ref, w1_hbm, b1_ref, w2_hbm, b2_ref, ys_ref,
                   w1p_s, b1p_s, w2b_s, act_s, w1_buf, w2_buf, sem):
    i = pl.program_id(0)
    active = i < na_ref[0]
    expert = te_ref[i]
    fresh = jnp.logical_or(i == 0, expert != te_ref[jnp.maximum(i - 1, 0)])
    slot = sl_ref[i]
    grp = 2 * LANES
    n_grp = w1_buf.shape[2] // grp

    def weight_copies(ex, s):
        return (pltpu.make_async_copy(w1_hbm.at[ex], w1_buf.at[s], sem.at[0, s]),
                pltpu.make_async_copy(w2_hbm.at[ex], w2_buf.at[s], sem.at[1, s]))

    @pl.when(jnp.logical_and(active, i == 0))
    def _():
        for cp in weight_copies(expert, slot):
            cp.start()

    @pl.when(jnp.logical_and(active, fresh))
    def _():
        for cp in weight_copies(expert, slot):
            cp.wait()

        @pl.when(nx_ref[i] >= 0)
        def _():
            for cp in weight_copies(nx_ref[i], 1 - slot):
                cp.start()

        src = lax.broadcasted_iota(jnp.int32, (grp, grp), 0)
        dst = lax.broadcasted_iota(jnp.int32, (grp, grp), 1)
        perm = jnp.where(src == jnp.where(dst < LANES, 2 * dst, 2 * (dst - LANES) + 1),
                         1.0, 0.0).astype(BF16)
        for g in range(n_grp):
            sl = slice(g * grp, (g + 1) * grp)
            w1p_s[:, sl] = jnp.dot(w1_buf[slot, :, sl].astype(BF16), perm,
                                   preferred_element_type=F32).astype(BF16)
            b = b1_ref[:, sl]
            b_hi = b.astype(BF16)
            b_lo = (b - b_hi.astype(F32)).astype(BF16)
            b1p_s[:, sl] = (jnp.dot(b_hi, perm, preferred_element_type=F32)
                            + jnp.dot(b_lo, perm, preferred_element_type=F32))
        w2b_s[...] = w2_buf[slot].astype(BF16)

    @pl.when(active)
    def _():
        x = _unpack_rows(_load_rows(xs_ref, act_s.shape[0])).astype(BF16)
        hid = jnp.dot(x, w1p_s[...], preferred_element_type=F32) + b1p_s[0:1, :]
        for g in range(n_grp):
            glu = jnp.minimum(hid[:, g * grp:g * grp + LANES], SWIGLU_LIMIT)
            lin = jnp.clip(hid[:, g * grp + LANES:(g + 1) * grp], -SWIGLU_LIMIT, SWIGLU_LIMIT)
            act_s[:, g * LANES:(g + 1) * LANES] = (
                glu * jax.nn.sigmoid(SWIGLU_ALPHA * glu) * (lin + 1.0)).astype(BF16)
        y = jnp.dot(act_s[...], w2b_s[...], preferred_element_type=F32) + b2_ref[...]
        _store_rows(ys_ref, _pack_rows(y))

    @pl.when(jnp.logical_not(active))
    def _():
        ys_ref[...] = jnp.zeros_like(ys_ref)


def _experts(plan, xs, w1, b1, w2, b2):
    tile_expert, n_active, next_expert, tile_slot = plan
    n_rows, w = xs.shape[0] // ROW_SLAB, xs.shape[1]
    tm = EXPERT_ROWS
    d, f2 = w1.shape[1], w1.shape[2]
    f = f2 // 2
    wspec = lambda r, c: pl.BlockSpec((None, r, c), lambda i, te, na, nx, sl: (te[i], 0, 0))
    rows = pl.BlockSpec((tm * ROW_SLAB, w), lambda i, te, na, nx, sl: (i, 0))
    rows_in = pl.BlockSpec((tm * ROW_SLAB, w),
                           lambda i, te, na, nx, sl: (jnp.minimum(i, na[0] - 1), 0))
    hbm = pl.BlockSpec(memory_space=pl.ANY)
    return pl.pallas_call(
        _expert_kernel,
        grid_spec=pltpu.PrefetchScalarGridSpec(
            num_scalar_prefetch=4,
            grid=(n_rows // tm,),
            in_specs=[rows_in, hbm, wspec(8, f2), hbm, wspec(1, d)],
            out_specs=rows,
            scratch_shapes=[pltpu.VMEM((d, f2), BF16), pltpu.VMEM((8, f2), F32),
                            pltpu.VMEM((f, d), BF16), pltpu.VMEM((tm, f), BF16),
                            pltpu.VMEM((2, d, f2), F32), pltpu.VMEM((2, f, d), F32),
                            pltpu.SemaphoreType.DMA((2, 2))]),
        out_shape=jax.ShapeDtypeStruct((n_rows * ROW_SLAB, w), jnp.uint32),
        compiler_params=_params("arbitrary"),
        name="expert",
    )(tile_expert, n_active, next_expert, tile_slot, xs, w1, b1, w2, b2)


def _combine_kernel(pos_ref, nxt_ref, ys_ref, wgt_ref, x1_ref, mod_ref, nw_ref, o_ref, buf_ref,
                    sem):
    tc = x1_ref.shape[0]
    step = pl.program_id(0) * pl.num_programs(1) + pl.program_id(1)
    last = pl.num_programs(0) * pl.num_programs(1) - 1
    slot = step % 2

    def gather(p_ref, s):
        def issue(t, carry):
            dst = pl.ds(pl.multiple_of(t * ROW_SLAB, ROW_SLAB), ROW_SLAB)
            for j in range(TOP_K):
                pltpu.make_async_copy(ys_ref.at[p_ref[0, t * TOP_K + j]], buf_ref.at[s, j, dst],
                                      sem.at[s]).start(priority=j % 2)
            return carry

        lax.fori_loop(0, tc, issue, 0, unroll=8)

    @pl.when(step == 0)
    def _():
        gather(pos_ref, slot)

    @pl.when(step < last)
    def _():
        gather(nxt_ref, 1 - slot)

    for j in range(TOP_K):
        pltpu.make_async_copy(ys_ref.at[pl.ds(0, tc)], ys_ref.at[pl.ds(0, tc)], sem.at[slot]).wait()

    sub = min(COMBINE_SUB_ROWS, tc)

    def mix(i, carry):
        r0 = pl.multiple_of(i * sub, sub)
        wgt = wgt_ref[pl.ds(r0, sub), :]
        acc = jnp.zeros((sub, x1_ref.shape[1]), F32)
        for j in range(TOP_K):
            acc = acc + wgt[:, j:j + 1] * _unpack_rows(_load_rows(buf_ref.at[slot, j], sub, r0))
        o_ref[pl.ds(r0, sub), :] = (x1_ref[pl.ds(r0, sub), :]
                                    + mod_ref[5:6, :] * _rms(acc, nw_ref[...], NORM_EPS))
        return carry

    lax.fori_loop(0, tc // sub, mix, 0)


def _combine(pos, ys, wgt, x1, mod3, post_ffn_norm):
    b, t, d = x1.shape
    tc = min(COMBINE_TOKENS, t)
    nt = t // tc
    pos3 = pos.reshape(b * nt, 1, tc * TOP_K)
    blk = lambda w: pl.BlockSpec((None, tc, w), lambda bi, ti: (bi, ti, 0))
    return pl.pallas_call(
        _combine_kernel,
        grid=(b, nt),
        in_specs=[pl.BlockSpec((None, 1, tc * TOP_K), lambda bi, ti: (bi * nt + ti, 0, 0),
                               memory_space=pltpu.SMEM),
                  pl.BlockSpec((None, 1, tc * TOP_K),
                               lambda bi, ti: (jnp.minimum(bi * nt + ti + 1, b * nt - 1), 0, 0),
                               memory_space=pltpu.SMEM),
                  pl.BlockSpec(memory_space=pl.ANY),
                  blk(TOP_K), blk(d),
                  pl.BlockSpec((None, N_MOD, d), lambda bi, ti: (bi, 0, 0)),
                  pl.BlockSpec((1, d), lambda bi, ti: (0, 0))],
        out_specs=blk(d),
        out_shape=jax.ShapeDtypeStruct((b, t, d), F32),
        scratch_shapes=[pltpu.VMEM((2, TOP_K, tc * ROW_SLAB, LANES), jnp.uint32),
                        pltpu.SemaphoreType.DMA((2,))],
        compiler_params=_params("arbitrary", "arbitrary"),
        name="combine",
    )(pos3, pos3, ys.reshape(ys.shape[0] // ROW_SLAB, ROW_SLAB, LANES), wgt, x1, mod3,
      post_ffn_norm)


def _stages(x, c, positions, ada_w, ada_b, pre_mix_norm, post_mix_norm, pre_ffn_norm,
            post_ffn_norm, w_in, w_out, da_lambda_q1, da_lambda_k1, da_lambda_q2, da_lambda_k2,
            da_subln, rw_mu, rw_w0, rw_w2, rw_a0, rw_a2, rw_g2, rw_k_k, rw_k_a, rw_r_k, rw_ln_w,
            rw_ln_b, router_w, router_b, moe_w1, moe_b1, moe_w2, moe_b2):
    b, t, d = x.shape
    res = {}
    lambda_init = 0.8 - 0.6 * math.exp(-0.3 * 0)
    mod = _mod(c, ada_w[0], ada_b[0])
    res["mod"] = mod
    mod3 = mod.reshape(b, N_MOD, d)
    inv_freq = ROPE_THETA ** (-jnp.arange(0, ROPE_DIM, 2, dtype=F32) / ROPE_DIM)
    invf = jnp.tile(inv_freq, LANES // (ROPE_DIM // 2)).reshape(1, LANES)
    q, k, v, rw = _proj(x, positions.reshape(b, t, 1), mod3, pre_mix_norm, invf,
                        w_in[0].astype(BF16), rw_mu)
    res.update(q=q, k=k, v=v, rw=rw)
    lam4 = jnp.concatenate([da_lambda_q1, da_lambda_k1, da_lambda_q2, da_lambda_k2], axis=0)
    y_da = _attn(q, k, v, lam4, da_subln, lambda_init)
    res["y_da"] = y_da
    y_rw = _rwkv(rw, rw_w0, rw_w2[0], rw_a0, rw_a2[0], rw_g2[0], rw_k_k, rw_k_a, rw_r_k[0],
                 rw_ln_w, rw_ln_b)
    res["y_rw"] = y_rw
    x1, h2p, top_idx, top_w = _out(y_da, y_rw, x, mod3, w_out[0].astype(BF16), post_mix_norm,
                                   pre_ffn_norm, router_w[0], router_b[0])
    res.update(x1=x1, top_idx=top_idx, top_w=top_w)
    n = b * t
    n_tiles = n * TOP_K // EXPERT_ROWS + N_EXPERTS
    pos, pads, plan = _route(top_idx.reshape(n, TOP_K), EXPERT_ROWS, n_tiles)
    xs = _dispatch(pos, pads, h2p.reshape(n * ROW_SLAB, LANES), n_tiles * EXPERT_ROWS)
    xs = xs.reshape(n_tiles * EXPERT_ROWS * ROW_SLAB, LANES)
    b1 = jnp.broadcast_to(moe_b1[0][:, None, :], (N_EXPERTS, 8, moe_b1.shape[-1]))
    ys = _experts(plan, xs, moe_w1[0], b1, moe_w2[0], moe_b2[0][:, None, :])
    res["final"] = _combine(pos, ys, top_w, x1, mod3, post_ffn_norm)
    return res


stages = _stages


def kernel(x, c, positions, ada_w, ada_b, pre_mix_norm, post_mix_norm, pre_ffn_norm, post_ffn_norm, w_in, w_out, da_lambda_q1, da_lambda_k1, da_lambda_q2, da_lambda_k2, da_subln, rw_mu, rw_w0, rw_w2, rw_a0, rw_a2, rw_g2, rw_k_k, rw_k_a, rw_r_k, rw_ln_w, rw_ln_b, router_w, router_b, moe_w1, moe_b1, moe_w2, moe_b2):
    res = _stages(x, c, positions, ada_w, ada_b, pre_mix_norm, post_mix_norm, pre_ffn_norm,
                  post_ffn_norm, w_in, w_out, da_lambda_q1, da_lambda_k1, da_lambda_q2,
                  da_lambda_k2, da_subln, rw_mu, rw_w0, rw_w2, rw_a0, rw_a2, rw_g2, rw_k_k,
                  rw_k_a, rw_r_k, rw_ln_w, rw_ln_b, router_w, router_b, moe_w1, moe_b1,
                  moe_w2, moe_b2)
    return res["final"]
```

```python
import functools
import math

import jax
import jax.numpy as jnp
from jax import lax
from jax.experimental import pallas as pl
from jax.experimental.pallas import tpu as pltpu

F32 = jnp.float32
BF16 = jnp.bfloat16

DA_HEADS = 4
DA_HEAD_DIM = 64
DA_V_DIM = 128
DA_WIDTH = 512
RW_HEADS = 8
RW_HEAD_DIM = 64
RW_WIDTH = 512
DECAY_LORA = 64
AAA_LORA = 64
GATE_LORA = 128
DA_COLS = 1536
RW_COLS = 1792
ROPE_THETA = 500000.0
ROPE_DIM = 16
N_EXPERTS = 32
TOP_K = 4
SWIGLU_ALPHA = 1.702
SWIGLU_LIMIT = 7.0
NORM_EPS = 1e-6
SUBLN_EPS = 1e-5
LN_X_EPS = 64e-5
N_MOD = 6

LANES = 128
SUBLANES = 8
VMEM_LIMIT_BYTES = 56 * 1024 * 1024

PROJ_ROWS = 512
ATTN_BLOCK = 512
ATTN_KV_BLOCK = 512
ATTN_HEAD_GROUP = 2
RW_CHUNK = 128
RW_BLOCK = 256
OUT_ROWS = 512
EXPERT_ROWS = 512
DISPATCH_TOKENS = 2048
COMBINE_TOKENS = 1024
COMBINE_SUB_ROWS = 256


def _params(*sem):
    return pltpu.CompilerParams(dimension_semantics=sem, vmem_limit_bytes=VMEM_LIMIT_BYTES)


def _bdot(a, b):
    return jnp.dot(a.astype(BF16), b.astype(BF16), preferred_element_type=F32)


def _bdot_nt(a, b):
    return lax.dot_general(a.astype(BF16), b.astype(BF16), (((1,), (1,)), ((), ())),
                           preferred_element_type=F32)


def _bdot_tn(a, b):
    return lax.dot_general(a.astype(BF16), b.astype(BF16), (((0,), (0,)), ((), ())),
                           preferred_element_type=F32)


def _rms(x, w, eps):
    return x * lax.rsqrt(jnp.mean(x * x, axis=-1, keepdims=True) + eps) * w


def _mod_kernel(c_ref, w_ref, b_ref, o_ref):
    c = c_ref[...]
    s = c * jax.nn.sigmoid(c)
    o_ref[...] = _bdot(s, w_ref[...]) + b_ref[...]


def _mod(c, ada_w, ada_b):
    b, d = c.shape
    n = ada_w.shape[1]
    return pl.pallas_call(
        _mod_kernel,
        grid=(n // d,),
        in_specs=[pl.BlockSpec((b, d), lambda j: (0, 0)),
                  pl.BlockSpec((d, d), lambda j: (0, j)),
                  pl.BlockSpec((1, d), lambda j: (0, j))],
        out_specs=pl.BlockSpec((b, d), lambda j: (0, j)),
        out_shape=jax.ShapeDtypeStruct((b, n), F32),
        compiler_params=_params("parallel"),
        name="mod",
    )(c, ada_w, ada_b.reshape(1, n))


def _proj_kernel(x_ref, pos_ref, mod_ref, nw_ref, invf_ref, w_ref, mu_ref,
                 q_ref, k_ref, v_ref, rw_ref, carry_ref):
    ti = pl.program_id(1)

    @pl.when(ti == 0)
    def _():
        carry_ref[...] = jnp.zeros_like(carry_ref)

    x = x_ref[...]
    h = _rms(x, nw_ref[...], NORM_EPS) * (1.0 + mod_ref[1:2, :]) + mod_ref[0:1, :]
    hb = h.astype(BF16)

    ang = pos_ref[...].astype(F32) * invf_ref[...]
    cos, sin = jnp.cos(ang), jnp.sin(ang)
    l64 = lax.broadcasted_iota(jnp.int32, ang.shape, 1) % DA_HEAD_DIM
    half = ROPE_DIM // 2
    c_tab = jnp.where(l64 < ROPE_DIM, cos, 1.0)
    s_lo = jnp.where(l64 < half, -sin, 0.0)
    s_hi = jnp.where((l64 >= half) & (l64 < ROPE_DIM), sin, 0.0)

    def rope(z):
        up = pltpu.roll(z, LANES - half, axis=1)
        dn = pltpu.roll(z, half, axis=1)
        return z * c_tab + up * s_lo + dn * s_hi

    for g in range(DA_WIDTH // LANES):
        sl = slice(g * LANES, (g + 1) * LANES)
        qg = jnp.dot(hb, w_ref[:, sl], preferred_element_type=F32)
        q_ref[:, sl] = (rope(qg) * (DA_HEAD_DIM ** -0.5)).astype(q_ref.dtype)
        kg = jnp.dot(hb, w_ref[:, DA_WIDTH + g * LANES:DA_WIDTH + (g + 1) * LANES],
                     preferred_element_type=F32)
        k_ref[:, sl] = rope(kg).astype(k_ref.dtype)
    v_ref[...] = jnp.dot(hb, w_ref[:, 2 * DA_WIDTH:DA_COLS],
                         preferred_element_type=F32).astype(v_ref.dtype)

    p = jnp.dot(hb, w_ref[:, DA_COLS:], preferred_element_type=F32)
    rows = p.shape[0]
    prev = pltpu.roll(p, 1, axis=0)
    first = lax.broadcasted_iota(jnp.int32, p.shape, 0) == 0
    prev = jnp.where(first, carry_ref[0:1, :], prev)
    rw_ref[...] = p + (prev - p) * mu_ref[...]
    carry_ref[0:1, :] = p[rows - 1:rows, :]


def _proj(x, pos3, mod3, norm_w, invf, w_in_b, mu):
    b, t, d = x.shape
    tm = min(PROJ_ROWS, t)
    n_in = w_in_b.shape[1]
    blk = lambda w: pl.BlockSpec((None, tm, w), lambda bi, ti: (bi, ti, 0))
    full = lambda r, c: pl.BlockSpec((r, c), lambda bi, ti: (0, 0))
    return pl.pallas_call(
        _proj_kernel,
        grid=(b, t // tm),
        in_specs=[blk(d), blk(1),
                  pl.BlockSpec((None, N_MOD, d), lambda bi, ti: (bi, 0, 0)),
                  full(1, d), full(1, LANES), full(d, n_in), full(1, RW_COLS)],
        out_specs=[blk(DA_WIDTH), blk(DA_WIDTH), blk(DA_WIDTH), blk(RW_COLS)],
        out_shape=[jax.ShapeDtypeStruct((b, t, DA_WIDTH), BF16)] * 3
        + [jax.ShapeDtypeStruct((b, t, RW_COLS), F32)],
        scratch_shapes=[pltpu.VMEM((8, RW_COLS), F32)],
        compiler_params=_params("parallel", "arbitrary"),
        name="proj",
    )(x, pos3, mod3, norm_w, invf, w_in_b, mu)


def _attn_kernel(q_ref, k_ref, v_ref, lam_ref, subln_ref, o_ref, m_ref, l_ref, acc_ref,
                 *, lambda_init):
    qi = pl.program_id(2)
    tq = q_ref.shape[0]
    heads = range(ATTN_HEAD_GROUP)
    hs = [slice(h * DA_V_DIM, (h + 1) * DA_V_DIM) for h in heads]
    lane = lax.broadcasted_iota(jnp.int32, (tq, DA_V_DIM), 1)
    qq = []
    for c in hs:
        q = q_ref[:, c]
        zero = jnp.zeros_like(q)
        qq.append(jnp.concatenate([jnp.where(lane < DA_HEAD_DIM, q, zero),
                                   jnp.where(lane >= DA_HEAD_DIM, q, zero)], axis=0))

    m_ref[...] = jnp.full(m_ref.shape, -jnp.inf, F32)
    l_ref[...] = jnp.zeros(l_ref.shape, F32)
    acc_ref[...] = jnp.zeros(acc_ref.shape, F32)
    tk = ATTN_KV_BLOCK if k_ref.shape[0] % ATTN_KV_BLOCK == 0 else tq
    rep = tk // LANES

    def step(j, masked):
        rows = pl.ds(pl.multiple_of(j * tk, tk), tk)
        s = [lax.dot_general(qq[h], k_ref[rows, hs[h]], (((1,), (1,)), ((), ())),
                             preferred_element_type=F32) for h in heads]
        if masked:
            qpos = qi * tq + lax.broadcasted_iota(jnp.int32, s[0].shape, 0) % tq
            kpos = j * tk + lax.broadcasted_iota(jnp.int32, s[0].shape, 1)
            s = [jnp.where(qpos >= kpos, x, -jnp.inf) for x in s]
        for h in heads:
            m_old = m_ref[h]
            m_new = jnp.maximum(m_old, jnp.max(s[h], axis=-1, keepdims=True))
            alpha = jnp.exp(m_old - m_new)
            p = jnp.exp(s[h] - jnp.concatenate([m_new] * rep, axis=1))
            l_ref[h] = alpha * l_ref[h] + jnp.sum(p, axis=-1, keepdims=True)
            acc_ref[h] = alpha * acc_ref[h] + jnp.dot(p.astype(v_ref.dtype), v_ref[rows, hs[h]],
                                                      preferred_element_type=F32)
            m_ref[h] = m_new

    def body(j, carry):
        step(j, False)
        return carry

    n_full = (qi * tq) // tk
    lax.fori_loop(0, n_full, body, 0)
    step(n_full, True)

    lam = (jnp.exp(jnp.sum(lam_ref[0:1, :] * lam_ref[1:2, :], axis=-1, keepdims=True))
           - jnp.exp(jnp.sum(lam_ref[2:3, :] * lam_ref[3:4, :], axis=-1, keepdims=True))
           + lambda_init)
    for h in heads:
        o = acc_ref[h] / l_ref[h]
        d = o[:tq, :] - lam * o[tq:, :]
        o_ref[:, hs[h]] = (_rms(d, subln_ref[...], SUBLN_EPS)
                           * (1.0 - lambda_init)).astype(o_ref.dtype)


def _attn(q, k, v, lam4, subln, lambda_init):
    b, t, _ = q.shape
    tq = min(ATTN_BLOCK, t)
    hg = ATTN_HEAD_GROUP
    gw = hg * DA_V_DIM
    return pl.pallas_call(
        functools.partial(_attn_kernel, lambda_init=lambda_init),
        grid=(b, DA_HEADS // hg, t // tq),
        in_specs=[pl.BlockSpec((None, tq, gw), lambda bi, h, qi: (bi, qi, h)),
                  pl.BlockSpec((None, t, gw), lambda bi, h, qi: (bi, 0, h)),
                  pl.BlockSpec((None, t, gw), lambda bi, h, qi: (bi, 0, h)),
                  pl.BlockSpec((4, DA_HEAD_DIM), lambda bi, h, qi: (0, 0)),
                  pl.BlockSpec((1, DA_V_DIM), lambda bi, h, qi: (0, 0))],
        out_specs=pl.BlockSpec((None, tq, gw), lambda bi, h, qi: (bi, qi, h)),
        out_shape=jax.ShapeDtypeStruct((b, t, DA_WIDTH), BF16),
        scratch_shapes=[pltpu.VMEM((hg, 2 * tq, LANES), F32), pltpu.VMEM((hg, 2 * tq, LANES), F32),
                        pltpu.VMEM((hg, 2 * tq, DA_V_DIM), F32)],
        compiler_params=_params("parallel", "parallel", "arbitrary"),
        name="attn",
    )(q, k, v, lam4, subln)


def _rwkv_kernel(rw_ref, w0_ref, w2_ref, a0_ref, a2_ref, g2_ref, kk_ref, ka_ref, rk_ref,
                 lnw_ref, lnb_ref, o_ref, state_ref, r_s, k_s, v_s, lw_s, kk_s, a_s, g_s, cum_s):
    ti = pl.program_id(1)

    @pl.when(ti == 0)
    def _():
        state_ref[...] = jnp.zeros_like(state_ref)

    w = RW_WIDTH
    rw = rw_ref[...]
    k = rw[:, w:2 * w]
    wl = rw[:, 3 * w:3 * w + DECAY_LORA]
    al = rw[:, 3 * w + DECAY_LORA:3 * w + DECAY_LORA + AAA_LORA]
    gl = rw[:, 3 * w + DECAY_LORA + AAA_LORA:]
    z = -(w0_ref[...] + _bdot(jnp.tanh(wl), w2_ref[...]))
    softplus = jnp.maximum(z, 0.0) + jnp.log(1.0 + jnp.exp(-jnp.abs(z)))
    a = jax.nn.sigmoid(a0_ref[...] + _bdot(al, a2_ref[...]))
    r_s[...] = rw[:, 0:w]
    v_s[...] = rw[:, 2 * w:3 * w]
    lw_s[...] = -jnp.exp(-softplus - 0.5)
    a_s[...] = a
    g_s[...] = _bdot(jax.nn.sigmoid(gl), g2_ref[...])
    kk_s[...] = k * kk_ref[...]
    k_s[...] = k * (1.0 + (a - 1.0) * ka_ref[...])

    c_len = RW_CHUNK
    n = RW_HEAD_DIM
    tb = rw_ref.shape[0]

    br = lax.broadcasted_iota(jnp.int32, (tb, tb), 0)
    bc = lax.broadcasted_iota(jnp.int32, (tb, tb), 1)
    tri = jnp.where((br >= bc) & (br // c_len == bc // c_len), 1.0, 0.0).astype(BF16)
    lw_all = lw_s[...]
    lw_hi = lw_all.astype(BF16)
    rem = lw_all - lw_hi.astype(F32)
    lw_mid = rem.astype(BF16)
    lw_lo = (rem - lw_mid.astype(F32)).astype(BF16)
    cum_s[...] = (jnp.dot(tri, lw_hi, preferred_element_type=F32)
                  + jnp.dot(tri, lw_mid, preferred_element_type=F32)
                  + jnp.dot(tri, lw_lo, preferred_element_type=F32))

    row = lax.broadcasted_iota(jnp.int32, (c_len, 2 * c_len), 0)
    col = lax.broadcasted_iota(jnp.int32, (c_len, 2 * c_len), 1)
    incl2 = row >= col % c_len
    strict2 = row > col % c_len
    eye = jnp.where(lax.broadcasted_iota(jnp.int32, (c_len, c_len), 0)
                    == lax.broadcasted_iota(jnp.int32, (c_len, c_len), 1), 1.0, 0.0).astype(F32)

    def chunk(ci, carry):
        rows = pl.ds(pl.multiple_of(ci * c_len, c_len), c_len)
        heads = range(RW_HEADS)
        sl = [slice(h * n, (h + 1) * n) for h in heads]
        r = [r_s[rows, c] for c in sl]
        kh = [k_s[rows, c] for c in sl]
        v = [v_s[rows, c] for c in sl]
        lw = [lw_s[rows, c] for c in sl]
        cum = [cum_s[rows, c] for c in sl]
        kk = [kk_s[rows, c] for c in sl]
        kk = [x * lax.rsqrt(jnp.maximum(jnp.sum(x * x, axis=-1, keepdims=True), 1e-24)) for x in kk]
        kka = [kk[h] * a_s[rows, sl[h]] for h in heads]
        end = [jnp.sum(x, axis=0, keepdims=True) for x in lw]
        e_neg = [jnp.exp(-x) for x in cum]
        e_end = [jnp.exp(end[h] - cum[h]) for h in heads]
        left = [jnp.concatenate([-kk[h] * jnp.exp(cum[h] - lw[h]), r[h] * jnp.exp(cum[h])], axis=0)
                for h in heads]
        g = [_bdot_nt(left[h], jnp.concatenate([kka[h] * e_neg[h], kh[h] * e_neg[h]], axis=0))
             for h in heads]
        a_a = [jnp.where(strict2, x[:c_len, :], 0.0) for x in g]
        a_r = [jnp.where(incl2, x[c_len:, :], 0.0) for x in g]
        pw = [x[:, :c_len] for x in a_a]
        inv = [eye + x for x in pw]
        for _ in range(c_len.bit_length() - 2):
            pw = [_bdot(x, x) for x in pw]
            inv = [inv[h] + _bdot(inv[h], pw[h]) for h in heads]
        akv = [_bdot(a_a[h][:, c_len:], v[h]) for h in heads]
        s0 = [state_ref[h] for h in heads]
        ls = [_bdot_nt(left[h], s0[h]) for h in heads]
        u = [_bdot(inv[h], ls[h][:c_len, :] + akv[h]) for h in heads]
        uv = [jnp.concatenate([u[h], v[h]], axis=0) for h in heads]
        y = [ls[h][c_len:, :] + _bdot(a_r[h], uv[h]) for h in heads]
        for h in heads:
            state_ref[h] = s0[h] * jnp.exp(end[h]) + _bdot_tn(
                uv[h], jnp.concatenate([kka[h] * e_end[h], kh[h] * e_end[h]], axis=0))
        for h in heads:
            mean = jnp.mean(y[h], axis=-1, keepdims=True)
            yc = y[h] - mean
            var = jnp.mean(yc * yc, axis=-1, keepdims=True)
            yn = yc * lax.rsqrt(var + LN_X_EPS) * lnw_ref[:, sl[h]] + lnb_ref[:, sl[h]]
            bonus = jnp.sum(r[h] * kh[h] * rk_ref[:, sl[h]], axis=-1, keepdims=True) * v[h]
            o_ref[rows, sl[h]] = ((yn + bonus) * g_s[rows, sl[h]]).astype(o_ref.dtype)
        return carry

    lax.fori_loop(0, rw_ref.shape[0] // c_len, chunk, 0, unroll=2)


def _rwkv(rw, w0, w2, a0, a2, g2, k_k, k_a, r_k, ln_w, ln_b):
    b, t, _ = rw.shape
    tb = min(RW_BLOCK, t)
    w = RW_WIDTH
    vec = pl.BlockSpec((1, w), lambda bi, ti: (0, 0))
    mat = lambda r: pl.BlockSpec((r, w), lambda bi, ti: (0, 0))
    return pl.pallas_call(
        _rwkv_kernel,
        grid=(b, t // tb),
        in_specs=[pl.BlockSpec((None, tb, RW_COLS), lambda bi, ti: (bi, ti, 0)),
                  vec, mat(DECAY_LORA), vec, mat(AAA_LORA), mat(GATE_LORA), vec, vec, vec, vec, vec],
        out_specs=pl.BlockSpec((None, tb, w), lambda bi, ti: (bi, ti, 0)),
        out_shape=jax.ShapeDtypeStruct((b, t, w), BF16),
        scratch_shapes=[pltpu.VMEM((RW_HEADS, RW_HEAD_DIM, RW_HEAD_DIM), F32)]
        + [pltpu.VMEM((tb, w), F32)] * 8,
        compiler_params=_params("parallel", "arbitrary"),
        name="rwkv",
    )(rw, w0, w2, a0, a2, g2, k_k, k_a, r_k.reshape(1, w), ln_w, ln_b)


def _pack_rows(x):
    half = x.shape[1] // 2
    hi = pltpu.bitcast(x[:, :half].astype(BF16).astype(F32), jnp.uint32)
    lo = pltpu.bitcast(x[:, half:].astype(BF16).astype(F32), jnp.uint32)
    return hi | (lo >> 16)


def _unpack_rows(u):
    hi = pltpu.bitcast(u & jnp.uint32(0xFFFF0000), F32)
    lo = pltpu.bitcast(u << 16, F32)
    return jnp.concatenate([hi, lo], axis=1)


ROW_SLAB = 4


def _store_rows(ref, u, r0=0):
    n = u.shape[0]
    for c in range(ROW_SLAB):
        ref[pl.ds(r0 * ROW_SLAB + c, n, stride=ROW_SLAB), :] = u[:, c * LANES:(c + 1) * LANES]


def _load_rows(ref, n, r0=0):
    return jnp.concatenate([ref[pl.ds(r0 * ROW_SLAB + c, n, stride=ROW_SLAB), :]
                            for c in range(ROW_SLAB)], axis=1)


def _out_kernel(yda_ref, yrw_ref, x_ref, mod_ref, wo_ref, pmn_ref, pfn_ref, rw_ref, rb_ref,
                x1_ref, h2_ref, idx_ref, wgt_ref):
    y = (jnp.dot(yda_ref[...], wo_ref[0:DA_WIDTH, :], preferred_element_type=F32)
         + jnp.dot(yrw_ref[...], wo_ref[DA_WIDTH:, :], preferred_element_type=F32))
    x1 = x_ref[...] + mod_ref[2:3, :] * _rms(y, pmn_ref[...], NORM_EPS)
    x1_ref[...] = x1
    h2 = _rms(x1, pfn_ref[...], NORM_EPS) * (1.0 + mod_ref[4:5, :]) + mod_ref[3:4, :]
    _store_rows(h2_ref, _pack_rows(h2))

    h_hi = h2.astype(BF16)
    h_lo = (h2 - h_hi.astype(F32)).astype(BF16)
    rw = rw_ref[...]
    w_hi = rw.astype(BF16)
    w_lo = (rw - w_hi.astype(F32)).astype(BF16)
    logits = (jnp.dot(h_hi, w_hi, preferred_element_type=F32)
              + jnp.dot(h_hi, w_lo, preferred_element_type=F32)
              + jnp.dot(h_lo, w_hi, preferred_element_type=F32)) + rb_ref[...]

    lane = lax.broadcasted_iota(jnp.int32, logits.shape, 1)
    slot = lax.broadcasted_iota(jnp.int32, idx_ref.shape, 1)
    idx = jnp.zeros(idx_ref.shape, jnp.int32)
    val = jnp.zeros(idx_ref.shape, F32)
    top = None
    for j in range(TOP_K):
        m = jnp.max(logits, axis=-1, keepdims=True)
        i = jnp.min(jnp.where(logits == m, lane, N_EXPERTS), axis=-1, keepdims=True)
        top = m if top is None else top
        idx = jnp.where(slot == j, i, idx)
        val = jnp.where(slot == j, jnp.exp(m - top), val)
        logits = jnp.where(lane == i, -jnp.inf, logits)
    idx_ref[...] = idx
    wgt_ref[...] = val / jnp.sum(val, axis=-1, keepdims=True)


def _out(y_da, y_rw, x, mod3, w_out_b, post_mix_norm, pre_ffn_norm, router_w, router_b):
    b, t, d = x.shape
    tm = min(OUT_ROWS, t)
    e = router_w.shape[1]
    blk = lambda w: pl.BlockSpec((None, tm, w), lambda bi, ti: (bi, ti, 0))
    full = lambda r, c: pl.BlockSpec((r, c), lambda bi, ti: (0, 0))
    return pl.pallas_call(
        _out_kernel,
        grid=(b, t // tm),
        in_specs=[blk(DA_WIDTH), blk(RW_WIDTH), blk(d),
                  pl.BlockSpec((None, N_MOD, d), lambda bi, ti: (bi, 0, 0)),
                  full(d, d), full(1, d), full(1, d), full(d, e), full(1, e)],
        out_specs=[blk(d),
                   pl.BlockSpec((None, tm * ROW_SLAB, LANES), lambda bi, ti: (bi, ti, 0)),
                   blk(TOP_K), blk(TOP_K)],
        out_shape=[jax.ShapeDtypeStruct((b, t, d), F32),
                   jax.ShapeDtypeStruct((b, t * ROW_SLAB, LANES), jnp.uint32),
                   jax.ShapeDtypeStruct((b, t, TOP_K), jnp.int32),
                   jax.ShapeDtypeStruct((b, t, TOP_K), F32)],
        compiler_params=_params("parallel", "parallel"),
        name="out",
    )(y_da, y_rw, x, mod3, w_out_b, post_mix_norm, pre_ffn_norm, router_w,
      router_b.reshape(1, e))


def _route(top_idx, rows_per_tile, n_tiles):
    experts = jnp.arange(N_EXPERTS, dtype=jnp.int32)
    chosen = (top_idx[:, :, None] == experts[None, None, :]).astype(jnp.int32)
    member = jnp.sum(chosen, axis=1)
    csum = jnp.cumsum(member, axis=0)
    counts = csum[-1]
    padded = (counts + rows_per_tile - 1) // rows_per_tile * rows_per_tile
    ends = jnp.cumsum(padded)
    starts = ends - padded
    pos = jnp.sum((csum - member + starts[None, :])[:, None, :] * chosen, axis=2).reshape(-1)
    n_active = ends[-1] // rows_per_tile
    tile_start = jnp.arange(n_tiles, dtype=jnp.int32) * rows_per_tile
    tile = jnp.minimum(tile_start, ends[-1] - 1)
    tile_expert = jnp.sum((tile[:, None] >= ends[None, :]).astype(jnp.int32), axis=1)
    used = padded > 0
    later = lax.cummin(jnp.where(used, experts, N_EXPERTS), reverse=True)
    following = jnp.concatenate([later[1:], jnp.full((1,), N_EXPERTS, jnp.int32)])
    following = jnp.where(following == N_EXPERTS, -1, following)
    run_index = jnp.cumsum(used.astype(jnp.int32)) - 1
    plan = (tile_expert.astype(jnp.int32), n_active.reshape(1).astype(jnp.int32),
            following[tile_expert].astype(jnp.int32), (run_index[tile_expert] % 2).astype(jnp.int32))
    pads = ((starts + counts).astype(jnp.int32), (padded - counts).astype(jnp.int32))
    return pos.astype(jnp.int32), pads, plan


def _dispatch_kernel(ps_ref, pl_ref, pos_ref, h_ref, xs_ref, zero_ref, sem, pad_sem):
    tb = h_ref.shape[0] // ROW_SLAB

    def issue(t, carry):
        src = h_ref.at[pl.ds(pl.multiple_of(t * ROW_SLAB, ROW_SLAB), ROW_SLAB)]
        for j in range(TOP_K):
            pltpu.make_async_copy(src, xs_ref.at[pos_ref[0, t * TOP_K + j]],
                                  sem).start(priority=j % 2)
        return carry

    lax.fori_loop(0, tb, issue, 0, unroll=8)

    @pl.when(pl.program_id(0) == pl.num_programs(0) - 1)
    def _():
        zero_ref[...] = jnp.zeros_like(zero_ref)
        bits = [1 << k for k in reversed(range(zero_ref.shape[0].bit_length()))]

        def pad_copy(e, b):
            below = pl_ref[e] & ~(2 * b - 1)
            return pltpu.make_async_copy(zero_ref.at[pl.ds(0, b)],
                                         xs_ref.at[pl.ds(ps_ref[e] + below, b)], pad_sem)

        for wait in (False, True):
            for e in range(N_EXPERTS):
                for b in bits:
                    @pl.when((pl_ref[e] & b) != 0)
                    def _(e=e, b=b):
                        if wait:
                            pad_copy(e, b).wait()
                        else:
                            pad_copy(e, b).start()

    for j in range(TOP_K):
        pltpu.make_async_copy(xs_ref.at[pl.ds(0, tb)], xs_ref.at[pl.ds(0, tb)], sem).wait()


def _dispatch(pos, pads, h2p, n_rows):
    pad_start, pad_len = pads
    n = h2p.shape[0] // ROW_SLAB
    w = h2p.shape[1]
    tb = min(DISPATCH_TOKENS, n)
    pos3 = pos.reshape(n // tb, 1, tb * TOP_K)
    return pl.pallas_call(
        _dispatch_kernel,
        grid_spec=pltpu.PrefetchScalarGridSpec(
            num_scalar_prefetch=2,
            grid=(n // tb,),
            in_specs=[pl.BlockSpec((None, 1, tb * TOP_K), lambda i, ps, pn: (i, 0, 0),
                                   memory_space=pltpu.SMEM),
                      pl.BlockSpec((tb * ROW_SLAB, w), lambda i, ps, pn: (i, 0))],
            out_specs=pl.BlockSpec(memory_space=pl.ANY),
            scratch_shapes=[pltpu.VMEM((EXPERT_ROWS // 2, ROW_SLAB, w), h2p.dtype),
                            pltpu.SemaphoreType.DMA(()), pltpu.SemaphoreType.DMA(())]),
        out_shape=jax.ShapeDtypeStruct((n_rows, ROW_SLAB, w), h2p.dtype),
        compiler_params=_params("arbitrary"),
        name="dispatch",
    )(pad_start, pad_len, pos3, h2p)


def _expert_kernel(te_ref, na_ref, nx_ref, sl_ref, xs_ref, w1_hbm, b1_ref, w2_hbm, b2_ref, ys_ref,
                   w1p_s, b1p_s, w2b_s, act_s, w1_buf, w2_buf, sem):
    i = pl.program_id(0)
    active = i < na_ref[0]
    expert = te_ref[i]
    fresh = jnp.logical_or(i == 0, expert != te_ref[jnp.maximum(i - 1, 0)])
    slot = sl_ref[i]
    grp = 2 * LANES
    n_grp = w1_buf.shape[2] // grp

    def weight_copies(ex, s):
        return (pltpu.make_async_copy(w1_hbm.at[ex], w1_buf.at[s], sem.at[0, s]),
                pltpu.make_async_copy(w2_hbm.at[ex], w2_buf.at[s], sem.at[1, s]))

    @pl.when(jnp.logical_and(active, i == 0))
    def _():
        for cp in weight_copies(expert, slot):
            cp.start()

    @pl.when(jnp.logical_and(active, fresh))
    def _():
        for cp in weight_copies(expert, slot):
            cp.wait()

        @pl.when(nx_ref[i] >= 0)
        def _():
            for cp in weight_copies(nx_ref[i], 1 - slot):
                cp.start()

        src = lax.broadcasted_iota(jnp.int32, (grp, grp), 0)
        dst = lax.broadcasted_iota(jnp.int32, (grp, grp), 1)
        perm = jnp.where(src == jnp.where(dst < LANES, 2 * dst, 2 * (dst - LANES) + 1),
                         1.0, 0.0).astype(BF16)
        for g in range(n_grp):
            sl = slice(g * grp, (g + 1) * grp)
            w1p_s[:, sl] = jnp.dot(w1_buf[slot, :, sl].astype(BF16), perm,
                                   preferred_element_type=F32).astype(BF16)
            b = b1_ref[:, sl]
            b_hi = b.astype(BF16)
            b_lo = (b - b_hi.astype(F32)).astype(BF16)
            b1p_s[:, sl] = (jnp.dot(b_hi, perm, preferred_element_type=F32)
                            + jnp.dot(b_lo, perm, preferred_element_type=F32))
        w2b_s[...] = w2_buf[slot].astype(BF16)

    @pl.when(active)
    def _():
        x = _unpack_rows(_load_rows(xs_ref, act_s.shape[0])).astype(BF16)
        hid = jnp.dot(x, w1p_s[...], preferred_element_type=F32) + b1p_s[0:1, :]
        for g in range(n_grp):
            glu = jnp.minimum(hid[:, g * grp:g * grp + LANES], SWIGLU_LIMIT)
            lin = jnp.clip(hid[:, g * grp + LANES:(g + 1) * grp], -SWIGLU_LIMIT, SWIGLU_LIMIT)
            act_s[:, g * LANES:(g + 1) * LANES] = (
                glu * jax.nn.sigmoid(SWIGLU_ALPHA * glu) * (lin + 1.0)).astype(BF16)
        y = jnp.dot(act_s[...], w2b_s[...], preferred_element_type=F32) + b2_ref[...]
        _store_rows(ys_ref, _pack_rows(y))

    @pl.when(jnp.logical_not(active))
    def _():
        ys_ref[...] = jnp.zeros_like(ys_ref)


def _experts(plan, xs, w1, b1, w2, b2):
    tile_expert, n_active, next_expert, tile_slot = plan
    n_rows, w = xs.shape[0] // ROW_SLAB, xs.shape[1]
    tm = EXPERT_ROWS
    d, f2 = w1.shape[1], w1.shape[2]
    f = f2 // 2
    wspec = lambda r, c: pl.BlockSpec((None, r, c), lambda i, te, na, nx, sl: (te[i], 0, 0))
    rows = pl.BlockSpec((tm * ROW_SLAB, w), lambda i, te, na, nx, sl: (i, 0))
    rows_in = pl.BlockSpec((tm * ROW_SLAB, w),
                           lambda i, te, na, nx, sl: (jnp.minimum(i, na[0] - 1), 0))
    hbm = pl.BlockSpec(memory_space=pl.ANY)
    return pl.pallas_call(
        _expert_kernel,
        grid_spec=pltpu.PrefetchScalarGridSpec(
            num_scalar_prefetch=4,
            grid=(n_rows // tm,),
            in_specs=[rows_in, hbm, wspec(8, f2), hbm, wspec(1, d)],
            out_specs=rows,
            scratch_shapes=[pltpu.VMEM((d, f2), BF16), pltpu.VMEM((8, f2), F32),
                            pltpu.VMEM((f, d), BF16), pltpu.VMEM((tm, f), BF16),
                            pltpu.VMEM((2, d, f2), F32), pltpu.VMEM((2, f, d), F32),
                            pltpu.SemaphoreType.DMA((2, 2))]),
        out_shape=jax.ShapeDtypeStruct((n_rows * ROW_SLAB, w), jnp.uint32),
        compiler_params=_params("arbitrary"),
        name="expert",
    )(tile_expert, n_active, next_expert, tile_slot, xs, w1, b1, w2, b2)


def _combine_kernel(pos_ref, nxt_ref, ys_ref, wgt_ref, x1_ref, mod_ref, nw_ref, o_ref, buf_ref,
                    sem):
    tc = x1_ref.shape[0]
    step = pl.program_id(0) * pl.num_programs(1) + pl.program_id(1)
    last = pl.num_programs(0) * pl.num_programs(1) - 1
    slot = step % 2

    def gather(p_ref, s):
        def issue(t, carry):
            dst = pl.ds(pl.multiple_of(t * ROW_SLAB, ROW_SLAB), ROW_SLAB)
            for j in range(TOP_K):
                pltpu.make_async_copy(ys_ref.at[p_ref[0, t * TOP_K + j]], buf_ref.at[s, j, dst],
                                      sem.at[s]).start(priority=j % 2)
            return carry

        lax.fori_loop(0, tc, issue, 0, unroll=8)

    @pl.when(step == 0)
    def _():
        gather(pos_ref, slot)

    @pl.when(step < last)
    def _():
        gather(nxt_ref, 1 - slot)

    for j in range(TOP_K):
        pltpu.make_async_copy(ys_ref.at[pl.ds(0, tc)], ys_ref.at[pl.ds(0, tc)], sem.at[slot]).wait()

    sub = min(COMBINE_SUB_ROWS, tc)

    def mix(i, carry):
        r0 = pl.multiple_of(i * sub, sub)
        wgt = wgt_ref[pl.ds(r0, sub), :]
        acc = jnp.zeros((sub, x1_ref.shape[1]), F32)
        for j in range(TOP_K):
            acc = acc + wgt[:, j:j + 1] * _unpack_rows(_load_rows(buf_ref.at[slot, j], sub, r0))
        o_ref[pl.ds(r0, sub), :] = (x1_ref[pl.ds(r0, sub), :]
                                    + mod_ref[5:6, :] * _rms(acc, nw_ref[...], NORM_EPS))
        return carry

    lax.fori_loop(0, tc // sub, mix, 0)


def _combine(pos, ys, wgt, x1, mod3, post_ffn_norm):
    b, t, d = x1.shape
    tc = min(COMBINE_TOKENS, t)
    nt = t // tc
    pos3 = pos.reshape(b * nt, 1, tc * TOP_K)
    blk = lambda w: pl.BlockSpec((None, tc, w), lambda bi, ti: (bi, ti, 0))
    return pl.pallas_call(
        _combine_kernel,
        grid=(b, nt),
        in_specs=[pl.BlockSpec((None, 1, tc * TOP_K), lambda bi, ti: (bi * nt + ti, 0, 0),
                               memory_space=pltpu.SMEM),
                  pl.BlockSpec((None, 1, tc * TOP_K),
                               lambda bi, ti: (jnp.minimum(bi * nt + ti + 1, b * nt - 1), 0, 0),
                               memory_space=pltpu.SMEM),
                  pl.BlockSpec(memory_space=pl.ANY),
                  blk(TOP_K), blk(d),
                  pl.BlockSpec((None, N_MOD, d), lambda bi, ti: (bi, 0, 0)),
                  pl.BlockSpec((1, d), lambda bi, ti: (0, 0))],
        out_specs=blk(d),
        out_shape=jax.ShapeDtypeStruct((b, t, d), F32),
        scratch_shapes=[pltpu.VMEM((2, TOP_K, tc * ROW_SLAB, LANES), jnp.uint32),
                        pltpu.SemaphoreType.DMA((2,))],
        compiler_params=_params("arbitrary", "arbitrary"),
        name="combine",
    )(pos3, pos3, ys.reshape(ys.shape[0] // ROW_SLAB, ROW_SLAB, LANES), wgt, x1, mod3,
      post_ffn_norm)


def _stages(x, c, positions, ada_w, ada_b, pre_mix_norm, post_mix_norm, pre_ffn_norm,
            post_ffn_norm, w_in, w_out, da_lambda_q1, da_lambda_k1, da_lambda_q2, da_lambda_k2,
            da_subln, rw_mu, rw_w0, rw_w2, rw_a0, rw_a2, rw_g2, rw_k_k, rw_k_a, rw_r_k, rw_ln_w,
            rw_ln_b, router_w, router_b, moe_w1, moe_b1, moe_w2, moe_b2):
    b, t, d = x.shape
    res = {}
    lambda_init = 0.8 - 0.6 * math.exp(-0.3 * 0)
    mod = _mod(c, ada_w[0], ada_b[0])
    res["mod"] = mod
    mod3 = mod.reshape(b, N_MOD, d)
    inv_freq = ROPE_THETA ** (-jnp.arange(0, ROPE_DIM, 2, dtype=F32) / ROPE_DIM)
    invf = jnp.tile(inv_freq, LANES // (ROPE_DIM // 2)).reshape(1, LANES)
    q, k, v, rw = _proj(x, positions.reshape(b, t, 1), mod3, pre_mix_norm, invf,
                        w_in[0].astype(BF16), rw_mu)
    res.update(q=q, k=k, v=v, rw=rw)
    lam4 = jnp.concatenate([da_lambda_q1, da_lambda_k1, da_lambda_q2, da_lambda_k2], axis=0)
    y_da = _attn(q, k, v, lam4, da_subln, lambda_init)
    res["y_da"] = y_da
    y_rw = _rwkv(rw, rw_w0, rw_w2[0], rw_a0, rw_a2[0], rw_g2[0], rw_k_k, rw_k_a, rw_r_k[0],
                 rw_ln_w, rw_ln_b)
    res["y_rw"] = y_rw
    x1, h2p, top_idx, top_w = _out(y_da, y_rw, x, mod3, w_out[0].astype(BF16), post_mix_norm,
                                   pre_ffn_norm, router_w[0], router_b[0])
    res.update(x1=x1, top_idx=top_idx, top_w=top_w)
    n = b * t
    n_tiles = n * TOP_K // EXPERT_ROWS + N_EXPERTS
    pos, pads, plan = _route(top_idx.reshape(n, TOP_K), EXPERT_ROWS, n_tiles)
    xs = _dispatch(pos, pads, h2p.reshape(n * ROW_SLAB, LANES), n_tiles * EXPERT_ROWS)
    xs = xs.reshape(n_tiles * EXPERT_ROWS * ROW_SLAB, LANES)
    b1 = jnp.broadcast_to(moe_b1[0][:, None, :], (N_EXPERTS, 8, moe_b1.shape[-1]))
    ys = _experts(plan, xs, moe_w1[0], b1, moe_w2[0], moe_b2[0][:, None, :])
    res["final"] = _combine(pos, ys, top_w, x1, mod3, post_ffn_norm)
    return res


stages = _stages


def kernel(x, c, positions, ada_w, ada_b, pre_mix_norm, post_mix_norm, pre_ffn_norm, post_ffn_norm, w_in, w_out, da_lambda_q1, da_lambda_k1, da_lambda_q2, da_lambda_k2, da_subln, rw_mu, rw_w0, rw_w2, rw_a0, rw_a2, rw_g2, rw_k_k, rw_k_a, rw_r_k, rw_ln_w, rw_ln_b, router_w, router_b, moe_w1, moe_b1, moe_w2, moe_b2):
    res = _stages(x, c, positions, ada_w, ada_b, pre_mix_norm, post_mix_norm, pre_ffn_norm,
                  post_ffn_norm, w_in, w_out, da_lambda_q1, da_lambda_k1, da_lambda_q2,
                  da_lambda_k2, da_subln, rw_mu, rw_w0, rw_w2, rw_a0, rw_a2, rw_g2, rw_k_k,
                  rw_k_a, rw_r_k, rw_ln_w, rw_ln_b, router_w, router_b, moe_w1, moe_b1,
                  moe_w2, moe_b2)
    return res["final"]
```

```python
import functools
import math

import jax
import jax.numpy as jnp
from jax import lax
from jax.experimental import pallas as pl
from jax.experimental.pallas import tpu as pltpu

F32 = jnp.float32
BF16 = jnp.bfloat16

DA_HEADS = 4
DA_HEAD_DIM = 64
DA_V_DIM = 128
DA_WIDTH = 512
RW_HEADS = 8
RW_HEAD_DIM = 64
RW_WIDTH = 512
DECAY_LORA = 64
AAA_LORA = 64
GATE_LORA = 128
DA_COLS = 1536
RW_COLS = 1792
ROPE_THETA = 500000.0
ROPE_DIM = 16
N_EXPERTS = 32
TOP_K = 4
SWIGLU_ALPHA = 1.702
SWIGLU_LIMIT = 7.0
NORM_EPS = 1e-6
SUBLN_EPS = 1e-5
LN_X_EPS = 64e-5
N_MOD = 6

LANES = 128
SUBLANES = 8
VMEM_LIMIT_BYTES = 56 * 1024 * 1024

PROJ_ROWS = 512
ATTN_BLOCK = 512
ATTN_KV_BLOCK = 512
ATTN_HEAD_GROUP = 4
RW_CHUNK = 128
RW_BLOCK = 256
OUT_ROWS = 512
EXPERT_ROWS = 512
DISPATCH_TOKENS = 2048
COMBINE_TOKENS = 1024
COMBINE_SUB_ROWS = 256


def _params(*sem):
    return pltpu.CompilerParams(dimension_semantics=sem, vmem_limit_bytes=VMEM_LIMIT_BYTES)


def _bdot(a, b):
    return jnp.dot(a.astype(BF16), b.astype(BF16), preferred_element_type=F32)


def _bdot_nt(a, b):
    return lax.dot_general(a.astype(BF16), b.astype(BF16), (((1,), (1,)), ((), ())),
                           preferred_element_type=F32)


def _bdot_tn(a, b):
    return lax.dot_general(a.astype(BF16), b.astype(BF16), (((0,), (0,)), ((), ())),
                           preferred_element_type=F32)


def _rms(x, w, eps):
    return x * lax.rsqrt(jnp.mean(x * x, axis=-1, keepdims=True) + eps) * w


def _mod_kernel(c_ref, w_ref, b_ref, o_ref):
    c = c_ref[...]
    s = c * jax.nn.sigmoid(c)
    o_ref[...] = _bdot(s, w_ref[...]) + b_ref[...]


def _mod(c, ada_w, ada_b):
    b, d = c.shape
    n = ada_w.shape[1]
    return pl.pallas_call(
        _mod_kernel,
        grid=(n // d,),
        in_specs=[pl.BlockSpec((b, d), lambda j: (0, 0)),
                  pl.BlockSpec((d, d), lambda j: (0, j)),
                  pl.BlockSpec((1, d), lambda j: (0, j))],
        out_specs=pl.BlockSpec((b, d), lambda j: (0, j)),
        out_shape=jax.ShapeDtypeStruct((b, n), F32),
        compiler_params=_params("parallel"),
        name="mod",
    )(c, ada_w, ada_b.reshape(1, n))


def _proj_kernel(x_ref, pos_ref, mod_ref, nw_ref, invf_ref, w_ref, mu_ref,
                 q_ref, k_ref, v_ref, rw_ref, carry_ref):
    ti = pl.program_id(1)

    @pl.when(ti == 0)
    def _():
        carry_ref[...] = jnp.zeros_like(carry_ref)

    x = x_ref[...]
    h = _rms(x, nw_ref[...], NORM_EPS) * (1.0 + mod_ref[1:2, :]) + mod_ref[0:1, :]
    hb = h.astype(BF16)

    ang = pos_ref[...].astype(F32) * invf_ref[...]
    cos, sin = jnp.cos(ang), jnp.sin(ang)
    l64 = lax.broadcasted_iota(jnp.int32, ang.shape, 1) % DA_HEAD_DIM
    half = ROPE_DIM // 2
    c_tab = jnp.where(l64 < ROPE_DIM, cos, 1.0)
    s_lo = jnp.where(l64 < half, -sin, 0.0)
    s_hi = jnp.where((l64 >= half) & (l64 < ROPE_DIM), sin, 0.0)

    def rope(z):
        up = pltpu.roll(z, LANES - half, axis=1)
        dn = pltpu.roll(z, half, axis=1)
        return z * c_tab + up * s_lo + dn * s_hi

    for g in range(DA_WIDTH // LANES):
        sl = slice(g * LANES, (g + 1) * LANES)
        qg = jnp.dot(hb, w_ref[:, sl], preferred_element_type=F32)
        q_ref[:, sl] = (rope(qg) * (DA_HEAD_DIM ** -0.5)).astype(q_ref.dtype)
        kg = jnp.dot(hb, w_ref[:, DA_WIDTH + g * LANES:DA_WIDTH + (g + 1) * LANES],
                     preferred_element_type=F32)
        k_ref[:, sl] = rope(kg).astype(k_ref.dtype)
    v_ref[...] = jnp.dot(hb, w_ref[:, 2 * DA_WIDTH:DA_COLS],
                         preferred_element_type=F32).astype(v_ref.dtype)

    p = jnp.dot(hb, w_ref[:, DA_COLS:], preferred_element_type=F32)
    rows = p.shape[0]
    prev = pltpu.roll(p, 1, axis=0)
    first = lax.broadcasted_iota(jnp.int32, p.shape, 0) == 0
    prev = jnp.where(first, carry_ref[0:1, :], prev)
    rw_ref[...] = p + (prev - p) * mu_ref[...]
    carry_ref[0:1, :] = p[rows - 1:rows, :]


def _proj(x, pos3, mod3, norm_w, invf, w_in_b, mu):
    b, t, d = x.shape
    tm = min(PROJ_ROWS, t)
    n_in = w_in_b.shape[1]
    blk = lambda w: pl.BlockSpec((None, tm, w), lambda bi, ti: (bi, ti, 0))
    full = lambda r, c: pl.BlockSpec((r, c), lambda bi, ti: (0, 0))
    return pl.pallas_call(
        _proj_kernel,
        grid=(b, t // tm),
        in_specs=[blk(d), blk(1),
                  pl.BlockSpec((None, N_MOD, d), lambda bi, ti: (bi, 0, 0)),
                  full(1, d), full(1, LANES), full(d, n_in), full(1, RW_COLS)],
        out_specs=[blk(DA_WIDTH), blk(DA_WIDTH), blk(DA_WIDTH), blk(RW_COLS)],
        out_shape=[jax.ShapeDtypeStruct((b, t, DA_WIDTH), BF16)] * 3
        + [jax.ShapeDtypeStruct((b, t, RW_COLS), F32)],
        scratch_shapes=[pltpu.VMEM((8, RW_COLS), F32)],
        compiler_params=_params("parallel", "arbitrary"),
        name="proj",
    )(x, pos3, mod3, norm_w, invf, w_in_b, mu)


def _attn_kernel(q_ref, k_ref, v_ref, lam_ref, subln_ref, o_ref, m_ref, acc_ref, *, lambda_init):
    qi = pl.program_id(2)
    tq = q_ref.shape[0]
    heads = range(ATTN_HEAD_GROUP)
    hs = [slice(h * DA_V_DIM, (h + 1) * DA_V_DIM) for h in heads]
    lane = lax.broadcasted_iota(jnp.int32, (tq, DA_V_DIM), 1)
    qq = []
    for c in hs:
        q = q_ref[:, c]
        zero = jnp.zeros_like(q)
        qq.append(jnp.concatenate([jnp.where(lane < DA_HEAD_DIM, q, zero),
                                   jnp.where(lane >= DA_HEAD_DIM, q, zero)], axis=0))

    m_ref[...] = jnp.full(m_ref.shape, -jnp.inf, F32)
    acc_ref[...] = jnp.zeros(acc_ref.shape, F32)
    tk = ATTN_KV_BLOCK if k_ref.shape[0] % ATTN_KV_BLOCK == 0 else tq
    rep = tk // LANES
    ones = jnp.ones((tk, DA_V_DIM), v_ref.dtype)

    def step(j, masked):
        rows = pl.ds(pl.multiple_of(j * tk, tk), tk)
        s = [lax.dot_general(qq[h], k_ref[rows, hs[h]], (((1,), (1,)), ((), ())),
                             preferred_element_type=F32) for h in heads]
        if masked:
            qpos = qi * tq + lax.broadcasted_iota(jnp.int32, s[0].shape, 0) % tq
            kpos = j * tk + lax.broadcasted_iota(jnp.int32, s[0].shape, 1)
            s = [jnp.where(qpos >= kpos, x, -jnp.inf) for x in s]
        for h in heads:
            m_old = m_ref[h]
            m_new = jnp.maximum(m_old, jnp.max(s[h], axis=-1, keepdims=True))
            alpha = jnp.exp(m_old - m_new)
            p = jnp.exp((s[h] - jnp.concatenate([m_new] * rep, axis=1)).astype(v_ref.dtype))
            v_ext = jnp.concatenate([v_ref[rows, hs[h]], ones], axis=1)
            acc_ref[h] = (jnp.concatenate([alpha, alpha], axis=1) * acc_ref[h]
                          + jnp.dot(p, v_ext, preferred_element_type=F32))
            m_ref[h] = m_new

    def body(j, carry):
        step(j, False)
        return carry

    n_full = (qi * tq) // tk
    lax.fori_loop(0, n_full, body, 0)
    step(n_full, True)

    lam = (jnp.exp(jnp.sum(lam_ref[0:1, :] * lam_ref[1:2, :], axis=-1, keepdims=True))
           - jnp.exp(jnp.sum(lam_ref[2:3, :] * lam_ref[3:4, :], axis=-1, keepdims=True))
           + lambda_init)
    for h in heads:
        o = acc_ref[h, :, :DA_V_DIM] / acc_ref[h, :, DA_V_DIM:]
        d = o[:tq, :] - lam * o[tq:, :]
        o_ref[:, hs[h]] = (_rms(d, subln_ref[...], SUBLN_EPS)
                           * (1.0 - lambda_init)).astype(o_ref.dtype)


def _attn(q, k, v, lam4, subln, lambda_init):
    b, t, _ = q.shape
    tq = min(ATTN_BLOCK, t)
    hg = ATTN_HEAD_GROUP
    gw = hg * DA_V_DIM
    return pl.pallas_call(
        functools.partial(_attn_kernel, lambda_init=lambda_init),
        grid=(b, DA_HEADS // hg, t // tq),
        in_specs=[pl.BlockSpec((None, tq, gw), lambda bi, h, qi: (bi, qi, h)),
                  pl.BlockSpec((None, t, gw), lambda bi, h, qi: (bi, 0, h)),
                  pl.BlockSpec((None, t, gw), lambda bi, h, qi: (bi, 0, h)),
                  pl.BlockSpec((4, DA_HEAD_DIM), lambda bi, h, qi: (0, 0)),
                  pl.BlockSpec((1, DA_V_DIM), lambda bi, h, qi: (0, 0))],
        out_specs=pl.BlockSpec((None, tq, gw), lambda bi, h, qi: (bi, qi, h)),
        out_shape=jax.ShapeDtypeStruct((b, t, DA_WIDTH), BF16),
        scratch_shapes=[pltpu.VMEM((hg, 2 * tq, LANES), F32),
                        pltpu.VMEM((hg, 2 * tq, 2 * DA_V_DIM), F32)],
        compiler_params=_params("parallel", "parallel", "arbitrary"),
        name="attn",
    )(q, k, v, lam4, subln)


def _rwkv_kernel(rw_ref, w0_ref, w2_ref, a0_ref, a2_ref, g2_ref, kk_ref, ka_ref, rk_ref,
                 lnw_ref, lnb_ref, o_ref, state_ref, r_s, k_s, v_s, lw_s, kk_s, a_s, g_s, cum_s):
    ti = pl.program_id(1)

    @pl.when(ti == 0)
    def _():
        state_ref[...] = jnp.zeros_like(state_ref)

    w = RW_WIDTH
    rw = rw_ref[...]
    k = rw[:, w:2 * w]
    wl = rw[:, 3 * w:3 * w + DECAY_LORA]
    al = rw[:, 3 * w + DECAY_LORA:3 * w + DECAY_LORA + AAA_LORA]
    gl = rw[:, 3 * w + DECAY_LORA + AAA_LORA:]
    z = -(w0_ref[...] + _bdot(jnp.tanh(wl), w2_ref[...]))
    softplus = jnp.maximum(z, 0.0) + jnp.log(1.0 + jnp.exp(-jnp.abs(z)))
    a = jax.nn.sigmoid(a0_ref[...] + _bdot(al, a2_ref[...]))
    r_s[...] = rw[:, 0:w]
    v_s[...] = rw[:, 2 * w:3 * w]
    lw_s[...] = -jnp.exp(-softplus - 0.5)
    a_s[...] = a
    g_s[...] = _bdot(jax.nn.sigmoid(gl), g2_ref[...])
    kk_s[...] = k * kk_ref[...]
    k_s[...] = k * (1.0 + (a - 1.0) * ka_ref[...])

    c_len = RW_CHUNK
    n = RW_HEAD_DIM
    tb = rw_ref.shape[0]

    br = lax.broadcasted_iota(jnp.int32, (tb, tb), 0)
    bc = lax.broadcasted_iota(jnp.int32, (tb, tb), 1)
    tri = jnp.where((br >= bc) & (br // c_len == bc // c_len), 1.0, 0.0).astype(BF16)
    lw_all = lw_s[...]
    lw_hi = lw_all.astype(BF16)
    rem = lw_all - lw_hi.astype(F32)
    lw_mid = rem.astype(BF16)
    lw_lo = (rem - lw_mid.astype(F32)).astype(BF16)
    cum_s[...] = (jnp.dot(tri, lw_hi, preferred_element_type=F32)
                  + jnp.dot(tri, lw_mid, preferred_element_type=F32)
                  + jnp.dot(tri, lw_lo, preferred_element_type=F32))

    row = lax.broadcasted_iota(jnp.int32, (c_len, 2 * c_len), 0)
    col = lax.broadcasted_iota(jnp.int32, (c_len, 2 * c_len), 1)
    incl2 = row >= col % c_len
    strict2 = row > col % c_len
    eye = jnp.where(lax.broadcasted_iota(jnp.int32, (c_len, c_len), 0)
                    == lax.broadcasted_iota(jnp.int32, (c_len, c_len), 1), 1.0, 0.0).astype(F32)

    def chunk(ci, carry):
        rows = pl.ds(pl.multiple_of(ci * c_len, c_len), c_len)
        heads = range(RW_HEADS)
        sl = [slice(h * n, (h + 1) * n) for h in heads]
        r = [r_s[rows, c] for c in sl]
        kh = [k_s[rows, c] for c in sl]
        v = [v_s[rows, c] for c in sl]
        lw = [lw_s[rows, c] for c in sl]
        cum = [cum_s[rows, c] for c in sl]
        kk = [kk_s[rows, c] for c in sl]
        kk = [x * lax.rsqrt(jnp.maximum(jnp.sum(x * x, axis=-1, keepdims=True), 1e-24)) for x in kk]
        kka = [kk[h] * a_s[rows, sl[h]] for h in heads]
        end = [jnp.sum(x, axis=0, keepdims=True) for x in lw]
        e_neg = [jnp.exp(-x) for x in cum]
        e_end = [jnp.exp(end[h] - cum[h]) for h in heads]
        left = [jnp.concatenate([-kk[h] * jnp.exp(cum[h] - lw[h]), r[h] * jnp.exp(cum[h])], axis=0)
                for h in heads]
        g = [_bdot_nt(left[h], jnp.concatenate([kka[h] * e_neg[h], kh[h] * e_neg[h]], axis=0))
             for h in heads]
        a_a = [jnp.where(strict2, x[:c_len, :], 0.0) for x in g]
        a_r = [jnp.where(incl2, x[c_len:, :], 0.0) for x in g]
        pw = [x[:, :c_len] for x in a_a]
        inv = [eye + x for x in pw]
        for _ in range(c_len.bit_length() - 2):
            pw = [_bdot(x, x) for x in pw]
            inv = [inv[h] + _bdot(inv[h], pw[h]) for h in heads]
        akv = [_bdot(a_a[h][:, c_len:], v[h]) for h in heads]
        s0 = [state_ref[h] for h in heads]
        ls = [_bdot_nt(left[h], s0[h]) for h in heads]
        u = [_bdot(inv[h], ls[h][:c_len, :] + akv[h]) for h in heads]
        uv = [jnp.concatenate([u[h], v[h]], axis=0) for h in heads]
        y = [ls[h][c_len:, :] + _bdot(a_r[h], uv[h]) for h in heads]
        for h in heads:
            state_ref[h] = s0[h] * jnp.exp(end[h]) + _bdot_tn(
                uv[h], jnp.concatenate([kka[h] * e_end[h], kh[h] * e_end[h]], axis=0))
        for h in heads:
            mean = jnp.mean(y[h], axis=-1, keepdims=True)
            yc = y[h] - mean
            var = jnp.mean(yc * yc, axis=-1, keepdims=True)
            yn = yc * lax.rsqrt(var + LN_X_EPS) * lnw_ref[:, sl[h]] + lnb_ref[:, sl[h]]
            bonus = jnp.sum(r[h] * kh[h] * rk_ref[:, sl[h]], axis=-1, keepdims=True) * v[h]
            o_ref[rows, sl[h]] = ((yn + bonus) * g_s[rows, sl[h]]).astype(o_ref.dtype)
        return carry

    lax.fori_loop(0, rw_ref.shape[0] // c_len, chunk, 0, unroll=2)


def _rwkv(rw, w0, w2, a0, a2, g2, k_k, k_a, r_k, ln_w, ln_b):
    b, t, _ = rw.shape
    tb = min(RW_BLOCK, t)
    w = RW_WIDTH
    vec = pl.BlockSpec((1, w), lambda bi, ti: (0, 0))
    mat = lambda r: pl.BlockSpec((r, w), lambda bi, ti: (0, 0))
    return pl.pallas_call(
        _rwkv_kernel,
        grid=(b, t // tb),
        in_specs=[pl.BlockSpec((None, tb, RW_COLS), lambda bi, ti: (bi, ti, 0)),
                  vec, mat(DECAY_LORA), vec, mat(AAA_LORA), mat(GATE_LORA), vec, vec, vec, vec, vec],
        out_specs=pl.BlockSpec((None, tb, w), lambda bi, ti: (bi, ti, 0)),
        out_shape=jax.ShapeDtypeStruct((b, t, w), BF16),
        scratch_shapes=[pltpu.VMEM((RW_HEADS, RW_HEAD_DIM, RW_HEAD_DIM), F32)]
        + [pltpu.VMEM((tb, w), F32)] * 8,
        compiler_params=_params("parallel", "arbitrary"),
        name="rwkv",
    )(rw, w0, w2, a0, a2, g2, k_k, k_a, r_k.reshape(1, w), ln_w, ln_b)


def _pack_rows(x):
    half = x.shape[1] // 2
    hi = pltpu.bitcast(x[:, :half].astype(BF16).astype(F32), jnp.uint32)
    lo = pltpu.bitcast(x[:, half:].astype(BF16).astype(F32), jnp.uint32)
    return hi | (lo >> 16)


def _unpack_rows(u):
    hi = pltpu.bitcast(u & jnp.uint32(0xFFFF0000), F32)
    lo = pltpu.bitcast(u << 16, F32)
    return jnp.concatenate([hi, lo], axis=1)


ROW_SLAB = 4


def _store_rows(ref, u, r0=0):
    n = u.shape[0]
    for c in range(ROW_SLAB):
        ref[pl.ds(r0 * ROW_SLAB + c, n, stride=ROW_SLAB), :] = u[:, c * LANES:(c + 1) * LANES]


def _load_rows(ref, n, r0=0):
    return jnp.concatenate([ref[pl.ds(r0 * ROW_SLAB + c, n, stride=ROW_SLAB), :]
                            for c in range(ROW_SLAB)], axis=1)


def _out_kernel(yda_ref, yrw_ref, x_ref, mod_ref, wo_ref, pmn_ref, pfn_ref, rw_ref, rb_ref,
                x1_ref, h2_ref, idx_ref, wgt_ref):
    y = (jnp.dot(yda_ref[...], wo_ref[0:DA_WIDTH, :], preferred_element_type=F32)
         + jnp.dot(yrw_ref[...], wo_ref[DA_WIDTH:, :], preferred_element_type=F32))
    x1 = x_ref[...] + mod_ref[2:3, :] * _rms(y, pmn_ref[...], NORM_EPS)
    x1_ref[...] = x1
    h2 = _rms(x1, pfn_ref[...], NORM_EPS) * (1.0 + mod_ref[4:5, :]) + mod_ref[3:4, :]
    _store_rows(h2_ref, _pack_rows(h2))

    h_hi = h2.astype(BF16)
    h_lo = (h2 - h_hi.astype(F32)).astype(BF16)
    rw = rw_ref[...]
    w_hi = rw.astype(BF16)
    w_lo = (rw - w_hi.astype(F32)).astype(BF16)
    logits = (jnp.dot(h_hi, w_hi, preferred_element_type=F32)
              + jnp.dot(h_hi, w_lo, preferred_element_type=F32)
              + jnp.dot(h_lo, w_hi, preferred_element_type=F32)) + rb_ref[...]

    lane = lax.broadcasted_iota(jnp.int32, logits.shape, 1)
    slot = lax.broadcasted_iota(jnp.int32, idx_ref.shape, 1)
    idx = jnp.zeros(idx_ref.shape, jnp.int32)
    val = jnp.zeros(idx_ref.shape, F32)
    top = None
    for j in range(TOP_K):
        m = jnp.max(logits, axis=-1, keepdims=True)
        i = jnp.min(jnp.where(logits == m, lane, N_EXPERTS), axis=-1, keepdims=True)
        top = m if top is None else top
        idx = jnp.where(slot == j, i, idx)
        val = jnp.where(slot == j, jnp.exp(m - top), val)
        logits = jnp.where(lane == i, -jnp.inf, logits)
    idx_ref[...] = idx
    wgt_ref[...] = val / jnp.sum(val, axis=-1, keepdims=True)


def _out(y_da, y_rw, x, mod3, w_out_b, post_mix_norm, pre_ffn_norm, router_w, router_b):
    b, t, d = x.shape
    tm = min(OUT_ROWS, t)
    e = router_w.shape[1]
    blk = lambda w: pl.BlockSpec((None, tm, w), lambda bi, ti: (bi, ti, 0))
    full = lambda r, c: pl.BlockSpec((r, c), lambda bi, ti: (0, 0))
    return pl.pallas_call(
        _out_kernel,
        grid=(b, t // tm),
        in_specs=[blk(DA_WIDTH), blk(RW_WIDTH), blk(d),
                  pl.BlockSpec((None, N_MOD, d), lambda bi, ti: (bi, 0, 0)),
                  full(d, d), full(1, d), full(1, d), full(d, e), full(1, e)],
        out_specs=[blk(d),
                   pl.BlockSpec((None, tm * ROW_SLAB, LANES), lambda bi, ti: (bi, ti, 0)),
                   blk(TOP_K), blk(TOP_K)],
        out_shape=[jax.ShapeDtypeStruct((b, t, d), F32),
                   jax.ShapeDtypeStruct((b, t * ROW_SLAB, LANES), jnp.uint32),
                   jax.ShapeDtypeStruct((b, t, TOP_K), jnp.int32),
                   jax.ShapeDtypeStruct((b, t, TOP_K), F32)],
        compiler_params=_params("parallel", "parallel"),
        name="out",
    )(y_da, y_rw, x, mod3, w_out_b, post_mix_norm, pre_ffn_norm, router_w,
      router_b.reshape(1, e))


def _route(top_idx, rows_per_tile, n_tiles):
    experts = jnp.arange(N_EXPERTS, dtype=jnp.int32)
    chosen = (top_idx[:, :, None] == experts[None, None, :]).astype(jnp.int32)
    member = jnp.sum(chosen, axis=1)
    csum = jnp.cumsum(member, axis=0)
    counts = csum[-1]
    padded = (counts + rows_per_tile - 1) // rows_per_tile * rows_per_tile
    ends = jnp.cumsum(padded)
    starts = ends - padded
    pos = jnp.sum((csum - member + starts[None, :])[:, None, :] * chosen, axis=2).reshape(-1)
    n_active = ends[-1] // rows_per_tile
    tile_start = jnp.arange(n_tiles, dtype=jnp.int32) * rows_per_tile
    tile = jnp.minimum(tile_start, ends[-1] - 1)
    tile_expert = jnp.sum((tile[:, None] >= ends[None, :]).astype(jnp.int32), axis=1)
    used = padded > 0
    later = lax.cummin(jnp.where(used, experts, N_EXPERTS), reverse=True)
    following = jnp.concatenate([later[1:], jnp.full((1,), N_EXPERTS, jnp.int32)])
    following = jnp.where(following == N_EXPERTS, -1, following)
    run_index = jnp.cumsum(used.astype(jnp.int32)) - 1
    plan = (tile_expert.astype(jnp.int32), n_active.reshape(1).astype(jnp.int32),
            following[tile_expert].astype(jnp.int32), (run_index[tile_expert] % 2).astype(jnp.int32))
    pads = ((starts + counts).astype(jnp.int32), (padded - counts).astype(jnp.int32))
    return pos.astype(jnp.int32), pads, plan


def _dispatch_kernel(ps_ref, pl_ref, pos_ref, h_ref, xs_ref, zero_ref, sem, pad_sem):
    tb = h_ref.shape[0] // ROW_SLAB

    def issue(t, carry):
        src = h_ref.at[pl.ds(pl.multiple_of(t * ROW_SLAB, ROW_SLAB), ROW_SLAB)]
        for j in range(TOP_K):
            pltpu.make_async_copy(src, xs_ref.at[pos_ref[0, t * TOP_K + j]],
                                  sem).start(priority=j % 2)
        return carry

    lax.fori_loop(0, tb, issue, 0, unroll=8)

    @pl.when(pl.program_id(0) == pl.num_programs(0) - 1)
    def _():
        zero_ref[...] = jnp.zeros_like(zero_ref)
        bits = [1 << k for k in reversed(range(zero_ref.shape[0].bit_length()))]

        def pad_copy(e, b):
            below = pl_ref[e] & ~(2 * b - 1)
            return pltpu.make_async_copy(zero_ref.at[pl.ds(0, b)],
                                         xs_ref.at[pl.ds(ps_ref[e] + below, b)], pad_sem)

        for wait in (False, True):
            for e in range(N_EXPERTS):
                for b in bits:
                    @pl.when((pl_ref[e] & b) != 0)
                    def _(e=e, b=b):
                        if wait:
                            pad_copy(e, b).wait()
                        else:
                            pad_copy(e, b).start()

    for j in range(TOP_K):
        pltpu.make_async_copy(xs_ref.at[pl.ds(0, tb)], xs_ref.at[pl.ds(0, tb)], sem).wait()


def _dispatch(pos, pads, h2p, n_rows):
    pad_start, pad_len = pads
    n = h2p.shape[0] // ROW_SLAB
    w = h2p.shape[1]
    tb = min(DISPATCH_TOKENS, n)
    pos3 = pos.reshape(n // tb, 1, tb * TOP_K)
    return pl.pallas_call(
        _dispatch_kernel,
        grid_spec=pltpu.PrefetchScalarGridSpec(
            num_scalar_prefetch=2,
            grid=(n // tb,),
            in_specs=[pl.BlockSpec((None, 1, tb * TOP_K), lambda i, ps, pn: (i, 0, 0),
                                   memory_space=pltpu.SMEM),
                      pl.BlockSpec((tb * ROW_SLAB, w), lambda i, ps, pn: (i, 0))],
            out_specs=pl.BlockSpec(memory_space=pl.ANY),
            scratch_shapes=[pltpu.VMEM((EXPERT_ROWS // 2, ROW_SLAB, w), h2p.dtype),
                            pltpu.SemaphoreType.DMA(()), pltpu.SemaphoreType.DMA(())]),
        out_shape=jax.ShapeDtypeStruct((n_rows, ROW_SLAB, w), h2p.dtype),
        compiler_params=_params("arbitrary"),
        name="dispatch",
    )(pad_start, pad_len, pos3, h2p)


def _expert_kernel(te_ref, na_ref, nx_ref, sl_ref, xs_ref, w1_hbm, b1_ref, w2_hbm, b2_ref, ys_ref,
                   w1p_s, b1p_s, w2b_s, act_s, w1_buf, w2_buf, sem):
    i = pl.program_id(0)
    active = i < na_ref[0]
    expert = te_ref[i]
    fresh = jnp.logical_or(i == 0, expert != te_ref[jnp.maximum(i - 1, 0)])
    slot = sl_ref[i]
    grp = 2 * LANES
    n_grp = w1_buf.shape[2] // grp

    def weight_copies(ex, s):
        return (pltpu.make_async_copy(w1_hbm.at[ex], w1_buf.at[s], sem.at[0, s]),
                pltpu.make_async_copy(w2_hbm.at[ex], w2_buf.at[s], sem.at[1, s]))

    @pl.when(jnp.logical_and(active, i == 0))
    def _():
        for cp in weight_copies(expert, slot):
            cp.start()

    @pl.when(jnp.logical_and(active, fresh))
    def _():
        for cp in weight_copies(expert, slot):
            cp.wait()

        @pl.when(nx_ref[i] >= 0)
        def _():
            for cp in weight_copies(nx_ref[i], 1 - slot):
                cp.start()

        src = lax.broadcasted_iota(jnp.int32, (grp, grp), 0)
        dst = lax.broadcasted_iota(jnp.int32, (grp, grp), 1)
        perm = jnp.where(src == jnp.where(dst < LANES, 2 * dst, 2 * (dst - LANES) + 1),
                         1.0, 0.0).astype(BF16)
        for g in range(n_grp):
            sl = slice(g * grp, (g + 1) * grp)
            w1p_s[:, sl] = jnp.dot(w1_buf[slot, :, sl].astype(BF16), perm,
                                   preferred_element_type=F32).astype(BF16)
            b = b1_ref[:, sl]
            b_hi = b.astype(BF16)
            b_lo = (b - b_hi.astype(F32)).astype(BF16)
            b1p_s[:, sl] = (jnp.dot(b_hi, perm, preferred_element_type=F32)
                            + jnp.dot(b_lo, perm, preferred_element_type=F32))
        w2b_s[...] = w2_buf[slot].astype(BF16)

    @pl.when(active)
    def _():
        x = _unpack_rows(_load_rows(xs_ref, act_s.shape[0])).astype(BF16)
        hid = jnp.dot(x, w1p_s[...], preferred_element_type=F32) + b1p_s[0:1, :]
        for g in range(n_grp):
            glu = jnp.minimum(hid[:, g * grp:g * grp + LANES], SWIGLU_LIMIT)
            lin = jnp.clip(hid[:, g * grp + LANES:(g + 1) * grp], -SWIGLU_LIMIT, SWIGLU_LIMIT)
            act_s[:, g * LANES:(g + 1) * LANES] = (
                glu * jax.nn.sigmoid(SWIGLU_ALPHA * glu) * (lin + 1.0)).astype(BF16)
        y = jnp.dot(act_s[...], w2b_s[...], preferred_element_type=F32) + b2_ref[...]
        _store_rows(ys_ref, _pack_rows(y))

    @pl.when(jnp.logical_not(active))
    def _():
        ys_ref[...] = jnp.zeros_like(ys_ref)


def _experts(plan, xs, w1, b1, w2, b2):
    tile_expert, n_active, next_expert, tile_slot = plan
    n_rows, w = xs.shape[0] // ROW_SLAB, xs.shape[1]
    tm = EXPERT_ROWS
    d, f2 = w1.shape[1], w1.shape[2]
    f = f2 // 2
    wspec = lambda r, c: pl.BlockSpec((None, r, c), lambda i, te, na, nx, sl: (te[i], 0, 0))
    rows = pl.BlockSpec((tm * ROW_SLAB, w), lambda i, te, na, nx, sl: (i, 0))
    rows_in = pl.BlockSpec((tm * ROW_SLAB, w),
                           lambda i, te, na, nx, sl: (jnp.minimum(i, na[0] - 1), 0))
    hbm = pl.BlockSpec(memory_space=pl.ANY)
    return pl.pallas_call(
        _expert_kernel,
        grid_spec=pltpu.PrefetchScalarGridSpec(
            num_scalar_prefetch=4,
            grid=(n_rows // tm,),
            in_specs=[rows_in, hbm, wspec(8, f2), hbm, wspec(1, d)],
            out_specs=rows,
            scratch_shapes=[pltpu.VMEM((d, f2), BF16), pltpu.VMEM((8, f2), F32),
                            pltpu.VMEM((f, d), BF16), pltpu.VMEM((tm, f), BF16),
                            pltpu.VMEM((2, d, f2), F32), pltpu.VMEM((2, f, d), F32),
                            pltpu.SemaphoreType.DMA((2, 2))]),
        out_shape=jax.ShapeDtypeStruct((n_rows * ROW_SLAB, w), jnp.uint32),
        compiler_params=_params("arbitrary"),
        name="expert",
    )(tile_expert, n_active, next_expert, tile_slot, xs, w1, b1, w2, b2)


def _combine_kernel(pos_ref, nxt_ref, ys_ref, wgt_ref, x1_ref, mod_ref, nw_ref, o_ref, buf_ref,
                    sem):
    tc = x1_ref.shape[0]
    step = pl.program_id(0) * pl.num_programs(1) + pl.program_id(1)
    last = pl.num_programs(0) * pl.num_programs(1) - 1
    slot = step % 2

    def gather(p_ref, s):
        def issue(t, carry):
            dst = pl.ds(pl.multiple_of(t * ROW_SLAB, ROW_SLAB), ROW_SLAB)
            for j in range(TOP_K):
                pltpu.make_async_copy(ys_ref.at[p_ref[0, t * TOP_K + j]], buf_ref.at[s, j, dst],
                                      sem.at[s]).start(priority=j % 2)
            return carry

        lax.fori_loop(0, tc, issue, 0, unroll=8)

    @pl.when(step == 0)
    def _():
        gather(pos_ref, slot)

    @pl.when(step < last)
    def _():
        gather(nxt_ref, 1 - slot)

    for j in range(TOP_K):
        pltpu.make_async_copy(ys_ref.at[pl.ds(0, tc)], ys_ref.at[pl.ds(0, tc)], sem.at[slot]).wait()

    sub = min(COMBINE_SUB_ROWS, tc)

    def mix(i, carry):
        r0 = pl.multiple_of(i * sub, sub)
        wgt = wgt_ref[pl.ds(r0, sub), :]
        acc = jnp.zeros((sub, x1_ref.shape[1]), F32)
        for j in range(TOP_K):
            acc = acc + wgt[:, j:j + 1] * _unpack_rows(_load_rows(buf_ref.at[slot, j], sub, r0))
        o_ref[pl.ds(r0, sub), :] = (x1_ref[pl.ds(r0, sub), :]
                                    + mod_ref[5:6, :] * _rms(acc, nw_ref[...], NORM_EPS))
        return carry

    lax.fori_loop(0, tc // sub, mix, 0)


def _combine(pos, ys, wgt, x1, mod3, post_ffn_norm):
    b, t, d = x1.shape
    tc = min(COMBINE_TOKENS, t)
    nt = t // tc
    pos3 = pos.reshape(b * nt, 1, tc * TOP_K)
    blk = lambda w: pl.BlockSpec((None, tc, w), lambda bi, ti: (bi, ti, 0))
    return pl.pallas_call(
        _combine_kernel,
        grid=(b, nt),
        in_specs=[pl.BlockSpec((None, 1, tc * TOP_K), lambda bi, ti: (bi * nt + ti, 0, 0),
                               memory_space=pltpu.SMEM),
                  pl.BlockSpec((None, 1, tc * TOP_K),
                               lambda bi, ti: (jnp.minimum(bi * nt + ti + 1, b * nt - 1), 0, 0),
                               memory_space=pltpu.SMEM),
                  pl.BlockSpec(memory_space=pl.ANY),
                  blk(TOP_K), blk(d),
                  pl.BlockSpec((None, N_MOD, d), lambda bi, ti: (bi, 0, 0)),
                  pl.BlockSpec((1, d), lambda bi, ti: (0, 0))],
        out_specs=blk(d),
        out_shape=jax.ShapeDtypeStruct((b, t, d), F32),
        scratch_shapes=[pltpu.VMEM((2, TOP_K, tc * ROW_SLAB, LANES), jnp.uint32),
                        pltpu.SemaphoreType.DMA((2,))],
        compiler_params=_params("arbitrary", "arbitrary"),
        name="combine",
    )(pos3, pos3, ys.reshape(ys.shape[0] // ROW_SLAB, ROW_SLAB, LANES), wgt, x1, mod3,
      post_ffn_norm)


def _stages(x, c, positions, ada_w, ada_b, pre_mix_norm, post_mix_norm, pre_ffn_norm,
            post_ffn_norm, w_in, w_out, da_lambda_q1, da_lambda_k1, da_lambda_q2, da_lambda_k2,
            da_subln, rw_mu, rw_w0, rw_w2, rw_a0, rw_a2, rw_g2, rw_k_k, rw_k_a, rw_r_k, rw_ln_w,
            rw_ln_b, router_w, router_b, moe_w1, moe_b1, moe_w2, moe_b2):
    b, t, d = x.shape
    res = {}
    lambda_init = 0.8 - 0.6 * math.exp(-0.3 * 0)
    mod = _mod(c, ada_w[0], ada_b[0])
    res["mod"] = mod
    mod3 = mod.reshape(b, N_MOD, d)
    inv_freq = ROPE_THETA ** (-jnp.arange(0, ROPE_DIM, 2, dtype=F32) / ROPE_DIM)
    invf = jnp.tile(inv_freq, LANES // (ROPE_DIM // 2)).reshape(1, LANES)
    q, k, v, rw = _proj(x, positions.reshape(b, t, 1), mod3, pre_mix_norm, invf,
                        w_in[0].astype(BF16), rw_mu)
    res.update(q=q, k=k, v=v, rw=rw)
    lam4 = jnp.concatenate([da_lambda_q1, da_lambda_k1, da_lambda_q2, da_lambda_k2], axis=0)
    y_da = _attn(q, k, v, lam4, da_subln, lambda_init)
    res["y_da"] = y_da
    y_rw = _rwkv(rw, rw_w0, rw_w2[0], rw_a0, rw_a2[0], rw_g2[0], rw_k_k, rw_k_a, rw_r_k[0],
                 rw_ln_w, rw_ln_b)
    res["y_rw"] = y_rw
    x1, h2p, top_idx, top_w = _out(y_da, y_rw, x, mod3, w_out[0].astype(BF16), post_mix_norm,
                                   pre_ffn_norm, router_w[0], router_b[0])
    res.update(x1=x1, top_idx=top_idx, top_w=top_w)
    n = b * t
    n_tiles = n * TOP_K // EXPERT_ROWS + N_EXPERTS
    pos, pads, plan = _route(top_idx.reshape(n, TOP_K), EXPERT_ROWS, n_tiles)
    xs = _dispatch(pos, pads, h2p.reshape(n * ROW_SLAB, LANES), n_tiles * EXPERT_ROWS)
    xs = xs.reshape(n_tiles * EXPERT_ROWS * ROW_SLAB, LANES)
    b1 = jnp.broadcast_to(moe_b1[0][:, None, :], (N_EXPERTS, 8, moe_b1.shape[-1]))
    ys = _experts(plan, xs, moe_w1[0], b1, moe_w2[0], moe_b2[0][:, None, :])
    res["final"] = _combine(pos, ys, top_w, x1, mod3, post_ffn_norm)
    return res


stages = _stages


def kernel(x, c, positions, ada_w, ada_b, pre_mix_norm, post_mix_norm, pre_ffn_norm, post_ffn_norm, w_in, w_out, da_lambda_q1, da_lambda_k1, da_lambda_q2, da_lambda_k2, da_subln, rw_mu, rw_w0, rw_w2, rw_a0, rw_a2, rw_g2, rw_k_k, rw_k_a, rw_r_k, rw_ln_w, rw_ln_b, router_w, router_b, moe_w1, moe_b1, moe_w2, moe_b2):
    res = _stages(x, c, positions, ada_w, ada_b, pre_mix_norm, post_mix_norm, pre_ffn_norm,
                  post_ffn_norm, w_in, w_out, da_lambda_q1, da_lambda_k1, da_lambda_q2,
                  da_lambda_k2, da_subln, rw_mu, rw_w0, rw_w2, rw_a0, rw_a2, rw_g2, rw_k_k,
                  rw_k_a, rw_r_k, rw_ln_w, rw_ln_b, router_w, router_b, moe_w1, moe_b1,
                  moe_w2, moe_b2)
    return res["final"]
```

```python
import functools
import math

import jax
import jax.numpy as jnp
from jax import lax
from jax.experimental import pallas as pl
from jax.experimental.pallas import tpu as pltpu

F32 = jnp.float32
BF16 = jnp.bfloat16

DA_HEADS = 4
DA_HEAD_DIM = 64
DA_V_DIM = 128
DA_WIDTH = 512
RW_HEADS = 8
RW_HEAD_DIM = 64
RW_WIDTH = 512
DECAY_LORA = 64
AAA_LORA = 64
GATE_LORA = 128
DA_COLS = 1536
RW_COLS = 1792
ROPE_THETA = 500000.0
ROPE_DIM = 16
N_EXPERTS = 32
TOP_K = 4
SWIGLU_ALPHA = 1.702
SWIGLU_LIMIT = 7.0
NORM_EPS = 1e-6
SUBLN_EPS = 1e-5
LN_X_EPS = 64e-5
N_MOD = 6

LANES = 128
SUBLANES = 8
VMEM_LIMIT_BYTES = 56 * 1024 * 1024

PROJ_ROWS = 512
ATTN_BLOCK = 512
ATTN_KV_BLOCK = 512
ATTN_HEAD_GROUP = 4
RW_CHUNK = 128
RW_BLOCK = 512
OUT_ROWS = 512
EXPERT_ROWS = 512
DISPATCH_TOKENS = 2048
COMBINE_TOKENS = 1024
COMBINE_SUB_ROWS = 256


def _params(*sem):
    return pltpu.CompilerParams(dimension_semantics=sem, vmem_limit_bytes=VMEM_LIMIT_BYTES)


def _bdot(a, b):
    return jnp.dot(a.astype(BF16), b.astype(BF16), preferred_element_type=F32)


def _bdot_nt(a, b):
    return lax.dot_general(a.astype(BF16), b.astype(BF16), (((1,), (1,)), ((), ())),
                           preferred_element_type=F32)


def _bdot_tn(a, b):
    return lax.dot_general(a.astype(BF16), b.astype(BF16), (((0,), (0,)), ((), ())),
                           preferred_element_type=F32)


def _rms(x, w, eps):
    return x * lax.rsqrt(jnp.mean(x * x, axis=-1, keepdims=True) + eps) * w


def _mod_kernel(c_ref, w_ref, b_ref, o_ref):
    c = c_ref[...]
    s = c * jax.nn.sigmoid(c)
    o_ref[...] = _bdot(s, w_ref[...]) + b_ref[...]


def _mod(c, ada_w, ada_b):
    b, d = c.shape
    n = ada_w.shape[1]
    return pl.pallas_call(
        _mod_kernel,
        grid=(n // d,),
        in_specs=[pl.BlockSpec((b, d), lambda j: (0, 0)),
                  pl.BlockSpec((d, d), lambda j: (0, j)),
                  pl.BlockSpec((1, d), lambda j: (0, j))],
        out_specs=pl.BlockSpec((b, d), lambda j: (0, j)),
        out_shape=jax.ShapeDtypeStruct((b, n), F32),
        compiler_params=_params("parallel"),
        name="mod",
    )(c, ada_w, ada_b.reshape(1, n))


def _proj_kernel(x_ref, pos_ref, mod_ref, nw_ref, invf_ref, w_ref, mu_ref,
                 q_ref, k_ref, v_ref, rw_ref, carry_ref):
    ti = pl.program_id(1)

    @pl.when(ti == 0)
    def _():
        carry_ref[...] = jnp.zeros_like(carry_ref)

    x = x_ref[...]
    h = _rms(x, nw_ref[...], NORM_EPS) * (1.0 + mod_ref[1:2, :]) + mod_ref[0:1, :]
    hb = h.astype(BF16)

    ang = pos_ref[...].astype(F32) * invf_ref[...]
    cos, sin = jnp.cos(ang), jnp.sin(ang)
    l64 = lax.broadcasted_iota(jnp.int32, ang.shape, 1) % DA_HEAD_DIM
    half = ROPE_DIM // 2
    c_tab = jnp.where(l64 < ROPE_DIM, cos, 1.0)
    s_lo = jnp.where(l64 < half, -sin, 0.0)
    s_hi = jnp.where((l64 >= half) & (l64 < ROPE_DIM), sin, 0.0)

    def rope(z):
        up = pltpu.roll(z, LANES - half, axis=1)
        dn = pltpu.roll(z, half, axis=1)
        return z * c_tab + up * s_lo + dn * s_hi

    for g in range(DA_WIDTH // LANES):
        sl = slice(g * LANES, (g + 1) * LANES)
        qg = jnp.dot(hb, w_ref[:, sl], preferred_element_type=F32)
        q_ref[:, sl] = (rope(qg) * (DA_HEAD_DIM ** -0.5)).astype(q_ref.dtype)
        kg = jnp.dot(hb, w_ref[:, DA_WIDTH + g * LANES:DA_WIDTH + (g + 1) * LANES],
                     preferred_element_type=F32)
        k_ref[:, sl] = rope(kg).astype(k_ref.dtype)
    v_ref[...] = jnp.dot(hb, w_ref[:, 2 * DA_WIDTH:DA_COLS],
                         preferred_element_type=F32).astype(v_ref.dtype)

    p = jnp.dot(hb, w_ref[:, DA_COLS:], preferred_element_type=F32)
    rows = p.shape[0]
    prev = pltpu.roll(p, 1, axis=0)
    first = lax.broadcasted_iota(jnp.int32, p.shape, 0) == 0
    prev = jnp.where(first, carry_ref[0:1, :], prev)
    rw_ref[...] = p + (prev - p) * mu_ref[...]
    carry_ref[0:1, :] = p[rows - 1:rows, :]


def _proj(x, pos3, mod3, norm_w, invf, w_in_b, mu):
    b, t, d = x.shape
    tm = min(PROJ_ROWS, t)
    n_in = w_in_b.shape[1]
    blk = lambda w: pl.BlockSpec((None, tm, w), lambda bi, ti: (bi, ti, 0))
    full = lambda r, c: pl.BlockSpec((r, c), lambda bi, ti: (0, 0))
    return pl.pallas_call(
        _proj_kernel,
        grid=(b, t // tm),
        in_specs=[blk(d), blk(1),
                  pl.BlockSpec((None, N_MOD, d), lambda bi, ti: (bi, 0, 0)),
                  full(1, d), full(1, LANES), full(d, n_in), full(1, RW_COLS)],
        out_specs=[blk(DA_WIDTH), blk(DA_WIDTH), blk(DA_WIDTH), blk(RW_COLS)],
        out_shape=[jax.ShapeDtypeStruct((b, t, DA_WIDTH), BF16)] * 3
        + [jax.ShapeDtypeStruct((b, t, RW_COLS), F32)],
        scratch_shapes=[pltpu.VMEM((8, RW_COLS), F32)],
        compiler_params=_params("parallel", "arbitrary"),
        name="proj",
    )(x, pos3, mod3, norm_w, invf, w_in_b, mu)


def _attn_kernel(q_ref, k_ref, v_ref, lam_ref, subln_ref, o_ref, m_ref, acc_ref, *, lambda_init):
    qi = pl.program_id(2)
    tq = q_ref.shape[0]
    heads = range(ATTN_HEAD_GROUP)
    hs = [slice(h * DA_V_DIM, (h + 1) * DA_V_DIM) for h in heads]
    lane = lax.broadcasted_iota(jnp.int32, (tq, DA_V_DIM), 1)
    qq = []
    for c in hs:
        q = q_ref[:, c]
        zero = jnp.zeros_like(q)
        qq.append(jnp.concatenate([jnp.where(lane < DA_HEAD_DIM, q, zero),
                                   jnp.where(lane >= DA_HEAD_DIM, q, zero)], axis=0))

    m_ref[...] = jnp.full(m_ref.shape, -jnp.inf, F32)
    acc_ref[...] = jnp.zeros(acc_ref.shape, F32)
    tk = ATTN_KV_BLOCK if k_ref.shape[0] % ATTN_KV_BLOCK == 0 else tq
    rep = tk // LANES
    ones = jnp.ones((tk, DA_V_DIM), v_ref.dtype)

    def step(j, masked):
        rows = pl.ds(pl.multiple_of(j * tk, tk), tk)
        s = [lax.dot_general(qq[h], k_ref[rows, hs[h]], (((1,), (1,)), ((), ())),
                             preferred_element_type=F32) for h in heads]
        if masked:
            qpos = qi * tq + lax.broadcasted_iota(jnp.int32, s[0].shape, 0) % tq
            kpos = j * tk + lax.broadcasted_iota(jnp.int32, s[0].shape, 1)
            s = [jnp.where(qpos >= kpos, x, -jnp.inf) for x in s]
        for h in heads:
            m_old = m_ref[h]
            m_new = jnp.maximum(m_old, jnp.max(s[h], axis=-1, keepdims=True))
            alpha = jnp.exp(m_old - m_new)
            p = jnp.exp((s[h] - jnp.concatenate([m_new] * rep, axis=1)).astype(v_ref.dtype))
            v_ext = jnp.concatenate([v_ref[rows, hs[h]], ones], axis=1)
            acc_ref[h] = (jnp.concatenate([alpha, alpha], axis=1) * acc_ref[h]
                          + jnp.dot(p, v_ext, preferred_element_type=F32))
            m_ref[h] = m_new

    def body(j, carry):
        step(j, False)
        return carry

    n_full = (qi * tq) // tk
    lax.fori_loop(0, n_full, body, 0)
    step(n_full, True)

    lam = (jnp.exp(jnp.sum(lam_ref[0:1, :] * lam_ref[1:2, :], axis=-1, keepdims=True))
           - jnp.exp(jnp.sum(lam_ref[2:3, :] * lam_ref[3:4, :], axis=-1, keepdims=True))
           + lambda_init)
    for h in heads:
        o = acc_ref[h, :, :DA_V_DIM] / acc_ref[h, :, DA_V_DIM:]
        d = o[:tq, :] - lam * o[tq:, :]
        o_ref[:, hs[h]] = (_rms(d, subln_ref[...], SUBLN_EPS)
                           * (1.0 - lambda_init)).astype(o_ref.dtype)


def _attn(q, k, v, lam4, subln, lambda_init):
    b, t, _ = q.shape
    tq = min(ATTN_BLOCK, t)
    hg = ATTN_HEAD_GROUP
    gw = hg * DA_V_DIM
    return pl.pallas_call(
        functools.partial(_attn_kernel, lambda_init=lambda_init),
        grid=(b, DA_HEADS // hg, t // tq),
        in_specs=[pl.BlockSpec((None, tq, gw), lambda bi, h, qi: (bi, qi, h)),
                  pl.BlockSpec((None, t, gw), lambda bi, h, qi: (bi, 0, h)),
                  pl.BlockSpec((None, t, gw), lambda bi, h, qi: (bi, 0, h)),
                  pl.BlockSpec((4, DA_HEAD_DIM), lambda bi, h, qi: (0, 0)),
                  pl.BlockSpec((1, DA_V_DIM), lambda bi, h, qi: (0, 0))],
        out_specs=pl.BlockSpec((None, tq, gw), lambda bi, h, qi: (bi, qi, h)),
        out_shape=jax.ShapeDtypeStruct((b, t, DA_WIDTH), BF16),
        scratch_shapes=[pltpu.VMEM((hg, 2 * tq, LANES), F32),
                        pltpu.VMEM((hg, 2 * tq, 2 * DA_V_DIM), F32)],
        compiler_params=_params("parallel", "parallel", "arbitrary"),
        name="attn",
    )(q, k, v, lam4, subln)


def _rwkv_kernel(rw_ref, w0_ref, w2_ref, a0_ref, a2_ref, g2_ref, kk_ref, ka_ref, rk_ref,
                 lnw_ref, lnb_ref, o_ref, state_ref, r_s, k_s, v_s, lw_s, kk_s, a_s, g_s, cum_s):
    ti = pl.program_id(1)

    @pl.when(ti == 0)
    def _():
        state_ref[...] = jnp.zeros_like(state_ref)

    w = RW_WIDTH
    rw = rw_ref[...]
    k = rw[:, w:2 * w]
    wl = rw[:, 3 * w:3 * w + DECAY_LORA]
    al = rw[:, 3 * w + DECAY_LORA:3 * w + DECAY_LORA + AAA_LORA]
    gl = rw[:, 3 * w + DECAY_LORA + AAA_LORA:]
    z = -(w0_ref[...] + _bdot(jnp.tanh(wl), w2_ref[...]))
    softplus = jnp.maximum(z, 0.0) + jnp.log(1.0 + jnp.exp(-jnp.abs(z)))
    a = jax.nn.sigmoid(a0_ref[...] + _bdot(al, a2_ref[...]))
    r_s[...] = rw[:, 0:w]
    v_s[...] = rw[:, 2 * w:3 * w]
    lw_s[...] = -jnp.exp(-softplus - 0.5)
    a_s[...] = a
    g_s[...] = _bdot(jax.nn.sigmoid(gl), g2_ref[...])
    kk_s[...] = k * kk_ref[...]
    k_s[...] = k * (1.0 + (a - 1.0) * ka_ref[...])

    c_len = RW_CHUNK
    n = RW_HEAD_DIM
    tb = rw_ref.shape[0]

    tri = jnp.where(lax.broadcasted_iota(jnp.int32, (c_len, c_len), 0)
                    >= lax.broadcasted_iota(jnp.int32, (c_len, c_len), 1), 1.0, 0.0).astype(BF16)
    for ci in range(tb // c_len):
        rows = slice(ci * c_len, (ci + 1) * c_len)
        lw_c = lw_s[rows, :]
        lw_hi = lw_c.astype(BF16)
        rem = lw_c - lw_hi.astype(F32)
        lw_mid = rem.astype(BF16)
        lw_lo = (rem - lw_mid.astype(F32)).astype(BF16)
        cum_s[rows, :] = (jnp.dot(tri, lw_hi, preferred_element_type=F32)
                          + jnp.dot(tri, lw_mid, preferred_element_type=F32)
                          + jnp.dot(tri, lw_lo, preferred_element_type=F32))

    row = lax.broadcasted_iota(jnp.int32, (c_len, 2 * c_len), 0)
    col = lax.broadcasted_iota(jnp.int32, (c_len, 2 * c_len), 1)
    incl2 = row >= col % c_len
    strict2 = row > col % c_len
    eye = jnp.where(lax.broadcasted_iota(jnp.int32, (c_len, c_len), 0)
                    == lax.broadcasted_iota(jnp.int32, (c_len, c_len), 1), 1.0, 0.0).astype(F32)

    def chunk(ci, carry):
        rows = pl.ds(pl.multiple_of(ci * c_len, c_len), c_len)
        heads = range(RW_HEADS)
        sl = [slice(h * n, (h + 1) * n) for h in heads]
        r = [r_s[rows, c] for c in sl]
        kh = [k_s[rows, c] for c in sl]
        v = [v_s[rows, c] for c in sl]
        lw = [lw_s[rows, c] for c in sl]
        cum = [cum_s[rows, c] for c in sl]
        kk = [kk_s[rows, c] for c in sl]
        kk = [x * lax.rsqrt(jnp.maximum(jnp.sum(x * x, axis=-1, keepdims=True), 1e-24)) for x in kk]
        kka = [kk[h] * a_s[rows, sl[h]] for h in heads]
        end = [jnp.sum(x, axis=0, keepdims=True) for x in lw]
        e_neg = [jnp.exp(-x) for x in cum]
        e_end = [jnp.exp(end[h] - cum[h]) for h in heads]
        left = [jnp.concatenate([-kk[h] * jnp.exp(cum[h] - lw[h]), r[h] * jnp.exp(cum[h])], axis=0)
                for h in heads]
        g = [_bdot_nt(left[h], jnp.concatenate([kka[h] * e_neg[h], kh[h] * e_neg[h]], axis=0))
             for h in heads]
        a_a = [jnp.where(strict2, x[:c_len, :], 0.0) for x in g]
        a_r = [jnp.where(incl2, x[c_len:, :], 0.0) for x in g]
        pw = [x[:, :c_len] for x in a_a]
        inv = [eye + x for x in pw]
        for _ in range(c_len.bit_length() - 2):
            pw = [_bdot(x, x) for x in pw]
            inv = [inv[h] + _bdot(inv[h], pw[h]) for h in heads]
        akv = [_bdot(a_a[h][:, c_len:], v[h]) for h in heads]
        s0 = [state_ref[h] for h in heads]
        ls = [_bdot_nt(left[h], s0[h]) for h in heads]
        u = [_bdot(inv[h], ls[h][:c_len, :] + akv[h]) for h in heads]
        uv = [jnp.concatenate([u[h], v[h]], axis=0) for h in heads]
        y = [ls[h][c_len:, :] + _bdot(a_r[h], uv[h]) for h in heads]
        for h in heads:
            state_ref[h] = s0[h] * jnp.exp(end[h]) + _bdot_tn(
                uv[h], jnp.concatenate([kka[h] * e_end[h], kh[h] * e_end[h]], axis=0))
        for h in heads:
            mean = jnp.mean(y[h], axis=-1, keepdims=True)
            yc = y[h] - mean
            var = jnp.mean(yc * yc, axis=-1, keepdims=True)
            yn = yc * lax.rsqrt(var + LN_X_EPS) * lnw_ref[:, sl[h]] + lnb_ref[:, sl[h]]
            bonus = jnp.sum(r[h] * kh[h] * rk_ref[:, sl[h]], axis=-1, keepdims=True) * v[h]
            o_ref[rows, sl[h]] = ((yn + bonus) * g_s[rows, sl[h]]).astype(o_ref.dtype)
        return carry

    lax.fori_loop(0, rw_ref.shape[0] // c_len, chunk, 0, unroll=True)


def _rwkv(rw, w0, w2, a0, a2, g2, k_k, k_a, r_k, ln_w, ln_b):
    b, t, _ = rw.shape
    tb = min(RW_BLOCK, t)
    w = RW_WIDTH
    vec = pl.BlockSpec((1, w), lambda bi, ti: (0, 0))
    mat = lambda r: pl.BlockSpec((r, w), lambda bi, ti: (0, 0))
    return pl.pallas_call(
        _rwkv_kernel,
        grid=(b, t // tb),
        in_specs=[pl.BlockSpec((None, tb, RW_COLS), lambda bi, ti: (bi, ti, 0)),
                  vec, mat(DECAY_LORA), vec, mat(AAA_LORA), mat(GATE_LORA), vec, vec, vec, vec, vec],
        out_specs=pl.BlockSpec((None, tb, w), lambda bi, ti: (bi, ti, 0)),
        out_shape=jax.ShapeDtypeStruct((b, t, w), BF16),
        scratch_shapes=[pltpu.VMEM((RW_HEADS, RW_HEAD_DIM, RW_HEAD_DIM), F32)]
        + [pltpu.VMEM((tb, w), F32)] * 8,
        compiler_params=_params("parallel", "arbitrary"),
        name="rwkv",
    )(rw, w0, w2, a0, a2, g2, k_k, k_a, r_k.reshape(1, w), ln_w, ln_b)


def _pack_rows(x):
    half = x.shape[1] // 2
    hi = pltpu.bitcast(x[:, :half].astype(BF16).astype(F32), jnp.uint32)
    lo = pltpu.bitcast(x[:, half:].astype(BF16).astype(F32), jnp.uint32)
    return hi | (lo >> 16)


def _unpack_rows(u):
    hi = pltpu.bitcast(u & jnp.uint32(0xFFFF0000), F32)
    lo = pltpu.bitcast(u << 16, F32)
    return jnp.concatenate([hi, lo], axis=1)


ROW_SLAB = 4


def _store_rows(ref, u, r0=0):
    n = u.shape[0]
    for c in range(ROW_SLAB):
        ref[pl.ds(r0 * ROW_SLAB + c, n, stride=ROW_SLAB), :] = u[:, c * LANES:(c + 1) * LANES]


def _load_rows(ref, n, r0=0):
    return jnp.concatenate([ref[pl.ds(r0 * ROW_SLAB + c, n, stride=ROW_SLAB), :]
                            for c in range(ROW_SLAB)], axis=1)


def _out_kernel(yda_ref, yrw_ref, x_ref, mod_ref, wo_ref, pmn_ref, pfn_ref, rw_ref, rb_ref,
                x1_ref, h2_ref, idx_ref, wgt_ref):
    y = (jnp.dot(yda_ref[...], wo_ref[0:DA_WIDTH, :], preferred_element_type=F32)
         + jnp.dot(yrw_ref[...], wo_ref[DA_WIDTH:, :], preferred_element_type=F32))
    x1 = x_ref[...] + mod_ref[2:3, :] * _rms(y, pmn_ref[...], NORM_EPS)
    x1_ref[...] = x1
    h2 = _rms(x1, pfn_ref[...], NORM_EPS) * (1.0 + mod_ref[4:5, :]) + mod_ref[3:4, :]
    _store_rows(h2_ref, _pack_rows(h2))

    h_hi = h2.astype(BF16)
    h_lo = (h2 - h_hi.astype(F32)).astype(BF16)
    rw = rw_ref[...]
    w_hi = rw.astype(BF16)
    w_lo = (rw - w_hi.astype(F32)).astype(BF16)
    logits = (jnp.dot(h_hi, w_hi, preferred_element_type=F32)
              + jnp.dot(h_hi, w_lo, preferred_element_type=F32)
              + jnp.dot(h_lo, w_hi, preferred_element_type=F32)) + rb_ref[...]

    lane = lax.broadcasted_iota(jnp.int32, logits.shape, 1)
    slot = lax.broadcasted_iota(jnp.int32, idx_ref.shape, 1)
    idx = jnp.zeros(idx_ref.shape, jnp.int32)
    val = jnp.zeros(idx_ref.shape, F32)
    top = None
    for j in range(TOP_K):
        m = jnp.max(logits, axis=-1, keepdims=True)
        i = jnp.min(jnp.where(logits == m, lane, N_EXPERTS), axis=-1, keepdims=True)
        top = m if top is None else top
        idx = jnp.where(slot == j, i, idx)
        val = jnp.where(slot == j, jnp.exp(m - top), val)
        logits = jnp.where(lane == i, -jnp.inf, logits)
    idx_ref[...] = idx
    wgt_ref[...] = val / jnp.sum(val, axis=-1, keepdims=True)


def _out(y_da, y_rw, x, mod3, w_out_b, post_mix_norm, pre_ffn_norm, router_w, router_b):
    b, t, d = x.shape
    tm = min(OUT_ROWS, t)
    e = router_w.shape[1]
    blk = lambda w: pl.BlockSpec((None, tm, w), lambda bi, ti: (bi, ti, 0))
    full = lambda r, c: pl.BlockSpec((r, c), lambda bi, ti: (0, 0))
    return pl.pallas_call(
        _out_kernel,
        grid=(b, t // tm),
        in_specs=[blk(DA_WIDTH), blk(RW_WIDTH), blk(d),
                  pl.BlockSpec((None, N_MOD, d), lambda bi, ti: (bi, 0, 0)),
                  full(d, d), full(1, d), full(1, d), full(d, e), full(1, e)],
        out_specs=[blk(d),
                   pl.BlockSpec((None, tm * ROW_SLAB, LANES), lambda bi, ti: (bi, ti, 0)),
                   blk(TOP_K), blk(TOP_K)],
        out_shape=[jax.ShapeDtypeStruct((b, t, d), F32),
                   jax.ShapeDtypeStruct((b, t * ROW_SLAB, LANES), jnp.uint32),
                   jax.ShapeDtypeStruct((b, t, TOP_K), jnp.int32),
                   jax.ShapeDtypeStruct((b, t, TOP_K), F32)],
        compiler_params=_params("parallel", "parallel"),
        name="out",
    )(y_da, y_rw, x, mod3, w_out_b, post_mix_norm, pre_ffn_norm, router_w,
      router_b.reshape(1, e))


def _route(top_idx, rows_per_tile, n_tiles):
    experts = jnp.arange(N_EXPERTS, dtype=jnp.int32)
    chosen = (top_idx[:, :, None] == experts[None, None, :]).astype(jnp.int32)
    member = jnp.sum(chosen, axis=1)
    csum = jnp.cumsum(member, axis=0)
    counts = csum[-1]
    padded = (counts + rows_per_tile - 1) // rows_per_tile * rows_per_tile
    ends = jnp.cumsum(padded)
    starts = ends - padded
    pos = jnp.sum((csum - member + starts[None, :])[:, None, :] * chosen, axis=2).reshape(-1)
    n_active = ends[-1] // rows_per_tile
    tile_start = jnp.arange(n_tiles, dtype=jnp.int32) * rows_per_tile
    tile = jnp.minimum(tile_start, ends[-1] - 1)
    tile_expert = jnp.sum((tile[:, None] >= ends[None, :]).astype(jnp.int32), axis=1)
    used = padded > 0
    later = lax.cummin(jnp.where(used, experts, N_EXPERTS), reverse=True)
    following = jnp.concatenate([later[1:], jnp.full((1,), N_EXPERTS, jnp.int32)])
    following = jnp.where(following == N_EXPERTS, -1, following)
    run_index = jnp.cumsum(used.astype(jnp.int32)) - 1
    plan = (tile_expert.astype(jnp.int32), n_active.reshape(1).astype(jnp.int32),
            following[tile_expert].astype(jnp.int32), (run_index[tile_expert] % 2).astype(jnp.int32))
    pads = ((starts + counts).astype(jnp.int32), (padded - counts).astype(jnp.int32))
    return pos.astype(jnp.int32), pads, plan


def _dispatch_kernel(ps_ref, pl_ref, pos_ref, h_ref, xs_ref, zero_ref, sem, pad_sem):
    tb = h_ref.shape[0] // ROW_SLAB

    def issue(t, carry):
        src = h_ref.at[pl.ds(pl.multiple_of(t * ROW_SLAB, ROW_SLAB), ROW_SLAB)]
        for j in range(TOP_K):
            pltpu.make_async_copy(src, xs_ref.at[pos_ref[0, t * TOP_K + j]],
                                  sem).start(priority=j % 2)
        return carry

    lax.fori_loop(0, tb, issue, 0, unroll=8)

    @pl.when(pl.program_id(0) == pl.num_programs(0) - 1)
    def _():
        zero_ref[...] = jnp.zeros_like(zero_ref)
        bits = [1 << k for k in reversed(range(zero_ref.shape[0].bit_length()))]

        def pad_copy(e, b):
            below = pl_ref[e] & ~(2 * b - 1)
            return pltpu.make_async_copy(zero_ref.at[pl.ds(0, b)],
                                         xs_ref.at[pl.ds(ps_ref[e] + below, b)], pad_sem)

        for wait in (False, True):
            for e in range(N_EXPERTS):
                for b in bits:
                    @pl.when((pl_ref[e] & b) != 0)
                    def _(e=e, b=b):
                        if wait:
                            pad_copy(e, b).wait()
                        else:
                            pad_copy(e, b).start()

    for j in range(TOP_K):
        pltpu.make_async_copy(xs_ref.at[pl.ds(0, tb)], xs_ref.at[pl.ds(0, tb)], sem).wait()


def _dispatch(pos, pads, h2p, n_rows):
    pad_start, pad_len = pads
    n = h2p.shape[0] // ROW_SLAB
    w = h2p.shape[1]
    tb = min(DISPATCH_TOKENS, n)
    pos3 = pos.reshape(n // tb, 1, tb * TOP_K)
    return pl.pallas_call(
        _dispatch_kernel,
        grid_spec=pltpu.PrefetchScalarGridSpec(
            num_scalar_prefetch=2,
            grid=(n // tb,),
            in_specs=[pl.BlockSpec((None, 1, tb * TOP_K), lambda i, ps, pn: (i, 0, 0),
                                   memory_space=pltpu.SMEM),
                      pl.BlockSpec((tb * ROW_SLAB, w), lambda i, ps, pn: (i, 0))],
            out_specs=pl.BlockSpec(memory_space=pl.ANY),
            scratch_shapes=[pltpu.VMEM((EXPERT_ROWS // 2, ROW_SLAB, w), h2p.dtype),
                            pltpu.SemaphoreType.DMA(()), pltpu.SemaphoreType.DMA(())]),
        out_shape=jax.ShapeDtypeStruct((n_rows, ROW_SLAB, w), h2p.dtype),
        compiler_params=_params("arbitrary"),
        name="dispatch",
    )(pad_start, pad_len, pos3, h2p)


def _expert_kernel(te_ref, na_ref, nx_ref, sl_ref, xs_ref, w1_hbm, b1_ref, w2_hbm, b2_ref, ys_ref,
                   w1p_s, b1p_s, w2b_s, act_s, w1_buf, w2_buf, sem):
    i = pl.program_id(0)
    active = i < na_ref[0]
    expert = te_ref[i]
    fresh = jnp.logical_or(i == 0, expert != te_ref[jnp.maximum(i - 1, 0)])
    slot = sl_ref[i]
    grp = 2 * LANES
    n_grp = w1_buf.shape[2] // grp

    def weight_copies(ex, s):
        return (pltpu.make_async_copy(w1_hbm.at[ex], w1_buf.at[s], sem.at[0, s]),
                pltpu.make_async_copy(w2_hbm.at[ex], w2_buf.at[s], sem.at[1, s]))

    @pl.when(jnp.logical_and(active, i == 0))
    def _():
        for cp in weight_copies(expert, slot):
            cp.start()

    @pl.when(jnp.logical_and(active, fresh))
    def _():
        for cp in weight_copies(expert, slot):
            cp.wait()

        @pl.when(nx_ref[i] >= 0)
        def _():
            for cp in weight_copies(nx_ref[i], 1 - slot):
                cp.start()

        src = lax.broadcasted_iota(jnp.int32, (grp, grp), 0)
        dst = lax.broadcasted_iota(jnp.int32, (grp, grp), 1)
        perm = jnp.where(src == jnp.where(dst < LANES, 2 * dst, 2 * (dst - LANES) + 1),
                         1.0, 0.0).astype(BF16)
        for g in range(n_grp):
            sl = slice(g * grp, (g + 1) * grp)
            w1p_s[:, sl] = jnp.dot(w1_buf[slot, :, sl].astype(BF16), perm,
                                   preferred_element_type=F32).astype(BF16)
            b = b1_ref[:, sl]
            b_hi = b.astype(BF16)
            b_lo = (b - b_hi.astype(F32)).astype(BF16)
            b1p_s[:, sl] = (jnp.dot(b_hi, perm, preferred_element_type=F32)
                            + jnp.dot(b_lo, perm, preferred_element_type=F32))
        w2b_s[...] = w2_buf[slot].astype(BF16)

    @pl.when(active)
    def _():
        x = _unpack_rows(_load_rows(xs_ref, act_s.shape[0])).astype(BF16)
        hid = jnp.dot(x, w1p_s[...], preferred_element_type=F32) + b1p_s[0:1, :]
        for g in range(n_grp):
            glu = jnp.minimum(hid[:, g * grp:g * grp + LANES], SWIGLU_LIMIT)
            lin = jnp.clip(hid[:, g * grp + LANES:(g + 1) * grp], -SWIGLU_LIMIT, SWIGLU_LIMIT)
            act_s[:, g * LANES:(g + 1) * LANES] = (
                glu * jax.nn.sigmoid(SWIGLU_ALPHA * glu) * (lin + 1.0)).astype(BF16)
        y = jnp.dot(act_s[...], w2b_s[...], preferred_element_type=F32) + b2_ref[...]
        _store_rows(ys_ref, _pack_rows(y))

    @pl.when(jnp.logical_not(active))
    def _():
        ys_ref[...] = jnp.zeros_like(ys_ref)


def _experts(plan, xs, w1, b1, w2, b2):
    tile_expert, n_active, next_expert, tile_slot = plan
    n_rows, w = xs.shape[0] // ROW_SLAB, xs.shape[1]
    tm = EXPERT_ROWS
    d, f2 = w1.shape[1], w1.shape[2]
    f = f2 // 2
    wspec = lambda r, c: pl.BlockSpec((None, r, c), lambda i, te, na, nx, sl: (te[i], 0, 0))
    rows = pl.BlockSpec((tm * ROW_SLAB, w), lambda i, te, na, nx, sl: (i, 0))
    rows_in = pl.BlockSpec((tm * ROW_SLAB, w),
                           lambda i, te, na, nx, sl: (jnp.minimum(i, na[0] - 1), 0))
    hbm = pl.BlockSpec(memory_space=pl.ANY)
    return pl.pallas_call(
        _expert_kernel,
        grid_spec=pltpu.PrefetchScalarGridSpec(
            num_scalar_prefetch=4,
            grid=(n_rows // tm,),
            in_specs=[rows_in, hbm, wspec(8, f2), hbm, wspec(1, d)],
            out_specs=rows,
            scratch_shapes=[pltpu.VMEM((d, f2), BF16), pltpu.VMEM((8, f2), F32),
                            pltpu.VMEM((f, d), BF16), pltpu.VMEM((tm, f), BF16),
                            pltpu.VMEM((2, d, f2), F32), pltpu.VMEM((2, f, d), F32),
                            pltpu.SemaphoreType.DMA((2, 2))]),
        out_shape=jax.ShapeDtypeStruct((n_rows * ROW_SLAB, w), jnp.uint32),
        compiler_params=_params("arbitrary"),
        name="expert",
    )(tile_expert, n_active, next_expert, tile_slot, xs, w1, b1, w2, b2)


def _combine_kernel(pos_ref, nxt_ref, ys_ref, wgt_ref, x1_ref, mod_ref, nw_ref, o_ref, buf_ref,
                    sem):
    tc = x1_ref.shape[0]
    step = pl.program_id(0) * pl.num_programs(1) + pl.program_id(1)
    last = pl.num_programs(0) * pl.num_programs(1) - 1
    slot = step % 2

    def gather(p_ref, s):
        def issue(t, carry):
            dst = pl.ds(pl.multiple_of(t * ROW_SLAB, ROW_SLAB), ROW_SLAB)
            for j in range(TOP_K):
                pltpu.make_async_copy(ys_ref.at[p_ref[0, t * TOP_K + j]], buf_ref.at[s, j, dst],
                                      sem.at[s]).start(priority=j % 2)
            return carry

        lax.fori_loop(0, tc, issue, 0, unroll=8)

    @pl.when(step == 0)
    def _():
        gather(pos_ref, slot)

    @pl.when(step < last)
    def _():
        gather(nxt_ref, 1 - slot)

    for j in range(TOP_K):
        pltpu.make_async_copy(ys_ref.at[pl.ds(0, tc)], ys_ref.at[pl.ds(0, tc)], sem.at[slot]).wait()

    sub = min(COMBINE_SUB_ROWS, tc)

    def mix(i, carry):
        r0 = pl.multiple_of(i * sub, sub)
        wgt = wgt_ref[pl.ds(r0, sub), :]
        acc = jnp.zeros((sub, x1_ref.shape[1]), F32)
        for j in range(TOP_K):
            acc = acc + wgt[:, j:j + 1] * _unpack_rows(_load_rows(buf_ref.at[slot, j], sub, r0))
        o_ref[pl.ds(r0, sub), :] = (x1_ref[pl.ds(r0, sub), :]
                                    + mod_ref[5:6, :] * _rms(acc, nw_ref[...], NORM_EPS))
        return carry

    lax.fori_loop(0, tc // sub, mix, 0)


def _combine(pos, ys, wgt, x1, mod3, post_ffn_norm):
    b, t, d = x1.shape
    tc = min(COMBINE_TOKENS, t)
    nt = t // tc
    pos3 = pos.reshape(b * nt, 1, tc * TOP_K)
    blk = lambda w: pl.BlockSpec((None, tc, w), lambda bi, ti: (bi, ti, 0))
    return pl.pallas_call(
        _combine_kernel,
        grid=(b, nt),
        in_specs=[pl.BlockSpec((None, 1, tc * TOP_K), lambda bi, ti: (bi * nt + ti, 0, 0),
                               memory_space=pltpu.SMEM),
                  pl.BlockSpec((None, 1, tc * TOP_K),
                               lambda bi, ti: (jnp.minimum(bi * nt + ti + 1, b * nt - 1), 0, 0),
                               memory_space=pltpu.SMEM),
                  pl.BlockSpec(memory_space=pl.ANY),
                  blk(TOP_K), blk(d),
                  pl.BlockSpec((None, N_MOD, d), lambda bi, ti: (bi, 0, 0)),
                  pl.BlockSpec((1, d), lambda bi, ti: (0, 0))],
        out_specs=blk(d),
        out_shape=jax.ShapeDtypeStruct((b, t, d), F32),
        scratch_shapes=[pltpu.VMEM((2, TOP_K, tc * ROW_SLAB, LANES), jnp.uint32),
                        pltpu.SemaphoreType.DMA((2,))],
        compiler_params=_params("arbitrary", "arbitrary"),
        name="combine",
    )(pos3, pos3, ys.reshape(ys.shape[0] // ROW_SLAB, ROW_SLAB, LANES), wgt, x1, mod3,
      post_ffn_norm)


def _stages(x, c, positions, ada_w, ada_b, pre_mix_norm, post_mix_norm, pre_ffn_norm,
            post_ffn_norm, w_in, w_out, da_lambda_q1, da_lambda_k1, da_lambda_q2, da_lambda_k2,
            da_subln, rw_mu, rw_w0, rw_w2, rw_a0, rw_a2, rw_g2, rw_k_k, rw_k_a, rw_r_k, rw_ln_w,
            rw_ln_b, router_w, router_b, moe_w1, moe_b1, moe_w2, moe_b2):
    b, t, d = x.shape
    res = {}
    lambda_init = 0.8 - 0.6 * math.exp(-0.3 * 0)
    mod = _mod(c, ada_w[0], ada_b[0])
    res["mod"] = mod
    mod3 = mod.reshape(b, N_MOD, d)
    inv_freq = ROPE_THETA ** (-jnp.arange(0, ROPE_DIM, 2, dtype=F32) / ROPE_DIM)
    invf = jnp.tile(inv_freq, LANES // (ROPE_DIM // 2)).reshape(1, LANES)
    q, k, v, rw = _proj(x, positions.reshape(b, t, 1), mod3, pre_mix_norm, invf,
                        w_in[0].astype(BF16), rw_mu)
    res.update(q=q, k=k, v=v, rw=rw)
    lam4 = jnp.concatenate([da_lambda_q1, da_lambda_k1, da_lambda_q2, da_lambda_k2], axis=0)
    y_da = _attn(q, k, v, lam4, da_subln, lambda_init)
    res["y_da"] = y_da
    y_rw = _rwkv(rw, rw_w0, rw_w2[0], rw_a0, rw_a2[0], rw_g2[0], rw_k_k, rw_k_a, rw_r_k[0],
                 rw_ln_w, rw_ln_b)
    res["y_rw"] = y_rw
    x1, h2p, top_idx, top_w = _out(y_da, y_rw, x, mod3, w_out[0].astype(BF16), post_mix_norm,
                                   pre_ffn_norm, router_w[0], router_b[0])
    res.update(x1=x1, top_idx=top_idx, top_w=top_w)
    n = b * t
    n_tiles = n * TOP_K // EXPERT_ROWS + N_EXPERTS
    pos, pads, plan = _route(top_idx.reshape(n, TOP_K), EXPERT_ROWS, n_tiles)
    xs = _dispatch(pos, pads, h2p.reshape(n * ROW_SLAB, LANES), n_tiles * EXPERT_ROWS)
    xs = xs.reshape(n_tiles * EXPERT_ROWS * ROW_SLAB, LANES)
    b1 = jnp.broadcast_to(moe_b1[0][:, None, :], (N_EXPERTS, 8, moe_b1.shape[-1]))
    ys = _experts(plan, xs, moe_w1[0], b1, moe_w2[0], moe_b2[0][:, None, :])
    res["final"] = _combine(pos, ys, top_w, x1, mod3, post_ffn_norm)
    return res


stages = _stages


def kernel(x, c, positions, ada_w, ada_b, pre_mix_norm, post_mix_norm, pre_ffn_norm, post_ffn_norm, w_in, w_out, da_lambda_q1, da_lambda_k1, da_lambda_q2, da_lambda_k2, da_subln, rw_mu, rw_w0, rw_w2, rw_a0, rw_a2, rw_g2, rw_k_k, rw_k_a, rw_r_k, rw_ln_w, rw_ln_b, router_w, router_b, moe_w1, moe_b1, moe_w2, moe_b2):
    res = _stages(x, c, positions, ada_w, ada_b, pre_mix_norm, post_mix_norm, pre_ffn_norm,
                  post_ffn_norm, w_in, w_out, da_lambda_q1, da_lambda_k1, da_lambda_q2,
                  da_lambda_k2, da_subln, rw_mu, rw_w0, rw_w2, rw_a0, rw_a2, rw_g2, rw_k_k,
                  rw_k_a, rw_r_k, rw_ln_w, rw_ln_b, router_w, router_b, moe_w1, moe_b1,
                  moe_w2, moe_b2)
    return res["final"]
```

```python
import functools
import math

import jax
import jax.numpy as jnp
from jax import lax
from jax.experimental import pallas as pl
from jax.experimental.pallas import tpu as pltpu

F32 = jnp.float32
BF16 = jnp.bfloat16

DA_HEADS = 4
DA_HEAD_DIM = 64
DA_V_DIM = 128
DA_WIDTH = 512
RW_HEADS = 8
RW_HEAD_DIM = 64
RW_WIDTH = 512
DECAY_LORA = 64
AAA_LORA = 64
GATE_LORA = 128
DA_COLS = 1536
RW_COLS = 1792
ROPE_THETA = 500000.0
ROPE_DIM = 16
N_EXPERTS = 32
TOP_K = 4
SWIGLU_ALPHA = 1.702
SWIGLU_LIMIT = 7.0
NORM_EPS = 1e-6
SUBLN_EPS = 1e-5
LN_X_EPS = 64e-5
N_MOD = 6

LANES = 128
SUBLANES = 8
VMEM_LIMIT_BYTES = 56 * 1024 * 1024

PROJ_ROWS = 512
ATTN_BLOCK = 512
ATTN_KV_BLOCK = 512
ATTN_HEAD_GROUP = 4
RW_CHUNK = 128
RW_BLOCK = 512
OUT_ROWS = 512
EXPERT_ROWS = 512
DISPATCH_TOKENS = 2048
COMBINE_TOKENS = 1024
COMBINE_SUB_ROWS = 256


def _params(*sem):
    return pltpu.CompilerParams(dimension_semantics=sem, vmem_limit_bytes=VMEM_LIMIT_BYTES)


def _bdot(a, b):
    return jnp.dot(a.astype(BF16), b.astype(BF16), preferred_element_type=F32)


def _bdot_nt(a, b):
    return lax.dot_general(a.astype(BF16), b.astype(BF16), (((1,), (1,)), ((), ())),
                           preferred_element_type=F32)


def _bdot_tn(a, b):
    return lax.dot_general(a.astype(BF16), b.astype(BF16), (((0,), (0,)), ((), ())),
                           preferred_element_type=F32)


def _rms(x, w, eps):
    return x * lax.rsqrt(jnp.mean(x * x, axis=-1, keepdims=True) + eps) * w


def _mod_kernel(c_ref, w_ref, b_ref, o_ref):
    c = c_ref[...]
    s = c * jax.nn.sigmoid(c)
    o_ref[...] = _bdot(s, w_ref[...]) + b_ref[...]


def _mod(c, ada_w, ada_b):
    b, d = c.shape
    n = ada_w.shape[1]
    return pl.pallas_call(
        _mod_kernel,
        grid=(n // d,),
        in_specs=[pl.BlockSpec((b, d), lambda j: (0, 0)),
                  pl.BlockSpec((d, d), lambda j: (0, j)),
                  pl.BlockSpec((1, d), lambda j: (0, j))],
        out_specs=pl.BlockSpec((b, d), lambda j: (0, j)),
        out_shape=jax.ShapeDtypeStruct((b, n), F32),
        compiler_params=_params("parallel"),
        name="mod",
    )(c, ada_w, ada_b.reshape(1, n))


def _proj_kernel(x_ref, pos_ref, mod_ref, nw_ref, invf_ref, w_ref, mu_ref,
                 q_ref, k_ref, v_ref, rw_ref, carry_ref):
    ti = pl.program_id(1)

    @pl.when(ti == 0)
    def _():
        carry_ref[...] = jnp.zeros_like(carry_ref)

    x = x_ref[...]
    h = _rms(x, nw_ref[...], NORM_EPS) * (1.0 + mod_ref[1:2, :]) + mod_ref[0:1, :]
    hb = h.astype(BF16)

    ang = pos_ref[...].astype(F32) * invf_ref[...]
    cos, sin = jnp.cos(ang), jnp.sin(ang)
    l64 = lax.broadcasted_iota(jnp.int32, ang.shape, 1) % DA_HEAD_DIM
    half = ROPE_DIM // 2
    c_tab = jnp.where(l64 < ROPE_DIM, cos, 1.0)
    s_lo = jnp.where(l64 < half, -sin, 0.0)
    s_hi = jnp.where((l64 >= half) & (l64 < ROPE_DIM), sin, 0.0)

    def rope(z):
        up = pltpu.roll(z, LANES - half, axis=1)
        dn = pltpu.roll(z, half, axis=1)
        return z * c_tab + up * s_lo + dn * s_hi

    for g in range(DA_WIDTH // LANES):
        sl = slice(g * LANES, (g + 1) * LANES)
        qg = jnp.dot(hb, w_ref[:, sl], preferred_element_type=F32)
        q_ref[:, sl] = (rope(qg) * (DA_HEAD_DIM ** -0.5)).astype(q_ref.dtype)
        kg = jnp.dot(hb, w_ref[:, DA_WIDTH + g * LANES:DA_WIDTH + (g + 1) * LANES],
                     preferred_element_type=F32)
        k_ref[:, sl] = rope(kg).astype(k_ref.dtype)
    v_ref[...] = jnp.dot(hb, w_ref[:, 2 * DA_WIDTH:DA_COLS],
                         preferred_element_type=F32).astype(v_ref.dtype)

    p = jnp.dot(hb, w_ref[:, DA_COLS:], preferred_element_type=F32)
    rows = p.shape[0]
    prev = pltpu.roll(p, 1, axis=0)
    first = lax.broadcasted_iota(jnp.int32, p.shape, 0) == 0
    prev = jnp.where(first, carry_ref[0:1, :], prev)
    rw_ref[...] = p + (prev - p) * mu_ref[...]
    carry_ref[0:1, :] = p[rows - 1:rows, :]


def _proj(x, pos3, mod3, norm_w, invf, w_in_b, mu):
    b, t, d = x.shape
    tm = min(PROJ_ROWS, t)
    n_in = w_in_b.shape[1]
    blk = lambda w: pl.BlockSpec((None, tm, w), lambda bi, ti: (bi, ti, 0))
    full = lambda r, c: pl.BlockSpec((r, c), lambda bi, ti: (0, 0))
    return pl.pallas_call(
        _proj_kernel,
        grid=(b, t // tm),
        in_specs=[blk(d), blk(1),
                  pl.BlockSpec((None, N_MOD, d), lambda bi, ti: (bi, 0, 0)),
                  full(1, d), full(1, LANES), full(d, n_in), full(1, RW_COLS)],
        out_specs=[blk(DA_WIDTH), blk(DA_WIDTH), blk(DA_WIDTH), blk(RW_COLS)],
        out_shape=[jax.ShapeDtypeStruct((b, t, DA_WIDTH), BF16)] * 3
        + [jax.ShapeDtypeStruct((b, t, RW_COLS), F32)],
        scratch_shapes=[pltpu.VMEM((8, RW_COLS), F32)],
        compiler_params=_params("parallel", "arbitrary"),
        name="proj",
    )(x, pos3, mod3, norm_w, invf, w_in_b, mu)


def _attn_kernel(q_ref, k_ref, v_ref, lam_ref, subln_ref, o_ref, m_ref, acc_ref, *, lambda_init):
    qi = pl.program_id(2)
    tq = q_ref.shape[0]
    heads = range(ATTN_HEAD_GROUP)
    hs = [slice(h * DA_V_DIM, (h + 1) * DA_V_DIM) for h in heads]
    lane = lax.broadcasted_iota(jnp.int32, (tq, DA_V_DIM), 1)
    qq = []
    for c in hs:
        q = q_ref[:, c]
        zero = jnp.zeros_like(q)
        qq.append(jnp.concatenate([jnp.where(lane < DA_HEAD_DIM, q, zero),
                                   jnp.where(lane >= DA_HEAD_DIM, q, zero)], axis=0))

    m_ref[...] = jnp.full(m_ref.shape, -jnp.inf, F32)
    acc_ref[...] = jnp.zeros(acc_ref.shape, F32)
    tk = ATTN_KV_BLOCK if k_ref.shape[0] % ATTN_KV_BLOCK == 0 else tq
    rep = tk // LANES
    ones = jnp.ones((tk, DA_V_DIM), v_ref.dtype)

    def step(j, masked):
        rows = pl.ds(pl.multiple_of(j * tk, tk), tk)
        s = [lax.dot_general(qq[h], k_ref[rows, hs[h]], (((1,), (1,)), ((), ())),
                             preferred_element_type=F32) for h in heads]
        if masked:
            qpos = qi * tq + lax.broadcasted_iota(jnp.int32, s[0].shape, 0) % tq
            kpos = j * tk + lax.broadcasted_iota(jnp.int32, s[0].shape, 1)
            s = [jnp.where(qpos >= kpos, x, -jnp.inf) for x in s]
        for h in heads:
            m_old = m_ref[h]
            m_new = jnp.maximum(m_old, jnp.max(s[h], axis=-1, keepdims=True))
            alpha = jnp.exp(m_old - m_new)
            p = jnp.exp((s[h] - jnp.concatenate([m_new] * rep, axis=1)).astype(v_ref.dtype))
            v_ext = jnp.concatenate([v_ref[rows, hs[h]], ones], axis=1)
            acc_ref[h] = (jnp.concatenate([alpha, alpha], axis=1) * acc_ref[h]
                          + jnp.dot(p, v_ext, preferred_element_type=F32))
            m_ref[h] = m_new

    def body(j, carry):
        step(j, False)
        return carry

    n_full = (qi * tq) // tk
    lax.fori_loop(0, n_full, body, 0)
    step(n_full, True)

    lam = (jnp.exp(jnp.sum(lam_ref[0:1, :] * lam_ref[1:2, :], axis=-1, keepdims=True))
           - jnp.exp(jnp.sum(lam_ref[2:3, :] * lam_ref[3:4, :], axis=-1, keepdims=True))
           + lambda_init)
    for h in heads:
        o = acc_ref[h, :, :DA_V_DIM] / acc_ref[h, :, DA_V_DIM:]
        d = o[:tq, :] - lam * o[tq:, :]
        o_ref[:, hs[h]] = (_rms(d, subln_ref[...], SUBLN_EPS)
                           * (1.0 - lambda_init)).astype(o_ref.dtype)


def _attn(q, k, v, lam4, subln, lambda_init):
    b, t, _ = q.shape
    tq = min(ATTN_BLOCK, t)
    hg = ATTN_HEAD_GROUP
    gw = hg * DA_V_DIM
    return pl.pallas_call(
        functools.partial(_attn_kernel, lambda_init=lambda_init),
        grid=(b, DA_HEADS // hg, t // tq),
        in_specs=[pl.BlockSpec((None, tq, gw), lambda bi, h, qi: (bi, qi, h)),
                  pl.BlockSpec((None, t, gw), lambda bi, h, qi: (bi, 0, h)),
                  pl.BlockSpec((None, t, gw), lambda bi, h, qi: (bi, 0, h)),
                  pl.BlockSpec((4, DA_HEAD_DIM), lambda bi, h, qi: (0, 0)),
                  pl.BlockSpec((1, DA_V_DIM), lambda bi, h, qi: (0, 0))],
        out_specs=pl.BlockSpec((None, tq, gw), lambda bi, h, qi: (bi, qi, h)),
        out_shape=jax.ShapeDtypeStruct((b, t, DA_WIDTH), BF16),
        scratch_shapes=[pltpu.VMEM((hg, 2 * tq, LANES), F32),
                        pltpu.VMEM((hg, 2 * tq, 2 * DA_V_DIM), F32)],
        compiler_params=_params("parallel", "parallel", "arbitrary"),
        name="attn",
    )(q, k, v, lam4, subln)


def _rwkv_kernel(rw_ref, w0_ref, w2_ref, a0_ref, a2_ref, g2_ref, kk_ref, ka_ref, rk_ref,
                 lnw_ref, lnb_ref, o_ref, state_ref, r_s, k_s, v_s, lw_s, kk_s, a_s, g_s, cum_s):
    ti = pl.program_id(1)

    @pl.when(ti == 0)
    def _():
        state_ref[...] = jnp.zeros_like(state_ref)

    w = RW_WIDTH
    rw = rw_ref[...]
    k = rw[:, w:2 * w]
    wl = rw[:, 3 * w:3 * w + DECAY_LORA]
    al = rw[:, 3 * w + DECAY_LORA:3 * w + DECAY_LORA + AAA_LORA]
    gl = rw[:, 3 * w + DECAY_LORA + AAA_LORA:]
    z = -(w0_ref[...] + _bdot(jnp.tanh(wl), w2_ref[...]))
    softplus = jnp.maximum(z, 0.0) + jnp.log(1.0 + jnp.exp(-jnp.abs(z)))
    a = jax.nn.sigmoid(a0_ref[...] + _bdot(al, a2_ref[...]))
    r_s[...] = rw[:, 0:w]
    v_s[...] = rw[:, 2 * w:3 * w]
    lw_s[...] = -jnp.exp(-softplus - 0.5)
    a_s[...] = a
    g_s[...] = _bdot(jax.nn.sigmoid(gl), g2_ref[...])
    kk_s[...] = k * kk_ref[...]
    k_s[...] = k * (1.0 + (a - 1.0) * ka_ref[...])

    c_len = RW_CHUNK
    n = RW_HEAD_DIM
    tb = rw_ref.shape[0]

    tri = jnp.where(lax.broadcasted_iota(jnp.int32, (c_len, c_len), 0)
                    >= lax.broadcasted_iota(jnp.int32, (c_len, c_len), 1), 1.0, 0.0).astype(BF16)
    for ci in range(tb // c_len):
        rows = slice(ci * c_len, (ci + 1) * c_len)
        lw_c = lw_s[rows, :]
        lw_hi = lw_c.astype(BF16)
        rem = lw_c - lw_hi.astype(F32)
        lw_mid = rem.astype(BF16)
        lw_lo = (rem - lw_mid.astype(F32)).astype(BF16)
        cum_s[rows, :] = (jnp.dot(tri, lw_hi, preferred_element_type=F32)
                          + jnp.dot(tri, lw_mid, preferred_element_type=F32)
                          + jnp.dot(tri, lw_lo, preferred_element_type=F32))

    row = lax.broadcasted_iota(jnp.int32, (c_len, 2 * c_len), 0)
    col = lax.broadcasted_iota(jnp.int32, (c_len, 2 * c_len), 1)
    incl2 = row >= col % c_len
    strict2 = row > col % c_len
    eye = jnp.where(lax.broadcasted_iota(jnp.int32, (c_len, c_len), 0)
                    == lax.broadcasted_iota(jnp.int32, (c_len, c_len), 1), 1.0, 0.0).astype(F32)

    def chunk(ci, carry):
        rows = pl.ds(pl.multiple_of(ci * c_len, c_len), c_len)
        heads = range(RW_HEADS)
        sl = [slice(h * n, (h + 1) * n) for h in heads]
        r = [r_s[rows, c] for c in sl]
        kh = [k_s[rows, c] for c in sl]
        v = [v_s[rows, c] for c in sl]
        lw = [lw_s[rows, c] for c in sl]
        cum = [cum_s[rows, c] for c in sl]
        kk = [kk_s[rows, c] for c in sl]
        kk = [x * lax.rsqrt(jnp.maximum(jnp.sum(x * x, axis=-1, keepdims=True), 1e-24)) for x in kk]
        kka = [kk[h] * a_s[rows, sl[h]] for h in heads]
        end = [jnp.sum(x, axis=0, keepdims=True) for x in lw]
        e_neg = [jnp.exp(-x) for x in cum]
        e_end = [jnp.exp(end[h] - cum[h]) for h in heads]
        left = [jnp.concatenate([-kk[h] * jnp.exp(cum[h] - lw[h]), r[h] * jnp.exp(cum[h])], axis=0)
                for h in heads]
        g = [_bdot_nt(left[h], jnp.concatenate([kka[h] * e_neg[h], kh[h] * e_neg[h]], axis=0))
             for h in heads]
        a_a = [jnp.where(strict2, x[:c_len, :], 0.0) for x in g]
        a_r = [jnp.where(incl2, x[c_len:, :], 0.0) for x in g]
        pw = [x[:, :c_len] for x in a_a]
        inv = [eye + x for x in pw]
        for _ in range(c_len.bit_length() - 2):
            pw = [_bdot(x, x) for x in pw]
            inv = [inv[h] + _bdot(inv[h], pw[h]) for h in heads]
        akv = [_bdot(a_a[h][:, c_len:], v[h]) for h in heads]
        s0 = [state_ref[h] for h in heads]
        ls = [_bdot_nt(left[h], s0[h]) for h in heads]
        u = [_bdot(inv[h], ls[h][:c_len, :] + akv[h]) for h in heads]
        uv = [jnp.concatenate([u[h], v[h]], axis=0) for h in heads]
        y = [ls[h][c_len:, :] + _bdot(a_r[h], uv[h]) for h in heads]
        for h in heads:
            state_ref[h] = s0[h] * jnp.exp(end[h]) + _bdot_tn(
                uv[h], jnp.concatenate([kka[h] * e_end[h], kh[h] * e_end[h]], axis=0))
        for h in heads:
            mean = jnp.mean(y[h], axis=-1, keepdims=True)
            yc = y[h] - mean
            var = jnp.mean(yc * yc, axis=-1, keepdims=True)
            yn = yc * lax.rsqrt(var + LN_X_EPS) * lnw_ref[:, sl[h]] + lnb_ref[:, sl[h]]
            bonus = jnp.sum(r[h] * kh[h] * rk_ref[:, sl[h]], axis=-1, keepdims=True) * v[h]
            o_ref[rows, sl[h]] = ((yn + bonus) * g_s[rows, sl[h]]).astype(o_ref.dtype)
        return carry

    lax.fori_loop(0, rw_ref.shape[0] // c_len, chunk, 0, unroll=True)


def _rwkv(rw, w0, w2, a0, a2, g2, k_k, k_a, r_k, ln_w, ln_b):
    b, t, _ = rw.shape
    tb = min(RW_BLOCK, t)
    w = RW_WIDTH
    vec = pl.BlockSpec((1, w), lambda bi, ti: (0, 0))
    mat = lambda r: pl.BlockSpec((r, w), lambda bi, ti: (0, 0))
    return pl.pallas_call(
        _rwkv_kernel,
        grid=(b, t // tb),
        in_specs=[pl.BlockSpec((None, tb, RW_COLS), lambda bi, ti: (bi, ti, 0)),
                  vec, mat(DECAY_LORA), vec, mat(AAA_LORA), mat(GATE_LORA), vec, vec, vec, vec, vec],
        out_specs=pl.BlockSpec((None, tb, w), lambda bi, ti: (bi, ti, 0)),
        out_shape=jax.ShapeDtypeStruct((b, t, w), BF16),
        scratch_shapes=[pltpu.VMEM((RW_HEADS, RW_HEAD_DIM, RW_HEAD_DIM), F32)]
        + [pltpu.VMEM((tb, w), F32)] * 8,
        compiler_params=_params("parallel", "arbitrary"),
        name="rwkv",
    )(rw, w0, w2, a0, a2, g2, k_k, k_a, r_k.reshape(1, w), ln_w, ln_b)


def _pack_rows(x):
    half = x.shape[1] // 2
    hi = pltpu.bitcast(x[:, :half].astype(BF16).astype(F32), jnp.uint32)
    lo = pltpu.bitcast(x[:, half:].astype(BF16).astype(F32), jnp.uint32)
    return hi | (lo >> 16)


def _unpack_rows(u):
    hi = pltpu.bitcast(u & jnp.uint32(0xFFFF0000), F32)
    lo = pltpu.bitcast(u << 16, F32)
    return jnp.concatenate([hi, lo], axis=1)


ROW_SLAB = 4


def _store_rows(ref, u, r0=0):
    n = u.shape[0]
    for c in range(ROW_SLAB):
        ref[pl.ds(r0 * ROW_SLAB + c, n, stride=ROW_SLAB), :] = u[:, c * LANES:(c + 1) * LANES]


def _load_rows(ref, n, r0=0):
    return jnp.concatenate([ref[pl.ds(r0 * ROW_SLAB + c, n, stride=ROW_SLAB), :]
                            for c in range(ROW_SLAB)], axis=1)


def _out_kernel(yda_ref, yrw_ref, x_ref, mod_ref, wo_ref, pmn_ref, pfn_ref, rw_ref, rb_ref,
                x1_ref, h2_ref, idx_ref, wgt_ref):
    y = (jnp.dot(yda_ref[...], wo_ref[0:DA_WIDTH, :], preferred_element_type=F32)
         + jnp.dot(yrw_ref[...], wo_ref[DA_WIDTH:, :], preferred_element_type=F32))
    x1 = x_ref[...] + mod_ref[2:3, :] * _rms(y, pmn_ref[...], NORM_EPS)
    x1_ref[...] = x1
    h2 = _rms(x1, pfn_ref[...], NORM_EPS) * (1.0 + mod_ref[4:5, :]) + mod_ref[3:4, :]
    _store_rows(h2_ref, _pack_rows(h2))

    h_hi = h2.astype(BF16)
    h_lo = (h2 - h_hi.astype(F32)).astype(BF16)
    rw = rw_ref[...]
    w_hi = rw.astype(BF16)
    w_lo = (rw - w_hi.astype(F32)).astype(BF16)
    logits = (jnp.dot(h_hi, w_hi, preferred_element_type=F32)
              + jnp.dot(h_hi, w_lo, preferred_element_type=F32)
              + jnp.dot(h_lo, w_hi, preferred_element_type=F32)) + rb_ref[...]

    lane = lax.broadcasted_iota(jnp.int32, logits.shape, 1)
    slot = lax.broadcasted_iota(jnp.int32, idx_ref.shape, 1)
    idx = jnp.zeros(idx_ref.shape, jnp.int32)
    val = jnp.zeros(idx_ref.shape, F32)
    top = None
    for j in range(TOP_K):
        m = jnp.max(logits, axis=-1, keepdims=True)
        i = jnp.min(jnp.where(logits == m, lane, N_EXPERTS), axis=-1, keepdims=True)
        top = m if top is None else top
        idx = jnp.where(slot == j, i, idx)
        val = jnp.where(slot == j, jnp.exp(m - top), val)
        logits = jnp.where(lane == i, -jnp.inf, logits)
    idx_ref[...] = idx
    wgt_ref[...] = val / jnp.sum(val, axis=-1, keepdims=True)


def _out(y_da, y_rw, x, mod3, w_out_b, post_mix_norm, pre_ffn_norm, router_w, router_b):
    b, t, d = x.shape
    tm = min(OUT_ROWS, t)
    e = router_w.shape[1]
    nt = t // tm
    blk = lambda w: pl.BlockSpec((None, tm, w), lambda bi, ti: (bi, ti, 0))
    full = lambda r, c: pl.BlockSpec((r, c), lambda bi, ti: (0, 0))
    flat = lambda r, c: pl.BlockSpec((r, c), lambda bi, ti: (bi * nt + ti, 0))
    return pl.pallas_call(
        _out_kernel,
        grid=(b, nt),
        in_specs=[blk(DA_WIDTH), blk(RW_WIDTH), blk(d),
                  pl.BlockSpec((None, N_MOD, d), lambda bi, ti: (bi, 0, 0)),
                  full(d, d), full(1, d), full(1, d), full(d, e), full(1, e)],
        out_specs=[blk(d), flat(tm * ROW_SLAB, LANES), flat(tm, TOP_K), flat(tm, TOP_K)],
        out_shape=[jax.ShapeDtypeStruct((b, t, d), F32),
                   jax.ShapeDtypeStruct((b * t * ROW_SLAB, LANES), jnp.uint32),
                   jax.ShapeDtypeStruct((b * t, TOP_K), jnp.int32),
                   jax.ShapeDtypeStruct((b * t, TOP_K), F32)],
        compiler_params=_params("parallel", "parallel"),
        name="out",
    )(y_da, y_rw, x, mod3, w_out_b, post_mix_norm, pre_ffn_norm, router_w,
      router_b.reshape(1, e))


def _route(top_idx, rows_per_tile, n_tiles):
    experts = jnp.arange(N_EXPERTS, dtype=jnp.int32)
    chosen = (top_idx[:, :, None] == experts[None, None, :]).astype(jnp.int32)
    member = jnp.sum(chosen, axis=1)
    csum = jnp.cumsum(member, axis=0)
    counts = csum[-1]
    padded = (counts + rows_per_tile - 1) // rows_per_tile * rows_per_tile
    ends = jnp.cumsum(padded)
    starts = ends - padded
    pos = jnp.sum((csum - member + starts[None, :])[:, None, :] * chosen, axis=2).reshape(-1)
    n_active = ends[-1] // rows_per_tile
    tile_start = jnp.arange(n_tiles, dtype=jnp.int32) * rows_per_tile
    tile = jnp.minimum(tile_start, ends[-1] - 1)
    tile_expert = jnp.sum((tile[:, None] >= ends[None, :]).astype(jnp.int32), axis=1)
    used = padded > 0
    later = lax.cummin(jnp.where(used, experts, N_EXPERTS), reverse=True)
    following = jnp.concatenate([later[1:], jnp.full((1,), N_EXPERTS, jnp.int32)])
    following = jnp.where(following == N_EXPERTS, -1, following)
    run_index = jnp.cumsum(used.astype(jnp.int32)) - 1
    plan = (tile_expert.astype(jnp.int32), n_active.reshape(1).astype(jnp.int32),
            following[tile_expert].astype(jnp.int32), (run_index[tile_expert] % 2).astype(jnp.int32))
    pads = ((starts + counts).astype(jnp.int32), (padded - counts).astype(jnp.int32))
    return pos.astype(jnp.int32), pads, plan


def _dispatch_kernel(ps_ref, pl_ref, pos_ref, h_ref, xs_ref, zero_ref, sem, pad_sem):
    tb = h_ref.shape[0] // ROW_SLAB

    def issue(t, carry):
        src = h_ref.at[pl.ds(pl.multiple_of(t * ROW_SLAB, ROW_SLAB), ROW_SLAB)]
        for j in range(TOP_K):
            pltpu.make_async_copy(src, xs_ref.at[pos_ref[0, t * TOP_K + j]],
                                  sem).start(priority=j % 2)
        return carry

    lax.fori_loop(0, tb, issue, 0, unroll=8)

    @pl.when(pl.program_id(0) == pl.num_programs(0) - 1)
    def _():
        zero_ref[...] = jnp.zeros_like(zero_ref)
        bits = [1 << k for k in reversed(range(zero_ref.shape[0].bit_length()))]

        def pad_copy(e, b):
            below = pl_ref[e] & ~(2 * b - 1)
            return pltpu.make_async_copy(zero_ref.at[pl.ds(0, b)],
                                         xs_ref.at[pl.ds(ps_ref[e] + below, b)], pad_sem)

        for wait in (False, True):
            for e in range(N_EXPERTS):
                for b in bits:
                    @pl.when((pl_ref[e] & b) != 0)
                    def _(e=e, b=b):
                        if wait:
                            pad_copy(e, b).wait()
                        else:
                            pad_copy(e, b).start()

    for j in range(TOP_K):
        pltpu.make_async_copy(xs_ref.at[pl.ds(0, tb)], xs_ref.at[pl.ds(0, tb)], sem).wait()


def _dispatch(pos, pads, h2p, n_rows):
    pad_start, pad_len = pads
    n = h2p.shape[0] // ROW_SLAB
    w = h2p.shape[1]
    tb = min(DISPATCH_TOKENS, n)
    pos3 = pos.reshape(n // tb, 1, tb * TOP_K)
    return pl.pallas_call(
        _dispatch_kernel,
        grid_spec=pltpu.PrefetchScalarGridSpec(
            num_scalar_prefetch=2,
            grid=(n // tb,),
            in_specs=[pl.BlockSpec((None, 1, tb * TOP_K), lambda i, ps, pn: (i, 0, 0),
                                   memory_space=pltpu.SMEM),
                      pl.BlockSpec((tb * ROW_SLAB, w), lambda i, ps, pn: (i, 0))],
            out_specs=pl.BlockSpec(memory_space=pl.ANY),
            scratch_shapes=[pltpu.VMEM((EXPERT_ROWS // 2, ROW_SLAB, w), h2p.dtype),
                            pltpu.SemaphoreType.DMA(()), pltpu.SemaphoreType.DMA(())]),
        out_shape=jax.ShapeDtypeStruct((n_rows, ROW_SLAB, w), h2p.dtype),
        compiler_params=_params("arbitrary"),
        name="dispatch",
    )(pad_start, pad_len, pos3, h2p)


def _expert_kernel(te_ref, na_ref, nx_ref, sl_ref, xs_ref, w1_hbm, b1_ref, w2_hbm, b2_ref, ys_ref,
                   w1p_s, b1p_s, w2b_s, act_s, w1_buf, w2_buf, sem):
    i = pl.program_id(0)
    active = i < na_ref[0]
    expert = te_ref[i]
    fresh = jnp.logical_or(i == 0, expert != te_ref[jnp.maximum(i - 1, 0)])
    slot = sl_ref[i]
    grp = 2 * LANES
    n_grp = w1_buf.shape[2] // grp

    def weight_copies(ex, s):
        return (pltpu.make_async_copy(w1_hbm.at[ex], w1_buf.at[s], sem.at[0, s]),
                pltpu.make_async_copy(w2_hbm.at[ex], w2_buf.at[s], sem.at[1, s]))

    @pl.when(jnp.logical_and(active, i == 0))
    def _():
        for cp in weight_copies(expert, slot):
            cp.start()

    @pl.when(jnp.logical_and(active, fresh))
    def _():
        for cp in weight_copies(expert, slot):
            cp.wait()

        @pl.when(nx_ref[i] >= 0)
        def _():
            for cp in weight_copies(nx_ref[i], 1 - slot):
                cp.start()

        src = lax.broadcasted_iota(jnp.int32, (grp, grp), 0)
        dst = lax.broadcasted_iota(jnp.int32, (grp, grp), 1)
        perm = jnp.where(src == jnp.where(dst < LANES, 2 * dst, 2 * (dst - LANES) + 1),
                         1.0, 0.0).astype(BF16)
        for g in range(n_grp):
            sl = slice(g * grp, (g + 1) * grp)
            w1p_s[:, sl] = jnp.dot(w1_buf[slot, :, sl].astype(BF16), perm,
                                   preferred_element_type=F32).astype(BF16)
            b = b1_ref[:, sl]
            b_hi = b.astype(BF16)
            b_lo = (b - b_hi.astype(F32)).astype(BF16)
            b1p_s[:, sl] = (jnp.dot(b_hi, perm, preferred_element_type=F32)
                            + jnp.dot(b_lo, perm, preferred_element_type=F32))
        w2b_s[...] = w2_buf[slot].astype(BF16)

    @pl.when(active)
    def _():
        x = _unpack_rows(_load_rows(xs_ref, act_s.shape[0])).astype(BF16)
        hid = jnp.dot(x, w1p_s[...], preferred_element_type=F32) + b1p_s[0:1, :]
        for g in range(n_grp):
            glu = jnp.minimum(hid[:, g * grp:g * grp + LANES], SWIGLU_LIMIT)
            lin = jnp.clip(hid[:, g * grp + LANES:(g + 1) * grp], -SWIGLU_LIMIT, SWIGLU_LIMIT)
            act_s[:, g * LANES:(g + 1) * LANES] = (
                glu * jax.nn.sigmoid(SWIGLU_ALPHA * glu) * (lin + 1.0)).astype(BF16)
        y = jnp.dot(act_s[...], w2b_s[...], preferred_element_type=F32) + b2_ref[...]
        _store_rows(ys_ref, _pack_rows(y))

    @pl.when(jnp.logical_not(active))
    def _():
        ys_ref[...] = jnp.zeros_like(ys_ref)


def _experts(plan, xs, w1, b1, w2, b2):
    tile_expert, n_active, next_expert, tile_slot = plan
    n_rows, w = xs.shape[0] // ROW_SLAB, xs.shape[1]
    tm = EXPERT_ROWS
    d, f2 = w1.shape[1], w1.shape[2]
    f = f2 // 2
    wspec = lambda r, c: pl.BlockSpec((None, r, c), lambda i, te, na, nx, sl: (te[i], 0, 0))
    rows = pl.BlockSpec((tm * ROW_SLAB, w), lambda i, te, na, nx, sl: (i, 0))
    rows_in = pl.BlockSpec((tm * ROW_SLAB, w),
                           lambda i, te, na, nx, sl: (jnp.minimum(i, na[0] - 1), 0))
    hbm = pl.BlockSpec(memory_space=pl.ANY)
    return pl.pallas_call(
        _expert_kernel,
        grid_spec=pltpu.PrefetchScalarGridSpec(
            num_scalar_prefetch=4,
            grid=(n_rows // tm,),
            in_specs=[rows_in, hbm, wspec(8, f2), hbm, wspec(1, d)],
            out_specs=rows,
            scratch_shapes=[pltpu.VMEM((d, f2), BF16), pltpu.VMEM((8, f2), F32),
                            pltpu.VMEM((f, d), BF16), pltpu.VMEM((tm, f), BF16),
                            pltpu.VMEM((2, d, f2), F32), pltpu.VMEM((2, f, d), F32),
                            pltpu.SemaphoreType.DMA((2, 2))]),
        out_shape=jax.ShapeDtypeStruct((n_rows * ROW_SLAB, w), jnp.uint32),
        compiler_params=_params("arbitrary"),
        name="expert",
    )(tile_expert, n_active, next_expert, tile_slot, xs, w1, b1, w2, b2)


def _combine_kernel(pos_ref, nxt_ref, ys_ref, wgt_ref, x1_ref, mod_ref, nw_ref, o_ref, buf_ref,
                    sem):
    tc = x1_ref.shape[0]
    step = pl.program_id(0) * pl.num_programs(1) + pl.program_id(1)
    last = pl.num_programs(0) * pl.num_programs(1) - 1
    slot = step % 2

    def gather(p_ref, s):
        def issue(t, carry):
            dst = pl.ds(pl.multiple_of(t * ROW_SLAB, ROW_SLAB), ROW_SLAB)
            for j in range(TOP_K):
                pltpu.make_async_copy(ys_ref.at[p_ref[0, t * TOP_K + j]], buf_ref.at[s, j, dst],
                                      sem.at[s]).start(priority=j % 2)
            return carry

        lax.fori_loop(0, tc, issue, 0, unroll=8)

    @pl.when(step == 0)
    def _():
        gather(pos_ref, slot)

    @pl.when(step < last)
    def _():
        gather(nxt_ref, 1 - slot)

    for j in range(TOP_K):
        pltpu.make_async_copy(ys_ref.at[pl.ds(0, tc)], ys_ref.at[pl.ds(0, tc)], sem.at[slot]).wait()

    sub = min(COMBINE_SUB_ROWS, tc)

    def mix(i, carry):
        r0 = pl.multiple_of(i * sub, sub)
        wgt = wgt_ref[pl.ds(r0, sub), :]
        acc = jnp.zeros((sub, x1_ref.shape[1]), F32)
        for j in range(TOP_K):
            acc = acc + wgt[:, j:j + 1] * _unpack_rows(_load_rows(buf_ref.at[slot, j], sub, r0))
        o_ref[pl.ds(r0, sub), :] = (x1_ref[pl.ds(r0, sub), :]
                                    + mod_ref[5:6, :] * _rms(acc, nw_ref[...], NORM_EPS))
        return carry

    lax.fori_loop(0, tc // sub, mix, 0)


def _combine(pos, ys, wgt, x1, mod3, post_ffn_norm):
    b, t, d = x1.shape
    tc = min(COMBINE_TOKENS, t)
    nt = t // tc
    pos3 = pos.reshape(b * nt, 1, tc * TOP_K)
    blk = lambda w: pl.BlockSpec((None, tc, w), lambda bi, ti: (bi, ti, 0))
    return pl.pallas_call(
        _combine_kernel,
        grid=(b, nt),
        in_specs=[pl.BlockSpec((None, 1, tc * TOP_K), lambda bi, ti: (bi * nt + ti, 0, 0),
                               memory_space=pltpu.SMEM),
                  pl.BlockSpec((None, 1, tc * TOP_K),
                               lambda bi, ti: (jnp.minimum(bi * nt + ti + 1, b * nt - 1), 0, 0),
                               memory_space=pltpu.SMEM),
                  pl.BlockSpec(memory_space=pl.ANY),
                  pl.BlockSpec((tc, TOP_K), lambda bi, ti: (bi * nt + ti, 0)), blk(d),
                  pl.BlockSpec((None, N_MOD, d), lambda bi, ti: (bi, 0, 0)),
                  pl.BlockSpec((1, d), lambda bi, ti: (0, 0))],
        out_specs=blk(d),
        out_shape=jax.ShapeDtypeStruct((b, t, d), F32),
        scratch_shapes=[pltpu.VMEM((2, TOP_K, tc * ROW_SLAB, LANES), jnp.uint32),
                        pltpu.SemaphoreType.DMA((2,))],
        compiler_params=_params("arbitrary", "arbitrary"),
        name="combine",
    )(pos3, pos3, ys.reshape(ys.shape[0] // ROW_SLAB, ROW_SLAB, LANES), wgt, x1, mod3,
      post_ffn_norm)


def _stages(x, c, positions, ada_w, ada_b, pre_mix_norm, post_mix_norm, pre_ffn_norm,
            post_ffn_norm, w_in, w_out, da_lambda_q1, da_lambda_k1, da_lambda_q2, da_lambda_k2,
            da_subln, rw_mu, rw_w0, rw_w2, rw_a0, rw_a2, rw_g2, rw_k_k, rw_k_a, rw_r_k, rw_ln_w,
            rw_ln_b, router_w, router_b, moe_w1, moe_b1, moe_w2, moe_b2):
    b, t, d = x.shape
    res = {}
    lambda_init = 0.8 - 0.6 * math.exp(-0.3 * 0)
    mod = _mod(c, ada_w[0], ada_b[0])
    res["mod"] = mod
    mod3 = mod.reshape(b, N_MOD, d)
    inv_freq = ROPE_THETA ** (-jnp.arange(0, ROPE_DIM, 2, dtype=F32) / ROPE_DIM)
    invf = jnp.tile(inv_freq, LANES // (ROPE_DIM // 2)).reshape(1, LANES)
    q, k, v, rw = _proj(x, positions.reshape(b, t, 1), mod3, pre_mix_norm, invf,
                        w_in[0].astype(BF16), rw_mu)
    res.update(q=q, k=k, v=v, rw=rw)
    lam4 = jnp.concatenate([da_lambda_q1, da_lambda_k1, da_lambda_q2, da_lambda_k2], axis=0)
    y_da = _attn(q, k, v, lam4, da_subln, lambda_init)
    res["y_da"] = y_da
    y_rw = _rwkv(rw, rw_w0, rw_w2[0], rw_a0, rw_a2[0], rw_g2[0], rw_k_k, rw_k_a, rw_r_k[0],
                 rw_ln_w, rw_ln_b)
    res["y_rw"] = y_rw
    x1, h2p, top_idx, top_w = _out(y_da, y_rw, x, mod3, w_out[0].astype(BF16), post_mix_norm,
                                   pre_ffn_norm, router_w[0], router_b[0])
    res.update(x1=x1, top_idx=top_idx, top_w=top_w)
    n = b * t
    n_tiles = n * TOP_K // EXPERT_ROWS + N_EXPERTS
    pos, pads, plan = _route(top_idx, EXPERT_ROWS, n_tiles)
    xs = _dispatch(pos, pads, h2p, n_tiles * EXPERT_ROWS)
    xs = xs.reshape(n_tiles * EXPERT_ROWS * ROW_SLAB, LANES)
    b1 = jnp.broadcast_to(moe_b1[0][:, None, :], (N_EXPERTS, 8, moe_b1.shape[-1]))
    ys = _experts(plan, xs, moe_w1[0], b1, moe_w2[0], moe_b2[0][:, None, :])
    res["final"] = _combine(pos, ys, top_w, x1, mod3, post_ffn_norm)
    return res


stages = _stages


def kernel(x, c, positions, ada_w, ada_b, pre_mix_norm, post_mix_norm, pre_ffn_norm, post_ffn_norm, w_in, w_out, da_lambda_q1, da_lambda_k1, da_lambda_q2, da_lambda_k2, da_subln, rw_mu, rw_w0, rw_w2, rw_a0, rw_a2, rw_g2, rw_k_k, rw_k_a, rw_r_k, rw_ln_w, rw_ln_b, router_w, router_b, moe_w1, moe_b1, moe_w2, moe_b2):
    res = _stages(x, c, positions, ada_w, ada_b, pre_mix_norm, post_mix_norm, pre_ffn_norm,
                  post_ffn_norm, w_in, w_out, da_lambda_q1, da_lambda_k1, da_lambda_q2,
                  da_lambda_k2, da_subln, rw_mu, rw_w0, rw_w2, rw_a0, rw_a2, rw_g2, rw_k_k,
                  rw_k_a, rw_r_k, rw_ln_w, rw_ln_b, router_w, router_b, moe_w1, moe_b1,
                  moe_w2, moe_b2)
    return res["final"]
```

```python
import functools
import math

import jax
import jax.numpy as jnp
from jax import lax
from jax.experimental import pallas as pl
from jax.experimental.pallas import tpu as pltpu

F32 = jnp.float32
BF16 = jnp.bfloat16

DA_HEADS = 4
DA_HEAD_DIM = 64
DA_V_DIM = 128
DA_WIDTH = 512
RW_HEADS = 8
RW_HEAD_DIM = 64
RW_WIDTH = 512
DECAY_LORA = 64
AAA_LORA = 64
GATE_LORA = 128
DA_COLS = 1536
RW_COLS = 1792
ROPE_THETA = 500000.0
ROPE_DIM = 16
N_EXPERTS = 32
TOP_K = 4
SWIGLU_ALPHA = 1.702
SWIGLU_LIMIT = 7.0
NORM_EPS = 1e-6
SUBLN_EPS = 1e-5
LN_X_EPS = 64e-5
N_MOD = 6

LANES = 128
SUBLANES = 8
VMEM_LIMIT_BYTES = 56 * 1024 * 1024

PROJ_ROWS = 512
ATTN_BLOCK = 512
ATTN_KV_BLOCK = 512
ATTN_HEAD_GROUP = 4
RW_CHUNK = 128
RW_BLOCK = 512
OUT_ROWS = 512
EXPERT_ROWS = 512
DISPATCH_TOKENS = 2048
COMBINE_TOKENS = 1024
COMBINE_SUB_ROWS = 256


def _params(*sem):
    return pltpu.CompilerParams(dimension_semantics=sem, vmem_limit_bytes=VMEM_LIMIT_BYTES)


def _bdot(a, b):
    return jnp.dot(a.astype(BF16), b.astype(BF16), preferred_element_type=F32)


def _bdot_nt(a, b):
    return lax.dot_general(a.astype(BF16), b.astype(BF16), (((1,), (1,)), ((), ())),
                           preferred_element_type=F32)


def _bdot_tn(a, b):
    return lax.dot_general(a.astype(BF16), b.astype(BF16), (((0,), (0,)), ((), ())),
                           preferred_element_type=F32)


def _rms(x, w, eps):
    return x * lax.rsqrt(jnp.mean(x * x, axis=-1, keepdims=True) + eps) * w


def _mod_kernel(c_ref, w_ref, b_ref, o_ref):
    c = c_ref[...]
    s = c * jax.nn.sigmoid(c)
    o_ref[...] = _bdot(s, w_ref[...]) + b_ref[...]


def _mod(c, ada_w, ada_b):
    b, d = c.shape
    n = ada_w.shape[1]
    return pl.pallas_call(
        _mod_kernel,
        grid=(n // d,),
        in_specs=[pl.BlockSpec((b, d), lambda j: (0, 0)),
                  pl.BlockSpec((d, d), lambda j: (0, j)),
                  pl.BlockSpec((1, d), lambda j: (0, j))],
        out_specs=pl.BlockSpec((b, d), lambda j: (0, j)),
        out_shape=jax.ShapeDtypeStruct((b, n), F32),
        compiler_params=_params("parallel"),
        name="mod",
    )(c, ada_w, ada_b.reshape(1, n))


def _proj_kernel(x_ref, pos_ref, mod_ref, nw_ref, invf_ref, w_ref, mu_ref,
                 q_ref, k_ref, v_ref, rw_ref, carry_ref):
    ti = pl.program_id(1)

    @pl.when(ti == 0)
    def _():
        carry_ref[...] = jnp.zeros_like(carry_ref)

    x = x_ref[...]
    h = _rms(x, nw_ref[...], NORM_EPS) * (1.0 + mod_ref[1:2, :]) + mod_ref[0:1, :]
    hb = h.astype(BF16)

    ang = pos_ref[...].astype(F32) * invf_ref[...]
    cos, sin = jnp.cos(ang), jnp.sin(ang)
    l64 = lax.broadcasted_iota(jnp.int32, ang.shape, 1) % DA_HEAD_DIM
    half = ROPE_DIM // 2
    c_tab = jnp.where(l64 < ROPE_DIM, cos, 1.0)
    s_lo = jnp.where(l64 < half, -sin, 0.0)
    s_hi = jnp.where((l64 >= half) & (l64 < ROPE_DIM), sin, 0.0)

    def rope(z):
        up = pltpu.roll(z, LANES - half, axis=1)
        dn = pltpu.roll(z, half, axis=1)
        return z * c_tab + up * s_lo + dn * s_hi

    for g in range(DA_WIDTH // LANES):
        sl = slice(g * LANES, (g + 1) * LANES)
        qg = jnp.dot(hb, w_ref[:, sl], preferred_element_type=F32)
        q_ref[:, sl] = (rope(qg) * (DA_HEAD_DIM ** -0.5)).astype(q_ref.dtype)
        kg = jnp.dot(hb, w_ref[:, DA_WIDTH + g * LANES:DA_WIDTH + (g + 1) * LANES],
                     preferred_element_type=F32)
        k_ref[:, sl] = rope(kg).astype(k_ref.dtype)
    v_ref[...] = jnp.dot(hb, w_ref[:, 2 * DA_WIDTH:DA_COLS],
                         preferred_element_type=F32).astype(v_ref.dtype)

    p = jnp.dot(hb, w_ref[:, DA_COLS:], preferred_element_type=F32)
    rows = p.shape[0]
    prev = pltpu.roll(p, 1, axis=0)
    first = lax.broadcasted_iota(jnp.int32, p.shape, 0) == 0
    prev = jnp.where(first, carry_ref[0:1, :], prev)
    rw_ref[...] = p + (prev - p) * mu_ref[...]
    carry_ref[0:1, :] = p[rows - 1:rows, :]


def _proj(x, pos3, mod3, norm_w, invf, w_in_b, mu):
    b, t, d = x.shape
    tm = min(PROJ_ROWS, t)
    n_in = w_in_b.shape[1]
    blk = lambda w: pl.BlockSpec((None, tm, w), lambda bi, ti: (bi, ti, 0))
    full = lambda r, c: pl.BlockSpec((r, c), lambda bi, ti: (0, 0))
    return pl.pallas_call(
        _proj_kernel,
        grid=(b, t // tm),
        in_specs=[blk(d), blk(1),
                  pl.BlockSpec((None, N_MOD, d), lambda bi, ti: (bi, 0, 0)),
                  full(1, d), full(1, LANES), full(d, n_in), full(1, RW_COLS)],
        out_specs=[blk(DA_WIDTH), blk(DA_WIDTH), blk(DA_WIDTH), blk(RW_COLS)],
        out_shape=[jax.ShapeDtypeStruct((b, t, DA_WIDTH), BF16)] * 3
        + [jax.ShapeDtypeStruct((b, t, RW_COLS), F32)],
        scratch_shapes=[pltpu.VMEM((8, RW_COLS), F32)],
        compiler_params=_params("parallel", "arbitrary"),
        name="proj",
    )(x, pos3, mod3, norm_w, invf, w_in_b, mu)


def _attn_kernel(q_ref, k_ref, v_ref, lam_ref, subln_ref, o_ref, m_ref, acc_ref, *, lambda_init):
    qi = pl.program_id(2)
    tq = q_ref.shape[0]
    heads = range(ATTN_HEAD_GROUP)
    hs = [slice(h * DA_V_DIM, (h + 1) * DA_V_DIM) for h in heads]
    lane = lax.broadcasted_iota(jnp.int32, (tq, DA_V_DIM), 1)
    qq = []
    for c in hs:
        q = q_ref[:, c]
        zero = jnp.zeros_like(q)
        qq.append(jnp.concatenate([jnp.where(lane < DA_HEAD_DIM, q, zero),
                                   jnp.where(lane >= DA_HEAD_DIM, q, zero)], axis=0))

    m_ref[...] = jnp.full(m_ref.shape, -jnp.inf, F32)
    acc_ref[...] = jnp.zeros(acc_ref.shape, F32)
    tk = ATTN_KV_BLOCK if k_ref.shape[0] % ATTN_KV_BLOCK == 0 else tq
    rep = tk // LANES
    ones = jnp.ones((tk, DA_V_DIM), v_ref.dtype)

    def step(j, masked):
        rows = pl.ds(pl.multiple_of(j * tk, tk), tk)
        s = [lax.dot_general(qq[h], k_ref[rows, hs[h]], (((1,), (1,)), ((), ())),
                             preferred_element_type=F32) for h in heads]
        if masked:
            qpos = qi * tq + lax.broadcasted_iota(jnp.int32, s[0].shape, 0) % tq
            kpos = j * tk + lax.broadcasted_iota(jnp.int32, s[0].shape, 1)
            s = [jnp.where(qpos >= kpos, x, -jnp.inf) for x in s]
        for h in heads:
            m_old = m_ref[h]
            m_new = jnp.maximum(m_old, jnp.max(s[h], axis=-1, keepdims=True))
            alpha = jnp.exp(m_old - m_new)
            p = jnp.exp((s[h] - jnp.concatenate([m_new] * rep, axis=1)).astype(v_ref.dtype))
            v_ext = jnp.concatenate([v_ref[rows, hs[h]], ones], axis=1)
            acc_ref[h] = (jnp.concatenate([alpha, alpha], axis=1) * acc_ref[h]
                          + jnp.dot(p, v_ext, preferred_element_type=F32))
            m_ref[h] = m_new

    def body(j, carry):
        step(j, False)
        return carry

    n_full = (qi * tq) // tk
    lax.fori_loop(0, n_full, body, 0)
    step(n_full, True)

    lam = (jnp.exp(jnp.sum(lam_ref[0:1, :] * lam_ref[1:2, :], axis=-1, keepdims=True))
           - jnp.exp(jnp.sum(lam_ref[2:3, :] * lam_ref[3:4, :], axis=-1, keepdims=True))
           + lambda_init)
    for h in heads:
        o = acc_ref[h, :, :DA_V_DIM] / acc_ref[h, :, DA_V_DIM:]
        d = o[:tq, :] - lam * o[tq:, :]
        o_ref[:, hs[h]] = (_rms(d, subln_ref[...], SUBLN_EPS)
                           * (1.0 - lambda_init)).astype(o_ref.dtype)


def _attn(q, k, v, lam4, subln, lambda_init):
    b, t, _ = q.shape
    tq = min(ATTN_BLOCK, t)
    hg = ATTN_HEAD_GROUP
    gw = hg * DA_V_DIM
    return pl.pallas_call(
        functools.partial(_attn_kernel, lambda_init=lambda_init),
        grid=(b, DA_HEADS // hg, t // tq),
        in_specs=[pl.BlockSpec((None, tq, gw), lambda bi, h, qi: (bi, qi, h)),
                  pl.BlockSpec((None, t, gw), lambda bi, h, qi: (bi, 0, h)),
                  pl.BlockSpec((None, t, gw), lambda bi, h, qi: (bi, 0, h)),
                  pl.BlockSpec((4, DA_HEAD_DIM), lambda bi, h, qi: (0, 0)),
                  pl.BlockSpec((1, DA_V_DIM), lambda bi, h, qi: (0, 0))],
        out_specs=pl.BlockSpec((None, tq, gw), lambda bi, h, qi: (bi, qi, h)),
        out_shape=jax.ShapeDtypeStruct((b, t, DA_WIDTH), BF16),
        scratch_shapes=[pltpu.VMEM((hg, 2 * tq, LANES), F32),
                        pltpu.VMEM((hg, 2 * tq, 2 * DA_V_DIM), F32)],
        compiler_params=_params("parallel", "parallel", "arbitrary"),
        name="attn",
    )(q, k, v, lam4, subln)


def _rwkv_kernel(rw_ref, w0_ref, w2_ref, a0_ref, a2_ref, g2_ref, kk_ref, ka_ref, rk_ref,
                 lnw_ref, lnb_ref, o_ref, state_ref, r_s, k_s, v_s, lw_s, kk_s, a_s, g_s, cum_s):
    ti = pl.program_id(1)

    @pl.when(ti == 0)
    def _():
        state_ref[...] = jnp.zeros_like(state_ref)

    w = RW_WIDTH
    rw = rw_ref[...]
    k = rw[:, w:2 * w]
    wl = rw[:, 3 * w:3 * w + DECAY_LORA]
    al = rw[:, 3 * w + DECAY_LORA:3 * w + DECAY_LORA + AAA_LORA]
    gl = rw[:, 3 * w + DECAY_LORA + AAA_LORA:]
    z = -(w0_ref[...] + _bdot(jnp.tanh(wl), w2_ref[...]))
    softplus = jnp.maximum(z, 0.0) + jnp.log(1.0 + jnp.exp(-jnp.abs(z)))
    a = jax.nn.sigmoid(a0_ref[...] + _bdot(al, a2_ref[...]))
    r_s[...] = rw[:, 0:w]
    v_s[...] = rw[:, 2 * w:3 * w]
    lw_s[...] = -jnp.exp(-softplus - 0.5)
    a_s[...] = a
    g_s[...] = _bdot(jax.nn.sigmoid(gl), g2_ref[...])
    kk_s[...] = k * kk_ref[...]
    k_s[...] = k * (1.0 + (a - 1.0) * ka_ref[...])

    c_len = RW_CHUNK
    n = RW_HEAD_DIM
    tb = rw_ref.shape[0]

    tri = jnp.where(lax.broadcasted_iota(jnp.int32, (c_len, c_len), 0)
                    >= lax.broadcasted_iota(jnp.int32, (c_len, c_len), 1), 1.0, 0.0).astype(BF16)
    for ci in range(tb // c_len):
        rows = slice(ci * c_len, (ci + 1) * c_len)
        lw_c = lw_s[rows, :]
        lw_hi = lw_c.astype(BF16)
        rem = lw_c - lw_hi.astype(F32)
        lw_mid = rem.astype(BF16)
        lw_lo = (rem - lw_mid.astype(F32)).astype(BF16)
        cum_s[rows, :] = (jnp.dot(tri, lw_hi, preferred_element_type=F32)
                          + jnp.dot(tri, lw_mid, preferred_element_type=F32)
                          + jnp.dot(tri, lw_lo, preferred_element_type=F32))

    row = lax.broadcasted_iota(jnp.int32, (c_len, 2 * c_len), 0)
    col = lax.broadcasted_iota(jnp.int32, (c_len, 2 * c_len), 1)
    incl2 = row >= col % c_len
    strict2 = row > col % c_len
    eye = jnp.where(lax.broadcasted_iota(jnp.int32, (c_len, c_len), 0)
                    == lax.broadcasted_iota(jnp.int32, (c_len, c_len), 1), 1.0, 0.0).astype(F32)

    def chunk(ci, carry):
        rows = pl.ds(pl.multiple_of(ci * c_len, c_len), c_len)
        heads = range(RW_HEADS)
        sl = [slice(h * n, (h + 1) * n) for h in heads]
        r = [r_s[rows, c] for c in sl]
        kh = [k_s[rows, c] for c in sl]
        v = [v_s[rows, c] for c in sl]
        lw = [lw_s[rows, c] for c in sl]
        cum = [cum_s[rows, c] for c in sl]
        kk = [kk_s[rows, c] for c in sl]
        kk = [x * lax.rsqrt(jnp.maximum(jnp.sum(x * x, axis=-1, keepdims=True), 1e-24)) for x in kk]
        kka = [kk[h] * a_s[rows, sl[h]] for h in heads]
        end = [jnp.sum(x, axis=0, keepdims=True) for x in lw]
        e_neg = [jnp.exp(-x) for x in cum]
        e_end = [jnp.exp(end[h] - cum[h]) for h in heads]
        left = [jnp.concatenate([-kk[h] * jnp.exp(cum[h] - lw[h]), r[h] * jnp.exp(cum[h])], axis=0)
                for h in heads]
        g = [_bdot_nt(left[h], jnp.concatenate([kka[h] * e_neg[h], kh[h] * e_neg[h]], axis=0))
             for h in heads]
        a_a = [jnp.where(strict2, x[:c_len, :], 0.0) for x in g]
        a_r = [jnp.where(incl2, x[c_len:, :], 0.0) for x in g]
        pw = [x[:, :c_len] for x in a_a]
        inv = [eye + x for x in pw]
        for _ in range(c_len.bit_length() - 2):
            pw = [_bdot(x, x) for x in pw]
            inv = [inv[h] + _bdot(inv[h], pw[h]) for h in heads]
        akv = [_bdot(a_a[h][:, c_len:], v[h]) for h in heads]
        s0 = [state_ref[h] for h in heads]
        ls = [_bdot_nt(left[h], s0[h]) for h in heads]
        u = [_bdot(inv[h], ls[h][:c_len, :] + akv[h]) for h in heads]
        uv = [jnp.concatenate([u[h], v[h]], axis=0) for h in heads]
        y = [ls[h][c_len:, :] + _bdot(a_r[h], uv[h]) for h in heads]
        for h in heads:
            state_ref[h] = s0[h] * jnp.exp(end[h]) + _bdot_tn(
                uv[h], jnp.concatenate([kka[h] * e_end[h], kh[h] * e_end[h]], axis=0))
        for h in heads:
            mean = jnp.mean(y[h], axis=-1, keepdims=True)
            yc = y[h] - mean
            var = jnp.mean(yc * yc, axis=-1, keepdims=True)
            yn = yc * lax.rsqrt(var + LN_X_EPS) * lnw_ref[:, sl[h]] + lnb_ref[:, sl[h]]
            bonus = jnp.sum(r[h] * kh[h] * rk_ref[:, sl[h]], axis=-1, keepdims=True) * v[h]
            o_ref[rows, sl[h]] = ((yn + bonus) * g_s[rows, sl[h]]).astype(o_ref.dtype)
        return carry

    lax.fori_loop(0, rw_ref.shape[0] // c_len, chunk, 0, unroll=True)


def _rwkv(rw, w0, w2, a0, a2, g2, k_k, k_a, r_k, ln_w, ln_b):
    b, t, _ = rw.shape
    tb = min(RW_BLOCK, t)
    w = RW_WIDTH
    vec = pl.BlockSpec((1, w), lambda bi, ti: (0, 0))
    mat = lambda r: pl.BlockSpec((r, w), lambda bi, ti: (0, 0))
    return pl.pallas_call(
        _rwkv_kernel,
        grid=(b, t // tb),
        in_specs=[pl.BlockSpec((None, tb, RW_COLS), lambda bi, ti: (bi, ti, 0)),
                  vec, mat(DECAY_LORA), vec, mat(AAA_LORA), mat(GATE_LORA), vec, vec, vec, vec, vec],
        out_specs=pl.BlockSpec((None, tb, w), lambda bi, ti: (bi, ti, 0)),
        out_shape=jax.ShapeDtypeStruct((b, t, w), BF16),
        scratch_shapes=[pltpu.VMEM((RW_HEADS, RW_HEAD_DIM, RW_HEAD_DIM), F32)]
        + [pltpu.VMEM((tb, w), F32)] * 8,
        compiler_params=_params("parallel", "arbitrary"),
        name="rwkv",
    )(rw, w0, w2, a0, a2, g2, k_k, k_a, r_k.reshape(1, w), ln_w, ln_b)


def _pack_rows(x):
    half = x.shape[1] // 2
    hi = pltpu.bitcast(x[:, :half].astype(BF16).astype(F32), jnp.uint32)
    lo = pltpu.bitcast(x[:, half:].astype(BF16).astype(F32), jnp.uint32)
    return hi | (lo >> 16)


def _unpack_rows(u):
    hi = pltpu.bitcast(u & jnp.uint32(0xFFFF0000), F32)
    lo = pltpu.bitcast(u << 16, F32)
    return jnp.concatenate([hi, lo], axis=1)


ROW_SLAB = 4


def _store_rows(ref, u, r0=0):
    n = u.shape[0]
    for c in range(ROW_SLAB):
        ref[pl.ds(r0 * ROW_SLAB + c, n, stride=ROW_SLAB), :] = u[:, c * LANES:(c + 1) * LANES]


def _load_rows(ref, n, r0=0):
    return jnp.concatenate([ref[pl.ds(r0 * ROW_SLAB + c, n, stride=ROW_SLAB), :]
                            for c in range(ROW_SLAB)], axis=1)


def _out_kernel(yda_ref, yrw_ref, x_ref, mod_ref, wo_ref, pmn_ref, pfn_ref, rw_ref, rb_ref,
                x1_ref, h2_ref, idx_ref, wgt_ref):
    y = (jnp.dot(yda_ref[...], wo_ref[0:DA_WIDTH, :], preferred_element_type=F32)
         + jnp.dot(yrw_ref[...], wo_ref[DA_WIDTH:, :], preferred_element_type=F32))
    x1 = x_ref[...] + mod_ref[2:3, :] * _rms(y, pmn_ref[...], NORM_EPS)
    x1_ref[...] = x1
    h2 = _rms(x1, pfn_ref[...], NORM_EPS) * (1.0 + mod_ref[4:5, :]) + mod_ref[3:4, :]
    _store_rows(h2_ref, _pack_rows(h2))

    h_hi = h2.astype(BF16)
    h_lo = (h2 - h_hi.astype(F32)).astype(BF16)
    rw = rw_ref[...]
    w_hi = rw.astype(BF16)
    w_lo = (rw - w_hi.astype(F32)).astype(BF16)
    logits = (jnp.dot(h_hi, w_hi, preferred_element_type=F32)
              + jnp.dot(h_hi, w_lo, preferred_element_type=F32)
              + jnp.dot(h_lo, w_hi, preferred_element_type=F32)) + rb_ref[...]

    lane = lax.broadcasted_iota(jnp.int32, logits.shape, 1)
    slot = lax.broadcasted_iota(jnp.int32, wgt_ref.shape, 1)
    wide = lax.broadcasted_iota(jnp.int32, (logits.shape[0], LANES), 1)
    idx = jnp.zeros(wide.shape, jnp.int32)
    val = jnp.zeros(wgt_ref.shape, F32)
    top = None
    for j in range(TOP_K):
        m = jnp.max(logits, axis=-1, keepdims=True)
        i = jnp.min(jnp.where(logits == m, lane, N_EXPERTS), axis=-1, keepdims=True)
        top = m if top is None else top
        idx = jnp.where(wide == j, i, idx)
        val = jnp.where(slot == j, jnp.exp(m - top), val)
        logits = jnp.where(lane == i, -jnp.inf, logits)
    idx_ref[...] = idx.T[0:TOP_K, :]
    wgt_ref[...] = val / jnp.sum(val, axis=-1, keepdims=True)


def _out(y_da, y_rw, x, mod3, w_out_b, post_mix_norm, pre_ffn_norm, router_w, router_b):
    b, t, d = x.shape
    tm = min(OUT_ROWS, t)
    e = router_w.shape[1]
    nt = t // tm
    blk = lambda w: pl.BlockSpec((None, tm, w), lambda bi, ti: (bi, ti, 0))
    full = lambda r, c: pl.BlockSpec((r, c), lambda bi, ti: (0, 0))
    flat = lambda r, c: pl.BlockSpec((r, c), lambda bi, ti: (bi * nt + ti, 0))
    return pl.pallas_call(
        _out_kernel,
        grid=(b, nt),
        in_specs=[blk(DA_WIDTH), blk(RW_WIDTH), blk(d),
                  pl.BlockSpec((None, N_MOD, d), lambda bi, ti: (bi, 0, 0)),
                  full(d, d), full(1, d), full(1, d), full(d, e), full(1, e)],
        out_specs=[blk(d), flat(tm * ROW_SLAB, LANES),
                   pl.BlockSpec((TOP_K, tm), lambda bi, ti: (0, bi * nt + ti)), flat(tm, TOP_K)],
        out_shape=[jax.ShapeDtypeStruct((b, t, d), F32),
                   jax.ShapeDtypeStruct((b * t * ROW_SLAB, LANES), jnp.uint32),
                   jax.ShapeDtypeStruct((TOP_K, b * t), jnp.int32),
                   jax.ShapeDtypeStruct((b * t, TOP_K), F32)],
        compiler_params=_params("parallel", "parallel"),
        name="out",
    )(y_da, y_rw, x, mod3, w_out_b, post_mix_norm, pre_ffn_norm, router_w,
      router_b.reshape(1, e))


def _route(top_idx, rows_per_tile, n_tiles):
    experts = jnp.arange(N_EXPERTS, dtype=jnp.int32)
    chosen = (top_idx[:, :, None] == experts[None, None, :]).astype(jnp.int32)
    member = jnp.sum(chosen, axis=0)
    csum = jnp.cumsum(member, axis=0)
    counts = csum[-1]
    padded = (counts + rows_per_tile - 1) // rows_per_tile * rows_per_tile
    ends = jnp.cumsum(padded)
    starts = ends - padded
    pos = jnp.sum((csum - member + starts[None, :])[None, :, :] * chosen, axis=2)
    n_active = ends[-1] // rows_per_tile
    tile_start = jnp.arange(n_tiles, dtype=jnp.int32) * rows_per_tile
    tile = jnp.minimum(tile_start, ends[-1] - 1)
    tile_expert = jnp.sum((tile[:, None] >= ends[None, :]).astype(jnp.int32), axis=1)
    used = padded > 0
    later = lax.cummin(jnp.where(used, experts, N_EXPERTS), reverse=True)
    following = jnp.concatenate([later[1:], jnp.full((1,), N_EXPERTS, jnp.int32)])
    following = jnp.where(following == N_EXPERTS, -1, following)
    run_index = jnp.cumsum(used.astype(jnp.int32)) - 1
    plan = (tile_expert.astype(jnp.int32), n_active.reshape(1).astype(jnp.int32),
            following[tile_expert].astype(jnp.int32), (run_index[tile_expert] % 2).astype(jnp.int32))
    pads = ((starts + counts).astype(jnp.int32), (padded - counts).astype(jnp.int32))
    return pos.astype(jnp.int32), pads, plan


def _dispatch_kernel(ps_ref, pl_ref, pos_ref, h_ref, xs_ref, zero_ref, sem, pad_sem):
    tb = h_ref.shape[0] // ROW_SLAB

    def issue(t, carry):
        src = h_ref.at[pl.ds(pl.multiple_of(t * ROW_SLAB, ROW_SLAB), ROW_SLAB)]
        for j in range(TOP_K):
            pltpu.make_async_copy(src, xs_ref.at[pos_ref[j, t]],
                                  sem).start(priority=j % 2)
        return carry

    lax.fori_loop(0, tb, issue, 0, unroll=8)

    @pl.when(pl.program_id(0) == pl.num_programs(0) - 1)
    def _():
        zero_ref[...] = jnp.zeros_like(zero_ref)
        bits = [1 << k for k in reversed(range(zero_ref.shape[0].bit_length()))]

        def pad_copy(e, b):
            below = pl_ref[e] & ~(2 * b - 1)
            return pltpu.make_async_copy(zero_ref.at[pl.ds(0, b)],
                                         xs_ref.at[pl.ds(ps_ref[e] + below, b)], pad_sem)

        for wait in (False, True):
            for e in range(N_EXPERTS):
                for b in bits:
                    @pl.when((pl_ref[e] & b) != 0)
                    def _(e=e, b=b):
                        if wait:
                            pad_copy(e, b).wait()
                        else:
                            pad_copy(e, b).start()

    for j in range(TOP_K):
        pltpu.make_async_copy(xs_ref.at[pl.ds(0, tb)], xs_ref.at[pl.ds(0, tb)], sem).wait()


def _dispatch(pos, pads, h2p, n_rows):
    pad_start, pad_len = pads
    n = h2p.shape[0] // ROW_SLAB
    w = h2p.shape[1]
    tb = min(DISPATCH_TOKENS, n)
    return pl.pallas_call(
        _dispatch_kernel,
        grid_spec=pltpu.PrefetchScalarGridSpec(
            num_scalar_prefetch=2,
            grid=(n // tb,),
            in_specs=[pl.BlockSpec((TOP_K, tb), lambda i, ps, pn: (0, i),
                                   memory_space=pltpu.SMEM),
                      pl.BlockSpec((tb * ROW_SLAB, w), lambda i, ps, pn: (i, 0))],
            out_specs=pl.BlockSpec(memory_space=pl.ANY),
            scratch_shapes=[pltpu.VMEM((EXPERT_ROWS // 2, ROW_SLAB, w), h2p.dtype),
                            pltpu.SemaphoreType.DMA(()), pltpu.SemaphoreType.DMA(())]),
        out_shape=jax.ShapeDtypeStruct((n_rows, ROW_SLAB, w), h2p.dtype),
        compiler_params=_params("arbitrary"),
        name="dispatch",
    )(pad_start, pad_len, pos, h2p)


def _expert_kernel(te_ref, na_ref, nx_ref, sl_ref, xs_ref, w1_hbm, b1_ref, w2_hbm, b2_ref, ys_ref,
                   w1p_s, b1p_s, w2b_s, act_s, w1_buf, w2_buf, sem):
    i = pl.program_id(0)
    active = i < na_ref[0]
    expert = te_ref[i]
    fresh = jnp.logical_or(i == 0, expert != te_ref[jnp.maximum(i - 1, 0)])
    slot = sl_ref[i]
    grp = 2 * LANES
    n_grp = w1_buf.shape[2] // grp

    def weight_copies(ex, s):
        return (pltpu.make_async_copy(w1_hbm.at[ex], w1_buf.at[s], sem.at[0, s]),
                pltpu.make_async_copy(w2_hbm.at[ex], w2_buf.at[s], sem.at[1, s]))

    @pl.when(jnp.logical_and(active, i == 0))
    def _():
        for cp in weight_copies(expert, slot):
            cp.start()

    @pl.when(jnp.logical_and(active, fresh))
    def _():
        for cp in weight_copies(expert, slot):
            cp.wait()

        @pl.when(nx_ref[i] >= 0)
        def _():
            for cp in weight_copies(nx_ref[i], 1 - slot):
                cp.start()

        src = lax.broadcasted_iota(jnp.int32, (grp, grp), 0)
        dst = lax.broadcasted_iota(jnp.int32, (grp, grp), 1)
        perm = jnp.where(src == jnp.where(dst < LANES, 2 * dst, 2 * (dst - LANES) + 1),
                         1.0, 0.0).astype(BF16)
        for g in range(n_grp):
            sl = slice(g * grp, (g + 1) * grp)
            w1p_s[:, sl] = jnp.dot(w1_buf[slot, :, sl].astype(BF16), perm,
                                   preferred_element_type=F32).astype(BF16)
            b = b1_ref[:, sl]
            b_hi = b.astype(BF16)
            b_lo = (b - b_hi.astype(F32)).astype(BF16)
            b1p_s[:, sl] = (jnp.dot(b_hi, perm, preferred_element_type=F32)
                            + jnp.dot(b_lo, perm, preferred_element_type=F32))
        w2b_s[...] = w2_buf[slot].astype(BF16)

    @pl.when(active)
    def _():
        x = _unpack_rows(_load_rows(xs_ref, act_s.shape[0])).astype(BF16)
        hid = jnp.dot(x, w1p_s[...], preferred_element_type=F32) + b1p_s[0:1, :]
        for g in range(n_grp):
            glu = jnp.minimum(hid[:, g * grp:g * grp + LANES], SWIGLU_LIMIT)
            lin = jnp.clip(hid[:, g * grp + LANES:(g + 1) * grp], -SWIGLU_LIMIT, SWIGLU_LIMIT)
            act_s[:, g * LANES:(g + 1) * LANES] = (
                glu * jax.nn.sigmoid(SWIGLU_ALPHA * glu) * (lin + 1.0)).astype(BF16)
        y = jnp.dot(act_s[...], w2b_s[...], preferred_element_type=F32) + b2_ref[...]
        _store_rows(ys_ref, _pack_rows(y))

    @pl.when(jnp.logical_not(active))
    def _():
        ys_ref[...] = jnp.zeros_like(ys_ref)


def _experts(plan, xs, w1, b1, w2, b2):
    tile_expert, n_active, next_expert, tile_slot = plan
    n_rows, w = xs.shape[0] // ROW_SLAB, xs.shape[1]
    tm = EXPERT_ROWS
    d, f2 = w1.shape[1], w1.shape[2]
    f = f2 // 2
    wspec = lambda r, c: pl.BlockSpec((None, r, c), lambda i, te, na, nx, sl: (te[i], 0, 0))
    rows = pl.BlockSpec((tm * ROW_SLAB, w), lambda i, te, na, nx, sl: (i, 0))
    rows_in = pl.BlockSpec((tm * ROW_SLAB, w),
                           lambda i, te, na, nx, sl: (jnp.minimum(i, na[0] - 1), 0))
    hbm = pl.BlockSpec(memory_space=pl.ANY)
    return pl.pallas_call(
        _expert_kernel,
        grid_spec=pltpu.PrefetchScalarGridSpec(
            num_scalar_prefetch=4,
            grid=(n_rows // tm,),
            in_specs=[rows_in, hbm, wspec(8, f2), hbm, wspec(1, d)],
            out_specs=rows,
            scratch_shapes=[pltpu.VMEM((d, f2), BF16), pltpu.VMEM((8, f2), F32),
                            pltpu.VMEM((f, d), BF16), pltpu.VMEM((tm, f), BF16),
                            pltpu.VMEM((2, d, f2), F32), pltpu.VMEM((2, f, d), F32),
                            pltpu.SemaphoreType.DMA((2, 2))]),
        out_shape=jax.ShapeDtypeStruct((n_rows * ROW_SLAB, w), jnp.uint32),
        compiler_params=_params("arbitrary"),
        name="expert",
    )(tile_expert, n_active, next_expert, tile_slot, xs, w1, b1, w2, b2)


def _combine_kernel(pos_ref, nxt_ref, ys_ref, wgt_ref, x1_ref, mod_ref, nw_ref, o_ref, buf_ref,
                    sem):
    tc = x1_ref.shape[0]
    step = pl.program_id(0) * pl.num_programs(1) + pl.program_id(1)
    last = pl.num_programs(0) * pl.num_programs(1) - 1
    slot = step % 2

    def gather(p_ref, s):
        def issue(t, carry):
            dst = pl.ds(pl.multiple_of(t * ROW_SLAB, ROW_SLAB), ROW_SLAB)
            for j in range(TOP_K):
                pltpu.make_async_copy(ys_ref.at[p_ref[j, t]], buf_ref.at[s, j, dst],
                                      sem.at[s]).start(priority=j % 2)
            return carry

        lax.fori_loop(0, tc, issue, 0, unroll=8)

    @pl.when(step == 0)
    def _():
        gather(pos_ref, slot)

    @pl.when(step < last)
    def _():
        gather(nxt_ref, 1 - slot)

    for j in range(TOP_K):
        pltpu.make_async_copy(ys_ref.at[pl.ds(0, tc)], ys_ref.at[pl.ds(0, tc)], sem.at[slot]).wait()

    sub = min(COMBINE_SUB_ROWS, tc)

    def mix(i, carry):
        r0 = pl.multiple_of(i * sub, sub)
        wgt = wgt_ref[pl.ds(r0, sub), :]
        acc = jnp.zeros((sub, x1_ref.shape[1]), F32)
        for j in range(TOP_K):
            acc = acc + wgt[:, j:j + 1] * _unpack_rows(_load_rows(buf_ref.at[slot, j], sub, r0))
        o_ref[pl.ds(r0, sub), :] = (x1_ref[pl.ds(r0, sub), :]
                                    + mod_ref[5:6, :] * _rms(acc, nw_ref[...], NORM_EPS))
        return carry

    lax.fori_loop(0, tc // sub, mix, 0)


def _combine(pos, ys, wgt, x1, mod3, post_ffn_norm):
    b, t, d = x1.shape
    tc = min(COMBINE_TOKENS, t)
    nt = t // tc
    blk = lambda w: pl.BlockSpec((None, tc, w), lambda bi, ti: (bi, ti, 0))
    return pl.pallas_call(
        _combine_kernel,
        grid=(b, nt),
        in_specs=[pl.BlockSpec((TOP_K, tc), lambda bi, ti: (0, bi * nt + ti),
                               memory_space=pltpu.SMEM),
                  pl.BlockSpec((TOP_K, tc),
                               lambda bi, ti: (0, jnp.minimum(bi * nt + ti + 1, b * nt - 1)),
                               memory_space=pltpu.SMEM),
                  pl.BlockSpec(memory_space=pl.ANY),
                  pl.BlockSpec((tc, TOP_K), lambda bi, ti: (bi * nt + ti, 0)), blk(d),
                  pl.BlockSpec((None, N_MOD, d), lambda bi, ti: (bi, 0, 0)),
                  pl.BlockSpec((1, d), lambda bi, ti: (0, 0))],
        out_specs=blk(d),
        out_shape=jax.ShapeDtypeStruct((b, t, d), F32),
        scratch_shapes=[pltpu.VMEM((2, TOP_K, tc * ROW_SLAB, LANES), jnp.uint32),
                        pltpu.SemaphoreType.DMA((2,))],
        compiler_params=_params("arbitrary", "arbitrary"),
        name="combine",
    )(pos, pos, ys.reshape(ys.shape[0] // ROW_SLAB, ROW_SLAB, LANES), wgt, x1, mod3,
      post_ffn_norm)


def _stages(x, c, positions, ada_w, ada_b, pre_mix_norm, post_mix_norm, pre_ffn_norm,
            post_ffn_norm, w_in, w_out, da_lambda_q1, da_lambda_k1, da_lambda_q2, da_lambda_k2,
            da_subln, rw_mu, rw_w0, rw_w2, rw_a0, rw_a2, rw_g2, rw_k_k, rw_k_a, rw_r_k, rw_ln_w,
            rw_ln_b, router_w, router_b, moe_w1, moe_b1, moe_w2, moe_b2):
    b, t, d = x.shape
    res = {}
    lambda_init = 0.8 - 0.6 * math.exp(-0.3 * 0)
    mod = _mod(c, ada_w[0], ada_b[0])
    res["mod"] = mod
    mod3 = mod.reshape(b, N_MOD, d)
    inv_freq = ROPE_THETA ** (-jnp.arange(0, ROPE_DIM, 2, dtype=F32) / ROPE_DIM)
    invf = jnp.tile(inv_freq, LANES // (ROPE_DIM // 2)).reshape(1, LANES)
    q, k, v, rw = _proj(x, positions.reshape(b, t, 1), mod3, pre_mix_norm, invf,
                        w_in[0].astype(BF16), rw_mu)
    res.update(q=q, k=k, v=v, rw=rw)
    lam4 = jnp.concatenate([da_lambda_q1, da_lambda_k1, da_lambda_q2, da_lambda_k2], axis=0)
    y_da = _attn(q, k, v, lam4, da_subln, lambda_init)
    res["y_da"] = y_da
    y_rw = _rwkv(rw, rw_w0, rw_w2[0], rw_a0, rw_a2[0], rw_g2[0], rw_k_k, rw_k_a, rw_r_k[0],
                 rw_ln_w, rw_ln_b)
    res["y_rw"] = y_rw
    x1, h2p, top_idx, top_w = _out(y_da, y_rw, x, mod3, w_out[0].astype(BF16), post_mix_norm,
                                   pre_ffn_norm, router_w[0], router_b[0])
    res.update(x1=x1, top_idx=top_idx, top_w=top_w)
    n = b * t
    n_tiles = n * TOP_K // EXPERT_ROWS + N_EXPERTS
    pos, pads, plan = _route(top_idx, EXPERT_ROWS, n_tiles)
    xs = _dispatch(pos, pads, h2p, n_tiles * EXPERT_ROWS)
    xs = xs.reshape(n_tiles * EXPERT_ROWS * ROW_SLAB, LANES)
    b1 = jnp.broadcast_to(moe_b1[0][:, None, :], (N_EXPERTS, 8, moe_b1.shape[-1]))
    ys = _experts(plan, xs, moe_w1[0], b1, moe_w2[0], moe_b2[0][:, None, :])
    res["final"] = _combine(pos, ys, top_w, x1, mod3, post_ffn_norm)
    return res


stages = _stages


def kernel(x, c, positions, ada_w, ada_b, pre_mix_norm, post_mix_norm, pre_ffn_norm, post_ffn_norm, w_in, w_out, da_lambda_q1, da_lambda_k1, da_lambda_q2, da_lambda_k2, da_subln, rw_mu, rw_w0, rw_w2, rw_a0, rw_a2, rw_g2, rw_k_k, rw_k_a, rw_r_k, rw_ln_w, rw_ln_b, router_w, router_b, moe_w1, moe_b1, moe_w2, moe_b2):
    res = _stages(x, c, positions, ada_w, ada_b, pre_mix_norm, post_mix_norm, pre_ffn_norm,
                  post_ffn_norm, w_in, w_out, da_lambda_q1, da_lambda_k1, da_lambda_q2,
                  da_lambda_k2, da_subln, rw_mu, rw_w0, rw_w2, rw_a0, rw_a2, rw_g2, rw_k_k,
                  rw_k_a, rw_r_k, rw_ln_w, rw_ln_b, router_w, router_b, moe_w1, moe_b1,
                  moe_w2, moe_b2)
    return res["final"]
```

```python
import functools
import math

import jax
import jax.numpy as jnp
from jax import lax
from jax.experimental import pallas as pl
from jax.experimental.pallas import tpu as pltpu

F32 = jnp.float32
BF16 = jnp.bfloat16

DA_HEADS = 4
DA_HEAD_DIM = 64
DA_V_DIM = 128
DA_WIDTH = 512
RW_HEADS = 8
RW_HEAD_DIM = 64
RW_WIDTH = 512
DECAY_LORA = 64
AAA_LORA = 64
GATE_LORA = 128
DA_COLS = 1536
RW_COLS = 1792
ROPE_THETA = 500000.0
ROPE_DIM = 16
N_EXPERTS = 32
TOP_K = 4
SWIGLU_ALPHA = 1.702
SWIGLU_LIMIT = 7.0
NORM_EPS = 1e-6
SUBLN_EPS = 1e-5
LN_X_EPS = 64e-5
N_MOD = 6

LANES = 128
SUBLANES = 8
VMEM_LIMIT_BYTES = 56 * 1024 * 1024

PROJ_ROWS = 512
ATTN_BLOCK = 512
ATTN_KV_BLOCK = 512
ATTN_HEAD_GROUP = 4
RW_CHUNK = 128
RW_BLOCK = 512
OUT_ROWS = 512
EXPERT_ROWS = 512
DISPATCH_TOKENS = 2048
COMBINE_TOKENS = 512
COMBINE_SUB_ROWS = 256


def _params(*sem):
    return pltpu.CompilerParams(dimension_semantics=sem, vmem_limit_bytes=VMEM_LIMIT_BYTES)


def _bdot(a, b):
    return jnp.dot(a.astype(BF16), b.astype(BF16), preferred_element_type=F32)


def _bdot_nt(a, b):
    return lax.dot_general(a.astype(BF16), b.astype(BF16), (((1,), (1,)), ((), ())),
                           preferred_element_type=F32)


def _bdot_tn(a, b):
    return lax.dot_general(a.astype(BF16), b.astype(BF16), (((0,), (0,)), ((), ())),
                           preferred_element_type=F32)


def _rms(x, w, eps):
    return x * lax.rsqrt(jnp.mean(x * x, axis=-1, keepdims=True) + eps) * w


def _mod_kernel(c_ref, w_ref, b_ref, o_ref):
    c = c_ref[...]
    s = c * jax.nn.sigmoid(c)
    o_ref[...] = _bdot(s, w_ref[...]) + b_ref[...]


def _mod(c, ada_w, ada_b):
    b, d = c.shape
    n = ada_w.shape[1]
    return pl.pallas_call(
        _mod_kernel,
        grid=(n // d,),
        in_specs=[pl.BlockSpec((b, d), lambda j: (0, 0)),
                  pl.BlockSpec((d, d), lambda j: (0, j)),
                  pl.BlockSpec((1, d), lambda j: (0, j))],
        out_specs=pl.BlockSpec((b, d), lambda j: (0, j)),
        out_shape=jax.ShapeDtypeStruct((b, n), F32),
        compiler_params=_params("parallel"),
        name="mod",
    )(c, ada_w, ada_b.reshape(1, n))


def _proj_kernel(x_ref, pos_ref, mod_ref, nw_ref, invf_ref, w_ref, mu_ref,
                 q_ref, k_ref, v_ref, rw_ref, carry_ref):
    ti = pl.program_id(1)

    @pl.when(ti == 0)
    def _():
        carry_ref[...] = jnp.zeros_like(carry_ref)

    x = x_ref[...]
    h = _rms(x, nw_ref[...], NORM_EPS) * (1.0 + mod_ref[1:2, :]) + mod_ref[0:1, :]
    hb = h.astype(BF16)

    ang = pos_ref[...].astype(F32) * invf_ref[...]
    cos, sin = jnp.cos(ang), jnp.sin(ang)
    l64 = lax.broadcasted_iota(jnp.int32, ang.shape, 1) % DA_HEAD_DIM
    half = ROPE_DIM // 2
    c_tab = jnp.where(l64 < ROPE_DIM, cos, 1.0)
    s_lo = jnp.where(l64 < half, -sin, 0.0)
    s_hi = jnp.where((l64 >= half) & (l64 < ROPE_DIM), sin, 0.0)

    def rope(z):
        up = pltpu.roll(z, LANES - half, axis=1)
        dn = pltpu.roll(z, half, axis=1)
        return z * c_tab + up * s_lo + dn * s_hi

    for g in range(DA_WIDTH // LANES):
        sl = slice(g * LANES, (g + 1) * LANES)
        qg = jnp.dot(hb, w_ref[:, sl], preferred_element_type=F32)
        q_ref[:, sl] = (rope(qg) * (DA_HEAD_DIM ** -0.5)).astype(q_ref.dtype)
        kg = jnp.dot(hb, w_ref[:, DA_WIDTH + g * LANES:DA_WIDTH + (g + 1) * LANES],
                     preferred_element_type=F32)
        k_ref[:, sl] = rope(kg).astype(k_ref.dtype)
    v_ref[...] = jnp.dot(hb, w_ref[:, 2 * DA_WIDTH:DA_COLS],
                         preferred_element_type=F32).astype(v_ref.dtype)

    p = jnp.dot(hb, w_ref[:, DA_COLS:], preferred_element_type=F32)
    rows = p.shape[0]
    prev = pltpu.roll(p, 1, axis=0)
    first = lax.broadcasted_iota(jnp.int32, p.shape, 0) == 0
    prev = jnp.where(first, carry_ref[0:1, :], prev)
    rw_ref[...] = p + (prev - p) * mu_ref[...]
    carry_ref[0:1, :] = p[rows - 1:rows, :]


def _proj(x, pos3, mod3, norm_w, invf, w_in_b, mu):
    b, t, d = x.shape
    tm = min(PROJ_ROWS, t)
    n_in = w_in_b.shape[1]
    blk = lambda w: pl.BlockSpec((None, tm, w), lambda bi, ti: (bi, ti, 0))
    full = lambda r, c: pl.BlockSpec((r, c), lambda bi, ti: (0, 0))
    return pl.pallas_call(
        _proj_kernel,
        grid=(b, t // tm),
        in_specs=[blk(d), blk(1),
                  pl.BlockSpec((None, N_MOD, d), lambda bi, ti: (bi, 0, 0)),
                  full(1, d), full(1, LANES), full(d, n_in), full(1, RW_COLS)],
        out_specs=[blk(DA_WIDTH), blk(DA_WIDTH), blk(DA_WIDTH), blk(RW_COLS)],
        out_shape=[jax.ShapeDtypeStruct((b, t, DA_WIDTH), BF16)] * 3
        + [jax.ShapeDtypeStruct((b, t, RW_COLS), F32)],
        scratch_shapes=[pltpu.VMEM((8, RW_COLS), F32)],
        compiler_params=_params("parallel", "arbitrary"),
        name="proj",
    )(x, pos3, mod3, norm_w, invf, w_in_b, mu)


def _attn_kernel(q_ref, k_ref, v_ref, lam_ref, subln_ref, o_ref, m_ref, acc_ref, *, lambda_init):
    qi = pl.program_id(2)
    tq = q_ref.shape[0]
    heads = range(ATTN_HEAD_GROUP)
    hs = [slice(h * DA_V_DIM, (h + 1) * DA_V_DIM) for h in heads]
    lane = lax.broadcasted_iota(jnp.int32, (tq, DA_V_DIM), 1)
    qq = []
    for c in hs:
        q = q_ref[:, c]
        zero = jnp.zeros_like(q)
        qq.append(jnp.concatenate([jnp.where(lane < DA_HEAD_DIM, q, zero),
                                   jnp.where(lane >= DA_HEAD_DIM, q, zero)], axis=0))

    m_ref[...] = jnp.full(m_ref.shape, -jnp.inf, F32)
    acc_ref[...] = jnp.zeros(acc_ref.shape, F32)
    tk = ATTN_KV_BLOCK if k_ref.shape[0] % ATTN_KV_BLOCK == 0 else tq
    rep = tk // LANES
    ones = jnp.ones((tk, DA_V_DIM), v_ref.dtype)

    def step(j, masked):
        rows = pl.ds(pl.multiple_of(j * tk, tk), tk)
        s = [lax.dot_general(qq[h], k_ref[rows, hs[h]], (((1,), (1,)), ((), ())),
                             preferred_element_type=F32) for h in heads]
        if masked:
            qpos = qi * tq + lax.broadcasted_iota(jnp.int32, s[0].shape, 0) % tq
            kpos = j * tk + lax.broadcasted_iota(jnp.int32, s[0].shape, 1)
            s = [jnp.where(qpos >= kpos, x, -jnp.inf) for x in s]
        for h in heads:
            m_old = m_ref[h]
            m_new = jnp.maximum(m_old, jnp.max(s[h], axis=-1, keepdims=True))
            alpha = jnp.exp(m_old - m_new)
            p = jnp.exp((s[h] - jnp.concatenate([m_new] * rep, axis=1)).astype(v_ref.dtype))
            v_ext = jnp.concatenate([v_ref[rows, hs[h]], ones], axis=1)
            acc_ref[h] = (jnp.concatenate([alpha, alpha], axis=1) * acc_ref[h]
                          + jnp.dot(p, v_ext, preferred_element_type=F32))
            m_ref[h] = m_new

    def body(j, carry):
        step(j, False)
        return carry

    n_full = (qi * tq) // tk
    lax.fori_loop(0, n_full, body, 0)
    step(n_full, True)

    lam = (jnp.exp(jnp.sum(lam_ref[0:1, :] * lam_ref[1:2, :], axis=-1, keepdims=True))
           - jnp.exp(jnp.sum(lam_ref[2:3, :] * lam_ref[3:4, :], axis=-1, keepdims=True))
           + lambda_init)
    for h in heads:
        o = acc_ref[h, :, :DA_V_DIM] / acc_ref[h, :, DA_V_DIM:]
        d = o[:tq, :] - lam * o[tq:, :]
        o_ref[:, hs[h]] = (_rms(d, subln_ref[...], SUBLN_EPS)
                           * (1.0 - lambda_init)).astype(o_ref.dtype)


def _attn(q, k, v, lam4, subln, lambda_init):
    b, t, _ = q.shape
    tq = min(ATTN_BLOCK, t)
    hg = ATTN_HEAD_GROUP
    gw = hg * DA_V_DIM
    return pl.pallas_call(
        functools.partial(_attn_kernel, lambda_init=lambda_init),
        grid=(b, DA_HEADS // hg, t // tq),
        in_specs=[pl.BlockSpec((None, tq, gw), lambda bi, h, qi: (bi, qi, h)),
                  pl.BlockSpec((None, t, gw), lambda bi, h, qi: (bi, 0, h)),
                  pl.BlockSpec((None, t, gw), lambda bi, h, qi: (bi, 0, h)),
                  pl.BlockSpec((4, DA_HEAD_DIM), lambda bi, h, qi: (0, 0)),
                  pl.BlockSpec((1, DA_V_DIM), lambda bi, h, qi: (0, 0))],
        out_specs=pl.BlockSpec((None, tq, gw), lambda bi, h, qi: (bi, qi, h)),
        out_shape=jax.ShapeDtypeStruct((b, t, DA_WIDTH), BF16),
        scratch_shapes=[pltpu.VMEM((hg, 2 * tq, LANES), F32),
                        pltpu.VMEM((hg, 2 * tq, 2 * DA_V_DIM), F32)],
        compiler_params=_params("parallel", "parallel", "arbitrary"),
        name="attn",
    )(q, k, v, lam4, subln)


def _rwkv_kernel(rw_ref, w0_ref, w2_ref, a0_ref, a2_ref, g2_ref, kk_ref, ka_ref, rk_ref,
                 lnw_ref, lnb_ref, o_ref, state_ref, r_s, k_s, v_s, lw_s, kk_s, a_s, g_s, cum_s):
    ti = pl.program_id(1)

    @pl.when(ti == 0)
    def _():
        state_ref[...] = jnp.zeros_like(state_ref)

    w = RW_WIDTH
    rw = rw_ref[...]
    k = rw[:, w:2 * w]
    wl = rw[:, 3 * w:3 * w + DECAY_LORA]
    al = rw[:, 3 * w + DECAY_LORA:3 * w + DECAY_LORA + AAA_LORA]
    gl = rw[:, 3 * w + DECAY_LORA + AAA_LORA:]
    z = -(w0_ref[...] + _bdot(jnp.tanh(wl), w2_ref[...]))
    softplus = jnp.maximum(z, 0.0) + jnp.log(1.0 + jnp.exp(-jnp.abs(z)))
    a = jax.nn.sigmoid(a0_ref[...] + _bdot(al, a2_ref[...]))
    r_s[...] = rw[:, 0:w]
    v_s[...] = rw[:, 2 * w:3 * w]
    lw_s[...] = -jnp.exp(-softplus - 0.5)
    a_s[...] = a
    g_s[...] = _bdot(jax.nn.sigmoid(gl), g2_ref[...])
    kk_s[...] = k * kk_ref[...]
    k_s[...] = k * (1.0 + (a - 1.0) * ka_ref[...])

    c_len = RW_CHUNK
    n = RW_HEAD_DIM
    tb = rw_ref.shape[0]

    tri = jnp.where(lax.broadcasted_iota(jnp.int32, (c_len, c_len), 0)
                    >= lax.broadcasted_iota(jnp.int32, (c_len, c_len), 1), 1.0, 0.0).astype(BF16)
    for ci in range(tb // c_len):
        rows = slice(ci * c_len, (ci + 1) * c_len)
        lw_c = lw_s[rows, :]
        lw_hi = lw_c.astype(BF16)
        rem = lw_c - lw_hi.astype(F32)
        lw_mid = rem.astype(BF16)
        lw_lo = (rem - lw_mid.astype(F32)).astype(BF16)
        cum_s[rows, :] = (jnp.dot(tri, lw_hi, preferred_element_type=F32)
                          + jnp.dot(tri, lw_mid, preferred_element_type=F32)
                          + jnp.dot(tri, lw_lo, preferred_element_type=F32))

    row = lax.broadcasted_iota(jnp.int32, (c_len, 2 * c_len), 0)
    col = lax.broadcasted_iota(jnp.int32, (c_len, 2 * c_len), 1)
    incl2 = row >= col % c_len
    strict2 = row > col % c_len
    eye = jnp.where(lax.broadcasted_iota(jnp.int32, (c_len, c_len), 0)
                    == lax.broadcasted_iota(jnp.int32, (c_len, c_len), 1), 1.0, 0.0).astype(F32)

    def chunk(ci, carry):
        rows = pl.ds(pl.multiple_of(ci * c_len, c_len), c_len)
        heads = range(RW_HEADS)
        sl = [slice(h * n, (h + 1) * n) for h in heads]
        r = [r_s[rows, c] for c in sl]
        kh = [k_s[rows, c] for c in sl]
        v = [v_s[rows, c] for c in sl]
        lw = [lw_s[rows, c] for c in sl]
        cum = [cum_s[rows, c] for c in sl]
        kk = [kk_s[rows, c] for c in sl]
        kk = [x * lax.rsqrt(jnp.maximum(jnp.sum(x * x, axis=-1, keepdims=True), 1e-24)) for x in kk]
        kka = [kk[h] * a_s[rows, sl[h]] for h in heads]
        end = [jnp.sum(x, axis=0, keepdims=True) for x in lw]
        e_neg = [jnp.exp(-x) for x in cum]
        e_end = [jnp.exp(end[h] - cum[h]) for h in heads]
        left = [jnp.concatenate([-kk[h] * jnp.exp(cum[h] - lw[h]), r[h] * jnp.exp(cum[h])], axis=0)
                for h in heads]
        g = [_bdot_nt(left[h], jnp.concatenate([kka[h] * e_neg[h], kh[h] * e_neg[h]], axis=0))
             for h in heads]
        a_a = [jnp.where(strict2, x[:c_len, :], 0.0) for x in g]
        a_r = [jnp.where(incl2, x[c_len:, :], 0.0) for x in g]
        pw = [x[:, :c_len] for x in a_a]
        inv = [eye + x for x in pw]
        for _ in range(c_len.bit_length() - 2):
            pw = [_bdot(x, x) for x in pw]
            inv = [inv[h] + _bdot(inv[h], pw[h]) for h in heads]
        akv = [_bdot(a_a[h][:, c_len:], v[h]) for h in heads]
        s0 = [state_ref[h] for h in heads]
        ls = [_bdot_nt(left[h], s0[h]) for h in heads]
        u = [_bdot(inv[h], ls[h][:c_len, :] + akv[h]) for h in heads]
        uv = [jnp.concatenate([u[h], v[h]], axis=0) for h in heads]
        y = [ls[h][c_len:, :] + _bdot(a_r[h], uv[h]) for h in heads]
        for h in heads:
            state_ref[h] = s0[h] * jnp.exp(end[h]) + _bdot_tn(
                uv[h], jnp.concatenate([kka[h] * e_end[h], kh[h] * e_end[h]], axis=0))
        for h in heads:
            mean = jnp.mean(y[h], axis=-1, keepdims=True)
            yc = y[h] - mean
            var = jnp.mean(yc * yc, axis=-1, keepdims=True)
            yn = yc * lax.rsqrt(var + LN_X_EPS) * lnw_ref[:, sl[h]] + lnb_ref[:, sl[h]]
            bonus = jnp.sum(r[h] * kh[h] * rk_ref[:, sl[h]], axis=-1, keepdims=True) * v[h]
            o_ref[rows, sl[h]] = ((yn + bonus) * g_s[rows, sl[h]]).astype(o_ref.dtype)
        return carry

    lax.fori_loop(0, rw_ref.shape[0] // c_len, chunk, 0, unroll=True)


def _rwkv(rw, w0, w2, a0, a2, g2, k_k, k_a, r_k, ln_w, ln_b):
    b, t, _ = rw.shape
    tb = min(RW_BLOCK, t)
    w = RW_WIDTH
    vec = pl.BlockSpec((1, w), lambda bi, ti: (0, 0))
    mat = lambda r: pl.BlockSpec((r, w), lambda bi, ti: (0, 0))
    return pl.pallas_call(
        _rwkv_kernel,
        grid=(b, t // tb),
        in_specs=[pl.BlockSpec((None, tb, RW_COLS), lambda bi, ti: (bi, ti, 0)),
                  vec, mat(DECAY_LORA), vec, mat(AAA_LORA), mat(GATE_LORA), vec, vec, vec, vec, vec],
        out_specs=pl.BlockSpec((None, tb, w), lambda bi, ti: (bi, ti, 0)),
        out_shape=jax.ShapeDtypeStruct((b, t, w), BF16),
        scratch_shapes=[pltpu.VMEM((RW_HEADS, RW_HEAD_DIM, RW_HEAD_DIM), F32)]
        + [pltpu.VMEM((tb, w), F32)] * 8,
        compiler_params=_params("parallel", "arbitrary"),
        name="rwkv",
    )(rw, w0, w2, a0, a2, g2, k_k, k_a, r_k.reshape(1, w), ln_w, ln_b)


ROW_DTYPE = F32


def _pack_rows(x):
    return x.astype(BF16).astype(ROW_DTYPE)


def _unpack_rows(u):
    return u


ROW_SLAB = 8


def _store_rows(ref, u, r0=0):
    n = u.shape[0]
    for c in range(ROW_SLAB):
        ref[pl.ds(r0 * ROW_SLAB + c, n, stride=ROW_SLAB), :] = u[:, c * LANES:(c + 1) * LANES]


def _load_rows(ref, n, r0=0):
    return jnp.concatenate([ref[pl.ds(r0 * ROW_SLAB + c, n, stride=ROW_SLAB), :]
                            for c in range(ROW_SLAB)], axis=1)


def _out_kernel(yda_ref, yrw_ref, x_ref, mod_ref, wo_ref, pmn_ref, pfn_ref, rw_ref, rb_ref,
                x1_ref, h2_ref, idx_ref, wgt_ref):
    y = (jnp.dot(yda_ref[...], wo_ref[0:DA_WIDTH, :], preferred_element_type=F32)
         + jnp.dot(yrw_ref[...], wo_ref[DA_WIDTH:, :], preferred_element_type=F32))
    x1 = x_ref[...] + mod_ref[2:3, :] * _rms(y, pmn_ref[...], NORM_EPS)
    x1_ref[...] = x1
    h2 = _rms(x1, pfn_ref[...], NORM_EPS) * (1.0 + mod_ref[4:5, :]) + mod_ref[3:4, :]
    _store_rows(h2_ref, _pack_rows(h2))

    h_hi = h2.astype(BF16)
    h_lo = (h2 - h_hi.astype(F32)).astype(BF16)
    rw = rw_ref[...]
    w_hi = rw.astype(BF16)
    w_lo = (rw - w_hi.astype(F32)).astype(BF16)
    logits = (jnp.dot(h_hi, w_hi, preferred_element_type=F32)
              + jnp.dot(h_hi, w_lo, preferred_element_type=F32)
              + jnp.dot(h_lo, w_hi, preferred_element_type=F32)) + rb_ref[...]

    lane = lax.broadcasted_iota(jnp.int32, logits.shape, 1)
    slot = lax.broadcasted_iota(jnp.int32, idx_ref.shape, 1)
    idx = jnp.zeros(idx_ref.shape, jnp.int32)
    val = jnp.zeros(idx_ref.shape, F32)
    top = None
    for j in range(TOP_K):
        m = jnp.max(logits, axis=-1, keepdims=True)
        i = jnp.min(jnp.where(logits == m, lane, N_EXPERTS), axis=-1, keepdims=True)
        top = m if top is None else top
        idx = jnp.where(slot == j, i, idx)
        val = jnp.where(slot == j, jnp.exp(m - top), val)
        logits = jnp.where(lane == i, -jnp.inf, logits)
    idx_ref[...] = idx
    wgt_ref[...] = val / jnp.sum(val, axis=-1, keepdims=True)


def _out(y_da, y_rw, x, mod3, w_out_b, post_mix_norm, pre_ffn_norm, router_w, router_b):
    b, t, d = x.shape
    tm = min(OUT_ROWS, t)
    e = router_w.shape[1]
    nt = t // tm
    blk = lambda w: pl.BlockSpec((None, tm, w), lambda bi, ti: (bi, ti, 0))
    full = lambda r, c: pl.BlockSpec((r, c), lambda bi, ti: (0, 0))
    flat = lambda r, c: pl.BlockSpec((r, c), lambda bi, ti: (bi * nt + ti, 0))
    return pl.pallas_call(
        _out_kernel,
        grid=(b, nt),
        in_specs=[blk(DA_WIDTH), blk(RW_WIDTH), blk(d),
                  pl.BlockSpec((None, N_MOD, d), lambda bi, ti: (bi, 0, 0)),
                  full(d, d), full(1, d), full(1, d), full(d, e), full(1, e)],
        out_specs=[blk(d), flat(tm * ROW_SLAB, LANES), flat(tm, TOP_K), flat(tm, TOP_K)],
        out_shape=[jax.ShapeDtypeStruct((b, t, d), F32),
                   jax.ShapeDtypeStruct((b * t * ROW_SLAB, LANES), ROW_DTYPE),
                   jax.ShapeDtypeStruct((b * t, TOP_K), jnp.int32),
                   jax.ShapeDtypeStruct((b * t, TOP_K), F32)],
        compiler_params=_params("parallel", "parallel"),
        name="out",
    )(y_da, y_rw, x, mod3, w_out_b, post_mix_norm, pre_ffn_norm, router_w,
      router_b.reshape(1, e))


def _route(top_idx, rows_per_tile, n_tiles):
    experts = jnp.arange(N_EXPERTS, dtype=jnp.int32)
    chosen = (top_idx[:, :, None] == experts[None, None, :]).astype(jnp.int32)
    member = jnp.sum(chosen, axis=1)
    csum = jnp.cumsum(member, axis=0)
    counts = csum[-1]
    padded = (counts + rows_per_tile - 1) // rows_per_tile * rows_per_tile
    ends = jnp.cumsum(padded)
    starts = ends - padded
    pos = jnp.sum((csum - member + starts[None, :])[:, None, :] * chosen, axis=2).reshape(-1)
    n_active = ends[-1] // rows_per_tile
    tile_start = jnp.arange(n_tiles, dtype=jnp.int32) * rows_per_tile
    tile = jnp.minimum(tile_start, ends[-1] - 1)
    tile_expert = jnp.sum((tile[:, None] >= ends[None, :]).astype(jnp.int32), axis=1)
    used = padded > 0
    later = lax.cummin(jnp.where(used, experts, N_EXPERTS), reverse=True)
    following = jnp.concatenate([later[1:], jnp.full((1,), N_EXPERTS, jnp.int32)])
    following = jnp.where(following == N_EXPERTS, -1, following)
    run_index = jnp.cumsum(used.astype(jnp.int32)) - 1
    plan = (tile_expert.astype(jnp.int32), n_active.reshape(1).astype(jnp.int32),
            following[tile_expert].astype(jnp.int32), (run_index[tile_expert] % 2).astype(jnp.int32))
    tail = n_tiles * rows_per_tile - ends[-1]
    pads = (jnp.concatenate([starts + counts, ends[-1:]]).astype(jnp.int32),
            jnp.concatenate([padded - counts, (tail // (rows_per_tile // 2))[None]]).astype(jnp.int32))
    return pos.astype(jnp.int32), pads, plan


def _dispatch_kernel(ps_ref, pl_ref, pos_ref, h_ref, xs_ref, zero_ref, sem, pad_sem):
    tb = h_ref.shape[0] // ROW_SLAB

    def issue(t, carry):
        src = h_ref.at[pl.ds(pl.multiple_of(t * ROW_SLAB, ROW_SLAB), ROW_SLAB)]
        for j in range(TOP_K):
            pltpu.make_async_copy(src, xs_ref.at[pos_ref[0, t * TOP_K + j]],
                                  sem).start(priority=j % 2)
        return carry

    lax.fori_loop(0, tb, issue, 0, unroll=8)

    @pl.when(pl.program_id(0) == pl.num_programs(0) - 1)
    def _():
        zero_ref[...] = jnp.zeros_like(zero_ref)
        bits = [1 << k for k in reversed(range(zero_ref.shape[0].bit_length()))]

        def pad_copy(e, b):
            below = pl_ref[e] & ~(2 * b - 1)
            return pltpu.make_async_copy(zero_ref.at[pl.ds(0, b)],
                                         xs_ref.at[pl.ds(ps_ref[e] + below, b)], pad_sem)

        def tail_copy(i):
            rows = pl.ds(ps_ref[N_EXPERTS] + i * zero_ref.shape[0], zero_ref.shape[0])
            return pltpu.make_async_copy(zero_ref, xs_ref.at[rows], pad_sem)

        def tail_start(i, carry):
            tail_copy(i).start()
            return carry

        def tail_wait(i, carry):
            tail_copy(i).wait()
            return carry

        for wait in (False, True):
            for e in range(N_EXPERTS):
                for b in bits:
                    @pl.when((pl_ref[e] & b) != 0)
                    def _(e=e, b=b):
                        if wait:
                            pad_copy(e, b).wait()
                        else:
                            pad_copy(e, b).start()
            lax.fori_loop(0, pl_ref[N_EXPERTS], tail_wait if wait else tail_start, 0)

    for j in range(TOP_K):
        pltpu.make_async_copy(xs_ref.at[pl.ds(0, tb)], xs_ref.at[pl.ds(0, tb)], sem).wait()


def _dispatch(pos, pads, h2p, n_rows):
    pad_start, pad_len = pads
    n = h2p.shape[0] // ROW_SLAB
    w = h2p.shape[1]
    tb = min(DISPATCH_TOKENS, n)
    pos3 = pos.reshape(n // tb, 1, tb * TOP_K)
    return pl.pallas_call(
        _dispatch_kernel,
        grid_spec=pltpu.PrefetchScalarGridSpec(
            num_scalar_prefetch=2,
            grid=(n // tb,),
            in_specs=[pl.BlockSpec((None, 1, tb * TOP_K), lambda i, ps, pn: (i, 0, 0),
                                   memory_space=pltpu.SMEM),
                      pl.BlockSpec((tb * ROW_SLAB, w), lambda i, ps, pn: (i, 0))],
            out_specs=pl.BlockSpec(memory_space=pl.ANY),
            scratch_shapes=[pltpu.VMEM((EXPERT_ROWS // 2, ROW_SLAB, w), h2p.dtype),
                            pltpu.SemaphoreType.DMA(()), pltpu.SemaphoreType.DMA(())]),
        out_shape=jax.ShapeDtypeStruct((n_rows, ROW_SLAB, w), h2p.dtype),
        compiler_params=_params("arbitrary"),
        name="dispatch",
    )(pad_start, pad_len, pos3, h2p)


def _expert_kernel(te_ref, na_ref, nx_ref, sl_ref, xs_ref, w1_hbm, b1_ref, w2_hbm, b2_ref, ys_ref,
                   w1p_s, b1p_s, w2b_s, act_s, w1_buf, w2_buf, sem):
    i = pl.program_id(0)
    active = i < na_ref[0]
    expert = te_ref[i]
    fresh = jnp.logical_or(i == 0, expert != te_ref[jnp.maximum(i - 1, 0)])
    slot = sl_ref[i]
    grp = 2 * LANES
    n_grp = w1_buf.shape[2] // grp

    def weight_copies(ex, s):
        return (pltpu.make_async_copy(w1_hbm.at[ex], w1_buf.at[s], sem.at[0, s]),
                pltpu.make_async_copy(w2_hbm.at[ex], w2_buf.at[s], sem.at[1, s]))

    @pl.when(jnp.logical_and(active, i == 0))
    def _():
        for cp in weight_copies(expert, slot):
            cp.start()

    @pl.when(jnp.logical_and(active, fresh))
    def _():
        for cp in weight_copies(expert, slot):
            cp.wait()

        @pl.when(nx_ref[i] >= 0)
        def _():
            for cp in weight_copies(nx_ref[i], 1 - slot):
                cp.start()

        src = lax.broadcasted_iota(jnp.int32, (grp, grp), 0)
        dst = lax.broadcasted_iota(jnp.int32, (grp, grp), 1)
        perm = jnp.where(src == jnp.where(dst < LANES, 2 * dst, 2 * (dst - LANES) + 1),
                         1.0, 0.0).astype(BF16)
        for g in range(n_grp):
            sl = slice(g * grp, (g + 1) * grp)
            w1p_s[:, sl] = jnp.dot(w1_buf[slot, :, sl].astype(BF16), perm,
                                   preferred_element_type=F32).astype(BF16)
            b = b1_ref[:, sl]
            b_hi = b.astype(BF16)
            b_lo = (b - b_hi.astype(F32)).astype(BF16)
            b1p_s[:, sl] = (jnp.dot(b_hi, perm, preferred_element_type=F32)
                            + jnp.dot(b_lo, perm, preferred_element_type=F32))
        w2b_s[...] = w2_buf[slot].astype(BF16)

    @pl.when(active)
    def _():
        x = _unpack_rows(_load_rows(xs_ref, act_s.shape[0])).astype(BF16)
        hid = jnp.dot(x, w1p_s[...], preferred_element_type=F32) + b1p_s[0:1, :]
        for g in range(n_grp):
            glu = jnp.minimum(hid[:, g * grp:g * grp + LANES], SWIGLU_LIMIT)
            lin = jnp.clip(hid[:, g * grp + LANES:(g + 1) * grp], -SWIGLU_LIMIT, SWIGLU_LIMIT)
            act_s[:, g * LANES:(g + 1) * LANES] = (
                glu * jax.nn.sigmoid(SWIGLU_ALPHA * glu) * (lin + 1.0)).astype(BF16)
        y = jnp.dot(act_s[...], w2b_s[...], preferred_element_type=F32) + b2_ref[...]
        _store_rows(ys_ref, _pack_rows(y))

    @pl.when(jnp.logical_not(active))
    def _():
        ys_ref[...] = jnp.zeros_like(ys_ref)


def _experts(plan, xs, w1, b1, w2, b2):
    tile_expert, n_active, next_expert, tile_slot = plan
    n_rows, w = xs.shape[0] // ROW_SLAB, xs.shape[1]
    tm = EXPERT_ROWS
    d, f2 = w1.shape[1], w1.shape[2]
    f = f2 // 2
    wspec = lambda r, c: pl.BlockSpec((None, r, c), lambda i, te, na, nx, sl: (te[i], 0, 0))
    rows = pl.BlockSpec((tm * ROW_SLAB, w), lambda i, te, na, nx, sl: (i, 0))
    hbm = pl.BlockSpec(memory_space=pl.ANY)
    return pl.pallas_call(
        _expert_kernel,
        grid_spec=pltpu.PrefetchScalarGridSpec(
            num_scalar_prefetch=4,
            grid=(n_rows // tm,),
            in_specs=[rows, hbm, wspec(8, f2), hbm, wspec(1, d)],
            out_specs=rows,
            scratch_shapes=[pltpu.VMEM((d, f2), BF16), pltpu.VMEM((8, f2), F32),
                            pltpu.VMEM((f, d), BF16), pltpu.VMEM((tm, f), BF16),
                            pltpu.VMEM((2, d, f2), F32), pltpu.VMEM((2, f, d), F32),
                            pltpu.SemaphoreType.DMA((2, 2))]),
        out_shape=jax.ShapeDtypeStruct((n_rows * ROW_SLAB, w), ROW_DTYPE),
        compiler_params=_params("arbitrary"),
        name="expert",
    )(tile_expert, n_active, next_expert, tile_slot, xs, w1, b1, w2, b2)


def _combine_kernel(pos_ref, nxt_ref, ys_ref, wgt_ref, x1_ref, mod_ref, nw_ref, o_ref, buf_ref,
                    sem):
    tc = x1_ref.shape[0]
    step = pl.program_id(0) * pl.num_programs(1) + pl.program_id(1)
    last = pl.num_programs(0) * pl.num_programs(1) - 1
    slot = step % 2

    def gather(p_ref, s):
        def issue(t, carry):
            dst = pl.ds(pl.multiple_of(t * ROW_SLAB, ROW_SLAB), ROW_SLAB)
            for j in range(TOP_K):
                pltpu.make_async_copy(ys_ref.at[p_ref[0, t * TOP_K + j]], buf_ref.at[s, j, dst],
                                      sem.at[s]).start(priority=j % 2)
            return carry

        lax.fori_loop(0, tc, issue, 0, unroll=8)

    @pl.when(step == 0)
    def _():
        gather(pos_ref, slot)

    @pl.when(step < last)
    def _():
        gather(nxt_ref, 1 - slot)

    for j in range(TOP_K):
        pltpu.make_async_copy(ys_ref.at[pl.ds(0, tc)], ys_ref.at[pl.ds(0, tc)], sem.at[slot]).wait()

    sub = min(COMBINE_SUB_ROWS, tc)

    def mix(i, carry):
        r0 = pl.multiple_of(i * sub, sub)
        wgt = wgt_ref[pl.ds(r0, sub), :]
        acc = jnp.zeros((sub, x1_ref.shape[1]), F32)
        for j in range(TOP_K):
            acc = acc + wgt[:, j:j + 1] * _unpack_rows(_load_rows(buf_ref.at[slot, j], sub, r0))
        o_ref[pl.ds(r0, sub), :] = (x1_ref[pl.ds(r0, sub), :]
                                    + mod_ref[5:6, :] * _rms(acc, nw_ref[...], NORM_EPS))
        return carry

    lax.fori_loop(0, tc // sub, mix, 0)


def _combine(pos, ys, wgt, x1, mod3, post_ffn_norm):
    b, t, d = x1.shape
    tc = min(COMBINE_TOKENS, t)
    nt = t // tc
    pos3 = pos.reshape(b * nt, 1, tc * TOP_K)
    blk = lambda w: pl.BlockSpec((None, tc, w), lambda bi, ti: (bi, ti, 0))
    return pl.pallas_call(
        _combine_kernel,
        grid=(b, nt),
        in_specs=[pl.BlockSpec((None, 1, tc * TOP_K), lambda bi, ti: (bi * nt + ti, 0, 0),
                               memory_space=pltpu.SMEM),
                  pl.BlockSpec((None, 1, tc * TOP_K),
                               lambda bi, ti: (jnp.minimum(bi * nt + ti + 1, b * nt - 1), 0, 0),
                               memory_space=pltpu.SMEM),
                  pl.BlockSpec(memory_space=pl.ANY),
                  pl.BlockSpec((tc, TOP_K), lambda bi, ti: (bi * nt + ti, 0)), blk(d),
                  pl.BlockSpec((None, N_MOD, d), lambda bi, ti: (bi, 0, 0)),
                  pl.BlockSpec((1, d), lambda bi, ti: (0, 0))],
        out_specs=blk(d),
        out_shape=jax.ShapeDtypeStruct((b, t, d), F32),
        scratch_shapes=[pltpu.VMEM((2, TOP_K, tc * ROW_SLAB, LANES), ROW_DTYPE),
                        pltpu.SemaphoreType.DMA((2,))],
        compiler_params=_params("arbitrary", "arbitrary"),
        name="combine",
    )(pos3, pos3, ys.reshape(ys.shape[0] // ROW_SLAB, ROW_SLAB, LANES), wgt, x1, mod3,
      post_ffn_norm)


def _stages(x, c, positions, ada_w, ada_b, pre_mix_norm, post_mix_norm, pre_ffn_norm,
            post_ffn_norm, w_in, w_out, da_lambda_q1, da_lambda_k1, da_lambda_q2, da_lambda_k2,
            da_subln, rw_mu, rw_w0, rw_w2, rw_a0, rw_a2, rw_g2, rw_k_k, rw_k_a, rw_r_k, rw_ln_w,
            rw_ln_b, router_w, router_b, moe_w1, moe_b1, moe_w2, moe_b2):
    b, t, d = x.shape
    res = {}
    lambda_init = 0.8 - 0.6 * math.exp(-0.3 * 0)
    mod = _mod(c, ada_w[0], ada_b[0])
    res["mod"] = mod
    mod3 = mod.reshape(b, N_MOD, d)
    inv_freq = ROPE_THETA ** (-jnp.arange(0, ROPE_DIM, 2, dtype=F32) / ROPE_DIM)
    invf = jnp.tile(inv_freq, LANES // (ROPE_DIM // 2)).reshape(1, LANES)
    q, k, v, rw = _proj(x, positions.reshape(b, t, 1), mod3, pre_mix_norm, invf,
                        w_in[0].astype(BF16), rw_mu)
    res.update(q=q, k=k, v=v, rw=rw)
    lam4 = jnp.concatenate([da_lambda_q1, da_lambda_k1, da_lambda_q2, da_lambda_k2], axis=0)
    y_da = _attn(q, k, v, lam4, da_subln, lambda_init)
    res["y_da"] = y_da
    y_rw = _rwkv(rw, rw_w0, rw_w2[0], rw_a0, rw_a2[0], rw_g2[0], rw_k_k, rw_k_a, rw_r_k[0],
                 rw_ln_w, rw_ln_b)
    res["y_rw"] = y_rw
    x1, h2p, top_idx, top_w = _out(y_da, y_rw, x, mod3, w_out[0].astype(BF16), post_mix_norm,
                                   pre_ffn_norm, router_w[0], router_b[0])
    res.update(x1=x1, top_idx=top_idx, top_w=top_w)
    n = b * t
    n_tiles = n * TOP_K // EXPERT_ROWS + N_EXPERTS
    pos, pads, plan = _route(top_idx, EXPERT_ROWS, n_tiles)
    xs = _dispatch(pos, pads, h2p, n_tiles * EXPERT_ROWS)
    xs = xs.reshape(n_tiles * EXPERT_ROWS * ROW_SLAB, LANES)
    b1 = jnp.broadcast_to(moe_b1[0][:, None, :], (N_EXPERTS, 8, moe_b1.shape[-1]))
    ys = _experts(plan, xs, moe_w1[0], b1, moe_w2[0], moe_b2[0][:, None, :])
    res["final"] = _combine(pos, ys, top_w, x1, mod3, post_ffn_norm)
    return res


stages = _stages


def kernel(x, c, positions, ada_w, ada_b, pre_mix_norm, post_mix_norm, pre_ffn_norm, post_ffn_norm, w_in, w_out, da_lambda_q1, da_lambda_k1, da_lambda_q2, da_lambda_k2, da_subln, rw_mu, rw_w0, rw_w2, rw_a0, rw_a2, rw_g2, rw_k_k, rw_k_a, rw_r_k, rw_ln_w, rw_ln_b, router_w, router_b, moe_w1, moe_b1, moe_w2, moe_b2):
    res = _stages(x, c, positions, ada_w, ada_b, pre_mix_norm, post_mix_norm, pre_ffn_norm,
                  post_ffn_norm, w_in, w_out, da_lambda_q1, da_lambda_k1, da_lambda_q2,
                  da_lambda_k2, da_subln, rw_mu, rw_w0, rw_w2, rw_a0, rw_a2, rw_g2, rw_k_k,
                  rw_k_a, rw_r_k, rw_ln_w, rw_ln_b, router_w, router_b, moe_w1, moe_b1,
                  moe_w2, moe_b2)
    return res["final"]
```
